```python
import jax, jax.numpy as jnp
from jax import lax
import numpy as np

D_MODEL = 1024
BATCH = 1
SEQ = 16384
DEPTH = 1

GRID_W = 64
D_CONV = 512
CONV_WIDTH = 31
N_HEADS = 8
HEAD_DIM = 64
D_ATTN = N_HEADS * HEAD_DIM
WIN_H_MAX = 8
WIN_W = 16
N_EXPERTS = 64
TOP_K = 8
N_GROUPS = 8
TOPK_GROUPS = 4
D_EXPERT = 256
D_SHARED = 256
ROUTE_SCALE = 2.5
DISPATCH_BLOCK = 128
EPS = 1e-6
SPLITS = (D_CONV, 2 * D_CONV, 2 * D_CONV + D_ATTN, 2 * D_CONV + 2 * D_ATTN,
          2 * D_CONV + 3 * D_ATTN, 2 * D_CONV + 3 * D_ATTN + D_MODEL)
D_IN = 2 * D_CONV + 3 * D_ATTN + 2 * D_MODEL

kernel_name = 'hybrid_conformer_natten_moe_block'


def rms_norm(x, g):
    xf = x.astype(jnp.float32)
    y = xf * lax.rsqrt(jnp.mean(xf * xf, axis=-1, keepdims=True) + EPS)
    return (y * g.astype(jnp.float32)).astype(x.dtype)


def layer_norm(x, g, b):
    xf = x.astype(jnp.float32)
    mu = jnp.mean(xf, axis=-1, keepdims=True)
    var = jnp.mean(jnp.square(xf - mu), axis=-1, keepdims=True)
    y = (xf - mu) * lax.rsqrt(var + EPS)
    return (y * g.astype(jnp.float32) + b.astype(jnp.float32)).astype(x.dtype)


def conformer_conv(a, gate, w_dw, b_dw, ln_g, ln_b, w_pw):
    u = a * jax.nn.sigmoid(gate)
    pad = CONV_WIDTH // 2
    u = lax.conv_general_dilated(
        u, w_dw.astype(u.dtype), window_strides=(1,), padding=[(pad, pad)],
        dimension_numbers=('NWC', 'WIO', 'NWC'), feature_group_count=D_CONV) + b_dw
    u = jax.nn.silu(layer_norm(u, ln_g, ln_b))
    return u @ w_pw


def neighbourhood_attention(q, k, v, rpb):
    b, s = q.shape[0], q.shape[1]
    rows = s // GRID_W
    win_h = min(WIN_H_MAX, rows)
    grid = (b, rows, GRID_W, N_HEADS, HEAD_DIM)
    q, k, v = q.reshape(grid), k.reshape(grid), v.reshape(grid)
    cols = jnp.arange(GRID_W)
    col_start = jnp.clip(cols - WIN_W // 2, 0, GRID_W - WIN_W)
    col_idx = col_start[:, None] + jnp.arange(WIN_W)
    col_bias_idx = col_idx - cols[:, None] + (WIN_W - 1)
    scale = HEAD_DIM ** -0.5

    def one_row(r):
        rs = jnp.clip(r - win_h // 2, 0, rows - win_h)
        q_r = lax.dynamic_index_in_dim(q, r, axis=1, keepdims=False)
        k_slab = lax.dynamic_slice_in_dim(k, rs, win_h, axis=1)
        v_slab = lax.dynamic_slice_in_dim(v, rs, win_h, axis=1)
        k_win = k_slab[:, :, col_idx]
        v_win = v_slab[:, :, col_idx]
        row_bias_idx = rs + jnp.arange(win_h) - r + (WIN_H_MAX - 1)
        bias = rpb[:, row_bias_idx[None, :, None], col_bias_idx[:, None, :]]
        sc = jnp.einsum('bqhd,biqjhd->bhqij', q_r, k_win).astype(jnp.float32) * scale
        sc = sc + bias.astype(jnp.float32)[None]
        p = jax.nn.softmax(sc.reshape(b, N_HEADS, GRID_W, win_h * WIN_W), axis=-1)
        p = p.reshape(b, N_HEADS, GRID_W, win_h, WIN_W).astype(v.dtype)
        return jnp.einsum('bhqij,biqjhd->bqhd', p, v_win)

    out = lax.map(one_row, jnp.arange(rows))
    return out.transpose(1, 0, 2, 3, 4).reshape(b, s, D_ATTN)


def moe_ffn(h, w_router, router_bias, w_gate, w_up, w_down, w_sh_gate, w_sh_up, w_sh_down):
    b, s, d = h.shape
    n = b * s
    hf = h.reshape(n, d)
    scores = jax.nn.sigmoid((hf @ w_router).astype(jnp.float32))
    biased = scores + router_bias.astype(jnp.float32)
    per_group = N_EXPERTS // N_GROUPS
    grp_score = lax.top_k(biased.reshape(n, N_GROUPS, per_group), 2)[0].sum(-1)
    _, grp_idx = lax.top_k(grp_score, TOPK_GROUPS)
    grp_mask = jnp.any(grp_idx[..., None] == jnp.arange(N_GROUPS), axis=1)
    masked = jnp.where(jnp.repeat(grp_mask, per_group, axis=1), biased, -jnp.inf)
    _, idx = lax.top_k(masked, TOP_K)
    wts = jnp.take_along_axis(scores, idx, axis=1)
    wts = wts / jnp.sum(wts, axis=-1, keepdims=True) * ROUTE_SCALE
    nk = n * TOP_K
    n_blocks = (nk + N_EXPERTS * (DISPATCH_BLOCK - 1) + DISPATCH_BLOCK - 1) // DISPATCH_BLOCK
    flat_e = idx.reshape(nk)
    order = jnp.argsort(flat_e)
    sorted_e = flat_e[order]
    counts = jnp.bincount(flat_e, length=N_EXPERTS)
    padded = (counts + DISPATCH_BLOCK - 1) // DISPATCH_BLOCK * DISPATCH_BLOCK
    start = jnp.cumsum(counts) - counts
    pad_end = jnp.cumsum(padded)
    pad_start = pad_end - padded
    dest = pad_start[sorted_e] + jnp.arange(nk) - start[sorted_e]
    n_slots = n_blocks * DISPATCH_BLOCK
    slot_tok = jnp.full((n_slots,), n, jnp.int32).at[dest].set((order // TOP_K).astype(jnp.int32))
    slot_w = jnp.zeros((n_slots,), jnp.float32).at[dest].set(wts.reshape(nk)[order])
    block_expert = jnp.clip(jnp.searchsorted(pad_end, jnp.arange(n_blocks) * DISPATCH_BLOCK, side='right'),
                            0, N_EXPERTS - 1)
    h_pad = jnp.concatenate([hf, jnp.zeros((1, d), hf.dtype)], axis=0)

    def expert_block(args):
        tok, w, e = args
        xb = h_pad[tok]
        y = (jax.nn.silu(xb @ w_gate[e]) * (xb @ w_up[e])) @ w_down[e]
        return y * w[:, None].astype(y.dtype)

    yb = lax.map(expert_block, (slot_tok.reshape(n_blocks, DISPATCH_BLOCK),
                                slot_w.reshape(n_blocks, DISPATCH_BLOCK), block_expert))
    routed = jax.ops.segment_sum(yb.reshape(n_slots, d), slot_tok, num_segments=n + 1)[:n]
    shared = (jax.nn.silu(hf @ w_sh_gate) * (hf @ w_sh_up)) @ w_sh_down
    return (routed + shared).reshape(b, s, d)


def setup_inputs(seed: int = 0) -> dict:
    key = jax.random.key(seed)
    ks = jax.random.split(key, 24)
    L = DEPTH
    f32 = jnp.float32

    def nrm(k, shape, scale):
        return jax.random.normal(k, shape, f32) * scale

    return {
        'x': nrm(ks[0], (BATCH, SEQ, D_MODEL), 1.0),
        'norm_mix_g': 1.0 + nrm(ks[1], (L, D_MODEL), 0.02),
        'w_in': nrm(ks[2], (L, D_MODEL, D_IN), D_MODEL ** -0.5),
        'q_norm_g': 1.0 + nrm(ks[3], (L, HEAD_DIM), 0.02),
        'k_norm_g': 1.0 + nrm(ks[4], (L, HEAD_DIM), 0.02),
        'rpb': nrm(ks[5], (L, N_HEADS, 2 * WIN_H_MAX - 1, 2 * WIN_W - 1), 0.5),
        'conv_dw_w': nrm(ks[6], (L, CONV_WIDTH, 1, D_CONV), CONV_WIDTH ** -0.5),
        'conv_dw_b': nrm(ks[7], (L, D_CONV), 0.02),
        'conv_ln_g': 1.0 + nrm(ks[8], (L, D_CONV), 0.02),
        'conv_ln_b': nrm(ks[9], (L, D_CONV), 0.02),
        'w_conv_out': nrm(ks[10], (L, D_CONV, D_MODEL), D_CONV ** -0.5),
        'w_attn_out': nrm(ks[11], (L, D_ATTN, D_MODEL), D_ATTN ** -0.5),
        'w_o': nrm(ks[12], (L, D_MODEL, D_MODEL), D_MODEL ** -0.5),
        'norm_ffn_g': 1.0 + nrm(ks[13], (L, D_MODEL), 0.02),
        'w_router': nrm(ks[14], (L, D_MODEL, N_EXPERTS), D_MODEL ** -0.5),
        'router_bias': nrm(ks[15], (L, N_EXPERTS), 0.01),
        'w_exp_gate': nrm(ks[16], (L, N_EXPERTS, D_MODEL, D_EXPERT), D_MODEL ** -0.5),
        'w_exp_up': nrm(ks[17], (L, N_EXPERTS, D_MODEL, D_EXPERT), D_MODEL ** -0.5),
        'w_exp_down': nrm(ks[18], (L, N_EXPERTS, D_EXPERT, D_MODEL), D_EXPERT ** -0.5),
        'w_sh_gate': nrm(ks[19], (L, D_MODEL, D_SHARED), D_MODEL ** -0.5),
        'w_sh_up': nrm(ks[20], (L, D_MODEL, D_SHARED), D_MODEL ** -0.5),
        'w_sh_down': nrm(ks[21], (L, D_SHARED, D_MODEL), D_SHARED ** -0.5),
    }


def reference(x, norm_mix_g, w_in, q_norm_g, k_norm_g, rpb, conv_dw_w, conv_dw_b, conv_ln_g,
              conv_ln_b, w_conv_out, w_attn_out, w_o, norm_ffn_g, w_router, router_bias,
              w_exp_gate, w_exp_up, w_exp_down, w_sh_gate, w_sh_up, w_sh_down):
    b, s, _ = x.shape
    for l in range(DEPTH):
        h = rms_norm(x, norm_mix_g[l])
        proj = h @ w_in[l]
        a, glu_gate, q, k, v, g_a, g_b = jnp.split(proj, SPLITS, axis=-1)
        y_conv = conformer_conv(a, glu_gate, conv_dw_w[l], conv_dw_b[l], conv_ln_g[l],
                                conv_ln_b[l], w_conv_out[l])
        heads = (b, s, N_HEADS, HEAD_DIM)
        q = rms_norm(q.reshape(heads), q_norm_g[l])
        k = rms_norm(k.reshape(heads), k_norm_g[l])
        v = v.reshape(heads)
        y_attn = neighbourhood_attention(q, k, v, rpb[l]) @ w_attn_out[l]
        merged = jax.nn.sigmoid(g_a) * y_conv + jax.nn.sigmoid(g_b) * y_attn
        x = x + merged @ w_o[l]
        x = x + moe_ffn(rms_norm(x, norm_ffn_g[l]), w_router[l], router_bias[l], w_exp_gate[l],
                        w_exp_up[l], w_exp_down[l], w_sh_gate[l], w_sh_up[l], w_sh_down[l])
    return x
```

```python
import functools

import jax
import jax.numpy as jnp
from jax import lax
from jax.experimental import pallas as pl
from jax.experimental.pallas import tpu as pltpu

D_MODEL = 1024
GRID_W = 64
D_CONV = 512
CONV_WIDTH = 31
CONV_PAD = CONV_WIDTH // 2
N_HEADS = 8
HEAD_DIM = 64
D_ATTN = N_HEADS * HEAD_DIM
WIN_H = 8
WIN_W = 16
N_EXPERTS = 64
TOP_K = 8
N_GROUPS = 8
GROUP_SIZE = N_EXPERTS // N_GROUPS
TOPK_GROUPS = 4
D_EXPERT = 256
D_SHARED = 256
ROUTE_SCALE = 2.5
EPS = 1e-6

NEG = -1e30
HALO = 16
ATT_ROWS = 4
ATT_BLK = ATT_ROWS * GRID_W
SLOT_BLK = 256
VMEM_LIMIT = 56 * 1024 * 1024

F32 = jnp.float32
BF16 = jnp.bfloat16
I32 = jnp.int32
U32 = jnp.uint32


def _cparams(*sem):
    return pltpu.CompilerParams(dimension_semantics=sem, vmem_limit_bytes=VMEM_LIMIT)


def _rms(x, g):
    return x * lax.rsqrt(jnp.mean(x * x, axis=-1, keepdims=True) + EPS) * g


def _pack_bf16_pair(a, b):
    def rnd(v):
        bits = lax.bitcast_convert_type(v, U32)
        return (bits + jnp.uint32(0x7FFF) + ((bits >> 16) & jnp.uint32(1))) >> 16
    return (rnd(a) << 16) | rnd(b)


def _unpack_bf16_pair(w):
    a = lax.bitcast_convert_type(w & jnp.uint32(0xFFFF0000), F32)
    b = lax.bitcast_convert_type(w << 16, F32)
    return a, b


def _inproj_kernel(x_ref, g_ref, w_ref, bsum_ref, qg_ref, kg_ref, u_ref, q_ref, k_ref, v_ref):
    h = _rms(x_ref[...], g_ref[...]).astype(BF16)
    ag = jnp.dot(h, w_ref[:, 0:2 * D_CONV], preferred_element_type=F32)
    u_ref[...] = ag[:, :D_CONV] * jax.nn.sigmoid(ag[:, D_CONV:])
    bsum = bsum_ref[...]

    def head_norm(z, g):
        ss = jnp.dot((z * z).astype(BF16), bsum, preferred_element_type=F32)
        return z * lax.rsqrt(ss * (1.0 / HEAD_DIM) + EPS) * g

    c0 = 2 * D_CONV
    q = jnp.dot(h, w_ref[:, c0:c0 + D_ATTN], preferred_element_type=F32)
    q_ref[...] = (head_norm(q, qg_ref[...]) * (HEAD_DIM ** -0.5)).astype(BF16)
    k = jnp.dot(h, w_ref[:, c0 + D_ATTN:c0 + 2 * D_ATTN], preferred_element_type=F32)
    k_ref[...] = head_norm(k, kg_ref[...]).astype(BF16)
    v = jnp.dot(h, w_ref[:, c0 + 2 * D_ATTN:c0 + 3 * D_ATTN], preferred_element_type=F32)
    v_ref[...] = v.astype(BF16)


def _inproj(x, g, w, bsum, qg, kg, tm=512):
    n = x.shape[0]
    wc = w.shape[1]
    full = lambda i: (0, 0)
    row = lambda i: (i, 0)
    return pl.pallas_call(
        _inproj_kernel,
        grid=(n // tm,),
        in_specs=[pl.BlockSpec((tm, D_MODEL), row), pl.BlockSpec((1, D_MODEL), full),
                  pl.BlockSpec((D_MODEL, wc), full), pl.BlockSpec((D_ATTN, D_ATTN), full),
                  pl.BlockSpec((1, D_ATTN), full), pl.BlockSpec((1, D_ATTN), full)],
        out_specs=[pl.BlockSpec((tm, D_CONV), row), pl.BlockSpec((tm, D_ATTN), row),
                   pl.BlockSpec((tm, D_ATTN), row), pl.BlockSpec((tm, D_ATTN), row)],
        out_shape=[jax.ShapeDtypeStruct((n, D_CONV), F32), jax.ShapeDtypeStruct((n, D_ATTN), BF16),
                   jax.ShapeDtypeStruct((n, D_ATTN), BF16), jax.ShapeDtypeStruct((n, D_ATTN), BF16)],
        compiler_params=_cparams("parallel"),
        name="inproj",
    )(x, g, w, bsum, qg, kg)


def _conv_kernel(up_ref, uc_ref, un_ref, w_ref, b_ref, lg_ref, lb_ref, o_ref, ext_ref, *, tc, ch):
    i = pl.program_id(0)
    last = pl.num_programs(0) - 1
    ext_ref[0:HALO, :] = jnp.where(i > 0, up_ref[...], 0.0)
    ext_ref[HALO:HALO + tc, :] = uc_ref[...]
    ext_ref[HALO + tc:2 * HALO + tc, :] = jnp.where(i < last, un_ref[...], 0.0)
    bias = b_ref[...]
    lg = lg_ref[...]
    lb = lb_ref[...]
    for c in range(tc // ch):
        base = c * ch + HALO - CONV_PAD
        acc = jnp.broadcast_to(bias, (ch, D_CONV))
        for j in range(CONV_WIDTH):
            acc = acc + ext_ref[base + j:base + j + ch, :] * w_ref[j:j + 1, :]
        mu = jnp.mean(acc, axis=-1, keepdims=True)
        d = acc - mu
        var = jnp.mean(d * d, axis=-1, keepdims=True)
        y = d * lax.rsqrt(var + EPS) * lg + lb
        o_ref[c * ch:(c + 1) * ch, :] = (y * jax.nn.sigmoid(y)).astype(BF16)


def _conv(u, w, b, lg, lb, tc=256, ch=64):
    n = u.shape[0]
    hb = tc // HALO
    nh = n // HALO
    full = lambda i: (0, 0)
    return pl.pallas_call(
        functools.partial(_conv_kernel, tc=tc, ch=ch),
        grid=(n // tc,),
        in_specs=[pl.BlockSpec((HALO, D_CONV), lambda i: (jnp.maximum(i * hb - 1, 0), 0)),
                  pl.BlockSpec((tc, D_CONV), lambda i: (i, 0)),
                  pl.BlockSpec((HALO, D_CONV), lambda i: (jnp.minimum((i + 1) * hb, nh - 1), 0)),
                  pl.BlockSpec((CONV_WIDTH, D_CONV), full), pl.BlockSpec((1, D_CONV), full),
                  pl.BlockSpec((1, D_CONV), full), pl.BlockSpec((1, D_CONV), full)],
        out_specs=pl.BlockSpec((tc, D_CONV), lambda i: (i, 0)),
        out_shape=jax.ShapeDtypeStruct((n, D_CONV), BF16),
        scratch_shapes=[pltpu.VMEM((tc + 2 * HALO, D_CONV), F32)],
        compiler_params=_cparams("parallel"),
        name="conv",
    )(u, u, u, w, b, lg, lb)


def _attn_kernel(q_ref, kp_ref, kc_ref, kn_ref, vp_ref, vc_ref, vn_ref, tb_ref, o_ref, *, rows):
    i = pl.program_id(0)
    nkey = 3 * ATT_BLK
    lrow = lax.broadcasted_iota(I32, (1, nkey), 1) >> 6
    lane = lax.broadcasted_iota(I32, (1, 2 * HEAD_DIM), 1)
    masks = []
    for j in range(ATT_ROWS):
        r = i * ATT_ROWS + j
        rs = jnp.clip(r - WIN_H // 2, 0, rows - WIN_H)
        lo = rs - (i - 1) * ATT_ROWS
        masks.append(jnp.where((lrow >= lo) & (lrow < lo + WIN_H), 0.0, NEG))
    dn = (((1,), (1,)), ((), ()))
    for p in range(N_HEADS // 2):
        sl = slice(2 * HEAD_DIM * p, 2 * HEAD_DIM * (p + 1))
        q2 = q_ref[:, sl]
        ks = (kp_ref[:, sl], kc_ref[:, sl], kn_ref[:, sl])
        vs = (vp_ref[:, sl], vc_ref[:, sl], vn_ref[:, sl])
        out_pair = None
        for half in range(2):
            h = 2 * p + half
            in_half = (lane >= HEAD_DIM * half) & (lane < HEAD_DIM * (half + 1))
            qm = jnp.where(in_half, q2, jnp.zeros_like(q2))
            s = jnp.concatenate([lax.dot_general(qm, kk, dn, preferred_element_type=F32) for kk in ks], axis=1)
            parts = []
            for j in range(ATT_ROWS):
                bias = jnp.concatenate([tb_ref[h, 2 * m - j + 3] for m in range(nkey // (2 * GRID_W))], axis=1)
                parts.append(s[j * GRID_W:(j + 1) * GRID_W, :] + bias + masks[j])
            s = jnp.concatenate(parts, axis=0)
            mx = jnp.max(s, axis=-1, keepdims=True)
            e = jnp.exp(s - mx)
            den = jnp.sum(e, axis=-1, keepdims=True)
            pb = e.astype(BF16)
            o = sum(jnp.dot(pb[:, ATT_BLK * t:ATT_BLK * (t + 1)], vs[t], preferred_element_type=F32)
                    for t in range(3))
            o = o / den
            out_pair = o if out_pair is None else jnp.where(in_half, o, out_pair)
        o_ref[:, sl] = out_pair.astype(BF16)


def _attn(q, k, v, tb):
    n = q.shape[0]
    rows = n // GRID_W
    nb = n // ATT_BLK
    cur = lambda i: (i, 0)
    prev = lambda i: (jnp.maximum(i - 1, 0), 0)
    nxt = lambda i: (jnp.minimum(i + 1, nb - 1), 0)
    blk = (ATT_BLK, D_ATTN)
    return pl.pallas_call(
        functools.partial(_attn_kernel, rows=rows),
        grid=(nb,),
        in_specs=[pl.BlockSpec(blk, cur), pl.BlockSpec(blk, prev), pl.BlockSpec(blk, cur), pl.BlockSpec(blk, nxt),
                  pl.BlockSpec(blk, prev), pl.BlockSpec(blk, cur), pl.BlockSpec(blk, nxt),
                  pl.BlockSpec(tb.shape, lambda i: (0, 0, 0, 0))],
        out_specs=pl.BlockSpec(blk, cur),
        out_shape=jax.ShapeDtypeStruct((n, D_ATTN), BF16),
        compiler_params=_cparams("parallel"),
        name="attn",
    )(q, k, k, k, v, v, v, tb)


def _bias_table(rpb):
    cols = jnp.arange(GRID_W)
    start = jnp.clip(cols - WIN_W // 2, 0, GRID_W - WIN_W)
    kc = cols[None, :]
    inwin = (kc >= start[:, None]) & (kc < start[:, None] + WIN_W)
    rel = jnp.clip(kc - cols[:, None] + (WIN_W - 1), 0, 2 * WIN_W - 2)
    t = jnp.where(inwin[None, None], rpb[:, :, rel], NEG)
    t = jnp.concatenate([t, jnp.full_like(t[:, :1], NEG)], axis=1)
    t_next = jnp.concatenate([t[:, 1:], t[:, -1:]], axis=1)
    return jnp.concatenate([t, t_next], axis=-1).astype(F32)


def _mix_route_kernel(x_ref, uc_ref, at_ref, g1_ref, wg_ref, wco_ref, wao_ref, wo_ref, g2_ref,
                      wrh_ref, wrl_ref, rb_ref, tri_ref, wsgu_ref, wsd_ref,
                      xo_ref, hp_ref, idx_ref, rank_ref, wt_ref, cnt_ref, carry_ref, *, tm):
    i = pl.program_id(0)

    @pl.when(i == 0)
    def _():
        carry_ref[...] = jnp.zeros_like(carry_ref)

    x = x_ref[...]
    h = _rms(x, g1_ref[...]).astype(BF16)
    gates = jax.nn.sigmoid(jnp.dot(h, wg_ref[...], preferred_element_type=F32))
    y_conv = jnp.dot(uc_ref[...], wco_ref[...], preferred_element_type=F32)
    y_attn = jnp.dot(at_ref[...], wao_ref[...], preferred_element_type=F32)
    merged = gates[:, :D_MODEL] * y_conv + gates[:, D_MODEL:] * y_attn
    x1 = x + jnp.dot(merged.astype(BF16), wo_ref[...], preferred_element_type=F32)
    h2 = _rms(x1, g2_ref[...])
    h2b = h2.astype(BF16)
    hp_ref[...] = _pack_bf16_pair(h2[:, :D_MODEL // 2], h2[:, D_MODEL // 2:])

    gu = jnp.dot(h2b, wsgu_ref[...], preferred_element_type=F32)
    mid = (jax.nn.silu(gu[:, :D_SHARED]) * gu[:, D_SHARED:]).astype(BF16)
    xo_ref[...] = x1 + jnp.dot(mid, wsd_ref[...], preferred_element_type=F32)

    h2l = (h2 - h2b.astype(F32)).astype(BF16)
    dn = (((1,), (1,)), ((), ()))
    logits = (lax.dot_general(wrh_ref[...], h2b, dn, preferred_element_type=F32)
              + lax.dot_general(wrl_ref[...], h2b, dn, preferred_element_type=F32)
              + lax.dot_general(wrh_ref[...], h2l, dn, preferred_element_type=F32))
    scores = jax.nn.sigmoid(logits)
    biased = scores + rb_ref[...]

    sub = lax.broadcasted_iota(I32, (GROUP_SIZE, tm), 0).astype(F32)
    groups, gscore = [], []
    for g in range(N_GROUPS):
        bg = biased[g * GROUP_SIZE:(g + 1) * GROUP_SIZE, :]
        m1 = jnp.max(bg, axis=0, keepdims=True)
        first = jnp.min(jnp.where(bg == m1, sub, float(GROUP_SIZE)), axis=0, keepdims=True)
        m2 = jnp.max(jnp.where(sub == first, -jnp.inf, bg), axis=0, keepdims=True)
        groups.append(bg)
        gscore.append(m1 + m2)
    masked = []
    for g in range(N_GROUPS):
        beaten = jnp.zeros((1, tm), F32)
        for o in range(N_GROUPS):
            if o == g:
                continue
            wins = (gscore[o] >= gscore[g]) if o < g else (gscore[o] > gscore[g])
            beaten = beaten + jnp.where(wins, 1.0, 0.0)
        masked.append(jnp.where(beaten < TOPK_GROUPS, groups[g], -jnp.inf))
    cur = jnp.concatenate(masked, axis=0)

    eid = lax.broadcasted_iota(I32, (N_EXPERTS, tm), 0).astype(F32)
    idx_rows, sc_rows = [], []
    sel = jnp.zeros((N_EXPERTS, tm), F32)
    for _ in range(TOP_K):
        m = jnp.max(cur, axis=0, keepdims=True)
        idx = jnp.min(jnp.where(cur == m, eid, float(N_EXPERTS)), axis=0, keepdims=True)
        hit = eid == idx
        sel = jnp.where(hit, 1.0, sel)
        cur = jnp.where(hit, -jnp.inf, cur)
        idx_rows.append(idx)
        sc_rows.append(jnp.sum(jnp.where(hit, scores, 0.0), axis=0, keepdims=True))
    wsum = sc_rows[0]
    for k in range(1, TOP_K):
        wsum = wsum + sc_rows[k]

    prefix = jnp.dot(sel.astype(BF16), tri_ref[...], preferred_element_type=F32)
    rank = carry_ref[:, 0:1] + prefix
    rank_rows = [jnp.sum(jnp.where(eid == idx_rows[k], rank, 0.0), axis=0, keepdims=True) for k in range(TOP_K)]
    carry_ref[...] = carry_ref[...] + jnp.sum(sel, axis=1, keepdims=True)
    cnt_ref[...] = carry_ref[...]

    idx_ref[...] = jnp.concatenate(idx_rows, axis=0).astype(I32)
    rank_ref[...] = jnp.concatenate(rank_rows, axis=0).astype(I32)
    wt_ref[...] = jnp.concatenate([s / wsum * ROUTE_SCALE for s in sc_rows], axis=0)


def _mix_route(x, uc, at, g1, wg, wco, wao, wo, g2, wrh, wrl, rb, wsgu, wsd, tm=256):
    n = x.shape[0]
    tri = (jnp.arange(tm)[:, None] < jnp.arange(tm)[None, :]).astype(BF16)
    row = lambda i: (i, 0)
    col = lambda i: (0, i)
    full = lambda i: (0, 0)
    ins = [x, uc, at, g1, wg, wco, wao, wo, g2, wrh, wrl, rb, tri, wsgu, wsd]
    in_specs = [pl.BlockSpec((tm, D_MODEL), row), pl.BlockSpec((tm, D_CONV), row), pl.BlockSpec((tm, D_ATTN), row)]
    in_specs += [pl.BlockSpec(a.shape, full) for a in ins[3:]]
    return pl.pallas_call(
        functools.partial(_mix_route_kernel, tm=tm),
        grid=(n // tm,),
        in_specs=in_specs,
        out_specs=[pl.BlockSpec((tm, D_MODEL), row), pl.BlockSpec((tm, D_MODEL // 2), row),
                   pl.BlockSpec((TOP_K, tm), col), pl.BlockSpec((TOP_K, tm), col), pl.BlockSpec((TOP_K, tm), col),
                   pl.BlockSpec((N_EXPERTS, 128), full)],
        out_shape=[jax.ShapeDtypeStruct((n, D_MODEL), F32), jax.ShapeDtypeStruct((n, D_MODEL // 2), U32),
                   jax.ShapeDtypeStruct((TOP_K, n), I32), jax.ShapeDtypeStruct((TOP_K, n), I32),
                   jax.ShapeDtypeStruct((TOP_K, n), F32), jax.ShapeDtypeStruct((N_EXPERTS, 128), F32)],
        scratch_shapes=[pltpu.VMEM((N_EXPERTS, 128), F32)],
        compiler_params=_cparams("arbitrary"),
        name="mix_route",
    )(*ins)


def _dispatch_kernel(meta_ref, idx_ref, rank_ref, hp_ref, xs_ref, zero_ref, sem, zsem, *, tm):
    i = pl.program_id(0)

    def tail_copy(e):
        start = pl.multiple_of(meta_ref[N_EXPERTS + e], SLOT_BLK)
        return pltpu.make_async_copy(zero_ref, xs_ref.at[pl.ds(start, SLOT_BLK)], zsem)

    @pl.when(i == 0)
    def _():
        zero_ref[...] = jnp.zeros_like(zero_ref)

        def start(e, c):
            @pl.when(meta_ref[2 * N_EXPERTS + e] > 0)
            def _():
                tail_copy(e).start()
            return c

        def wait(e, c):
            @pl.when(meta_ref[2 * N_EXPERTS + e] > 0)
            def _():
                tail_copy(e).wait()
            return c

        lax.fori_loop(0, N_EXPERTS, start, 0)
        lax.fori_loop(0, N_EXPERTS, wait, 0)

    def row_copy(t, k):
        slot = meta_ref[idx_ref[0, k, t]] + rank_ref[0, k, t]
        return pltpu.make_async_copy(hp_ref.at[pl.ds(t, 1)], xs_ref.at[pl.ds(slot, 1)], sem)

    def start_rows(t, c):
        for k in range(TOP_K):
            row_copy(t, k).start()
        return c

    def wait_rows(t, c):
        for k in range(TOP_K):
            row_copy(t, k).wait()
        return c

    lax.fori_loop(0, tm, start_rows, 0)
    lax.fori_loop(0, tm, wait_rows, 0)


def _dispatch(meta, idx3, rank3, hp, n_slots, tm):
    n = hp.shape[0]
    grid_spec = pltpu.PrefetchScalarGridSpec(
        num_scalar_prefetch=1,
        grid=(n // tm,),
        in_specs=[pl.BlockSpec((1, TOP_K, tm), lambda i, m: (i, 0, 0), memory_space=pltpu.SMEM),
                  pl.BlockSpec((1, TOP_K, tm), lambda i, m: (i, 0, 0), memory_space=pltpu.SMEM),
                  pl.BlockSpec((tm, D_MODEL // 2), lambda i, m: (i, 0))],
        out_specs=pl.BlockSpec(memory_space=pl.ANY),
        scratch_shapes=[pltpu.VMEM((SLOT_BLK, D_MODEL // 2), U32), pltpu.SemaphoreType.DMA,
                        pltpu.SemaphoreType.DMA],
    )
    return pl.pallas_call(
        functools.partial(_dispatch_kernel, tm=tm),
        grid_spec=grid_spec,
        out_shape=jax.ShapeDtypeStruct((n_slots, D_MODEL // 2), U32),
        compiler_params=_cparams("arbitrary"),
        name="dispatch",
    )(meta, idx3, rank3, hp)


def _experts_kernel(bexp_ref, bidx_ref, nused_ref, xs_ref, wg_ref, wu_ref, wd_ref, y_ref, wgu_s, wd_s):
    b = pl.program_id(0)

    @pl.when(b < nused_ref[0])
    def _():
        changed = (b == 0) | (bexp_ref[b] != bexp_ref[jnp.maximum(b - 1, 0)])

        @pl.when(changed)
        def _():
            wgu_s[:, :D_EXPERT] = wg_ref[0].astype(BF16)
            wgu_s[:, D_EXPERT:] = wu_ref[0].astype(BF16)
            wd_s[...] = wd_ref[0].astype(BF16)

        xa, xb = _unpack_bf16_pair(xs_ref[...])
        x = jnp.concatenate([xa.astype(BF16), xb.astype(BF16)], axis=1)
        gu = jnp.dot(x, wgu_s[...], preferred_element_type=F32)
        mid = (jax.nn.silu(gu[:, :D_EXPERT]) * gu[:, D_EXPERT:]).astype(BF16)
        y = jnp.dot(mid, wd_s[...], preferred_element_type=F32)
        y_ref[...] = _pack_bf16_pair(y[:, :D_MODEL // 2], y[:, D_MODEL // 2:])


def _experts(bexp, bidx, nused, xs, wg, wu, wd):
    n_blocks = bexp.shape[0]
    grid_spec = pltpu.PrefetchScalarGridSpec(
        num_scalar_prefetch=3,
        grid=(n_blocks,),
        in_specs=[pl.BlockSpec((SLOT_BLK, D_MODEL // 2), lambda b, be, bi, nu: (bi[b], 0)),
                  pl.BlockSpec((1, D_MODEL, D_EXPERT), lambda b, be, bi, nu: (be[b], 0, 0)),
                  pl.BlockSpec((1, D_MODEL, D_EXPERT), lambda b, be, bi, nu: (be[b], 0, 0)),
                  pl.BlockSpec((1, D_EXPERT, D_MODEL), lambda b, be, bi, nu: (be[b], 0, 0))],
        out_specs=pl.BlockSpec((SLOT_BLK, D_MODEL // 2), lambda b, be, bi, nu: (bi[b], 0)),
        scratch_shapes=[pltpu.VMEM((D_MODEL, 2 * D_EXPERT), BF16), pltpu.VMEM((D_EXPERT, D_MODEL), BF16)],
    )
    return pl.pallas_call(
        _experts_kernel,
        grid_spec=grid_spec,
        out_shape=jax.ShapeDtypeStruct(xs.shape, U32),
        compiler_params=_cparams("arbitrary"),
        name="experts",
    )(bexp, bidx, nused, xs, wg, wu, wd)


def _combine_kernel(meta_ref, idx_ref, rank_ref, wt_ref, xo_ref, y_ref, o_ref, buf_ref, sem, *, tm):
    def row_copy(t, k):
        slot = meta_ref[idx_ref[0, k, t]] + rank_ref[0, k, t]
        return pltpu.make_async_copy(y_ref.at[pl.ds(slot, 1)], buf_ref.at[k, pl.ds(t, 1)], sem)

    def start_rows(t, c):
        for k in range(TOP_K):
            row_copy(t, k).start()
        return c

    def wait_rows(t, c):
        for k in range(TOP_K):
            row_copy(t, k).wait()
        return c

    lax.fori_loop(0, tm, start_rows, 0)
    lax.fori_loop(0, tm, wait_rows, 0)

    half = D_MODEL // 2
    acc_a = xo_ref[:, :half]
    acc_b = xo_ref[:, half:]
    for k in range(TOP_K):
        ya, yb = _unpack_bf16_pair(buf_ref[k])
        w = wt_ref[:, k:k + 1]
        acc_a = acc_a + w * ya
        acc_b = acc_b + w * yb
    o_ref[:, :half] = acc_a
    o_ref[:, half:] = acc_b


def _combine(meta, idx3, rank3, wt, xo, y, tm):
    n = xo.shape[0]
    grid_spec = pltpu.PrefetchScalarGridSpec(
        num_scalar_prefetch=1,
        grid=(n // tm,),
        in_specs=[pl.BlockSpec((1, TOP_K, tm), lambda i, m: (i, 0, 0), memory_space=pltpu.SMEM),
                  pl.BlockSpec((1, TOP_K, tm), lambda i, m: (i, 0, 0), memory_space=pltpu.SMEM),
                  pl.BlockSpec((tm, TOP_K), lambda i, m: (i, 0)),
                  pl.BlockSpec((tm, D_MODEL), lambda i, m: (i, 0)),
                  pl.BlockSpec(memory_space=pl.ANY)],
        out_specs=pl.BlockSpec((tm, D_MODEL), lambda i, m: (i, 0)),
        scratch_shapes=[pltpu.VMEM((TOP_K, tm, D_MODEL // 2), U32), pltpu.SemaphoreType.DMA],
    )
    return pl.pallas_call(
        functools.partial(_combine_kernel, tm=tm),
        grid_spec=grid_spec,
        out_shape=jax.ShapeDtypeStruct((n, D_MODEL), F32),
        compiler_params=_cparams("arbitrary"),
        name="combine",
    )(meta, idx3, rank3, wt, xo, y)


def _layer(x, norm_mix_g, w_in, q_norm_g, k_norm_g, rpb, conv_dw_w, conv_dw_b, conv_ln_g, conv_ln_b,
           w_conv_out, w_attn_out, w_o, norm_ffn_g, w_router, router_bias, w_exp_gate, w_exp_up,
           w_exp_down, w_sh_gate, w_sh_up, w_sh_down):
    n = x.shape[0]
    c_qkv = 2 * D_CONV + 3 * D_ATTN
    w_in_b = w_in.astype(BF16)
    head_of = jnp.arange(D_ATTN) // HEAD_DIM
    bsum = (head_of[:, None] == head_of[None, :]).astype(BF16)
    row = lambda v: v.reshape(1, -1).astype(F32)

    u, q, k, v = _inproj(x, row(norm_mix_g), w_in_b[:, :c_qkv], bsum,
                         row(jnp.tile(q_norm_g, N_HEADS)), row(jnp.tile(k_norm_g, N_HEADS)))
    uc = _conv(u, conv_dw_w.reshape(CONV_WIDTH, D_CONV), row(conv_dw_b), row(conv_ln_g), row(conv_ln_b))
    at = _attn(q, k, v, _bias_table(rpb))

    wr_t = w_router.T
    wr_hi = wr_t.astype(BF16)
    wr_lo = (wr_t - wr_hi.astype(F32)).astype(BF16)
    wsgu = jnp.concatenate([w_sh_gate, w_sh_up], axis=1).astype(BF16)
    xo, hp, idx_t, rank_t, wt_t, cnt = _mix_route(
        x, uc, at, row(norm_mix_g), w_in_b[:, c_qkv:], w_conv_out.astype(BF16), w_attn_out.astype(BF16),
        w_o.astype(BF16), row(norm_ffn_g), wr_hi, wr_lo, router_bias.reshape(N_EXPERTS, 1).astype(F32),
        wsgu, w_sh_down.astype(BF16))

    counts = cnt[:, 0].astype(I32)
    padded = (counts + SLOT_BLK - 1) // SLOT_BLK * SLOT_BLK
    pad_end = jnp.cumsum(padded)
    pad_start = pad_end - padded
    n_blocks = (n * TOP_K + N_EXPERTS * (SLOT_BLK - 1)) // SLOT_BLK
    n_used = pad_end[-1] // SLOT_BLK
    bidx = jnp.minimum(jnp.arange(n_blocks, dtype=I32), n_used - 1)
    bexp = jnp.clip(jnp.searchsorted(pad_end, bidx * SLOT_BLK, side='right'), 0, N_EXPERTS - 1).astype(I32)
    meta = jnp.concatenate([pad_start, pad_end - SLOT_BLK, counts]).astype(I32)

    tmd = 256
    split = lambda a, tm: a.reshape(TOP_K, n // tm, tm).transpose(1, 0, 2)
    xs = _dispatch(meta, split(idx_t, tmd), split(rank_t, tmd), hp, n_blocks * SLOT_BLK, tmd)
    y = _experts(bexp, bidx, n_used.reshape(1).astype(I32), xs, w_exp_gate, w_exp_up, w_exp_down)
    tmc = 128
    return _combine(meta, split(idx_t, tmc), split(rank_t, tmc), wt_t.T, xo, y, tmc)


def kernel(x, norm_mix_g, w_in, q_norm_g, k_norm_g, rpb, conv_dw_w, conv_dw_b, conv_ln_g, conv_ln_b, w_conv_out, w_attn_out, w_o, norm_ffn_g, w_router, router_bias, w_exp_gate, w_exp_up, w_exp_down, w_sh_gate, w_sh_up, w_sh_down):
    b, s, d = x.shape
    assert b == 1 and d == D_MODEL and s % (2 * ATT_BLK) == 0, x.shape
    xf = x.reshape(b * s, d)
    depth = norm_mix_g.shape[0]
    for l in range(depth):
        xf = _layer(xf, norm_mix_g[l], w_in[l], q_norm_g[l], k_norm_g[l], rpb[l], conv_dw_w[l], conv_dw_b[l],
                    conv_ln_g[l], conv_ln_b[l], w_conv_out[l], w_attn_out[l], w_o[l], norm_ffn_g[l],
                    w_router[l], router_bias[l], w_exp_gate[l], w_exp_up[l], w_exp_down[l], w_sh_gate[l],
                    w_sh_up[l], w_sh_down[l])
    return xf.reshape(b, s, d)
```

```python
import functools

import jax
import jax.numpy as jnp
from jax import lax
from jax.experimental import pallas as pl
from jax.experimental.pallas import tpu as pltpu

D_MODEL = 1024
GRID_W = 64
D_CONV = 512
CONV_WIDTH = 31
CONV_PAD = CONV_WIDTH // 2
N_HEADS = 8
HEAD_DIM = 64
D_ATTN = N_HEADS * HEAD_DIM
WIN_H = 8
WIN_W = 16
N_EXPERTS = 64
TOP_K = 8
N_GROUPS = 8
GROUP_SIZE = N_EXPERTS // N_GROUPS
TOPK_GROUPS = 4
D_EXPERT = 256
D_SHARED = 256
ROUTE_SCALE = 2.5
EPS = 1e-6

NEG = -1e30
HALO = 16
ATT_ROWS = 4
ATT_BLK = ATT_ROWS * GRID_W
SLOT_BLK = 512
VMEM_LIMIT = 56 * 1024 * 1024

F32 = jnp.float32
BF16 = jnp.bfloat16
I32 = jnp.int32
U32 = jnp.uint32


def _cparams(*sem):
    return pltpu.CompilerParams(dimension_semantics=sem, vmem_limit_bytes=VMEM_LIMIT)


def _rms(x, g):
    return x * lax.rsqrt(jnp.mean(x * x, axis=-1, keepdims=True) + EPS) * g


def _pack_bf16_pair(a, b):
    def bf16_bits(v):
        return lax.bitcast_convert_type(v.astype(BF16).astype(F32), U32)
    return bf16_bits(a) | (bf16_bits(b) >> 16)


def _unpack_bf16_pair(w):
    a = lax.bitcast_convert_type(w & jnp.uint32(0xFFFF0000), F32)
    b = lax.bitcast_convert_type(w << 16, F32)
    return a, b


def _inproj_kernel(x_ref, g_ref, w_ref, bsum_ref, qg_ref, kg_ref, u_ref, q_ref, k_ref, v_ref):
    h = _rms(x_ref[...], g_ref[...]).astype(BF16)
    ag = jnp.dot(h, w_ref[:, 0:2 * D_CONV], preferred_element_type=F32)
    u_ref[...] = ag[:, :D_CONV] * jax.nn.sigmoid(ag[:, D_CONV:])
    bsum = bsum_ref[...]

    def head_norm(z, g):
        ss = jnp.dot((z * z).astype(BF16), bsum, preferred_element_type=F32)
        return z * lax.rsqrt(ss * (1.0 / HEAD_DIM) + EPS) * g

    c0 = 2 * D_CONV
    q = jnp.dot(h, w_ref[:, c0:c0 + D_ATTN], preferred_element_type=F32)
    q_ref[...] = (head_norm(q, qg_ref[...]) * (HEAD_DIM ** -0.5)).astype(BF16)
    k = jnp.dot(h, w_ref[:, c0 + D_ATTN:c0 + 2 * D_ATTN], preferred_element_type=F32)
    k_ref[...] = head_norm(k, kg_ref[...]).astype(BF16)
    v = jnp.dot(h, w_ref[:, c0 + 2 * D_ATTN:c0 + 3 * D_ATTN], preferred_element_type=F32)
    v_ref[...] = v.astype(BF16)


def _inproj(x, g, w, bsum, qg, kg, tm=512):
    n = x.shape[0]
    wc = w.shape[1]
    full = lambda i: (0, 0)
    row = lambda i: (i, 0)
    return pl.pallas_call(
        _inproj_kernel,
        grid=(n // tm,),
        in_specs=[pl.BlockSpec((tm, D_MODEL), row), pl.BlockSpec((1, D_MODEL), full),
                  pl.BlockSpec((D_MODEL, wc), full), pl.BlockSpec((D_ATTN, D_ATTN), full),
                  pl.BlockSpec((1, D_ATTN), full), pl.BlockSpec((1, D_ATTN), full)],
        out_specs=[pl.BlockSpec((tm, D_CONV), row), pl.BlockSpec((tm, D_ATTN), row),
                   pl.BlockSpec((tm, D_ATTN), row), pl.BlockSpec((tm, D_ATTN), row)],
        out_shape=[jax.ShapeDtypeStruct((n, D_CONV), F32), jax.ShapeDtypeStruct((n, D_ATTN), BF16),
                   jax.ShapeDtypeStruct((n, D_ATTN), BF16), jax.ShapeDtypeStruct((n, D_ATTN), BF16)],
        compiler_params=_cparams("parallel"),
        name="inproj",
    )(x, g, w, bsum, qg, kg)


def _conv_kernel(up_ref, uc_ref, un_ref, w_ref, b_ref, lg_ref, lb_ref, o_ref, ext_ref, *, tc, ch):
    i = pl.program_id(0)
    last = pl.num_programs(0) - 1
    ext_ref[0:HALO, :] = jnp.where(i > 0, up_ref[...], 0.0)
    ext_ref[HALO:HALO + tc, :] = uc_ref[...]
    ext_ref[HALO + tc:2 * HALO + tc, :] = jnp.where(i < last, un_ref[...], 0.0)
    bias = b_ref[...]
    lg = lg_ref[...]
    lb = lb_ref[...]
    for c in range(tc // ch):
        base = c * ch + HALO - CONV_PAD
        acc = jnp.broadcast_to(bias, (ch, D_CONV))
        for j in range(CONV_WIDTH):
            acc = acc + ext_ref[base + j:base + j + ch, :] * w_ref[j:j + 1, :]
        mu = jnp.mean(acc, axis=-1, keepdims=True)
        d = acc - mu
        var = jnp.mean(d * d, axis=-1, keepdims=True)
        y = d * lax.rsqrt(var + EPS) * lg + lb
        o_ref[c * ch:(c + 1) * ch, :] = (y * jax.nn.sigmoid(y)).astype(BF16)


def _conv(u, w, b, lg, lb, tc=256, ch=64):
    n = u.shape[0]
    hb = tc // HALO
    nh = n // HALO
    full = lambda i: (0, 0)
    return pl.pallas_call(
        functools.partial(_conv_kernel, tc=tc, ch=ch),
        grid=(n // tc,),
        in_specs=[pl.BlockSpec((HALO, D_CONV), lambda i: (jnp.maximum(i * hb - 1, 0), 0)),
                  pl.BlockSpec((tc, D_CONV), lambda i: (i, 0)),
                  pl.BlockSpec((HALO, D_CONV), lambda i: (jnp.minimum((i + 1) * hb, nh - 1), 0)),
                  pl.BlockSpec((CONV_WIDTH, D_CONV), full), pl.BlockSpec((1, D_CONV), full),
                  pl.BlockSpec((1, D_CONV), full), pl.BlockSpec((1, D_CONV), full)],
        out_specs=pl.BlockSpec((tc, D_CONV), lambda i: (i, 0)),
        out_shape=jax.ShapeDtypeStruct((n, D_CONV), BF16),
        scratch_shapes=[pltpu.VMEM((tc + 2 * HALO, D_CONV), F32)],
        compiler_params=_cparams("parallel"),
        name="conv",
    )(u, u, u, w, b, lg, lb)


def _attn_kernel(q_ref, kp_ref, kc_ref, kn_ref, vp_ref, vc_ref, vn_ref, tb_ref, o_ref, *, rows):
    i = pl.program_id(0)
    nkey = 3 * ATT_BLK
    lrow = lax.broadcasted_iota(I32, (1, nkey), 1) >> 6
    lane = lax.broadcasted_iota(I32, (1, 2 * HEAD_DIM), 1)
    masks = []
    for j in range(ATT_ROWS):
        r = i * ATT_ROWS + j
        rs = jnp.clip(r - WIN_H // 2, 0, rows - WIN_H)
        lo = rs - (i - 1) * ATT_ROWS
        masks.append(jnp.where((lrow >= lo) & (lrow < lo + WIN_H), 0.0, NEG))
    dn = (((1,), (1,)), ((), ()))
    for p in range(N_HEADS // 2):
        sl = slice(2 * HEAD_DIM * p, 2 * HEAD_DIM * (p + 1))
        q2 = q_ref[:, sl]
        ks = (kp_ref[:, sl], kc_ref[:, sl], kn_ref[:, sl])
        vs = (vp_ref[:, sl], vc_ref[:, sl], vn_ref[:, sl])
        out_pair = None
        for half in range(2):
            h = 2 * p + half
            in_half = (lane >= HEAD_DIM * half) & (lane < HEAD_DIM * (half + 1))
            qm = jnp.where(in_half, q2, jnp.zeros_like(q2))
            s = jnp.concatenate([lax.dot_general(qm, kk, dn, preferred_element_type=F32) for kk in ks], axis=1)
            parts = []
            for j in range(ATT_ROWS):
                bias = jnp.concatenate([tb_ref[h, 2 * m - j + 3] for m in range(nkey // (2 * GRID_W))], axis=1)
                parts.append(s[j * GRID_W:(j + 1) * GRID_W, :] + bias + masks[j])
            s = jnp.concatenate(parts, axis=0)
            mx = jnp.max(s, axis=-1, keepdims=True)
            e = jnp.exp(s - mx)
            den = jnp.sum(e, axis=-1, keepdims=True)
            pb = e.astype(BF16)
            o = sum(jnp.dot(pb[:, ATT_BLK * t:ATT_BLK * (t + 1)], vs[t], preferred_element_type=F32)
                    for t in range(3))
            o = o / den
            out_pair = o if out_pair is None else jnp.where(in_half, o, out_pair)
        o_ref[:, sl] = out_pair.astype(BF16)


def _attn(q, k, v, tb):
    n = q.shape[0]
    rows = n // GRID_W
    nb = n // ATT_BLK
    cur = lambda i: (i, 0)
    prev = lambda i: (jnp.maximum(i - 1, 0), 0)
    nxt = lambda i: (jnp.minimum(i + 1, nb - 1), 0)
    blk = (ATT_BLK, D_ATTN)
    return pl.pallas_call(
        functools.partial(_attn_kernel, rows=rows),
        grid=(nb,),
        in_specs=[pl.BlockSpec(blk, cur), pl.BlockSpec(blk, prev), pl.BlockSpec(blk, cur), pl.BlockSpec(blk, nxt),
                  pl.BlockSpec(blk, prev), pl.BlockSpec(blk, cur), pl.BlockSpec(blk, nxt),
                  pl.BlockSpec(tb.shape, lambda i: (0, 0, 0, 0))],
        out_specs=pl.BlockSpec(blk, cur),
        out_shape=jax.ShapeDtypeStruct((n, D_ATTN), BF16),
        compiler_params=_cparams("parallel"),
        name="attn",
    )(q, k, k, k, v, v, v, tb)


def _bias_table(rpb):
    cols = jnp.arange(GRID_W)
    start = jnp.clip(cols - WIN_W // 2, 0, GRID_W - WIN_W)
    kc = cols[None, :]
    inwin = (kc >= start[:, None]) & (kc < start[:, None] + WIN_W)
    off = GRID_W - WIN_W
    padded = jnp.pad(rpb, ((0, 0), (0, 0), (off, off)))
    rel = jnp.stack([padded[:, :, GRID_W - 1 - c:2 * GRID_W - 1 - c] for c in range(GRID_W)], axis=2)
    t = jnp.where(inwin[None, None], rel, NEG)
    t = jnp.concatenate([t, jnp.full_like(t[:, :1], NEG)], axis=1)
    t_next = jnp.concatenate([t[:, 1:], t[:, -1:]], axis=1)
    return jnp.concatenate([t, t_next], axis=-1).astype(F32)


def _mix_route_kernel(x_ref, uc_ref, at_ref, g1_ref, wg_ref, wco_ref, wao_ref, wo_ref, g2_ref,
                      wrh_ref, wrl_ref, rb_ref, tri_ref, wsgu_ref, wsd_ref,
                      xo_ref, hp_ref, idx_ref, rank_ref, wt_ref, cnt_ref, carry_ref, *, tm):
    i = pl.program_id(0)

    @pl.when(i == 0)
    def _():
        carry_ref[...] = jnp.zeros_like(carry_ref)

    x = x_ref[...]
    h = _rms(x, g1_ref[...]).astype(BF16)
    gates = jax.nn.sigmoid(jnp.dot(h, wg_ref[...], preferred_element_type=F32))
    y_conv = jnp.dot(uc_ref[...], wco_ref[...], preferred_element_type=F32)
    y_attn = jnp.dot(at_ref[...], wao_ref[...], preferred_element_type=F32)
    merged = gates[:, :D_MODEL] * y_conv + gates[:, D_MODEL:] * y_attn
    x1 = x + jnp.dot(merged.astype(BF16), wo_ref[...], preferred_element_type=F32)
    h2 = _rms(x1, g2_ref[...])
    h2b = h2.astype(BF16)
    hp_ref[...] = _pack_bf16_pair(h2[:, :D_MODEL // 2], h2[:, D_MODEL // 2:])

    gu = jnp.dot(h2b, wsgu_ref[...], preferred_element_type=F32)
    mid = (jax.nn.silu(gu[:, :D_SHARED]) * gu[:, D_SHARED:]).astype(BF16)
    xo_ref[...] = x1 + jnp.dot(mid, wsd_ref[...], preferred_element_type=F32)

    h2l = (h2 - h2b.astype(F32)).astype(BF16)
    dn = (((1,), (1,)), ((), ()))
    logits = (lax.dot_general(wrh_ref[...], h2b, dn, preferred_element_type=F32)
              + lax.dot_general(wrl_ref[...], h2b, dn, preferred_element_type=F32)
              + lax.dot_general(wrh_ref[...], h2l, dn, preferred_element_type=F32))
    scores = jax.nn.sigmoid(logits)
    biased = scores + rb_ref[...]

    sub = lax.broadcasted_iota(I32, (GROUP_SIZE, tm), 0).astype(F32)
    groups, gscore = [], []
    for g in range(N_GROUPS):
        bg = biased[g * GROUP_SIZE:(g + 1) * GROUP_SIZE, :]
        m1 = jnp.max(bg, axis=0, keepdims=True)
        first = jnp.min(jnp.where(bg == m1, sub, float(GROUP_SIZE)), axis=0, keepdims=True)
        m2 = jnp.max(jnp.where(sub == first, -jnp.inf, bg), axis=0, keepdims=True)
        groups.append(bg)
        gscore.append(m1 + m2)
    masked = []
    for g in range(N_GROUPS):
        beaten = jnp.zeros((1, tm), F32)
        for o in range(N_GROUPS):
            if o == g:
                continue
            wins = (gscore[o] >= gscore[g]) if o < g else (gscore[o] > gscore[g])
            beaten = beaten + jnp.where(wins, 1.0, 0.0)
        masked.append(jnp.where(beaten < TOPK_GROUPS, groups[g], -jnp.inf))
    cur = jnp.concatenate(masked, axis=0)

    eid = lax.broadcasted_iota(I32, (N_EXPERTS, tm), 0).astype(F32)
    idx_rows, sc_rows = [], []
    sel = jnp.zeros((N_EXPERTS, tm), F32)
    for _ in range(TOP_K):
        m = jnp.max(cur, axis=0, keepdims=True)
        idx = jnp.min(jnp.where(cur == m, eid, float(N_EXPERTS)), axis=0, keepdims=True)
        hit = eid == idx
        sel = jnp.where(hit, 1.0, sel)
        cur = jnp.where(hit, -jnp.inf, cur)
        idx_rows.append(idx)
        sc_rows.append(jnp.sum(jnp.where(hit, scores, 0.0), axis=0, keepdims=True))
    wsum = sc_rows[0]
    for k in range(1, TOP_K):
        wsum = wsum + sc_rows[k]

    prefix = jnp.dot(sel.astype(BF16), tri_ref[...], preferred_element_type=F32)
    rank = carry_ref[:, 0:1] + prefix
    rank_rows = [jnp.sum(jnp.where(eid == idx_rows[k], rank, 0.0), axis=0, keepdims=True) for k in range(TOP_K)]
    carry_ref[...] = carry_ref[...] + jnp.sum(sel, axis=1, keepdims=True)
    cnt_ref[...] = carry_ref[...]

    idx_ref[...] = jnp.concatenate(idx_rows, axis=0).astype(I32)
    rank_ref[...] = jnp.concatenate(rank_rows, axis=0).astype(I32)
    wt_ref[...] = jnp.concatenate([s / wsum * ROUTE_SCALE for s in sc_rows], axis=0)


def _mix_route(x, uc, at, g1, wg, wco, wao, wo, g2, wrh, wrl, rb, wsgu, wsd, tm=256):
    n = x.shape[0]
    tri = (jnp.arange(tm)[:, None] < jnp.arange(tm)[None, :]).astype(BF16)
    row = lambda i: (i, 0)
    col = lambda i: (0, i)
    full = lambda i: (0, 0)
    ins = [x, uc, at, g1, wg, wco, wao, wo, g2, wrh, wrl, rb, tri, wsgu, wsd]
    in_specs = [pl.BlockSpec((tm, D_MODEL), row), pl.BlockSpec((tm, D_CONV), row), pl.BlockSpec((tm, D_ATTN), row)]
    in_specs += [pl.BlockSpec(a.shape, full) for a in ins[3:]]
    return pl.pallas_call(
        functools.partial(_mix_route_kernel, tm=tm),
        grid=(n // tm,),
        in_specs=in_specs,
        out_specs=[pl.BlockSpec((tm, D_MODEL), row), pl.BlockSpec((tm, D_MODEL // 2), row),
                   pl.BlockSpec((TOP_K, tm), col), pl.BlockSpec((TOP_K, tm), col), pl.BlockSpec((TOP_K, tm), col),
                   pl.BlockSpec((N_EXPERTS, 128), full)],
        out_shape=[jax.ShapeDtypeStruct((n, D_MODEL), F32), jax.ShapeDtypeStruct((n, D_MODEL // 2), U32),
                   jax.ShapeDtypeStruct((TOP_K, n), I32), jax.ShapeDtypeStruct((TOP_K, n), I32),
                   jax.ShapeDtypeStruct((TOP_K, n), F32), jax.ShapeDtypeStruct((N_EXPERTS, 128), F32)],
        scratch_shapes=[pltpu.VMEM((N_EXPERTS, 128), F32)],
        compiler_params=_cparams("arbitrary"),
        name="mix_route",
    )(*ins)


def _dispatch_kernel(meta_ref, idx_ref, rank_ref, hp_ref, xs_ref, zero_ref, sem, zsem, *, tm):
    i = pl.program_id(0)

    def tail_copy(e):
        start = pl.multiple_of(meta_ref[N_EXPERTS + e], SLOT_BLK)
        return pltpu.make_async_copy(zero_ref, xs_ref.at[pl.ds(start, SLOT_BLK)], zsem)

    @pl.when(i == 0)
    def _():
        zero_ref[...] = jnp.zeros_like(zero_ref)

        def start(e, c):
            @pl.when(meta_ref[2 * N_EXPERTS + e] > 0)
            def _():
                tail_copy(e).start()
            return c

        def wait(e, c):
            @pl.when(meta_ref[2 * N_EXPERTS + e] > 0)
            def _():
                tail_copy(e).wait()
            return c

        lax.fori_loop(0, N_EXPERTS, start, 0)
        lax.fori_loop(0, N_EXPERTS, wait, 0)

    def row_copy(t, k):
        slot = meta_ref[idx_ref[0, k, t]] + rank_ref[0, k, t]
        return pltpu.make_async_copy(hp_ref.at[pl.ds(t, 1)], xs_ref.at[pl.ds(slot, 1)], sem)

    def start_rows(t, c):
        for k in range(TOP_K):
            row_copy(t, k).start()
        return c

    def wait_rows(t, c):
        for k in range(TOP_K):
            row_copy(t, k).wait()
        return c

    lax.fori_loop(0, tm, start_rows, 0)
    lax.fori_loop(0, tm, wait_rows, 0)


def _dispatch(meta, idx3, rank3, hp, n_slots, tm):
    n = hp.shape[0]
    grid_spec = pltpu.PrefetchScalarGridSpec(
        num_scalar_prefetch=1,
        grid=(n // tm,),
        in_specs=[pl.BlockSpec((1, TOP_K, tm), lambda i, m: (i, 0, 0), memory_space=pltpu.SMEM),
                  pl.BlockSpec((1, TOP_K, tm), lambda i, m: (i, 0, 0), memory_space=pltpu.SMEM),
                  pl.BlockSpec((tm, D_MODEL // 2), lambda i, m: (i, 0))],
        out_specs=pl.BlockSpec(memory_space=pl.ANY),
        scratch_shapes=[pltpu.VMEM((SLOT_BLK, D_MODEL // 2), U32), pltpu.SemaphoreType.DMA,
                        pltpu.SemaphoreType.DMA],
    )
    return pl.pallas_call(
        functools.partial(_dispatch_kernel, tm=tm),
        grid_spec=grid_spec,
        out_shape=jax.ShapeDtypeStruct((n_slots, D_MODEL // 2), U32),
        compiler_params=_cparams("arbitrary"),
        name="dispatch",
    )(meta, idx3, rank3, hp)


def _experts_kernel(bexp_ref, bidx_ref, nused_ref, xs_ref, wg_ref, wu_ref, wd_ref, y_ref, wgu_s, wd_s):
    b = pl.program_id(0)

    @pl.when(b < nused_ref[0])
    def _():
        changed = (b == 0) | (bexp_ref[b] != bexp_ref[jnp.maximum(b - 1, 0)])

        @pl.when(changed)
        def _():
            wgu_s[:, :D_EXPERT] = wg_ref[0].astype(BF16)
            wgu_s[:, D_EXPERT:] = wu_ref[0].astype(BF16)
            wd_s[...] = wd_ref[0].astype(BF16)

        xa, xb = _unpack_bf16_pair(xs_ref[...])
        x = jnp.concatenate([xa.astype(BF16), xb.astype(BF16)], axis=1)
        gu = jnp.dot(x, wgu_s[...], preferred_element_type=F32)
        mid = (jax.nn.silu(gu[:, :D_EXPERT]) * gu[:, D_EXPERT:]).astype(BF16)
        y = jnp.dot(mid, wd_s[...], preferred_element_type=F32)
        y_ref[...] = _pack_bf16_pair(y[:, :D_MODEL // 2], y[:, D_MODEL // 2:])


def _experts(bexp, bidx, nused, xs, wg, wu, wd):
    n_blocks = bexp.shape[0]
    grid_spec = pltpu.PrefetchScalarGridSpec(
        num_scalar_prefetch=3,
        grid=(n_blocks,),
        in_specs=[pl.BlockSpec((SLOT_BLK, D_MODEL // 2), lambda b, be, bi, nu: (bi[b], 0)),
                  pl.BlockSpec((1, D_MODEL, D_EXPERT), lambda b, be, bi, nu: (be[b], 0, 0)),
                  pl.BlockSpec((1, D_MODEL, D_EXPERT), lambda b, be, bi, nu: (be[b], 0, 0)),
                  pl.BlockSpec((1, D_EXPERT, D_MODEL), lambda b, be, bi, nu: (be[b], 0, 0))],
        out_specs=pl.BlockSpec((SLOT_BLK, D_MODEL // 2), lambda b, be, bi, nu: (bi[b], 0)),
        scratch_shapes=[pltpu.VMEM((D_MODEL, 2 * D_EXPERT), BF16), pltpu.VMEM((D_EXPERT, D_MODEL), BF16)],
    )
    return pl.pallas_call(
        _experts_kernel,
        grid_spec=grid_spec,
        out_shape=jax.ShapeDtypeStruct(xs.shape, U32),
        compiler_params=_cparams("arbitrary"),
        name="experts",
    )(bexp, bidx, nused, xs, wg, wu, wd)


def _combine_kernel(meta_ref, idx_ref, rank_ref, wt_ref, xo_ref, y_ref, o_ref, buf_ref, sem, *, tm):
    def row_copy(t, k):
        slot = meta_ref[idx_ref[0, k, t]] + rank_ref[0, k, t]
        return pltpu.make_async_copy(y_ref.at[pl.ds(slot, 1)], buf_ref.at[k, pl.ds(t, 1)], sem)

    def start_rows(t, c):
        for k in range(TOP_K):
            row_copy(t, k).start()
        return c

    def wait_rows(t, c):
        for k in range(TOP_K):
            row_copy(t, k).wait()
        return c

    lax.fori_loop(0, tm, start_rows, 0)
    lax.fori_loop(0, tm, wait_rows, 0)

    half = D_MODEL // 2
    acc_a = xo_ref[:, :half]
    acc_b = xo_ref[:, half:]
    for k in range(TOP_K):
        ya, yb = _unpack_bf16_pair(buf_ref[k])
        w = wt_ref[:, k:k + 1]
        acc_a = acc_a + w * ya
        acc_b = acc_b + w * yb
    o_ref[:, :half] = acc_a
    o_ref[:, half:] = acc_b


def _combine(meta, idx3, rank3, wt, xo, y, tm):
    n = xo.shape[0]
    grid_spec = pltpu.PrefetchScalarGridSpec(
        num_scalar_prefetch=1,
        grid=(n // tm,),
        in_specs=[pl.BlockSpec((1, TOP_K, tm), lambda i, m: (i, 0, 0), memory_space=pltpu.SMEM),
                  pl.BlockSpec((1, TOP_K, tm), lambda i, m: (i, 0, 0), memory_space=pltpu.SMEM),
                  pl.BlockSpec((tm, TOP_K), lambda i, m: (i, 0)),
                  pl.BlockSpec((tm, D_MODEL), lambda i, m: (i, 0)),
                  pl.BlockSpec(memory_space=pl.ANY)],
        out_specs=pl.BlockSpec((tm, D_MODEL), lambda i, m: (i, 0)),
        scratch_shapes=[pltpu.VMEM((TOP_K, tm, D_MODEL // 2), U32), pltpu.SemaphoreType.DMA],
    )
    return pl.pallas_call(
        functools.partial(_combine_kernel, tm=tm),
        grid_spec=grid_spec,
        out_shape=jax.ShapeDtypeStruct((n, D_MODEL), F32),
        compiler_params=_cparams("arbitrary"),
        name="combine",
    )(meta, idx3, rank3, wt, xo, y)


def _layer(x, norm_mix_g, w_in, q_norm_g, k_norm_g, rpb, conv_dw_w, conv_dw_b, conv_ln_g, conv_ln_b,
           w_conv_out, w_attn_out, w_o, norm_ffn_g, w_router, router_bias, w_exp_gate, w_exp_up,
           w_exp_down, w_sh_gate, w_sh_up, w_sh_down):
    n = x.shape[0]
    c_qkv = 2 * D_CONV + 3 * D_ATTN
    w_in_b = w_in.astype(BF16)
    head_of = jnp.arange(D_ATTN) // HEAD_DIM
    bsum = (head_of[:, None] == head_of[None, :]).astype(BF16)
    row = lambda v: v.reshape(1, -1).astype(F32)

    u, q, k, v = _inproj(x, row(norm_mix_g), w_in_b[:, :c_qkv], bsum,
                         row(jnp.tile(q_norm_g, N_HEADS)), row(jnp.tile(k_norm_g, N_HEADS)))
    uc = _conv(u, conv_dw_w.reshape(CONV_WIDTH, D_CONV), row(conv_dw_b), row(conv_ln_g), row(conv_ln_b))
    at = _attn(q, k, v, _bias_table(rpb))

    wr_t = w_router.T
    wr_hi = wr_t.astype(BF16)
    wr_lo = (wr_t - wr_hi.astype(F32)).astype(BF16)
    wsgu = jnp.concatenate([w_sh_gate, w_sh_up], axis=1).astype(BF16)
    xo, hp, idx_t, rank_t, wt_t, cnt = _mix_route(
        x, uc, at, row(norm_mix_g), w_in_b[:, c_qkv:], w_conv_out.astype(BF16), w_attn_out.astype(BF16),
        w_o.astype(BF16), row(norm_ffn_g), wr_hi, wr_lo, router_bias.reshape(N_EXPERTS, 1).astype(F32),
        wsgu, w_sh_down.astype(BF16))

    counts = cnt[:, 0].astype(I32)
    padded = (counts + SLOT_BLK - 1) // SLOT_BLK * SLOT_BLK
    pad_end = jnp.cumsum(padded)
    pad_start = pad_end - padded
    n_blocks = (n * TOP_K + N_EXPERTS * (SLOT_BLK - 1)) // SLOT_BLK
    n_used = pad_end[-1] // SLOT_BLK
    bidx = jnp.minimum(jnp.arange(n_blocks, dtype=I32), n_used - 1)
    bexp = jnp.sum((pad_end[None, :] <= (bidx * SLOT_BLK)[:, None]).astype(I32), axis=1)
    bexp = jnp.minimum(bexp, N_EXPERTS - 1)
    meta = jnp.concatenate([pad_start, pad_end - SLOT_BLK, counts]).astype(I32)

    tmd = 256
    split = lambda a, tm: a.reshape(TOP_K, n // tm, tm).transpose(1, 0, 2)
    xs = _dispatch(meta, split(idx_t, tmd), split(rank_t, tmd), hp, n_blocks * SLOT_BLK, tmd)
    y = _experts(bexp, bidx, n_used.reshape(1).astype(I32), xs, w_exp_gate, w_exp_up, w_exp_down)
    tmc = 128
    return _combine(meta, split(idx_t, tmc), split(rank_t, tmc), wt_t.T, xo, y, tmc)


def kernel(x, norm_mix_g, w_in, q_norm_g, k_norm_g, rpb, conv_dw_w, conv_dw_b, conv_ln_g, conv_ln_b, w_conv_out, w_attn_out, w_o, norm_ffn_g, w_router, router_bias, w_exp_gate, w_exp_up, w_exp_down, w_sh_gate, w_sh_up, w_sh_down):
    b, s, d = x.shape
    assert b == 1 and d == D_MODEL and s % (2 * ATT_BLK) == 0, x.shape
    xf = x.reshape(b * s, d)
    depth = norm_mix_g.shape[0]
    for l in range(depth):
        xf = _layer(xf, norm_mix_g[l], w_in[l], q_norm_g[l], k_norm_g[l], rpb[l], conv_dw_w[l], conv_dw_b[l],
                    conv_ln_g[l], conv_ln_b[l], w_conv_out[l], w_attn_out[l], w_o[l], norm_ffn_g[l],
                    w_router[l], router_bias[l], w_exp_gate[l], w_exp_up[l], w_exp_down[l], w_sh_gate[l],
                    w_sh_up[l], w_sh_down[l])
    return xf.reshape(b, s, d)
```

```python
import functools

import jax
import jax.numpy as jnp
from jax import lax
from jax.experimental import pallas as pl
from jax.experimental.pallas import tpu as pltpu
from jax.experimental.pallas import tpu_sc as plsc

D_MODEL = 1024
GRID_W = 64
D_CONV = 512
CONV_WIDTH = 31
CONV_PAD = CONV_WIDTH // 2
N_HEADS = 8
HEAD_DIM = 64
D_ATTN = N_HEADS * HEAD_DIM
WIN_H = 8
WIN_W = 16
N_EXPERTS = 64
TOP_K = 8
N_GROUPS = 8
GROUP_SIZE = N_EXPERTS // N_GROUPS
TOPK_GROUPS = 4
D_EXPERT = 256
D_SHARED = 256
ROUTE_SCALE = 2.5
EPS = 1e-6

NEG = -1e30
HALO = 16
ATT_ROWS = 4
ATT_BLK = ATT_ROWS * GRID_W
SLOT_BLK = 512
SC_CHUNK = 64
VMEM_LIMIT = 56 * 1024 * 1024

F32 = jnp.float32
BF16 = jnp.bfloat16
I32 = jnp.int32
U32 = jnp.uint32


def _cparams(*sem):
    return pltpu.CompilerParams(dimension_semantics=sem, vmem_limit_bytes=VMEM_LIMIT)


def _rms(x, g):
    return x * lax.rsqrt(jnp.mean(x * x, axis=-1, keepdims=True) + EPS) * g


def _pack_bf16_pair(a, b):
    def bf16_bits(v):
        return lax.bitcast_convert_type(v.astype(BF16).astype(F32), U32)
    return bf16_bits(a) | (bf16_bits(b) >> 16)


def _unpack_bf16_pair(w):
    a = lax.bitcast_convert_type(w & jnp.uint32(0xFFFF0000), F32)
    b = lax.bitcast_convert_type(w << 16, F32)
    return a, b


def _inproj_kernel(x_ref, g_ref, w_ref, bsum_ref, qg_ref, kg_ref, u_ref, q_ref, k_ref, v_ref):
    h = _rms(x_ref[...], g_ref[...]).astype(BF16)
    ag = jnp.dot(h, w_ref[:, 0:2 * D_CONV], preferred_element_type=F32)
    u_ref[...] = ag[:, :D_CONV] * jax.nn.sigmoid(ag[:, D_CONV:])
    bsum = bsum_ref[...]

    def head_norm(z, g):
        ss = jnp.dot((z * z).astype(BF16), bsum, preferred_element_type=F32)
        return z * lax.rsqrt(ss * (1.0 / HEAD_DIM) + EPS) * g

    c0 = 2 * D_CONV
    q = jnp.dot(h, w_ref[:, c0:c0 + D_ATTN], preferred_element_type=F32)
    q_ref[...] = (head_norm(q, qg_ref[...]) * (HEAD_DIM ** -0.5)).astype(BF16)
    k = jnp.dot(h, w_ref[:, c0 + D_ATTN:c0 + 2 * D_ATTN], preferred_element_type=F32)
    k_ref[...] = head_norm(k, kg_ref[...]).astype(BF16)
    v = jnp.dot(h, w_ref[:, c0 + 2 * D_ATTN:c0 + 3 * D_ATTN], preferred_element_type=F32)
    v_ref[...] = v.astype(BF16)


def _inproj(x, g, w, bsum, qg, kg, tm=512):
    n = x.shape[0]
    wc = w.shape[1]
    full = lambda i: (0, 0)
    row = lambda i: (i, 0)
    return pl.pallas_call(
        _inproj_kernel,
        grid=(n // tm,),
        in_specs=[pl.BlockSpec((tm, D_MODEL), row), pl.BlockSpec((1, D_MODEL), full),
                  pl.BlockSpec((D_MODEL, wc), full), pl.BlockSpec((D_ATTN, D_ATTN), full),
                  pl.BlockSpec((1, D_ATTN), full), pl.BlockSpec((1, D_ATTN), full)],
        out_specs=[pl.BlockSpec((tm, D_CONV), row), pl.BlockSpec((tm, D_ATTN), row),
                   pl.BlockSpec((tm, D_ATTN), row), pl.BlockSpec((tm, D_ATTN), row)],
        out_shape=[jax.ShapeDtypeStruct((n, D_CONV), F32), jax.ShapeDtypeStruct((n, D_ATTN), BF16),
                   jax.ShapeDtypeStruct((n, D_ATTN), BF16), jax.ShapeDtypeStruct((n, D_ATTN), BF16)],
        compiler_params=_cparams("parallel"),
        name="inproj",
    )(x, g, w, bsum, qg, kg)


def _conv_kernel(up_ref, uc_ref, un_ref, w_ref, b_ref, lg_ref, lb_ref, o_ref, ext_ref, *, tc, ch):
    i = pl.program_id(0)
    last = pl.num_programs(0) - 1
    ext_ref[0:HALO, :] = jnp.where(i > 0, up_ref[...], 0.0)
    ext_ref[HALO:HALO + tc, :] = uc_ref[...]
    ext_ref[HALO + tc:2 * HALO + tc, :] = jnp.where(i < last, un_ref[...], 0.0)
    bias = b_ref[...]
    lg = lg_ref[...]
    lb = lb_ref[...]
    for c in range(tc // ch):
        base = c * ch + HALO - CONV_PAD
        acc = jnp.broadcast_to(bias, (ch, D_CONV))
        for j in range(CONV_WIDTH):
            acc = acc + ext_ref[base + j:base + j + ch, :] * w_ref[j:j + 1, :]
        mu = jnp.mean(acc, axis=-1, keepdims=True)
        d = acc - mu
        var = jnp.mean(d * d, axis=-1, keepdims=True)
        y = d * lax.rsqrt(var + EPS) * lg + lb
        o_ref[c * ch:(c + 1) * ch, :] = (y * jax.nn.sigmoid(y)).astype(BF16)


def _conv(u, w, b, lg, lb, tc=256, ch=64):
    n = u.shape[0]
    hb = tc // HALO
    nh = n // HALO
    full = lambda i: (0, 0)
    return pl.pallas_call(
        functools.partial(_conv_kernel, tc=tc, ch=ch),
        grid=(n // tc,),
        in_specs=[pl.BlockSpec((HALO, D_CONV), lambda i: (jnp.maximum(i * hb - 1, 0), 0)),
                  pl.BlockSpec((tc, D_CONV), lambda i: (i, 0)),
                  pl.BlockSpec((HALO, D_CONV), lambda i: (jnp.minimum((i + 1) * hb, nh - 1), 0)),
                  pl.BlockSpec((CONV_WIDTH, D_CONV), full), pl.BlockSpec((1, D_CONV), full),
                  pl.BlockSpec((1, D_CONV), full), pl.BlockSpec((1, D_CONV), full)],
        out_specs=pl.BlockSpec((tc, D_CONV), lambda i: (i, 0)),
        out_shape=jax.ShapeDtypeStruct((n, D_CONV), BF16),
        scratch_shapes=[pltpu.VMEM((tc + 2 * HALO, D_CONV), F32)],
        compiler_params=_cparams("parallel"),
        name="conv",
    )(u, u, u, w, b, lg, lb)


def _attn_kernel(q_ref, kp_ref, kc_ref, kn_ref, vp_ref, vc_ref, vn_ref, tb_ref, o_ref, *, rows):
    i = pl.program_id(0)
    nkey = 3 * ATT_BLK
    lrow = lax.broadcasted_iota(I32, (1, nkey), 1) >> 6
    lane = lax.broadcasted_iota(I32, (1, 2 * HEAD_DIM), 1)
    masks = []
    for j in range(ATT_ROWS):
        r = i * ATT_ROWS + j
        rs = jnp.clip(r - WIN_H // 2, 0, rows - WIN_H)
        lo = rs - (i - 1) * ATT_ROWS
        masks.append(jnp.where((lrow >= lo) & (lrow < lo + WIN_H), 0.0, NEG))
    dn = (((1,), (1,)), ((), ()))
    for p in range(N_HEADS // 2):
        sl = slice(2 * HEAD_DIM * p, 2 * HEAD_DIM * (p + 1))
        q2 = q_ref[:, sl]
        ks = (kp_ref[:, sl], kc_ref[:, sl], kn_ref[:, sl])
        vs = (vp_ref[:, sl], vc_ref[:, sl], vn_ref[:, sl])
        out_pair = None
        for half in range(2):
            h = 2 * p + half
            in_half = (lane >= HEAD_DIM * half) & (lane < HEAD_DIM * (half + 1))
            qm = jnp.where(in_half, q2, jnp.zeros_like(q2))
            s = jnp.concatenate([lax.dot_general(qm, kk, dn, preferred_element_type=F32) for kk in ks], axis=1)
            parts = []
            for j in range(ATT_ROWS):
                bias = jnp.concatenate([tb_ref[h, 2 * m - j + 3] for m in range(nkey // (2 * GRID_W))], axis=1)
                parts.append(s[j * GRID_W:(j + 1) * GRID_W, :] + bias + masks[j])
            s = jnp.concatenate(parts, axis=0)
            mx = jnp.max(s, axis=-1, keepdims=True)
            e = jnp.exp(s - mx)
            den = jnp.sum(e, axis=-1, keepdims=True)
            pb = e.astype(BF16)
            o = sum(jnp.dot(pb[:, ATT_BLK * t:ATT_BLK * (t + 1)], vs[t], preferred_element_type=F32)
                    for t in range(3))
            o = o / den
            out_pair = o if out_pair is None else jnp.where(in_half, o, out_pair)
        o_ref[:, sl] = out_pair.astype(BF16)


def _attn(q, k, v, tb):
    n = q.shape[0]
    rows = n // GRID_W
    nb = n // ATT_BLK
    cur = lambda i: (i, 0)
    prev = lambda i: (jnp.maximum(i - 1, 0), 0)
    nxt = lambda i: (jnp.minimum(i + 1, nb - 1), 0)
    blk = (ATT_BLK, D_ATTN)
    return pl.pallas_call(
        functools.partial(_attn_kernel, rows=rows),
        grid=(nb,),
        in_specs=[pl.BlockSpec(blk, cur), pl.BlockSpec(blk, prev), pl.BlockSpec(blk, cur), pl.BlockSpec(blk, nxt),
                  pl.BlockSpec(blk, prev), pl.BlockSpec(blk, cur), pl.BlockSpec(blk, nxt),
                  pl.BlockSpec(tb.shape, lambda i: (0, 0, 0, 0))],
        out_specs=pl.BlockSpec(blk, cur),
        out_shape=jax.ShapeDtypeStruct((n, D_ATTN), BF16),
        compiler_params=_cparams("parallel"),
        name="attn",
    )(q, k, k, k, v, v, v, tb)


def _bias_table(rpb):
    cols = jnp.arange(GRID_W)
    start = jnp.clip(cols - WIN_W // 2, 0, GRID_W - WIN_W)
    kc = cols[None, :]
    inwin = (kc >= start[:, None]) & (kc < start[:, None] + WIN_W)
    off = GRID_W - WIN_W
    padded = jnp.pad(rpb, ((0, 0), (0, 0), (off, off)))
    rel = jnp.stack([padded[:, :, GRID_W - 1 - c:2 * GRID_W - 1 - c] for c in range(GRID_W)], axis=2)
    t = jnp.where(inwin[None, None], rel, NEG)
    t = jnp.concatenate([t, jnp.full_like(t[:, :1], NEG)], axis=1)
    t_next = jnp.concatenate([t[:, 1:], t[:, -1:]], axis=1)
    return jnp.concatenate([t, t_next], axis=-1).astype(F32)


def _mix_route_kernel(x_ref, uc_ref, at_ref, g1_ref, wg_ref, wco_ref, wao_ref, wo_ref, g2_ref,
                      wrh_ref, wrl_ref, rb_ref, tri_ref, wsgu_ref, wsd_ref,
                      xo_ref, hp_ref, idx_ref, rank_ref, wt_ref, cnt_ref, carry_ref, *, tm):
    i = pl.program_id(0)

    @pl.when(i == 0)
    def _():
        carry_ref[...] = jnp.zeros_like(carry_ref)

    x = x_ref[...]
    h = _rms(x, g1_ref[...]).astype(BF16)
    gates = jax.nn.sigmoid(jnp.dot(h, wg_ref[...], preferred_element_type=F32))
    y_conv = jnp.dot(uc_ref[...], wco_ref[...], preferred_element_type=F32)
    y_attn = jnp.dot(at_ref[...], wao_ref[...], preferred_element_type=F32)
    merged = gates[:, :D_MODEL] * y_conv + gates[:, D_MODEL:] * y_attn
    x1 = x + jnp.dot(merged.astype(BF16), wo_ref[...], preferred_element_type=F32)
    h2 = _rms(x1, g2_ref[...])
    h2b = h2.astype(BF16)
    hp_ref[...] = _pack_bf16_pair(h2[:, :D_MODEL // 2], h2[:, D_MODEL // 2:])

    gu = jnp.dot(h2b, wsgu_ref[...], preferred_element_type=F32)
    mid = (jax.nn.silu(gu[:, :D_SHARED]) * gu[:, D_SHARED:]).astype(BF16)
    xo_ref[...] = x1 + jnp.dot(mid, wsd_ref[...], preferred_element_type=F32)

    h2l = (h2 - h2b.astype(F32)).astype(BF16)
    dn = (((1,), (1,)), ((), ()))
    logits = (lax.dot_general(wrh_ref[...], h2b, dn, preferred_element_type=F32)
              + lax.dot_general(wrl_ref[...], h2b, dn, preferred_element_type=F32)
              + lax.dot_general(wrh_ref[...], h2l, dn, preferred_element_type=F32))
    scores = jax.nn.sigmoid(logits)
    biased = scores + rb_ref[...]

    sub = lax.broadcasted_iota(I32, (GROUP_SIZE, tm), 0).astype(F32)
    groups, gscore = [], []
    for g in range(N_GROUPS):
        bg = biased[g * GROUP_SIZE:(g + 1) * GROUP_SIZE, :]
        m1 = jnp.max(bg, axis=0, keepdims=True)
        first = jnp.min(jnp.where(bg == m1, sub, float(GROUP_SIZE)), axis=0, keepdims=True)
        m2 = jnp.max(jnp.where(sub == first, -jnp.inf, bg), axis=0, keepdims=True)
        groups.append(bg)
        gscore.append(m1 + m2)
    masked = []
    for g in range(N_GROUPS):
        beaten = jnp.zeros((1, tm), F32)
        for o in range(N_GROUPS):
            if o == g:
                continue
            wins = (gscore[o] >= gscore[g]) if o < g else (gscore[o] > gscore[g])
            beaten = beaten + jnp.where(wins, 1.0, 0.0)
        masked.append(jnp.where(beaten < TOPK_GROUPS, groups[g], -jnp.inf))
    cur = jnp.concatenate(masked, axis=0)

    eid = lax.broadcasted_iota(I32, (N_EXPERTS, tm), 0).astype(F32)
    idx_rows, sc_rows = [], []
    sel = jnp.zeros((N_EXPERTS, tm), F32)
    for _ in range(TOP_K):
        m = jnp.max(cur, axis=0, keepdims=True)
        idx = jnp.min(jnp.where(cur == m, eid, float(N_EXPERTS)), axis=0, keepdims=True)
        hit = eid == idx
        sel = jnp.where(hit, 1.0, sel)
        cur = jnp.where(hit, -jnp.inf, cur)
        idx_rows.append(idx)
        sc_rows.append(jnp.sum(jnp.where(hit, scores, 0.0), axis=0, keepdims=True))
    wsum = sc_rows[0]
    for k in range(1, TOP_K):
        wsum = wsum + sc_rows[k]

    prefix = jnp.dot(sel.astype(BF16), tri_ref[...], preferred_element_type=F32)
    rank = carry_ref[:, 0:1] + prefix
    rank_rows = [jnp.sum(jnp.where(eid == idx_rows[k], rank, 0.0), axis=0, keepdims=True) for k in range(TOP_K)]
    carry_ref[...] = carry_ref[...] + jnp.sum(sel, axis=1, keepdims=True)
    cnt_ref[...] = carry_ref[...]

    idx_ref[...] = jnp.concatenate(idx_rows, axis=0).astype(I32)
    rank_ref[...] = jnp.concatenate(rank_rows, axis=0).astype(I32)
    wt_ref[...] = jnp.concatenate([s / wsum * ROUTE_SCALE for s in sc_rows], axis=0)


def _mix_route(x, uc, at, g1, wg, wco, wao, wo, g2, wrh, wrl, rb, wsgu, wsd, tm=256):
    n = x.shape[0]
    tri = (jnp.arange(tm)[:, None] < jnp.arange(tm)[None, :]).astype(BF16)
    row = lambda i: (i, 0)
    col = lambda i: (0, i)
    full = lambda i: (0, 0)
    ins = [x, uc, at, g1, wg, wco, wao, wo, g2, wrh, wrl, rb, tri, wsgu, wsd]
    in_specs = [pl.BlockSpec((tm, D_MODEL), row), pl.BlockSpec((tm, D_CONV), row), pl.BlockSpec((tm, D_ATTN), row)]
    in_specs += [pl.BlockSpec(a.shape, full) for a in ins[3:]]
    return pl.pallas_call(
        functools.partial(_mix_route_kernel, tm=tm),
        grid=(n // tm,),
        in_specs=in_specs,
        out_specs=[pl.BlockSpec((tm, D_MODEL), row), pl.BlockSpec((tm, D_MODEL // 2), row),
                   pl.BlockSpec((TOP_K, tm), col), pl.BlockSpec((TOP_K, tm), col), pl.BlockSpec((TOP_K, tm), col),
                   pl.BlockSpec((N_EXPERTS, 128), full)],
        out_shape=[jax.ShapeDtypeStruct((n, D_MODEL), F32), jax.ShapeDtypeStruct((n, D_MODEL // 2), U32),
                   jax.ShapeDtypeStruct((TOP_K, n), I32), jax.ShapeDtypeStruct((TOP_K, n), I32),
                   jax.ShapeDtypeStruct((TOP_K, n), F32), jax.ShapeDtypeStruct((N_EXPERTS, 128), F32)],
        scratch_shapes=[pltpu.VMEM((N_EXPERTS, 128), F32)],
        compiler_params=_cparams("arbitrary"),
        name="mix_route",
    )(*ins)


def _sc_workers():
    info = plsc.get_sparse_core_info()
    return info.num_cores, info.num_subcores


def _sc_mesh():
    return plsc.VectorSubcoreMesh(core_axis_name="c", subcore_axis_name="s")


def _worker_id(n_cores):
    return lax.axis_index("s") * n_cores + lax.axis_index("c")


def _dispatch(slots3, hp, n_slots):
    n_chunks = slots3.shape[0]
    n_cores, n_sub = _sc_workers()
    per_worker = n_chunks // (n_cores * n_sub)
    assert per_worker * n_cores * n_sub == n_chunks

    @functools.partial(
        pl.kernel, mesh=_sc_mesh(),
        out_type=jax.ShapeDtypeStruct((n_slots, D_MODEL // 2), U32),
        scratch_types=[pltpu.VMEM((TOP_K, SC_CHUNK), I32), pltpu.VMEM((SC_CHUNK, D_MODEL // 2), U32),
                       pltpu.SemaphoreType.DMA],
        name="dispatch")
    def run(slots_hbm, hp_hbm, xs_hbm, idx_v, rows_v, sem):
        first = _worker_id(n_cores) * per_worker

        @pl.loop(0, per_worker)
        def _(j):
            c = first + j
            pltpu.sync_copy(slots_hbm.at[c], idx_v)
            pltpu.sync_copy(hp_hbm.at[pl.ds(c * SC_CHUNK, SC_CHUNK)], rows_v)
            copies = [pltpu.async_copy(rows_v, xs_hbm.at[idx_v.at[k]], sem) for k in range(TOP_K)]
            for cp in copies:
                cp.wait()

    return run(slots3, hp)


def _experts_kernel(bexp_ref, bidx_ref, bval_ref, nused_ref, xs_ref, wg_ref, wu_ref, wd_ref, y_ref, wgu_s, wd_s):
    b = pl.program_id(0)

    @pl.when(b < nused_ref[0])
    def _():
        changed = (b == 0) | (bexp_ref[b] != bexp_ref[jnp.maximum(b - 1, 0)])

        @pl.when(changed)
        def _():
            wgu_s[:, :D_EXPERT] = wg_ref[0].astype(BF16)
            wgu_s[:, D_EXPERT:] = wu_ref[0].astype(BF16)
            wd_s[...] = wd_ref[0].astype(BF16)

        srow = lax.broadcasted_iota(I32, (SLOT_BLK, 1), 0)
        xw = jnp.where(srow < bval_ref[b], xs_ref[...], jnp.uint32(0))
        xa, xb = _unpack_bf16_pair(xw)
        x = jnp.concatenate([xa.astype(BF16), xb.astype(BF16)], axis=1)
        gu = jnp.dot(x, wgu_s[...], preferred_element_type=F32)
        mid = (jax.nn.silu(gu[:, :D_EXPERT]) * gu[:, D_EXPERT:]).astype(BF16)
        y = jnp.dot(mid, wd_s[...], preferred_element_type=F32)
        y_ref[...] = _pack_bf16_pair(y[:, :D_MODEL // 2], y[:, D_MODEL // 2:])


def _experts(bexp, bidx, bval, nused, xs, wg, wu, wd):
    n_blocks = bexp.shape[0]
    xmap = lambda b, be, bi, bv, nu: (bi[b], 0)
    wmap = lambda b, be, bi, bv, nu: (be[b], 0, 0)
    grid_spec = pltpu.PrefetchScalarGridSpec(
        num_scalar_prefetch=4,
        grid=(n_blocks,),
        in_specs=[pl.BlockSpec((SLOT_BLK, D_MODEL // 2), xmap),
                  pl.BlockSpec((1, D_MODEL, D_EXPERT), wmap),
                  pl.BlockSpec((1, D_MODEL, D_EXPERT), wmap),
                  pl.BlockSpec((1, D_EXPERT, D_MODEL), wmap)],
        out_specs=pl.BlockSpec((SLOT_BLK, D_MODEL // 2), xmap),
        scratch_shapes=[pltpu.VMEM((D_MODEL, 2 * D_EXPERT), BF16), pltpu.VMEM((D_EXPERT, D_MODEL), BF16)],
    )
    return pl.pallas_call(
        _experts_kernel,
        grid_spec=grid_spec,
        out_shape=jax.ShapeDtypeStruct(xs.shape, U32),
        compiler_params=_cparams("arbitrary"),
        name="experts",
    )(bexp, bidx, bval, nused, xs, wg, wu, wd)


def _gather(slots3, y, n):
    n_chunks = slots3.shape[0]
    n_cores, n_sub = _sc_workers()
    per_worker = n_chunks // (n_cores * n_sub)
    assert per_worker * n_cores * n_sub == n_chunks

    @functools.partial(
        pl.kernel, mesh=_sc_mesh(),
        out_type=jax.ShapeDtypeStruct((TOP_K, n, D_MODEL // 2), U32),
        scratch_types=[pltpu.VMEM((TOP_K, SC_CHUNK), I32),
                       pltpu.VMEM((SC_CHUNK, D_MODEL // 2), U32), pltpu.VMEM((SC_CHUNK, D_MODEL // 2), U32),
                       pltpu.SemaphoreType.DMA, pltpu.SemaphoreType.DMA, pltpu.SemaphoreType.DMA],
        name="gather")
    def run(slots_hbm, y_hbm, g_hbm, idx_v, buf0, buf1, gsem, wsem0, wsem1):
        first = _worker_id(n_cores) * per_worker
        bufs = (buf0, buf1)
        wsems = (wsem0, wsem1)

        @pl.loop(0, per_worker)
        def _(j):
            c = first + j
            pltpu.sync_copy(slots_hbm.at[c], idx_v)
            gathers = [None] * TOP_K
            writes = [None] * TOP_K
            gathers[0] = pltpu.async_copy(y_hbm.at[idx_v.at[0]], bufs[0], gsem)
            for k in range(TOP_K):
                gathers[k].wait()
                if k >= 1:
                    writes[k - 1].wait()
                if k + 1 < TOP_K:
                    gathers[k + 1] = pltpu.async_copy(y_hbm.at[idx_v.at[k + 1]], bufs[(k + 1) % 2], gsem)
                writes[k] = pltpu.async_copy(bufs[k % 2], g_hbm.at[k, pl.ds(c * SC_CHUNK, SC_CHUNK)],
                                             wsems[k % 2])
            writes[TOP_K - 1].wait()

    return run(slots3, y)


def _combine_kernel(wt_ref, xo_ref, g_ref, o_ref):
    half = D_MODEL // 2
    acc_a = xo_ref[:, :half]
    acc_b = xo_ref[:, half:]
    for k in range(TOP_K):
        ya, yb = _unpack_bf16_pair(g_ref[k])
        w = wt_ref[:, k:k + 1]
        acc_a = acc_a + w * ya
        acc_b = acc_b + w * yb
    o_ref[:, :half] = acc_a
    o_ref[:, half:] = acc_b


def _combine(wt, xo, g, tm=256):
    n = xo.shape[0]
    return pl.pallas_call(
        _combine_kernel,
        grid=(n // tm,),
        in_specs=[pl.BlockSpec((tm, TOP_K), lambda i: (i, 0)),
                  pl.BlockSpec((tm, D_MODEL), lambda i: (i, 0)),
                  pl.BlockSpec((TOP_K, tm, D_MODEL // 2), lambda i: (0, i, 0))],
        out_specs=pl.BlockSpec((tm, D_MODEL), lambda i: (i, 0)),
        out_shape=jax.ShapeDtypeStruct((n, D_MODEL), F32),
        compiler_params=_cparams("parallel"),
        name="combine",
    )(wt, xo, g)


def _layer(x, norm_mix_g, w_in, q_norm_g, k_norm_g, rpb, conv_dw_w, conv_dw_b, conv_ln_g, conv_ln_b,
           w_conv_out, w_attn_out, w_o, norm_ffn_g, w_router, router_bias, w_exp_gate, w_exp_up,
           w_exp_down, w_sh_gate, w_sh_up, w_sh_down):
    n = x.shape[0]
    c_qkv = 2 * D_CONV + 3 * D_ATTN
    w_in_b = w_in.astype(BF16)
    head_of = jnp.arange(D_ATTN) // HEAD_DIM
    bsum = (head_of[:, None] == head_of[None, :]).astype(BF16)
    row = lambda v: v.reshape(1, -1).astype(F32)

    u, q, k, v = _inproj(x, row(norm_mix_g), w_in_b[:, :c_qkv], bsum,
                         row(jnp.tile(q_norm_g, N_HEADS)), row(jnp.tile(k_norm_g, N_HEADS)))
    uc = _conv(u, conv_dw_w.reshape(CONV_WIDTH, D_CONV), row(conv_dw_b), row(conv_ln_g), row(conv_ln_b))
    at = _attn(q, k, v, _bias_table(rpb))

    wr_t = w_router.T
    wr_hi = wr_t.astype(BF16)
    wr_lo = (wr_t - wr_hi.astype(F32)).astype(BF16)
    wsgu = jnp.concatenate([w_sh_gate, w_sh_up], axis=1).astype(BF16)
    xo, hp, idx_t, rank_t, wt_t, cnt = _mix_route(
        x, uc, at, row(norm_mix_g), w_in_b[:, c_qkv:], w_conv_out.astype(BF16), w_attn_out.astype(BF16),
        w_o.astype(BF16), row(norm_ffn_g), wr_hi, wr_lo, router_bias.reshape(N_EXPERTS, 1).astype(F32),
        wsgu, w_sh_down.astype(BF16))

    counts = cnt[:, 0].astype(I32)
    padded = (counts + SLOT_BLK - 1) // SLOT_BLK * SLOT_BLK
    pad_end = jnp.cumsum(padded)
    pad_start = pad_end - padded
    n_blocks = (n * TOP_K + N_EXPERTS * (SLOT_BLK - 1)) // SLOT_BLK
    n_used = pad_end[-1] // SLOT_BLK
    bidx = jnp.minimum(jnp.arange(n_blocks, dtype=I32), n_used - 1)
    is_done = (pad_end[None, :] <= (bidx * SLOT_BLK)[:, None]).astype(I32)
    bexp = jnp.minimum(jnp.sum(is_done, axis=1), N_EXPERTS - 1)
    owner = (bexp[:, None] == jnp.arange(N_EXPERTS)[None, :]).astype(I32)
    bval = jnp.sum(owner * (pad_start + counts)[None, :], axis=1) - bidx * SLOT_BLK
    bval = jnp.clip(bval, 0, SLOT_BLK).astype(I32)

    onehot = (idx_t[:, :, None] == jnp.arange(N_EXPERTS)[None, None, :]).astype(I32)
    slots = jnp.sum(onehot * pad_start[None, None, :], axis=2) + rank_t
    slots3 = slots.reshape(TOP_K, n // SC_CHUNK, SC_CHUNK).transpose(1, 0, 2)

    xs = _dispatch(slots3, hp, n_blocks * SLOT_BLK)
    y = _experts(bexp, bidx, bval, n_used.reshape(1).astype(I32), xs, w_exp_gate, w_exp_up, w_exp_down)
    return _combine(wt_t.T, xo, _gather(slots3, y, n))


def kernel(x, norm_mix_g, w_in, q_norm_g, k_norm_g, rpb, conv_dw_w, conv_dw_b, conv_ln_g, conv_ln_b, w_conv_out, w_attn_out, w_o, norm_ffn_g, w_router, router_bias, w_exp_gate, w_exp_up, w_exp_down, w_sh_gate, w_sh_up, w_sh_down):
    b, s, d = x.shape
    assert b == 1 and d == D_MODEL and s % (2 * ATT_BLK) == 0, x.shape
    xf = x.reshape(b * s, d)
    depth = norm_mix_g.shape[0]
    for l in range(depth):
        xf = _layer(xf, norm_mix_g[l], w_in[l], q_norm_g[l], k_norm_g[l], rpb[l], conv_dw_w[l], conv_dw_b[l],
                    conv_ln_g[l], conv_ln_b[l], w_conv_out[l], w_attn_out[l], w_o[l], norm_ffn_g[l],
                    w_router[l], router_bias[l], w_exp_gate[l], w_exp_up[l], w_exp_down[l], w_sh_gate[l],
                    w_sh_up[l], w_sh_down[l])
    return xf.reshape(b, s, d)
```

```python
import functools

import jax
import jax.numpy as jnp
from jax import lax
from jax.experimental import pallas as pl
from jax.experimental.pallas import tpu as pltpu
from jax.experimental.pallas import tpu_sc as plsc

D_MODEL = 1024
GRID_W = 64
D_CONV = 512
CONV_WIDTH = 31
CONV_PAD = CONV_WIDTH // 2
N_HEADS = 8
HEAD_DIM = 64
D_ATTN = N_HEADS * HEAD_DIM
WIN_H = 8
WIN_W = 16
N_EXPERTS = 64
TOP_K = 8
N_GROUPS = 8
GROUP_SIZE = N_EXPERTS // N_GROUPS
TOPK_GROUPS = 4
D_EXPERT = 256
D_SHARED = 256
ROUTE_SCALE = 2.5
EPS = 1e-6
LOG2E = 1.4426950408889634

SUBLANES = 8
NEG = -1e30
HALO = 16
ATT_ROWS = 4
ATT_BLK = ATT_ROWS * GRID_W
SLOT_BLK = 512
SC_CHUNK = 64
VMEM_LIMIT = 56 * 1024 * 1024

F32 = jnp.float32
BF16 = jnp.bfloat16
I32 = jnp.int32
U32 = jnp.uint32


def _cparams(*sem):
    return pltpu.CompilerParams(dimension_semantics=sem, vmem_limit_bytes=VMEM_LIMIT)


def _rms(x, g):
    return x * lax.rsqrt(jnp.mean(x * x, axis=-1, keepdims=True) + EPS) * g


def _pack_bf16_pair(a, b):
    def bf16_bits(v):
        return lax.bitcast_convert_type(v.astype(BF16).astype(F32), U32)
    return bf16_bits(a) | (bf16_bits(b) >> 16)


def _unpack_bf16_pair(w):
    a = lax.bitcast_convert_type(w & jnp.uint32(0xFFFF0000), F32)
    b = lax.bitcast_convert_type(w << 16, F32)
    return a, b


def _inproj_kernel(x_ref, g_ref, w_ref, bsum_ref, qg_ref, kg_ref, u_ref, q_ref, k_ref, v_ref):
    h = _rms(x_ref[...], g_ref[...]).astype(BF16)
    ag = jnp.dot(h, w_ref[:, 0:2 * D_CONV], preferred_element_type=F32)
    u_ref[...] = ag[:, :D_CONV] * jax.nn.sigmoid(ag[:, D_CONV:])
    bsum = bsum_ref[...]

    def head_norm(z, g):
        ss = jnp.dot((z * z).astype(BF16), bsum, preferred_element_type=F32)
        return z * lax.rsqrt(ss * (1.0 / HEAD_DIM) + EPS) * g

    c0 = 2 * D_CONV
    q = jnp.dot(h, w_ref[:, c0:c0 + D_ATTN], preferred_element_type=F32)
    q_ref[...] = (head_norm(q, qg_ref[...]) * (HEAD_DIM ** -0.5 * LOG2E)).astype(BF16)
    k = jnp.dot(h, w_ref[:, c0 + D_ATTN:c0 + 2 * D_ATTN], preferred_element_type=F32)
    k_ref[...] = head_norm(k, kg_ref[...]).astype(BF16)
    v = jnp.dot(h, w_ref[:, c0 + 2 * D_ATTN:c0 + 3 * D_ATTN], preferred_element_type=F32)
    v_ref[...] = v.astype(BF16)


def _inproj(x, g, w, bsum, qg, kg, tm=512):
    n = x.shape[0]
    wc = w.shape[1]
    full = lambda i: (0, 0)
    row = lambda i: (i, 0)
    return pl.pallas_call(
        _inproj_kernel,
        grid=(n // tm,),
        in_specs=[pl.BlockSpec((tm, D_MODEL), row), pl.BlockSpec((1, D_MODEL), full),
                  pl.BlockSpec((D_MODEL, wc), full), pl.BlockSpec((D_ATTN, D_ATTN), full),
                  pl.BlockSpec((1, D_ATTN), full), pl.BlockSpec((1, D_ATTN), full)],
        out_specs=[pl.BlockSpec((tm, D_CONV), row), pl.BlockSpec((tm, D_ATTN), row),
                   pl.BlockSpec((tm, D_ATTN), row), pl.BlockSpec((tm, D_ATTN), row)],
        out_shape=[jax.ShapeDtypeStruct((n, D_CONV), F32), jax.ShapeDtypeStruct((n, D_ATTN), BF16),
                   jax.ShapeDtypeStruct((n, D_ATTN), BF16), jax.ShapeDtypeStruct((n, D_ATTN), BF16)],
        compiler_params=_cparams("parallel"),
        name="inproj",
    )(x, g, w, bsum, qg, kg)


def _conv_kernel(up_ref, uc_ref, un_ref, w_ref, b_ref, lg_ref, lb_ref, o_ref, ext_ref, sh_ref, *, tc, ch):
    i = pl.program_id(0)
    last = pl.num_programs(0) - 1
    ext_ref[0:HALO, :] = jnp.where(i > 0, up_ref[...], 0.0)
    ext_ref[HALO:HALO + tc, :] = uc_ref[...]
    ext_ref[HALO + tc:2 * HALO + tc, :] = jnp.where(i < last, un_ref[...], 0.0)
    span = sh_ref.shape[1]
    for b in range(SUBLANES):
        sh_ref[b] = ext_ref[b:b + span, :]
    bias = b_ref[...]
    lg = lg_ref[...]
    lb = lb_ref[...]
    for c in range(tc // ch):
        acc = jnp.broadcast_to(bias, (ch, D_CONV))
        for j in range(CONV_WIDTH):
            shift = HALO - CONV_PAD + j
            row0 = c * ch + shift // SUBLANES * SUBLANES
            acc = acc + sh_ref[shift % SUBLANES, row0:row0 + ch, :] * w_ref[j:j + 1, :]
        mu = jnp.mean(acc, axis=-1, keepdims=True)
        d = acc - mu
        var = jnp.mean(d * d, axis=-1, keepdims=True)
        y = d * lax.rsqrt(var + EPS) * lg + lb
        o_ref[c * ch:(c + 1) * ch, :] = (y * jax.nn.sigmoid(y)).astype(BF16)


def _conv(u, w, b, lg, lb, tc=256, ch=64):
    n = u.shape[0]
    hb = tc // HALO
    nh = n // HALO
    full = lambda i: (0, 0)
    return pl.pallas_call(
        functools.partial(_conv_kernel, tc=tc, ch=ch),
        grid=(n // tc,),
        in_specs=[pl.BlockSpec((HALO, D_CONV), lambda i: (jnp.maximum(i * hb - 1, 0), 0)),
                  pl.BlockSpec((tc, D_CONV), lambda i: (i, 0)),
                  pl.BlockSpec((HALO, D_CONV), lambda i: (jnp.minimum((i + 1) * hb, nh - 1), 0)),
                  pl.BlockSpec((CONV_WIDTH, D_CONV), full), pl.BlockSpec((1, D_CONV), full),
                  pl.BlockSpec((1, D_CONV), full), pl.BlockSpec((1, D_CONV), full)],
        out_specs=pl.BlockSpec((tc, D_CONV), lambda i: (i, 0)),
        out_shape=jax.ShapeDtypeStruct((n, D_CONV), BF16),
        scratch_shapes=[pltpu.VMEM((tc + 2 * HALO, D_CONV), F32),
                        pltpu.VMEM((SUBLANES, tc + 2 * HALO - SUBLANES, D_CONV), F32)],
        compiler_params=_cparams("parallel"),
        name="conv",
    )(u, u, u, w, b, lg, lb)


def _attn_kernel(q_ref, kp_ref, kc_ref, kn_ref, vp_ref, vc_ref, vn_ref, tb_ref, o_ref, *, rows):
    i = pl.program_id(0)
    nkey = 3 * ATT_BLK
    lrow = lax.broadcasted_iota(I32, (1, nkey), 1) >> 6
    lane = lax.broadcasted_iota(I32, (1, 2 * HEAD_DIM), 1)
    masks = []
    for j in range(ATT_ROWS):
        r = i * ATT_ROWS + j
        rs = jnp.clip(r - WIN_H // 2, 0, rows - WIN_H)
        lo = rs - (i - 1) * ATT_ROWS
        masks.append(jnp.where((lrow >= lo) & (lrow < lo + WIN_H), 0.0, NEG))
    dn = (((1,), (1,)), ((), ()))
    for p in range(N_HEADS // 2):
        sl = slice(2 * HEAD_DIM * p, 2 * HEAD_DIM * (p + 1))
        q2 = q_ref[:, sl]
        ks = (kp_ref[:, sl], kc_ref[:, sl], kn_ref[:, sl])
        vs = (vp_ref[:, sl], vc_ref[:, sl], vn_ref[:, sl])
        out_pair = None
        for half in range(2):
            h = 2 * p + half
            in_half = (lane >= HEAD_DIM * half) & (lane < HEAD_DIM * (half + 1))
            qm = jnp.where(in_half, q2, jnp.zeros_like(q2))
            s = jnp.concatenate([lax.dot_general(qm, kk, dn, preferred_element_type=F32) for kk in ks], axis=1)
            parts = []
            for j in range(ATT_ROWS):
                bias = jnp.concatenate([tb_ref[h, 2 * m - j + 3] for m in range(nkey // (2 * GRID_W))], axis=1)
                parts.append(s[j * GRID_W:(j + 1) * GRID_W, :] + bias + masks[j])
            s = jnp.concatenate(parts, axis=0)
            mx = jnp.max(s, axis=-1, keepdims=True)
            e = jnp.exp2(s - mx)
            den = jnp.sum(e, axis=-1, keepdims=True)
            pb = e.astype(BF16)
            o = sum(jnp.dot(pb[:, ATT_BLK * t:ATT_BLK * (t + 1)], vs[t], preferred_element_type=F32)
                    for t in range(3))
            o = o / den
            out_pair = o if out_pair is None else jnp.where(in_half, o, out_pair)
        o_ref[:, sl] = out_pair.astype(BF16)


def _attn(q, k, v, tb):
    n = q.shape[0]
    rows = n // GRID_W
    nb = n // ATT_BLK
    cur = lambda i: (i, 0)
    prev = lambda i: (jnp.maximum(i - 1, 0), 0)
    nxt = lambda i: (jnp.minimum(i + 1, nb - 1), 0)
    blk = (ATT_BLK, D_ATTN)
    return pl.pallas_call(
        functools.partial(_attn_kernel, rows=rows),
        grid=(nb,),
        in_specs=[pl.BlockSpec(blk, cur), pl.BlockSpec(blk, prev), pl.BlockSpec(blk, cur), pl.BlockSpec(blk, nxt),
                  pl.BlockSpec(blk, prev), pl.BlockSpec(blk, cur), pl.BlockSpec(blk, nxt),
                  pl.BlockSpec(tb.shape, lambda i: (0, 0, 0, 0))],
        out_specs=pl.BlockSpec(blk, cur),
        out_shape=jax.ShapeDtypeStruct((n, D_ATTN), BF16),
        compiler_params=_cparams("parallel"),
        name="attn",
    )(q, k, k, k, v, v, v, tb)


def _bias_table(rpb):
    cols = jnp.arange(GRID_W)
    start = jnp.clip(cols - WIN_W // 2, 0, GRID_W - WIN_W)
    kc = cols[None, :]
    inwin = (kc >= start[:, None]) & (kc < start[:, None] + WIN_W)
    off = GRID_W - WIN_W
    padded = jnp.pad(rpb, ((0, 0), (0, 0), (off, off)))
    rel = jnp.stack([padded[:, :, GRID_W - 1 - c:2 * GRID_W - 1 - c] for c in range(GRID_W)], axis=2)
    t = jnp.where(inwin[None, None], rel * LOG2E, NEG)
    t = jnp.concatenate([t, jnp.full_like(t[:, :1], NEG)], axis=1)
    t_next = jnp.concatenate([t[:, 1:], t[:, -1:]], axis=1)
    return jnp.concatenate([t, t_next], axis=-1).astype(F32)


def _mix_route_kernel(x_ref, uc_ref, at_ref, g1_ref, wg_ref, wco_ref, wao_ref, wo_ref, g2_ref,
                      wrh_ref, wrl_ref, rb_ref, tri_ref, wsgu_ref, wsd_ref,
                      xo_ref, hp_ref, idx_ref, rank_ref, wt_ref, cnt_ref, carry_ref, *, tm):
    i = pl.program_id(0)

    @pl.when(i == 0)
    def _():
        carry_ref[...] = jnp.zeros_like(carry_ref)

    x = x_ref[...]
    h = _rms(x, g1_ref[...]).astype(BF16)
    gates = jax.nn.sigmoid(jnp.dot(h, wg_ref[...], preferred_element_type=F32))
    y_conv = jnp.dot(uc_ref[...], wco_ref[...], preferred_element_type=F32)
    y_attn = jnp.dot(at_ref[...], wao_ref[...], preferred_element_type=F32)
    merged = gates[:, :D_MODEL] * y_conv + gates[:, D_MODEL:] * y_attn
    x1 = x + jnp.dot(merged.astype(BF16), wo_ref[...], preferred_element_type=F32)
    h2 = _rms(x1, g2_ref[...])
    h2b = h2.astype(BF16)
    hp_ref[...] = _pack_bf16_pair(h2[:, :D_MODEL // 2], h2[:, D_MODEL // 2:])

    gu = jnp.dot(h2b, wsgu_ref[...], preferred_element_type=F32)
    mid = (jax.nn.silu(gu[:, :D_SHARED]) * gu[:, D_SHARED:]).astype(BF16)
    xo_ref[...] = x1 + jnp.dot(mid, wsd_ref[...], preferred_element_type=F32)

    h2l = (h2 - h2b.astype(F32)).astype(BF16)
    dn = (((1,), (1,)), ((), ()))
    logits = (lax.dot_general(wrh_ref[...], h2b, dn, preferred_element_type=F32)
              + lax.dot_general(wrl_ref[...], h2b, dn, preferred_element_type=F32)
              + lax.dot_general(wrh_ref[...], h2l, dn, preferred_element_type=F32))
    scores = jax.nn.sigmoid(logits)
    biased = scores + rb_ref[...]

    sub = lax.broadcasted_iota(I32, (GROUP_SIZE, tm), 0).astype(F32)
    groups, gscore = [], []
    for g in range(N_GROUPS):
        bg = biased[g * GROUP_SIZE:(g + 1) * GROUP_SIZE, :]
        m1 = jnp.max(bg, axis=0, keepdims=True)
        first = jnp.min(jnp.where(bg == m1, sub, float(GROUP_SIZE)), axis=0, keepdims=True)
        m2 = jnp.max(jnp.where(sub == first, -jnp.inf, bg), axis=0, keepdims=True)
        groups.append(bg)
        gscore.append(m1 + m2)
    masked = []
    for g in range(N_GROUPS):
        beaten = jnp.zeros((1, tm), F32)
        for o in range(N_GROUPS):
            if o == g:
                continue
            wins = (gscore[o] >= gscore[g]) if o < g else (gscore[o] > gscore[g])
            beaten = beaten + jnp.where(wins, 1.0, 0.0)
        masked.append(jnp.where(beaten < TOPK_GROUPS, groups[g], -jnp.inf))
    cur = jnp.concatenate(masked, axis=0)

    eid = lax.broadcasted_iota(I32, (N_EXPERTS, tm), 0).astype(F32)
    idx_rows, sc_rows = [], []
    sel = jnp.zeros((N_EXPERTS, tm), F32)
    for _ in range(TOP_K):
        m = jnp.max(cur, axis=0, keepdims=True)
        idx = jnp.min(jnp.where(cur == m, eid, float(N_EXPERTS)), axis=0, keepdims=True)
        hit = eid == idx
        sel = jnp.where(hit, 1.0, sel)
        cur = jnp.where(hit, -jnp.inf, cur)
        idx_rows.append(idx)
        sc_rows.append(jnp.sum(jnp.where(hit, scores, 0.0), axis=0, keepdims=True))
    wsum = sc_rows[0]
    for k in range(1, TOP_K):
        wsum = wsum + sc_rows[k]

    prefix = jnp.dot(sel.astype(BF16), tri_ref[...], preferred_element_type=F32)
    rank = carry_ref[:, 0:1] + prefix
    rank_rows = [jnp.sum(jnp.where(eid == idx_rows[k], rank, 0.0), axis=0, keepdims=True) for k in range(TOP_K)]
    carry_ref[...] = carry_ref[...] + jnp.sum(sel, axis=1, keepdims=True)
    cnt_ref[...] = carry_ref[...]

    idx_ref[...] = jnp.concatenate(idx_rows, axis=0).astype(I32)
    rank_ref[...] = jnp.concatenate(rank_rows, axis=0).astype(I32)
    wt_ref[...] = jnp.concatenate([s / wsum * ROUTE_SCALE for s in sc_rows], axis=0)


def _mix_route(x, uc, at, g1, wg, wco, wao, wo, g2, wrh, wrl, rb, wsgu, wsd, tm=512):
    n = x.shape[0]
    tri = (jnp.arange(tm)[:, None] < jnp.arange(tm)[None, :]).astype(BF16)
    row = lambda i: (i, 0)
    col = lambda i: (0, i)
    full = lambda i: (0, 0)
    ins = [x, uc, at, g1, wg, wco, wao, wo, g2, wrh, wrl, rb, tri, wsgu, wsd]
    in_specs = [pl.BlockSpec((tm, D_MODEL), row), pl.BlockSpec((tm, D_CONV), row), pl.BlockSpec((tm, D_ATTN), row)]
    in_specs += [pl.BlockSpec(a.shape, full) for a in ins[3:]]
    return pl.pallas_call(
        functools.partial(_mix_route_kernel, tm=tm),
        grid=(n // tm,),
        in_specs=in_specs,
        out_specs=[pl.BlockSpec((tm, D_MODEL), row), pl.BlockSpec((tm, D_MODEL // 2), row),
                   pl.BlockSpec((TOP_K, tm), col), pl.BlockSpec((TOP_K, tm), col), pl.BlockSpec((TOP_K, tm), col),
                   pl.BlockSpec((N_EXPERTS, 128), full)],
        out_shape=[jax.ShapeDtypeStruct((n, D_MODEL), F32), jax.ShapeDtypeStruct((n, D_MODEL // 2), U32),
                   jax.ShapeDtypeStruct((TOP_K, n), I32), jax.ShapeDtypeStruct((TOP_K, n), I32),
                   jax.ShapeDtypeStruct((TOP_K, n), F32), jax.ShapeDtypeStruct((N_EXPERTS, 128), F32)],
        scratch_shapes=[pltpu.VMEM((N_EXPERTS, 128), F32)],
        compiler_params=_cparams("arbitrary"),
        name="mix_route",
    )(*ins)


def _sc_workers():
    info = plsc.get_sparse_core_info()
    return info.num_cores, info.num_subcores


def _sc_mesh():
    return plsc.VectorSubcoreMesh(core_axis_name="c", subcore_axis_name="s")


def _worker_id(n_cores):
    return lax.axis_index("s") * n_cores + lax.axis_index("c")


def _dispatch(slots3, hp, n_slots):
    n_chunks = slots3.shape[0]
    n_cores, n_sub = _sc_workers()
    per_worker = n_chunks // (n_cores * n_sub)
    assert per_worker * n_cores * n_sub == n_chunks

    @functools.partial(
        pl.kernel, mesh=_sc_mesh(),
        out_type=jax.ShapeDtypeStruct((n_slots, D_MODEL // 2), U32),
        scratch_types=[pltpu.VMEM((TOP_K, SC_CHUNK), I32), pltpu.VMEM((SC_CHUNK, D_MODEL // 2), U32),
                       pltpu.SemaphoreType.DMA],
        name="dispatch")
    def run(slots_hbm, hp_hbm, xs_hbm, idx_v, rows_v, sem):
        first = _worker_id(n_cores) * per_worker

        @pl.loop(0, per_worker)
        def _(j):
            c = first + j
            pltpu.sync_copy(slots_hbm.at[c], idx_v)
            pltpu.sync_copy(hp_hbm.at[pl.ds(c * SC_CHUNK, SC_CHUNK)], rows_v)
            copies = [pltpu.async_copy(rows_v, xs_hbm.at[idx_v.at[k]], sem) for k in range(TOP_K)]
            for cp in copies:
                cp.wait()

    return run(slots3, hp)


def _experts_kernel(start_ref, count_ref, wg_ref, wu_ref, wd_ref, xs_ref, y_ref,
                    xbuf, ybuf, wgu_s, wd_s, in_sem, out_sem):
    e = pl.program_id(0)
    start = start_ref[e]
    count = count_ref[e]
    n_chunks = (count + SLOT_BLK - 1) // SLOT_BLK

    def rows(c):
        return pl.ds(pl.multiple_of(start + c * SLOT_BLK, SLOT_BLK), SLOT_BLK)

    def in_copy(c, slot):
        return pltpu.make_async_copy(xs_ref.at[rows(c)], xbuf.at[slot], in_sem.at[slot])

    def out_copy(c, slot):
        return pltpu.make_async_copy(ybuf.at[slot], y_ref.at[rows(c)], out_sem.at[slot])

    @pl.when(n_chunks > 0)
    def _():
        in_copy(0, 0).start()
        wgu_s[:, :D_EXPERT] = wg_ref[0].astype(BF16)
        wgu_s[:, D_EXPERT:] = wu_ref[0].astype(BF16)
        wd_s[...] = wd_ref[0].astype(BF16)
        srow = lax.broadcasted_iota(I32, (SLOT_BLK, 1), 0)

        def chunk(c, carry):
            slot = c & 1
            in_copy(c, slot).wait()

            @pl.when(c + 1 < n_chunks)
            def _():
                in_copy(c + 1, 1 - slot).start()

            @pl.when(c >= 2)
            def _():
                out_copy(c - 2, slot).wait()

            xw = jnp.where(srow < count - c * SLOT_BLK, xbuf[slot], jnp.uint32(0))
            xa, xb = _unpack_bf16_pair(xw)
            x = jnp.concatenate([xa.astype(BF16), xb.astype(BF16)], axis=1)
            gu = jnp.dot(x, wgu_s[...], preferred_element_type=F32)
            mid = (jax.nn.silu(gu[:, :D_EXPERT]) * gu[:, D_EXPERT:]).astype(BF16)
            y = jnp.dot(mid, wd_s[...], preferred_element_type=F32)
            ybuf[slot] = _pack_bf16_pair(y[:, :D_MODEL // 2], y[:, D_MODEL // 2:])
            out_copy(c, slot).start()
            return carry

        lax.fori_loop(0, n_chunks, chunk, 0)

        @pl.when(n_chunks >= 2)
        def _():
            out_copy(n_chunks - 2, n_chunks & 1).wait()

        out_copy(n_chunks - 1, (n_chunks - 1) & 1).wait()


def _experts(pad_start, counts, xs, wg, wu, wd):
    wmap = lambda e, st, ct: (e, 0, 0)
    buf = pltpu.VMEM((2, SLOT_BLK, D_MODEL // 2), U32)
    grid_spec = pltpu.PrefetchScalarGridSpec(
        num_scalar_prefetch=2,
        grid=(N_EXPERTS,),
        in_specs=[pl.BlockSpec((1, D_MODEL, D_EXPERT), wmap),
                  pl.BlockSpec((1, D_MODEL, D_EXPERT), wmap),
                  pl.BlockSpec((1, D_EXPERT, D_MODEL), wmap),
                  pl.BlockSpec(memory_space=pl.ANY)],
        out_specs=pl.BlockSpec(memory_space=pl.ANY),
        scratch_shapes=[buf, buf, pltpu.VMEM((D_MODEL, 2 * D_EXPERT), BF16), pltpu.VMEM((D_EXPERT, D_MODEL), BF16),
                        pltpu.SemaphoreType.DMA((2,)), pltpu.SemaphoreType.DMA((2,))],
    )
    return pl.pallas_call(
        _experts_kernel,
        grid_spec=grid_spec,
        out_shape=jax.ShapeDtypeStruct(xs.shape, U32),
        compiler_params=_cparams("arbitrary"),
        name="experts",
    )(pad_start, counts, wg, wu, wd, xs)


def _gather(slots3, y, n):
    n_chunks = slots3.shape[0]
    n_cores, n_sub = _sc_workers()
    per_worker = n_chunks // (n_cores * n_sub)
    assert per_worker * n_cores * n_sub == n_chunks

    @functools.partial(
        pl.kernel, mesh=_sc_mesh(),
        out_type=jax.ShapeDtypeStruct((TOP_K, n, D_MODEL // 2), U32),
        scratch_types=[pltpu.VMEM((TOP_K, SC_CHUNK), I32),
                       pltpu.VMEM((SC_CHUNK, D_MODEL // 2), U32), pltpu.VMEM((SC_CHUNK, D_MODEL // 2), U32),
                       pltpu.SemaphoreType.DMA, pltpu.SemaphoreType.DMA, pltpu.SemaphoreType.DMA],
        name="gather")
    def run(slots_hbm, y_hbm, g_hbm, idx_v, buf0, buf1, gsem, wsem0, wsem1):
        first = _worker_id(n_cores) * per_worker
        bufs = (buf0, buf1)
        wsems = (wsem0, wsem1)

        @pl.loop(0, per_worker)
        def _(j):
            c = first + j
            pltpu.sync_copy(slots_hbm.at[c], idx_v)
            gathers = [None] * TOP_K
            writes = [None] * TOP_K
            gathers[0] = pltpu.async_copy(y_hbm.at[idx_v.at[0]], bufs[0], gsem)
            for k in range(TOP_K):
                gathers[k].wait()
                if k >= 1:
                    writes[k - 1].wait()
                if k + 1 < TOP_K:
                    gathers[k + 1] = pltpu.async_copy(y_hbm.at[idx_v.at[k + 1]], bufs[(k + 1) % 2], gsem)
                writes[k] = pltpu.async_copy(bufs[k % 2], g_hbm.at[k, pl.ds(c * SC_CHUNK, SC_CHUNK)],
                                             wsems[k % 2])
            writes[TOP_K - 1].wait()

    return run(slots3, y)


def _combine_kernel(wt_ref, xo_ref, g_ref, o_ref):
    half = D_MODEL // 2
    acc_a = xo_ref[:, :half]
    acc_b = xo_ref[:, half:]
    for k in range(TOP_K):
        ya, yb = _unpack_bf16_pair(g_ref[k])
        w = wt_ref[:, k:k + 1]
        acc_a = acc_a + w * ya
        acc_b = acc_b + w * yb
    o_ref[:, :half] = acc_a
    o_ref[:, half:] = acc_b


def _combine(wt, xo, g, tm=256):
    n = xo.shape[0]
    return pl.pallas_call(
        _combine_kernel,
        grid=(n // tm,),
        in_specs=[pl.BlockSpec((tm, TOP_K), lambda i: (i, 0)),
                  pl.BlockSpec((tm, D_MODEL), lambda i: (i, 0)),
                  pl.BlockSpec((TOP_K, tm, D_MODEL // 2), lambda i: (0, i, 0))],
        out_specs=pl.BlockSpec((tm, D_MODEL), lambda i: (i, 0)),
        out_shape=jax.ShapeDtypeStruct((n, D_MODEL), F32),
        compiler_params=_cparams("parallel"),
        name="combine",
    )(wt, xo, g)


def _layer(x, norm_mix_g, w_in, q_norm_g, k_norm_g, rpb, conv_dw_w, conv_dw_b, conv_ln_g, conv_ln_b,
           w_conv_out, w_attn_out, w_o, norm_ffn_g, w_router, router_bias, w_exp_gate, w_exp_up,
           w_exp_down, w_sh_gate, w_sh_up, w_sh_down):
    n = x.shape[0]
    c_qkv = 2 * D_CONV + 3 * D_ATTN
    w_in_b = w_in.astype(BF16)
    head_of = jnp.arange(D_ATTN) // HEAD_DIM
    bsum = (head_of[:, None] == head_of[None, :]).astype(BF16)
    row = lambda v: v.reshape(1, -1).astype(F32)

    u, q, k, v = _inproj(x, row(norm_mix_g), w_in_b[:, :c_qkv], bsum,
                         row(jnp.tile(q_norm_g, N_HEADS)), row(jnp.tile(k_norm_g, N_HEADS)))
    uc = _conv(u, conv_dw_w.reshape(CONV_WIDTH, D_CONV), row(conv_dw_b), row(conv_ln_g), row(conv_ln_b))
    at = _attn(q, k, v, _bias_table(rpb))

    wr_t = w_router.T
    wr_hi = wr_t.astype(BF16)
    wr_lo = (wr_t - wr_hi.astype(F32)).astype(BF16)
    wsgu = jnp.concatenate([w_sh_gate, w_sh_up], axis=1).astype(BF16)
    xo, hp, idx_t, rank_t, wt_t, cnt = _mix_route(
        x, uc, at, row(norm_mix_g), w_in_b[:, c_qkv:], w_conv_out.astype(BF16), w_attn_out.astype(BF16),
        w_o.astype(BF16), row(norm_ffn_g), wr_hi, wr_lo, router_bias.reshape(N_EXPERTS, 1).astype(F32),
        wsgu, w_sh_down.astype(BF16))

    counts = cnt[:, 0].astype(I32)
    padded = (counts + SLOT_BLK - 1) // SLOT_BLK * SLOT_BLK
    pad_end = jnp.cumsum(padded)
    pad_start = pad_end - padded
    n_blocks = (n * TOP_K + N_EXPERTS * (SLOT_BLK - 1)) // SLOT_BLK

    onehot = (idx_t[:, :, None] == jnp.arange(N_EXPERTS)[None, None, :]).astype(I32)
    slots = jnp.sum(onehot * pad_start[None, None, :], axis=2) + rank_t
    slots3 = slots.reshape(TOP_K, n // SC_CHUNK, SC_CHUNK).transpose(1, 0, 2)

    xs = _dispatch(slots3, hp, n_blocks * SLOT_BLK)
    y = _experts(pad_start.astype(I32), counts, xs, w_exp_gate, w_exp_up, w_exp_down)
    return _combine(wt_t.T, xo, _gather(slots3, y, n))


def kernel(x, norm_mix_g, w_in, q_norm_g, k_norm_g, rpb, conv_dw_w, conv_dw_b, conv_ln_g, conv_ln_b, w_conv_out, w_attn_out, w_o, norm_ffn_g, w_router, router_bias, w_exp_gate, w_exp_up, w_exp_down, w_sh_gate, w_sh_up, w_sh_down):
    b, s, d = x.shape
    assert b == 1 and d == D_MODEL and s % (2 * ATT_BLK) == 0, x.shape
    xf = x.reshape(b * s, d)
    depth = norm_mix_g.shape[0]
    for l in range(depth):
        xf = _layer(xf, norm_mix_g[l], w_in[l], q_norm_g[l], k_norm_g[l], rpb[l], conv_dw_w[l], conv_dw_b[l],
                    conv_ln_g[l], conv_ln_b[l], w_conv_out[l], w_attn_out[l], w_o[l], norm_ffn_g[l],
                    w_router[l], router_bias[l], w_exp_gate[l], w_exp_up[l], w_exp_down[l], w_sh_gate[l],
                    w_sh_up[l], w_sh_down[l])
    return xf.reshape(b, s, d)
```

```python
import functools

import jax
import jax.numpy as jnp
from jax import lax
from jax.experimental import pallas as pl
from jax.experimental.pallas import tpu as pltpu
from jax.experimental.pallas import tpu_sc as plsc

D_MODEL = 1024
GRID_W = 64
D_CONV = 512
CONV_WIDTH = 31
CONV_PAD = CONV_WIDTH // 2
N_HEADS = 8
HEAD_DIM = 64
D_ATTN = N_HEADS * HEAD_DIM
WIN_H = 8
WIN_W = 16
N_EXPERTS = 64
TOP_K = 8
N_GROUPS = 8
GROUP_SIZE = N_EXPERTS // N_GROUPS
TOPK_GROUPS = 4
D_EXPERT = 256
D_SHARED = 256
ROUTE_SCALE = 2.5
EPS = 1e-6
LOG2E = 1.4426950408889634

SUBLANES = 8
NEG = -1e30
HALO = 16
ATT_ROWS = 4
ATT_BLK = ATT_ROWS * GRID_W
SLOT_BLK = 512
DMA_ROWS = 64
SC_CHUNK = 64
VMEM_LIMIT = 56 * 1024 * 1024

F32 = jnp.float32
BF16 = jnp.bfloat16
I32 = jnp.int32
U32 = jnp.uint32


def _cparams(*sem):
    return pltpu.CompilerParams(dimension_semantics=sem, vmem_limit_bytes=VMEM_LIMIT)


def _rms(x, g):
    return x * lax.rsqrt(jnp.mean(x * x, axis=-1, keepdims=True) + EPS) * g


def _pack_bf16_pair(a, b):
    def bf16_bits(v):
        return lax.bitcast_convert_type(v.astype(BF16).astype(F32), U32)
    return bf16_bits(a) | (bf16_bits(b) >> 16)


def _unpack_bf16_pair(w):
    a = lax.bitcast_convert_type(w & jnp.uint32(0xFFFF0000), F32)
    b = lax.bitcast_convert_type(w << 16, F32)
    return a, b


def _inproj_kernel(x_ref, g_ref, w_ref, bsum_ref, qg_ref, kg_ref, u_ref, q_ref, k_ref, v_ref):
    h = _rms(x_ref[...], g_ref[...]).astype(BF16)
    ag = jnp.dot(h, w_ref[:, 0:2 * D_CONV], preferred_element_type=F32)
    u_ref[...] = ag[:, :D_CONV] * jax.nn.sigmoid(ag[:, D_CONV:])
    bsum = bsum_ref[...]

    def head_norm(z, g):
        ss = jnp.dot((z * z).astype(BF16), bsum, preferred_element_type=F32)
        return z * lax.rsqrt(ss * (1.0 / HEAD_DIM) + EPS) * g

    c0 = 2 * D_CONV
    q = jnp.dot(h, w_ref[:, c0:c0 + D_ATTN], preferred_element_type=F32)
    q_ref[...] = (head_norm(q, qg_ref[...]) * (HEAD_DIM ** -0.5 * LOG2E)).astype(BF16)
    k = jnp.dot(h, w_ref[:, c0 + D_ATTN:c0 + 2 * D_ATTN], preferred_element_type=F32)
    k_ref[...] = head_norm(k, kg_ref[...]).astype(BF16)
    v = jnp.dot(h, w_ref[:, c0 + 2 * D_ATTN:c0 + 3 * D_ATTN], preferred_element_type=F32)
    v_ref[...] = v.astype(BF16)


def _inproj(x, g, w, bsum, qg, kg, tm=512):
    n = x.shape[0]
    wc = w.shape[1]
    full = lambda i: (0, 0)
    row = lambda i: (i, 0)
    return pl.pallas_call(
        _inproj_kernel,
        grid=(n // tm,),
        in_specs=[pl.BlockSpec((tm, D_MODEL), row), pl.BlockSpec((1, D_MODEL), full),
                  pl.BlockSpec((D_MODEL, wc), full), pl.BlockSpec((D_ATTN, D_ATTN), full),
                  pl.BlockSpec((1, D_ATTN), full), pl.BlockSpec((1, D_ATTN), full)],
        out_specs=[pl.BlockSpec((tm, D_CONV), row), pl.BlockSpec((tm, D_ATTN), row),
                   pl.BlockSpec((tm, D_ATTN), row), pl.BlockSpec((tm, D_ATTN), row)],
        out_shape=[jax.ShapeDtypeStruct((n, D_CONV), F32), jax.ShapeDtypeStruct((n, D_ATTN), BF16),
                   jax.ShapeDtypeStruct((n, D_ATTN), BF16), jax.ShapeDtypeStruct((n, D_ATTN), BF16)],
        compiler_params=_cparams("parallel"),
        name="inproj",
    )(x, g, w, bsum, qg, kg)


def _conv_kernel(up_ref, uc_ref, un_ref, w_ref, b_ref, lg_ref, lb_ref, o_ref, ext_ref, sh_ref, *, tc, ch):
    i = pl.program_id(0)
    last = pl.num_programs(0) - 1
    ext_ref[0:HALO, :] = jnp.where(i > 0, up_ref[...], 0.0)
    ext_ref[HALO:HALO + tc, :] = uc_ref[...]
    ext_ref[HALO + tc:2 * HALO + tc, :] = jnp.where(i < last, un_ref[...], 0.0)
    span = sh_ref.shape[1]
    for b in range(SUBLANES):
        sh_ref[b] = ext_ref[b:b + span, :]
    bias = b_ref[...]
    lg = lg_ref[...]
    lb = lb_ref[...]
    for c in range(tc // ch):
        acc = jnp.broadcast_to(bias, (ch, D_CONV))
        for j in range(CONV_WIDTH):
            shift = HALO - CONV_PAD + j
            row0 = c * ch + shift // SUBLANES * SUBLANES
            acc = acc + sh_ref[shift % SUBLANES, row0:row0 + ch, :] * w_ref[j:j + 1, :]
        mu = jnp.mean(acc, axis=-1, keepdims=True)
        d = acc - mu
        var = jnp.mean(d * d, axis=-1, keepdims=True)
        y = d * lax.rsqrt(var + EPS) * lg + lb
        o_ref[c * ch:(c + 1) * ch, :] = (y * jax.nn.sigmoid(y)).astype(BF16)


def _conv(u, w, b, lg, lb, tc=256, ch=64):
    n = u.shape[0]
    hb = tc // HALO
    nh = n // HALO
    full = lambda i: (0, 0)
    return pl.pallas_call(
        functools.partial(_conv_kernel, tc=tc, ch=ch),
        grid=(n // tc,),
        in_specs=[pl.BlockSpec((HALO, D_CONV), lambda i: (jnp.maximum(i * hb - 1, 0), 0)),
                  pl.BlockSpec((tc, D_CONV), lambda i: (i, 0)),
                  pl.BlockSpec((HALO, D_CONV), lambda i: (jnp.minimum((i + 1) * hb, nh - 1), 0)),
                  pl.BlockSpec((CONV_WIDTH, D_CONV), full), pl.BlockSpec((1, D_CONV), full),
                  pl.BlockSpec((1, D_CONV), full), pl.BlockSpec((1, D_CONV), full)],
        out_specs=pl.BlockSpec((tc, D_CONV), lambda i: (i, 0)),
        out_shape=jax.ShapeDtypeStruct((n, D_CONV), BF16),
        scratch_shapes=[pltpu.VMEM((tc + 2 * HALO, D_CONV), F32),
                        pltpu.VMEM((SUBLANES, tc + 2 * HALO - SUBLANES, D_CONV), F32)],
        compiler_params=_cparams("parallel"),
        name="conv",
    )(u, u, u, w, b, lg, lb)


def _attn_kernel(q_ref, kp_ref, kc_ref, kn_ref, vp_ref, vc_ref, vn_ref, tb_ref, o_ref, *, rows):
    i = pl.program_id(0)
    nkey = 3 * ATT_BLK
    lrow = lax.broadcasted_iota(I32, (1, nkey), 1) >> 6
    lane = lax.broadcasted_iota(I32, (1, 2 * HEAD_DIM), 1)
    masks = []
    for j in range(ATT_ROWS):
        r = i * ATT_ROWS + j
        rs = jnp.clip(r - WIN_H // 2, 0, rows - WIN_H)
        lo = rs - (i - 1) * ATT_ROWS
        masks.append(jnp.where((lrow >= lo) & (lrow < lo + WIN_H), 0.0, NEG))
    dn = (((1,), (1,)), ((), ()))
    for p in range(N_HEADS // 2):
        sl = slice(2 * HEAD_DIM * p, 2 * HEAD_DIM * (p + 1))
        q2 = q_ref[:, sl]
        ks = (kp_ref[:, sl], kc_ref[:, sl], kn_ref[:, sl])
        vs = (vp_ref[:, sl], vc_ref[:, sl], vn_ref[:, sl])
        out_pair = None
        for half in range(2):
            h = 2 * p + half
            in_half = (lane >= HEAD_DIM * half) & (lane < HEAD_DIM * (half + 1))
            qm = jnp.where(in_half, q2, jnp.zeros_like(q2))
            s = jnp.concatenate([lax.dot_general(qm, kk, dn, preferred_element_type=F32) for kk in ks], axis=1)
            parts = []
            for j in range(ATT_ROWS):
                bias = jnp.concatenate([tb_ref[h, 2 * m - j + 3] for m in range(nkey // (2 * GRID_W))], axis=1)
                parts.append(s[j * GRID_W:(j + 1) * GRID_W, :] + bias + masks[j])
            s = jnp.concatenate(parts, axis=0)
            mx = jnp.max(s, axis=-1, keepdims=True)
            e = jnp.exp2(s - mx)
            den = jnp.sum(e, axis=-1, keepdims=True)
            pb = e.astype(BF16)
            o = sum(jnp.dot(pb[:, ATT_BLK * t:ATT_BLK * (t + 1)], vs[t], preferred_element_type=F32)
                    for t in range(3))
            o = o / den
            out_pair = o if out_pair is None else jnp.where(in_half, o, out_pair)
        o_ref[:, sl] = out_pair.astype(BF16)


def _attn(q, k, v, tb):
    n = q.shape[0]
    rows = n // GRID_W
    nb = n // ATT_BLK
    cur = lambda i: (i, 0)
    prev = lambda i: (jnp.maximum(i - 1, 0), 0)
    nxt = lambda i: (jnp.minimum(i + 1, nb - 1), 0)
    blk = (ATT_BLK, D_ATTN)
    return pl.pallas_call(
        functools.partial(_attn_kernel, rows=rows),
        grid=(nb,),
        in_specs=[pl.BlockSpec(blk, cur), pl.BlockSpec(blk, prev), pl.BlockSpec(blk, cur), pl.BlockSpec(blk, nxt),
                  pl.BlockSpec(blk, prev), pl.BlockSpec(blk, cur), pl.BlockSpec(blk, nxt),
                  pl.BlockSpec(tb.shape, lambda i: (0, 0, 0, 0))],
        out_specs=pl.BlockSpec(blk, cur),
        out_shape=jax.ShapeDtypeStruct((n, D_ATTN), BF16),
        compiler_params=_cparams("parallel"),
        name="attn",
    )(q, k, k, k, v, v, v, tb)


def _bias_table(rpb):
    cols = jnp.arange(GRID_W)
    start = jnp.clip(cols - WIN_W // 2, 0, GRID_W - WIN_W)
    kc = cols[None, :]
    inwin = (kc >= start[:, None]) & (kc < start[:, None] + WIN_W)
    off = GRID_W - WIN_W
    padded = jnp.pad(rpb, ((0, 0), (0, 0), (off, off)))
    rel = jnp.stack([padded[:, :, GRID_W - 1 - c:2 * GRID_W - 1 - c] for c in range(GRID_W)], axis=2)
    t = jnp.where(inwin[None, None], rel * LOG2E, NEG)
    t = jnp.concatenate([t, jnp.full_like(t[:, :1], NEG)], axis=1)
    t_next = jnp.concatenate([t[:, 1:], t[:, -1:]], axis=1)
    return jnp.concatenate([t, t_next], axis=-1).astype(F32)


def _mix_route_kernel(x_ref, uc_ref, at_ref, g1_ref, wg_ref, wco_ref, wao_ref, wo_ref, g2_ref,
                      wrh_ref, wrl_ref, rb_ref, tri_ref, wsgu_ref, wsd_ref,
                      xo_ref, hp_ref, idx_ref, rank_ref, wt_ref, cnt_ref, carry_ref, *, tm):
    i = pl.program_id(0)

    @pl.when(i == 0)
    def _():
        carry_ref[...] = jnp.zeros_like(carry_ref)

    x = x_ref[...]
    h = _rms(x, g1_ref[...]).astype(BF16)
    gates = jax.nn.sigmoid(jnp.dot(h, wg_ref[...], preferred_element_type=F32))
    y_conv = jnp.dot(uc_ref[...], wco_ref[...], preferred_element_type=F32)
    y_attn = jnp.dot(at_ref[...], wao_ref[...], preferred_element_type=F32)
    merged = gates[:, :D_MODEL] * y_conv + gates[:, D_MODEL:] * y_attn
    x1 = x + jnp.dot(merged.astype(BF16), wo_ref[...], preferred_element_type=F32)
    h2 = _rms(x1, g2_ref[...])
    h2b = h2.astype(BF16)
    hp_ref[...] = _pack_bf16_pair(h2[:, :D_MODEL // 2], h2[:, D_MODEL // 2:])

    gu = jnp.dot(h2b, wsgu_ref[...], preferred_element_type=F32)
    mid = (jax.nn.silu(gu[:, :D_SHARED]) * gu[:, D_SHARED:]).astype(BF16)
    xo_ref[...] = x1 + jnp.dot(mid, wsd_ref[...], preferred_element_type=F32)

    h2l = (h2 - h2b.astype(F32)).astype(BF16)
    dn = (((1,), (1,)), ((), ()))
    logits = (lax.dot_general(wrh_ref[...], h2b, dn, preferred_element_type=F32)
              + lax.dot_general(wrl_ref[...], h2b, dn, preferred_element_type=F32)
              + lax.dot_general(wrh_ref[...], h2l, dn, preferred_element_type=F32))
    scores = jax.nn.sigmoid(logits)
    biased = scores + rb_ref[...]

    sub = lax.broadcasted_iota(I32, (GROUP_SIZE, tm), 0).astype(F32)
    groups, gscore = [], []
    for g in range(N_GROUPS):
        bg = biased[g * GROUP_SIZE:(g + 1) * GROUP_SIZE, :]
        m1 = jnp.max(bg, axis=0, keepdims=True)
        first = jnp.min(jnp.where(bg == m1, sub, float(GROUP_SIZE)), axis=0, keepdims=True)
        m2 = jnp.max(jnp.where(sub == first, -jnp.inf, bg), axis=0, keepdims=True)
        groups.append(bg)
        gscore.append(m1 + m2)
    masked = []
    for g in range(N_GROUPS):
        beaten = jnp.zeros((1, tm), F32)
        for o in range(N_GROUPS):
            if o == g:
                continue
            wins = (gscore[o] >= gscore[g]) if o < g else (gscore[o] > gscore[g])
            beaten = beaten + jnp.where(wins, 1.0, 0.0)
        masked.append(jnp.where(beaten < TOPK_GROUPS, groups[g], -jnp.inf))
    cur = jnp.concatenate(masked, axis=0)

    eid = lax.broadcasted_iota(I32, (N_EXPERTS, tm), 0).astype(F32)
    idx_rows, sc_rows = [], []
    sel = jnp.zeros((N_EXPERTS, tm), F32)
    for _ in range(TOP_K):
        m = jnp.max(cur, axis=0, keepdims=True)
        idx = jnp.min(jnp.where(cur == m, eid, float(N_EXPERTS)), axis=0, keepdims=True)
        hit = eid == idx
        sel = jnp.where(hit, 1.0, sel)
        cur = jnp.where(hit, -jnp.inf, cur)
        idx_rows.append(idx)
        sc_rows.append(jnp.sum(jnp.where(hit, scores, 0.0), axis=0, keepdims=True))
    wsum = sc_rows[0]
    for k in range(1, TOP_K):
        wsum = wsum + sc_rows[k]

    prefix = jnp.dot(sel.astype(BF16), tri_ref[...], preferred_element_type=F32)
    rank = carry_ref[:, 0:1] + prefix
    rank_rows = [jnp.sum(jnp.where(eid == idx_rows[k], rank, 0.0), axis=0, keepdims=True) for k in range(TOP_K)]
    carry_ref[...] = carry_ref[...] + jnp.sum(sel, axis=1, keepdims=True)
    cnt_ref[...] = carry_ref[...]

    idx_ref[...] = jnp.concatenate(idx_rows, axis=0).astype(I32)
    rank_ref[...] = jnp.concatenate(rank_rows, axis=0).astype(I32)
    wt_ref[...] = jnp.concatenate([s / wsum * ROUTE_SCALE for s in sc_rows], axis=0)


def _mix_route(x, uc, at, g1, wg, wco, wao, wo, g2, wrh, wrl, rb, wsgu, wsd, tm=512):
    n = x.shape[0]
    tri = (jnp.arange(tm)[:, None] < jnp.arange(tm)[None, :]).astype(BF16)
    row = lambda i: (i, 0)
    col = lambda i: (0, i)
    full = lambda i: (0, 0)
    ins = [x, uc, at, g1, wg, wco, wao, wo, g2, wrh, wrl, rb, tri, wsgu, wsd]
    in_specs = [pl.BlockSpec((tm, D_MODEL), row), pl.BlockSpec((tm, D_CONV), row), pl.BlockSpec((tm, D_ATTN), row)]
    in_specs += [pl.BlockSpec(a.shape, full) for a in ins[3:]]
    return pl.pallas_call(
        functools.partial(_mix_route_kernel, tm=tm),
        grid=(n // tm,),
        in_specs=in_specs,
        out_specs=[pl.BlockSpec((tm, D_MODEL), row), pl.BlockSpec((tm, D_MODEL // 2), row),
                   pl.BlockSpec((TOP_K, tm), col), pl.BlockSpec((TOP_K, tm), col), pl.BlockSpec((TOP_K, tm), col),
                   pl.BlockSpec((N_EXPERTS, 128), full)],
        out_shape=[jax.ShapeDtypeStruct((n, D_MODEL), F32), jax.ShapeDtypeStruct((n, D_MODEL // 2), U32),
                   jax.ShapeDtypeStruct((TOP_K, n), I32), jax.ShapeDtypeStruct((TOP_K, n), I32),
                   jax.ShapeDtypeStruct((TOP_K, n), F32), jax.ShapeDtypeStruct((N_EXPERTS, 128), F32)],
        scratch_shapes=[pltpu.VMEM((N_EXPERTS, 128), F32)],
        compiler_params=_cparams("arbitrary"),
        name="mix_route",
    )(*ins)


def _sc_workers():
    info = plsc.get_sparse_core_info()
    return info.num_cores, info.num_subcores


def _sc_mesh():
    return plsc.VectorSubcoreMesh(core_axis_name="c", subcore_axis_name="s")


def _worker_id(n_cores):
    return lax.axis_index("s") * n_cores + lax.axis_index("c")


def _dispatch(slots3, hp, n_slots):
    n_chunks = slots3.shape[0]
    n_cores, n_sub = _sc_workers()
    per_worker = n_chunks // (n_cores * n_sub)
    assert per_worker * n_cores * n_sub == n_chunks

    @functools.partial(
        pl.kernel, mesh=_sc_mesh(),
        out_type=jax.ShapeDtypeStruct((n_slots, D_MODEL // 2), U32),
        scratch_types=[pltpu.VMEM((TOP_K, SC_CHUNK), I32), pltpu.VMEM((SC_CHUNK, D_MODEL // 2), U32),
                       pltpu.SemaphoreType.DMA],
        name="dispatch")
    def run(slots_hbm, hp_hbm, xs_hbm, idx_v, rows_v, sem):
        first = _worker_id(n_cores) * per_worker

        @pl.loop(0, per_worker)
        def _(j):
            c = first + j
            pltpu.sync_copy(slots_hbm.at[c], idx_v)
            pltpu.sync_copy(hp_hbm.at[pl.ds(c * SC_CHUNK, SC_CHUNK)], rows_v)
            copies = [pltpu.async_copy(rows_v, xs_hbm.at[idx_v.at[k]], sem) for k in range(TOP_K)]
            for cp in copies:
                cp.wait()

    return run(slots3, hp)


class _CopyGroup:
    def __init__(self, copies):
        self.copies = copies

    def start(self):
        for cp in self.copies:
            cp.start()

    def wait(self):
        for cp in self.copies:
            cp.wait()


def _experts_kernel(start_ref, count_ref, wg_ref, wu_ref, wd_ref, xs_ref, y_ref,
                    xbuf, ybuf, wgu_s, wd_s, in_sem, out_sem):
    e = pl.program_id(0)
    start = start_ref[e]
    count = count_ref[e]
    n_chunks = (count + SLOT_BLK - 1) // SLOT_BLK

    pieces = range(SLOT_BLK // DMA_ROWS)

    def hbm_rows(c, p):
        return pl.ds(pl.multiple_of(start + c * SLOT_BLK + p * DMA_ROWS, DMA_ROWS), DMA_ROWS)

    def in_copy(c, slot):
        return _CopyGroup([pltpu.make_async_copy(xs_ref.at[hbm_rows(c, p)],
                                                 xbuf.at[slot, pl.ds(p * DMA_ROWS, DMA_ROWS)], in_sem.at[slot])
                           for p in pieces])

    def out_copy(c, slot):
        return _CopyGroup([pltpu.make_async_copy(ybuf.at[slot, pl.ds(p * DMA_ROWS, DMA_ROWS)],
                                                 y_ref.at[hbm_rows(c, p)], out_sem.at[slot])
                           for p in pieces])

    @pl.when(n_chunks > 0)
    def _():
        in_copy(0, 0).start()
        wgu_s[:, :D_EXPERT] = wg_ref[0].astype(BF16)
        wgu_s[:, D_EXPERT:] = wu_ref[0].astype(BF16)
        wd_s[...] = wd_ref[0].astype(BF16)
        srow = lax.broadcasted_iota(I32, (SLOT_BLK, 1), 0)

        def chunk(c, carry):
            slot = c & 1
            in_copy(c, slot).wait()

            @pl.when(c + 1 < n_chunks)
            def _():
                in_copy(c + 1, 1 - slot).start()

            @pl.when(c >= 2)
            def _():
                out_copy(c - 2, slot).wait()

            xw = jnp.where(srow < count - c * SLOT_BLK, xbuf[slot], jnp.uint32(0))
            xa, xb = _unpack_bf16_pair(xw)
            x = jnp.concatenate([xa.astype(BF16), xb.astype(BF16)], axis=1)
            gu = jnp.dot(x, wgu_s[...], preferred_element_type=F32)
            mid = (jax.nn.silu(gu[:, :D_EXPERT]) * gu[:, D_EXPERT:]).astype(BF16)
            y = jnp.dot(mid, wd_s[...], preferred_element_type=F32)
            ybuf[slot] = _pack_bf16_pair(y[:, :D_MODEL // 2], y[:, D_MODEL // 2:])
            out_copy(c, slot).start()
            return carry

        lax.fori_loop(0, n_chunks, chunk, 0)

        @pl.when(n_chunks >= 2)
        def _():
            out_copy(n_chunks - 2, n_chunks & 1).wait()

        out_copy(n_chunks - 1, (n_chunks - 1) & 1).wait()


def _experts(pad_start, counts, xs, wg, wu, wd):
    wmap = lambda e, st, ct: (e, 0, 0)
    buf = pltpu.VMEM((2, SLOT_BLK, D_MODEL // 2), U32)
    grid_spec = pltpu.PrefetchScalarGridSpec(
        num_scalar_prefetch=2,
        grid=(N_EXPERTS,),
        in_specs=[pl.BlockSpec((1, D_MODEL, D_EXPERT), wmap),
                  pl.BlockSpec((1, D_MODEL, D_EXPERT), wmap),
                  pl.BlockSpec((1, D_EXPERT, D_MODEL), wmap),
                  pl.BlockSpec(memory_space=pl.ANY)],
        out_specs=pl.BlockSpec(memory_space=pl.ANY),
        scratch_shapes=[buf, buf, pltpu.VMEM((D_MODEL, 2 * D_EXPERT), BF16), pltpu.VMEM((D_EXPERT, D_MODEL), BF16),
                        pltpu.SemaphoreType.DMA((2,)), pltpu.SemaphoreType.DMA((2,))],
    )
    return pl.pallas_call(
        _experts_kernel,
        grid_spec=grid_spec,
        out_shape=jax.ShapeDtypeStruct(xs.shape, U32),
        compiler_params=_cparams("arbitrary"),
        name="experts",
    )(pad_start, counts, wg, wu, wd, xs)


def _gather(slots3, y, n):
    n_chunks = slots3.shape[0]
    n_cores, n_sub = _sc_workers()
    per_worker = n_chunks // (n_cores * n_sub)
    assert per_worker * n_cores * n_sub == n_chunks

    @functools.partial(
        pl.kernel, mesh=_sc_mesh(),
        out_type=jax.ShapeDtypeStruct((TOP_K, n, D_MODEL // 2), U32),
        scratch_types=[pltpu.VMEM((TOP_K, SC_CHUNK), I32),
                       pltpu.VMEM((SC_CHUNK, D_MODEL // 2), U32), pltpu.VMEM((SC_CHUNK, D_MODEL // 2), U32),
                       pltpu.SemaphoreType.DMA, pltpu.SemaphoreType.DMA, pltpu.SemaphoreType.DMA],
        name="gather")
    def run(slots_hbm, y_hbm, g_hbm, idx_v, buf0, buf1, gsem, wsem0, wsem1):
        first = _worker_id(n_cores) * per_worker
        bufs = (buf0, buf1)
        wsems = (wsem0, wsem1)

        @pl.loop(0, per_worker)
        def _(j):
            c = first + j
            pltpu.sync_copy(slots_hbm.at[c], idx_v)
            gathers = [None] * TOP_K
            writes = [None] * TOP_K
            gathers[0] = pltpu.async_copy(y_hbm.at[idx_v.at[0]], bufs[0], gsem)
            for k in range(TOP_K):
                gathers[k].wait()
                if k >= 1:
                    writes[k - 1].wait()
                if k + 1 < TOP_K:
                    gathers[k + 1] = pltpu.async_copy(y_hbm.at[idx_v.at[k + 1]], bufs[(k + 1) % 2], gsem)
                writes[k] = pltpu.async_copy(bufs[k % 2], g_hbm.at[k, pl.ds(c * SC_CHUNK, SC_CHUNK)],
                                             wsems[k % 2])
            writes[TOP_K - 1].wait()

    return run(slots3, y)


def _combine_kernel(wt_ref, xo_ref, g_ref, o_ref):
    half = D_MODEL // 2
    acc_a = xo_ref[:, :half]
    acc_b = xo_ref[:, half:]
    for k in range(TOP_K):
        ya, yb = _unpack_bf16_pair(g_ref[k])
        w = wt_ref[:, k:k + 1]
        acc_a = acc_a + w * ya
        acc_b = acc_b + w * yb
    o_ref[:, :half] = acc_a
    o_ref[:, half:] = acc_b


def _combine(wt, xo, g, tm=256):
    n = xo.shape[0]
    return pl.pallas_call(
        _combine_kernel,
        grid=(n // tm,),
        in_specs=[pl.BlockSpec((tm, TOP_K), lambda i: (i, 0)),
                  pl.BlockSpec((tm, D_MODEL), lambda i: (i, 0)),
                  pl.BlockSpec((TOP_K, tm, D_MODEL // 2), lambda i: (0, i, 0))],
        out_specs=pl.BlockSpec((tm, D_MODEL), lambda i: (i, 0)),
        out_shape=jax.ShapeDtypeStruct((n, D_MODEL), F32),
        compiler_params=_cparams("parallel"),
        name="combine",
    )(wt, xo, g)


def _layer(x, norm_mix_g, w_in, q_norm_g, k_norm_g, rpb, conv_dw_w, conv_dw_b, conv_ln_g, conv_ln_b,
           w_conv_out, w_attn_out, w_o, norm_ffn_g, w_router, router_bias, w_exp_gate, w_exp_up,
           w_exp_down, w_sh_gate, w_sh_up, w_sh_down):
    n = x.shape[0]
    c_qkv = 2 * D_CONV + 3 * D_ATTN
    w_in_b = w_in.astype(BF16)
    head_of = jnp.arange(D_ATTN) // HEAD_DIM
    bsum = (head_of[:, None] == head_of[None, :]).astype(BF16)
    row = lambda v: v.reshape(1, -1).astype(F32)

    u, q, k, v = _inproj(x, row(norm_mix_g), w_in_b[:, :c_qkv], bsum,
                         row(jnp.tile(q_norm_g, N_HEADS)), row(jnp.tile(k_norm_g, N_HEADS)))
    uc = _conv(u, conv_dw_w.reshape(CONV_WIDTH, D_CONV), row(conv_dw_b), row(conv_ln_g), row(conv_ln_b))
    at = _attn(q, k, v, _bias_table(rpb))

    wr_t = w_router.T
    wr_hi = wr_t.astype(BF16)
    wr_lo = (wr_t - wr_hi.astype(F32)).astype(BF16)
    wsgu = jnp.concatenate([w_sh_gate, w_sh_up], axis=1).astype(BF16)
    xo, hp, idx_t, rank_t, wt_t, cnt = _mix_route(
        x, uc, at, row(norm_mix_g), w_in_b[:, c_qkv:], w_conv_out.astype(BF16), w_attn_out.astype(BF16),
        w_o.astype(BF16), row(norm_ffn_g), wr_hi, wr_lo, router_bias.reshape(N_EXPERTS, 1).astype(F32),
        wsgu, w_sh_down.astype(BF16))

    counts = cnt[:, 0].astype(I32)
    padded = (counts + SLOT_BLK - 1) // SLOT_BLK * SLOT_BLK
    pad_end = jnp.cumsum(padded)
    pad_start = pad_end - padded
    n_blocks = (n * TOP_K + N_EXPERTS * (SLOT_BLK - 1)) // SLOT_BLK

    onehot = (idx_t[:, :, None] == jnp.arange(N_EXPERTS)[None, None, :]).astype(I32)
    slots = jnp.sum(onehot * pad_start[None, None, :], axis=2) + rank_t
    slots3 = slots.reshape(TOP_K, n // SC_CHUNK, SC_CHUNK).transpose(1, 0, 2)

    xs = _dispatch(slots3, hp, n_blocks * SLOT_BLK)
    y = _experts(pad_start.astype(I32), counts, xs, w_exp_gate, w_exp_up, w_exp_down)
    return _combine(wt_t.T, xo, _gather(slots3, y, n))


def kernel(x, norm_mix_g, w_in, q_norm_g, k_norm_g, rpb, conv_dw_w, conv_dw_b, conv_ln_g, conv_ln_b, w_conv_out, w_attn_out, w_o, norm_ffn_g, w_router, router_bias, w_exp_gate, w_exp_up, w_exp_down, w_sh_gate, w_sh_up, w_sh_down):
    b, s, d = x.shape
    assert b == 1 and d == D_MODEL and s % (2 * ATT_BLK) == 0, x.shape
    xf = x.reshape(b * s, d)
    depth = norm_mix_g.shape[0]
    for l in range(depth):
        xf = _layer(xf, norm_mix_g[l], w_in[l], q_norm_g[l], k_norm_g[l], rpb[l], conv_dw_w[l], conv_dw_b[l],
                    conv_ln_g[l], conv_ln_b[l], w_conv_out[l], w_attn_out[l], w_o[l], norm_ffn_g[l],
                    w_router[l], router_bias[l], w_exp_gate[l], w_exp_up[l], w_exp_down[l], w_sh_gate[l],
                    w_sh_up[l], w_sh_down[l])
    return xf.reshape(b, s, d)
```

```python
import functools

import jax
import jax.numpy as jnp
from jax import lax
from jax.experimental import pallas as pl
from jax.experimental.pallas import tpu as pltpu
from jax.experimental.pallas import tpu_sc as plsc

D_MODEL = 1024
GRID_W = 64
D_CONV = 512
CONV_WIDTH = 31
CONV_PAD = CONV_WIDTH // 2
N_HEADS = 8
HEAD_DIM = 64
D_ATTN = N_HEADS * HEAD_DIM
WIN_H = 8
WIN_W = 16
N_EXPERTS = 64
TOP_K = 8
N_GROUPS = 8
GROUP_SIZE = N_EXPERTS // N_GROUPS
TOPK_GROUPS = 4
D_EXPERT = 256
D_SHARED = 256
ROUTE_SCALE = 2.5
EPS = 1e-6
LOG2E = 1.4426950408889634

SUBLANES = 8
NEG = -1e30
HALO = 16
ATT_ROWS = 4
ATT_BLK = ATT_ROWS * GRID_W
SLOT_BLK = 512
DMA_ROWS = 64
SC_CHUNK = 64
VMEM_LIMIT = 56 * 1024 * 1024

F32 = jnp.float32
BF16 = jnp.bfloat16
I32 = jnp.int32
U32 = jnp.uint32


def _cparams(*sem):
    return pltpu.CompilerParams(dimension_semantics=sem, vmem_limit_bytes=VMEM_LIMIT)


def _rms(x, g):
    return x * lax.rsqrt(jnp.mean(x * x, axis=-1, keepdims=True) + EPS) * g


def _pack_bf16_pair(a, b):
    def bf16_bits(v):
        return lax.bitcast_convert_type(v.astype(BF16).astype(F32), U32)
    return bf16_bits(a) | (bf16_bits(b) >> 16)


def _unpack_bf16_pair(w):
    a = lax.bitcast_convert_type(w & jnp.uint32(0xFFFF0000), F32)
    b = lax.bitcast_convert_type(w << 16, F32)
    return a, b


def _inproj_kernel(x_ref, g_ref, w_ref, bsum_ref, qg_ref, kg_ref, u_ref, q_ref, k_ref, v_ref):
    h = _rms(x_ref[...], g_ref[...]).astype(BF16)
    ag = jnp.dot(h, w_ref[:, 0:2 * D_CONV], preferred_element_type=F32)
    u_ref[...] = ag[:, :D_CONV] * jax.nn.sigmoid(ag[:, D_CONV:])
    bsum = bsum_ref[...]

    def head_norm(z, g):
        ss = jnp.dot((z * z).astype(BF16), bsum, preferred_element_type=F32)
        return z * lax.rsqrt(ss * (1.0 / HEAD_DIM) + EPS) * g

    c0 = 2 * D_CONV
    q = jnp.dot(h, w_ref[:, c0:c0 + D_ATTN], preferred_element_type=F32)
    q_ref[...] = (head_norm(q, qg_ref[...]) * (HEAD_DIM ** -0.5 * LOG2E)).astype(BF16)
    k = jnp.dot(h, w_ref[:, c0 + D_ATTN:c0 + 2 * D_ATTN], preferred_element_type=F32)
    k_ref[...] = head_norm(k, kg_ref[...]).astype(BF16)
    v = jnp.dot(h, w_ref[:, c0 + 2 * D_ATTN:c0 + 3 * D_ATTN], preferred_element_type=F32)
    v_ref[...] = v.astype(BF16)


def _inproj(x, g, w, bsum, qg, kg, tm=512):
    n = x.shape[0]
    wc = w.shape[1]
    full = lambda i: (0, 0)
    row = lambda i: (i, 0)
    return pl.pallas_call(
        _inproj_kernel,
        grid=(n // tm,),
        in_specs=[pl.BlockSpec((tm, D_MODEL), row), pl.BlockSpec((1, D_MODEL), full),
                  pl.BlockSpec((D_MODEL, wc), full), pl.BlockSpec((D_ATTN, D_ATTN), full),
                  pl.BlockSpec((1, D_ATTN), full), pl.BlockSpec((1, D_ATTN), full)],
        out_specs=[pl.BlockSpec((tm, D_CONV), row), pl.BlockSpec((tm, D_ATTN), row),
                   pl.BlockSpec((tm, D_ATTN), row), pl.BlockSpec((tm, D_ATTN), row)],
        out_shape=[jax.ShapeDtypeStruct((n, D_CONV), F32), jax.ShapeDtypeStruct((n, D_ATTN), BF16),
                   jax.ShapeDtypeStruct((n, D_ATTN), BF16), jax.ShapeDtypeStruct((n, D_ATTN), BF16)],
        compiler_params=_cparams("parallel"),
        name="inproj",
    )(x, g, w, bsum, qg, kg)


def _conv_kernel(up_ref, uc_ref, un_ref, w_ref, b_ref, lg_ref, lb_ref, o_ref, ext_ref, sh_ref, *, tc, ch):
    i = pl.program_id(0)
    last = pl.num_programs(0) - 1
    ext_ref[0:HALO, :] = jnp.where(i > 0, up_ref[...], 0.0)
    ext_ref[HALO:HALO + tc, :] = uc_ref[...]
    ext_ref[HALO + tc:2 * HALO + tc, :] = jnp.where(i < last, un_ref[...], 0.0)
    span = sh_ref.shape[1]
    for b in range(SUBLANES):
        sh_ref[b] = ext_ref[b:b + span, :]
    bias = b_ref[...]
    lg = lg_ref[...]
    lb = lb_ref[...]
    for c in range(tc // ch):
        acc = jnp.broadcast_to(bias, (ch, D_CONV))
        for j in range(CONV_WIDTH):
            shift = HALO - CONV_PAD + j
            row0 = c * ch + shift // SUBLANES * SUBLANES
            acc = acc + sh_ref[shift % SUBLANES, row0:row0 + ch, :] * w_ref[j:j + 1, :]
        mu = jnp.mean(acc, axis=-1, keepdims=True)
        d = acc - mu
        var = jnp.mean(d * d, axis=-1, keepdims=True)
        y = d * lax.rsqrt(var + EPS) * lg + lb
        o_ref[c * ch:(c + 1) * ch, :] = (y * jax.nn.sigmoid(y)).astype(BF16)


def _conv(u, w, b, lg, lb, tc=256, ch=64):
    n = u.shape[0]
    hb = tc // HALO
    nh = n // HALO
    full = lambda i: (0, 0)
    return pl.pallas_call(
        functools.partial(_conv_kernel, tc=tc, ch=ch),
        grid=(n // tc,),
        in_specs=[pl.BlockSpec((HALO, D_CONV), lambda i: (jnp.maximum(i * hb - 1, 0), 0)),
                  pl.BlockSpec((tc, D_CONV), lambda i: (i, 0)),
                  pl.BlockSpec((HALO, D_CONV), lambda i: (jnp.minimum((i + 1) * hb, nh - 1), 0)),
                  pl.BlockSpec((CONV_WIDTH, D_CONV), full), pl.BlockSpec((1, D_CONV), full),
                  pl.BlockSpec((1, D_CONV), full), pl.BlockSpec((1, D_CONV), full)],
        out_specs=pl.BlockSpec((tc, D_CONV), lambda i: (i, 0)),
        out_shape=jax.ShapeDtypeStruct((n, D_CONV), BF16),
        scratch_shapes=[pltpu.VMEM((tc + 2 * HALO, D_CONV), F32),
                        pltpu.VMEM((SUBLANES, tc + 2 * HALO - SUBLANES, D_CONV), F32)],
        compiler_params=_cparams("parallel"),
        name="conv",
    )(u, u, u, w, b, lg, lb)


def _attn_kernel(q_ref, kp_ref, kc_ref, kn_ref, vp_ref, vc_ref, vn_ref, tb_ref, o_ref, *, rows):
    i = pl.program_id(0)
    nkey = 3 * ATT_BLK
    lrow = lax.broadcasted_iota(I32, (1, nkey), 1) >> 6
    lane = lax.broadcasted_iota(I32, (1, 2 * HEAD_DIM), 1)
    masks = []
    for j in range(ATT_ROWS):
        r = i * ATT_ROWS + j
        rs = jnp.clip(r - WIN_H // 2, 0, rows - WIN_H)
        lo = rs - (i - 1) * ATT_ROWS
        masks.append(jnp.where((lrow >= lo) & (lrow < lo + WIN_H), 0.0, NEG))
    dn = (((1,), (1,)), ((), ()))
    for p in range(N_HEADS // 2):
        sl = slice(2 * HEAD_DIM * p, 2 * HEAD_DIM * (p + 1))
        q2 = q_ref[:, sl]
        ks = (kp_ref[:, sl], kc_ref[:, sl], kn_ref[:, sl])
        vs = (vp_ref[:, sl], vc_ref[:, sl], vn_ref[:, sl])
        out_pair = None
        for half in range(2):
            h = 2 * p + half
            in_half = (lane >= HEAD_DIM * half) & (lane < HEAD_DIM * (half + 1))
            qm = jnp.where(in_half, q2, jnp.zeros_like(q2))
            s = jnp.concatenate([lax.dot_general(qm, kk, dn, preferred_element_type=F32) for kk in ks], axis=1)
            parts = []
            for j in range(ATT_ROWS):
                bias = jnp.concatenate([tb_ref[h, 2 * m - j + 3] for m in range(nkey // (2 * GRID_W))], axis=1)
                parts.append(s[j * GRID_W:(j + 1) * GRID_W, :] + bias + masks[j])
            s = jnp.concatenate(parts, axis=0)
            mx = jnp.max(s, axis=-1, keepdims=True)
            e = jnp.exp2(s - mx)
            den = jnp.sum(e, axis=-1, keepdims=True)
            pb = e.astype(BF16)
            o = sum(jnp.dot(pb[:, ATT_BLK * t:ATT_BLK * (t + 1)], vs[t], preferred_element_type=F32)
                    for t in range(3))
            o = o / den
            out_pair = o if out_pair is None else jnp.where(in_half, o, out_pair)
        o_ref[:, sl] = out_pair.astype(BF16)


def _attn(q, k, v, tb):
    n = q.shape[0]
    rows = n // GRID_W
    nb = n // ATT_BLK
    cur = lambda i: (i, 0)
    prev = lambda i: (jnp.maximum(i - 1, 0), 0)
    nxt = lambda i: (jnp.minimum(i + 1, nb - 1), 0)
    blk = (ATT_BLK, D_ATTN)
    return pl.pallas_call(
        functools.partial(_attn_kernel, rows=rows),
        grid=(nb,),
        in_specs=[pl.BlockSpec(blk, cur), pl.BlockSpec(blk, prev), pl.BlockSpec(blk, cur), pl.BlockSpec(blk, nxt),
                  pl.BlockSpec(blk, prev), pl.BlockSpec(blk, cur), pl.BlockSpec(blk, nxt),
                  pl.BlockSpec(tb.shape, lambda i: (0, 0, 0, 0))],
        out_specs=pl.BlockSpec(blk, cur),
        out_shape=jax.ShapeDtypeStruct((n, D_ATTN), BF16),
        compiler_params=_cparams("parallel"),
        name="attn",
    )(q, k, k, k, v, v, v, tb)


def _bias_table(rpb):
    cols = jnp.arange(GRID_W)
    start = jnp.clip(cols - WIN_W // 2, 0, GRID_W - WIN_W)
    kc = cols[None, :]
    inwin = (kc >= start[:, None]) & (kc < start[:, None] + WIN_W)
    off = GRID_W - WIN_W
    padded = jnp.pad(rpb, ((0, 0), (0, 0), (off, off)))
    rel = jnp.stack([padded[:, :, GRID_W - 1 - c:2 * GRID_W - 1 - c] for c in range(GRID_W)], axis=2)
    t = jnp.where(inwin[None, None], rel * LOG2E, NEG)
    t = jnp.concatenate([t, jnp.full_like(t[:, :1], NEG)], axis=1)
    t_next = jnp.concatenate([t[:, 1:], t[:, -1:]], axis=1)
    return jnp.concatenate([t, t_next], axis=-1).astype(F32)


def _mix_route_kernel(x_ref, uc_ref, at_ref, g1_ref, wg_ref, wco_ref, wao_ref, wo_ref, g2_ref,
                      wrh_ref, wrl_ref, rb_ref, tri_ref, wsgu_ref, wsd_ref,
                      xo_ref, hp_ref, idx_ref, rank_ref, wt_ref, cnt_ref, carry_ref, *, tm):
    i = pl.program_id(0)

    @pl.when(i == 0)
    def _():
        carry_ref[...] = jnp.zeros_like(carry_ref)

    x = x_ref[...]
    h = _rms(x, g1_ref[...]).astype(BF16)
    gates = jax.nn.sigmoid(jnp.dot(h, wg_ref[...], preferred_element_type=F32))
    y_conv = jnp.dot(uc_ref[...], wco_ref[...], preferred_element_type=F32)
    y_attn = jnp.dot(at_ref[...], wao_ref[...], preferred_element_type=F32)
    merged = gates[:, :D_MODEL] * y_conv + gates[:, D_MODEL:] * y_attn
    x1 = x + jnp.dot(merged.astype(BF16), wo_ref[...], preferred_element_type=F32)
    h2 = _rms(x1, g2_ref[...])
    h2b = h2.astype(BF16)
    hp_ref[...] = _pack_bf16_pair(h2[:, :D_MODEL // 2], h2[:, D_MODEL // 2:])

    gu = jnp.dot(h2b, wsgu_ref[...], preferred_element_type=F32)
    mid = (jax.nn.silu(gu[:, :D_SHARED]) * gu[:, D_SHARED:]).astype(BF16)
    xo_ref[...] = x1 + jnp.dot(mid, wsd_ref[...], preferred_element_type=F32)

    h2l = (h2 - h2b.astype(F32)).astype(BF16)
    dn = (((1,), (1,)), ((), ()))
    logits = (lax.dot_general(wrh_ref[...], h2b, dn, preferred_element_type=F32)
              + lax.dot_general(wrl_ref[...], h2b, dn, preferred_element_type=F32)
              + lax.dot_general(wrh_ref[...], h2l, dn, preferred_element_type=F32))
    scores = jax.nn.sigmoid(logits)
    biased = scores + rb_ref[...]

    sub = lax.broadcasted_iota(I32, (GROUP_SIZE, tm), 0).astype(F32)
    groups, gscore = [], []
    for g in range(N_GROUPS):
        bg = biased[g * GROUP_SIZE:(g + 1) * GROUP_SIZE, :]
        m1 = jnp.max(bg, axis=0, keepdims=True)
        first = jnp.min(jnp.where(bg == m1, sub, float(GROUP_SIZE)), axis=0, keepdims=True)
        m2 = jnp.max(jnp.where(sub == first, -jnp.inf, bg), axis=0, keepdims=True)
        groups.append(bg)
        gscore.append(m1 + m2)
    masked = []
    for g in range(N_GROUPS):
        beaten = jnp.zeros((1, tm), F32)
        for o in range(N_GROUPS):
            if o == g:
                continue
            wins = (gscore[o] >= gscore[g]) if o < g else (gscore[o] > gscore[g])
            beaten = beaten + jnp.where(wins, 1.0, 0.0)
        masked.append(jnp.where(beaten < TOPK_GROUPS, groups[g], -jnp.inf))
    cur = jnp.concatenate(masked, axis=0)

    eid = lax.broadcasted_iota(I32, (N_EXPERTS, tm), 0).astype(F32)
    idx_rows, sc_rows = [], []
    sel = jnp.zeros((N_EXPERTS, tm), F32)
    for _ in range(TOP_K):
        m = jnp.max(cur, axis=0, keepdims=True)
        idx = jnp.min(jnp.where(cur == m, eid, float(N_EXPERTS)), axis=0, keepdims=True)
        hit = eid == idx
        sel = jnp.where(hit, 1.0, sel)
        cur = jnp.where(hit, -jnp.inf, cur)
        idx_rows.append(idx)
        sc_rows.append(jnp.sum(jnp.where(hit, scores, 0.0), axis=0, keepdims=True))
    wsum = sc_rows[0]
    for k in range(1, TOP_K):
        wsum = wsum + sc_rows[k]

    prefix = jnp.dot(sel.astype(BF16), tri_ref[...], preferred_element_type=F32)
    rank = carry_ref[:, 0:1] + prefix
    rank_rows = [jnp.sum(jnp.where(eid == idx_rows[k], rank, 0.0), axis=0, keepdims=True) for k in range(TOP_K)]
    carry_ref[...] = carry_ref[...] + jnp.sum(sel, axis=1, keepdims=True)
    cnt_ref[...] = carry_ref[...]

    idx_ref[...] = jnp.concatenate(idx_rows, axis=0).astype(I32)
    rank_ref[...] = jnp.concatenate(rank_rows, axis=0).astype(I32)
    wt_ref[...] = jnp.concatenate([s / wsum * ROUTE_SCALE for s in sc_rows], axis=0)


def _mix_route(x, uc, at, g1, wg, wco, wao, wo, g2, wrh, wrl, rb, wsgu, wsd, tm=512):
    n = x.shape[0]
    tri = (jnp.arange(tm)[:, None] < jnp.arange(tm)[None, :]).astype(BF16)
    row = lambda i: (i, 0)
    col = lambda i: (0, i)
    full = lambda i: (0, 0)
    ins = [x, uc, at, g1, wg, wco, wao, wo, g2, wrh, wrl, rb, tri, wsgu, wsd]
    in_specs = [pl.BlockSpec((tm, D_MODEL), row), pl.BlockSpec((tm, D_CONV), row), pl.BlockSpec((tm, D_ATTN), row)]
    in_specs += [pl.BlockSpec(a.shape, full) for a in ins[3:]]
    return pl.pallas_call(
        functools.partial(_mix_route_kernel, tm=tm),
        grid=(n // tm,),
        in_specs=in_specs,
        out_specs=[pl.BlockSpec((tm, D_MODEL), row), pl.BlockSpec((tm, D_MODEL // 2), row),
                   pl.BlockSpec((TOP_K, tm), col), pl.BlockSpec((TOP_K, tm), col), pl.BlockSpec((TOP_K, tm), col),
                   pl.BlockSpec((N_EXPERTS, 128), full)],
        out_shape=[jax.ShapeDtypeStruct((n, D_MODEL), F32), jax.ShapeDtypeStruct((n, D_MODEL // 2), U32),
                   jax.ShapeDtypeStruct((TOP_K, n), I32), jax.ShapeDtypeStruct((TOP_K, n), I32),
                   jax.ShapeDtypeStruct((TOP_K, n), F32), jax.ShapeDtypeStruct((N_EXPERTS, 128), F32)],
        scratch_shapes=[pltpu.VMEM((N_EXPERTS, 128), F32)],
        compiler_params=_cparams("arbitrary"),
        name="mix_route",
    )(*ins)


def _sc_workers():
    info = plsc.get_sparse_core_info()
    return info.num_cores, info.num_subcores


def _sc_mesh():
    return plsc.VectorSubcoreMesh(core_axis_name="c", subcore_axis_name="s")


def _worker_id(n_cores):
    return lax.axis_index("s") * n_cores + lax.axis_index("c")


def _dispatch(slots3, hp, n_slots):
    n_chunks = slots3.shape[0]
    n_cores, n_sub = _sc_workers()
    per_worker = n_chunks // (n_cores * n_sub)
    assert per_worker * n_cores * n_sub == n_chunks

    @functools.partial(
        pl.kernel, mesh=_sc_mesh(),
        out_type=jax.ShapeDtypeStruct((n_slots, D_MODEL // 2), U32),
        scratch_types=[pltpu.VMEM((TOP_K, SC_CHUNK), I32), pltpu.VMEM((SC_CHUNK, D_MODEL // 2), U32),
                       pltpu.SemaphoreType.DMA],
        name="dispatch")
    def run(slots_hbm, hp_hbm, xs_hbm, idx_v, rows_v, sem):
        first = _worker_id(n_cores) * per_worker

        @pl.loop(0, per_worker)
        def _(j):
            c = first + j
            pltpu.sync_copy(slots_hbm.at[c], idx_v)
            pltpu.sync_copy(hp_hbm.at[pl.ds(c * SC_CHUNK, SC_CHUNK)], rows_v)
            copies = [pltpu.async_copy(rows_v, xs_hbm.at[idx_v.at[k]], sem) for k in range(TOP_K)]
            for cp in copies:
                cp.wait()

    return run(slots3, hp)


class _CopyGroup:
    def __init__(self, copies):
        self.copies = copies

    def start(self):
        for i, cp in enumerate(self.copies):
            cp.start(priority=i % 2)

    def wait(self):
        for cp in self.copies:
            cp.wait()


def _experts_kernel(start_ref, count_ref, wg_ref, wu_ref, wd_ref, xs_ref, y_ref,
                    xbuf, ybuf, wgu_s, wd_s, in_sem, out_sem):
    e = pl.program_id(0)
    start = start_ref[e]
    count = count_ref[e]
    n_chunks = (count + SLOT_BLK - 1) // SLOT_BLK

    pieces = range(SLOT_BLK // DMA_ROWS)

    def hbm_rows(c, p):
        return pl.ds(pl.multiple_of(start + c * SLOT_BLK + p * DMA_ROWS, DMA_ROWS), DMA_ROWS)

    def in_copy(c, slot):
        return _CopyGroup([pltpu.make_async_copy(xs_ref.at[hbm_rows(c, p)],
                                                 xbuf.at[slot, pl.ds(p * DMA_ROWS, DMA_ROWS)], in_sem.at[slot])
                           for p in pieces])

    def out_copy(c, slot):
        return _CopyGroup([pltpu.make_async_copy(ybuf.at[slot, pl.ds(p * DMA_ROWS, DMA_ROWS)],
                                                 y_ref.at[hbm_rows(c, p)], out_sem.at[slot])
                           for p in pieces])

    @pl.when(n_chunks > 0)
    def _():
        in_copy(0, 0).start()
        wgu_s[:, :D_EXPERT] = wg_ref[0].astype(BF16)
        wgu_s[:, D_EXPERT:] = wu_ref[0].astype(BF16)
        wd_s[...] = wd_ref[0].astype(BF16)
        srow = lax.broadcasted_iota(I32, (SLOT_BLK, 1), 0)

        def chunk(c, carry):
            slot = c & 1
            in_copy(c, slot).wait()

            @pl.when(c + 1 < n_chunks)
            def _():
                in_copy(c + 1, 1 - slot).start()

            @pl.when(c >= 2)
            def _():
                out_copy(c - 2, slot).wait()

            xw = jnp.where(srow < count - c * SLOT_BLK, xbuf[slot], jnp.uint32(0))
            xa, xb = _unpack_bf16_pair(xw)
            x = jnp.concatenate([xa.astype(BF16), xb.astype(BF16)], axis=1)
            gu = jnp.dot(x, wgu_s[...], preferred_element_type=F32)
            mid = (jax.nn.silu(gu[:, :D_EXPERT]) * gu[:, D_EXPERT:]).astype(BF16)
            y = jnp.dot(mid, wd_s[...], preferred_element_type=F32)
            ybuf[slot] = _pack_bf16_pair(y[:, :D_MODEL // 2], y[:, D_MODEL // 2:])
            out_copy(c, slot).start()
            return carry

        lax.fori_loop(0, n_chunks, chunk, 0)

        @pl.when(n_chunks >= 2)
        def _():
            out_copy(n_chunks - 2, n_chunks & 1).wait()

        out_copy(n_chunks - 1, (n_chunks - 1) & 1).wait()


def _experts(pad_start, counts, xs, wg, wu, wd):
    wmap = lambda e, st, ct: (e, 0, 0)
    buf = pltpu.VMEM((2, SLOT_BLK, D_MODEL // 2), U32)
    grid_spec = pltpu.PrefetchScalarGridSpec(
        num_scalar_prefetch=2,
        grid=(N_EXPERTS,),
        in_specs=[pl.BlockSpec((1, D_MODEL, D_EXPERT), wmap),
                  pl.BlockSpec((1, D_MODEL, D_EXPERT), wmap),
                  pl.BlockSpec((1, D_EXPERT, D_MODEL), wmap),
                  pl.BlockSpec(memory_space=pl.ANY)],
        out_specs=pl.BlockSpec(memory_space=pl.ANY),
        scratch_shapes=[buf, buf, pltpu.VMEM((D_MODEL, 2 * D_EXPERT), BF16), pltpu.VMEM((D_EXPERT, D_MODEL), BF16),
                        pltpu.SemaphoreType.DMA((2,)), pltpu.SemaphoreType.DMA((2,))],
    )
    return pl.pallas_call(
        _experts_kernel,
        grid_spec=grid_spec,
        out_shape=jax.ShapeDtypeStruct(xs.shape, U32),
        compiler_params=_cparams("arbitrary"),
        name="experts",
    )(pad_start, counts, wg, wu, wd, xs)


def _gather(slots3, y, n):
    n_chunks = slots3.shape[0]
    n_cores, n_sub = _sc_workers()
    per_worker = n_chunks // (n_cores * n_sub)
    assert per_worker * n_cores * n_sub == n_chunks

    @functools.partial(
        pl.kernel, mesh=_sc_mesh(),
        out_type=jax.ShapeDtypeStruct((TOP_K, n, D_MODEL // 2), U32),
        scratch_types=[pltpu.VMEM((TOP_K, SC_CHUNK), I32),
                       pltpu.VMEM((SC_CHUNK, D_MODEL // 2), U32), pltpu.VMEM((SC_CHUNK, D_MODEL // 2), U32),
                       pltpu.SemaphoreType.DMA, pltpu.SemaphoreType.DMA, pltpu.SemaphoreType.DMA],
        name="gather")
    def run(slots_hbm, y_hbm, g_hbm, idx_v, buf0, buf1, gsem, wsem0, wsem1):
        first = _worker_id(n_cores) * per_worker
        bufs = (buf0, buf1)
        wsems = (wsem0, wsem1)

        @pl.loop(0, per_worker)
        def _(j):
            c = first + j
            pltpu.sync_copy(slots_hbm.at[c], idx_v)
            gathers = [None] * TOP_K
            writes = [None] * TOP_K
            gathers[0] = pltpu.async_copy(y_hbm.at[idx_v.at[0]], bufs[0], gsem)
            for k in range(TOP_K):
                gathers[k].wait()
                if k >= 1:
                    writes[k - 1].wait()
                if k + 1 < TOP_K:
                    gathers[k + 1] = pltpu.async_copy(y_hbm.at[idx_v.at[k + 1]], bufs[(k + 1) % 2], gsem)
                writes[k] = pltpu.async_copy(bufs[k % 2], g_hbm.at[k, pl.ds(c * SC_CHUNK, SC_CHUNK)],
                                             wsems[k % 2])
            writes[TOP_K - 1].wait()

    return run(slots3, y)


def _combine_kernel(wt_ref, xo_ref, g_ref, o_ref):
    half = D_MODEL // 2
    acc_a = xo_ref[:, :half]
    acc_b = xo_ref[:, half:]
    for k in range(TOP_K):
        ya, yb = _unpack_bf16_pair(g_ref[k])
        w = wt_ref[:, k:k + 1]
        acc_a = acc_a + w * ya
        acc_b = acc_b + w * yb
    o_ref[:, :half] = acc_a
    o_ref[:, half:] = acc_b


def _combine(wt, xo, g, tm=256):
    n = xo.shape[0]
    return pl.pallas_call(
        _combine_kernel,
        grid=(n // tm,),
        in_specs=[pl.BlockSpec((tm, TOP_K), lambda i: (i, 0)),
                  pl.BlockSpec((tm, D_MODEL), lambda i: (i, 0)),
                  pl.BlockSpec((TOP_K, tm, D_MODEL // 2), lambda i: (0, i, 0))],
        out_specs=pl.BlockSpec((tm, D_MODEL), lambda i: (i, 0)),
        out_shape=jax.ShapeDtypeStruct((n, D_MODEL), F32),
        compiler_params=_cparams("parallel"),
        name="combine",
    )(wt, xo, g)


def _layer(x, norm_mix_g, w_in, q_norm_g, k_norm_g, rpb, conv_dw_w, conv_dw_b, conv_ln_g, conv_ln_b,
           w_conv_out, w_attn_out, w_o, norm_ffn_g, w_router, router_bias, w_exp_gate, w_exp_up,
           w_exp_down, w_sh_gate, w_sh_up, w_sh_down):
    n = x.shape[0]
    c_qkv = 2 * D_CONV + 3 * D_ATTN
    w_in_b = w_in.astype(BF16)
    head_of = jnp.arange(D_ATTN) // HEAD_DIM
    bsum = (head_of[:, None] == head_of[None, :]).astype(BF16)
    row = lambda v: v.reshape(1, -1).astype(F32)

    u, q, k, v = _inproj(x, row(norm_mix_g), w_in_b[:, :c_qkv], bsum,
                         row(jnp.tile(q_norm_g, N_HEADS)), row(jnp.tile(k_norm_g, N_HEADS)))
    uc = _conv(u, conv_dw_w.reshape(CONV_WIDTH, D_CONV), row(conv_dw_b), row(conv_ln_g), row(conv_ln_b))
    at = _attn(q, k, v, _bias_table(rpb))

    wr_t = w_router.T
    wr_hi = wr_t.astype(BF16)
    wr_lo = (wr_t - wr_hi.astype(F32)).astype(BF16)
    wsgu = jnp.concatenate([w_sh_gate, w_sh_up], axis=1).astype(BF16)
    xo, hp, idx_t, rank_t, wt_t, cnt = _mix_route(
        x, uc, at, row(norm_mix_g), w_in_b[:, c_qkv:], w_conv_out.astype(BF16), w_attn_out.astype(BF16),
        w_o.astype(BF16), row(norm_ffn_g), wr_hi, wr_lo, router_bias.reshape(N_EXPERTS, 1).astype(F32),
        wsgu, w_sh_down.astype(BF16))

    counts = cnt[:, 0].astype(I32)
    padded = (counts + SLOT_BLK - 1) // SLOT_BLK * SLOT_BLK
    pad_end = jnp.cumsum(padded)
    pad_start = pad_end - padded
    n_blocks = (n * TOP_K + N_EXPERTS * (SLOT_BLK - 1)) // SLOT_BLK

    onehot = (idx_t[:, :, None] == jnp.arange(N_EXPERTS)[None, None, :]).astype(I32)
    slots = jnp.sum(onehot * pad_start[None, None, :], axis=2) + rank_t
    slots3 = slots.reshape(TOP_K, n // SC_CHUNK, SC_CHUNK).transpose(1, 0, 2)

    xs = _dispatch(slots3, hp, n_blocks * SLOT_BLK)
    y = _experts(pad_start.astype(I32), counts, xs, w_exp_gate, w_exp_up, w_exp_down)
    return _combine(wt_t.T, xo, _gather(slots3, y, n))


def kernel(x, norm_mix_g, w_in, q_norm_g, k_norm_g, rpb, conv_dw_w, conv_dw_b, conv_ln_g, conv_ln_b, w_conv_out, w_attn_out, w_o, norm_ffn_g, w_router, router_bias, w_exp_gate, w_exp_up, w_exp_down, w_sh_gate, w_sh_up, w_sh_down):
    b, s, d = x.shape
    assert b == 1 and d == D_MODEL and s % (2 * ATT_BLK) == 0, x.shape
    xf = x.reshape(b * s, d)
    depth = norm_mix_g.shape[0]
    for l in range(depth):
        xf = _layer(xf, norm_mix_g[l], w_in[l], q_norm_g[l], k_norm_g[l], rpb[l], conv_dw_w[l], conv_dw_b[l],
                    conv_ln_g[l], conv_ln_b[l], w_conv_out[l], w_attn_out[l], w_o[l], norm_ffn_g[l],
                    w_router[l], router_bias[l], w_exp_gate[l], w_exp_up[l], w_exp_down[l], w_sh_gate[l],
                    w_sh_up[l], w_sh_down[l])
    return xf.reshape(b, s, d)
```

```python
import functools

import jax
import jax.numpy as jnp
from jax import lax
from jax.experimental import pallas as pl
from jax.experimental.pallas import tpu as pltpu
from jax.experimental.pallas import tpu_sc as plsc

D_MODEL = 1024
GRID_W = 64
D_CONV = 512
CONV_WIDTH = 31
CONV_PAD = CONV_WIDTH // 2
N_HEADS = 8
HEAD_DIM = 64
D_ATTN = N_HEADS * HEAD_DIM
WIN_H = 8
WIN_W = 16
N_EXPERTS = 64
TOP_K = 8
N_GROUPS = 8
GROUP_SIZE = N_EXPERTS // N_GROUPS
TOPK_GROUPS = 4
D_EXPERT = 256
D_SHARED = 256
ROUTE_SCALE = 2.5
EPS = 1e-6
LOG2E = 1.4426950408889634

SUBLANES = 8
NEG = -1e30
HALO = 16
ATT_ROWS = 4
ATT_BLK = ATT_ROWS * GRID_W
SLOT_BLK = 512
SC_CHUNK = 64
VMEM_LIMIT = 56 * 1024 * 1024

F32 = jnp.float32
BF16 = jnp.bfloat16
I32 = jnp.int32
U32 = jnp.uint32


def _cparams(*sem):
    return pltpu.CompilerParams(dimension_semantics=sem, vmem_limit_bytes=VMEM_LIMIT)


def _rms(x, g):
    return x * lax.rsqrt(jnp.mean(x * x, axis=-1, keepdims=True) + EPS) * g


def _pack_bf16_pair(a, b):
    def bf16_bits(v):
        return lax.bitcast_convert_type(v.astype(BF16).astype(F32), U32)
    return bf16_bits(a) | (bf16_bits(b) >> 16)


def _unpack_bf16_pair(w):
    a = lax.bitcast_convert_type(w & jnp.uint32(0xFFFF0000), F32)
    b = lax.bitcast_convert_type(w << 16, F32)
    return a, b


def _inproj_kernel(x_ref, g_ref, w_ref, bsum_ref, qg_ref, kg_ref, u_ref, q_ref, k_ref, v_ref):
    h = _rms(x_ref[...], g_ref[...]).astype(BF16)
    ag = jnp.dot(h, w_ref[:, 0:2 * D_CONV], preferred_element_type=F32)
    u_ref[...] = ag[:, :D_CONV] * jax.nn.sigmoid(ag[:, D_CONV:])
    bsum = bsum_ref[...]

    def head_norm(z, g):
        ss = jnp.dot((z * z).astype(BF16), bsum, preferred_element_type=F32)
        return z * lax.rsqrt(ss * (1.0 / HEAD_DIM) + EPS) * g

    c0 = 2 * D_CONV
    q = jnp.dot(h, w_ref[:, c0:c0 + D_ATTN], preferred_element_type=F32)
    q_ref[...] = (head_norm(q, qg_ref[...]) * (HEAD_DIM ** -0.5 * LOG2E)).astype(BF16)
    k = jnp.dot(h, w_ref[:, c0 + D_ATTN:c0 + 2 * D_ATTN], preferred_element_type=F32)
    k_ref[...] = head_norm(k, kg_ref[...]).astype(BF16)
    v = jnp.dot(h, w_ref[:, c0 + 2 * D_ATTN:c0 + 3 * D_ATTN], preferred_element_type=F32)
    v_ref[...] = v.astype(BF16)


def _inproj(x, g, w, bsum, qg, kg, tm=512):
    n = x.shape[0]
    wc = w.shape[1]
    full = lambda i: (0, 0)
    row = lambda i: (i, 0)
    return pl.pallas_call(
        _inproj_kernel,
        grid=(n // tm,),
        in_specs=[pl.BlockSpec((tm, D_MODEL), row), pl.BlockSpec((1, D_MODEL), full),
                  pl.BlockSpec((D_MODEL, wc), full), pl.BlockSpec((D_ATTN, D_ATTN), full),
                  pl.BlockSpec((1, D_ATTN), full), pl.BlockSpec((1, D_ATTN), full)],
        out_specs=[pl.BlockSpec((tm, D_CONV), row), pl.BlockSpec((tm, D_ATTN), row),
                   pl.BlockSpec((tm, D_ATTN), row), pl.BlockSpec((tm, D_ATTN), row)],
        out_shape=[jax.ShapeDtypeStruct((n, D_CONV), F32), jax.ShapeDtypeStruct((n, D_ATTN), BF16),
                   jax.ShapeDtypeStruct((n, D_ATTN), BF16), jax.ShapeDtypeStruct((n, D_ATTN), BF16)],
        compiler_params=_cparams("parallel"),
        name="inproj",
    )(x, g, w, bsum, qg, kg)


def _conv_kernel(up_ref, uc_ref, un_ref, w_ref, b_ref, lg_ref, lb_ref, o_ref, ext_ref, sh_ref, *, tc, ch):
    i = pl.program_id(0)
    last = pl.num_programs(0) - 1
    ext_ref[0:HALO, :] = jnp.where(i > 0, up_ref[...], 0.0)
    ext_ref[HALO:HALO + tc, :] = uc_ref[...]
    ext_ref[HALO + tc:2 * HALO + tc, :] = jnp.where(i < last, un_ref[...], 0.0)
    span = sh_ref.shape[1]
    for b in range(SUBLANES):
        sh_ref[b] = ext_ref[b:b + span, :]
    bias = b_ref[...]
    lg = lg_ref[...]
    lb = lb_ref[...]
    for c in range(tc // ch):
        acc = jnp.broadcast_to(bias, (ch, D_CONV))
        for j in range(CONV_WIDTH):
            shift = HALO - CONV_PAD + j
            row0 = c * ch + shift // SUBLANES * SUBLANES
            acc = acc + sh_ref[shift % SUBLANES, row0:row0 + ch, :] * w_ref[j:j + 1, :]
        mu = jnp.mean(acc, axis=-1, keepdims=True)
        d = acc - mu
        var = jnp.mean(d * d, axis=-1, keepdims=True)
        y = d * lax.rsqrt(var + EPS) * lg + lb
        o_ref[c * ch:(c + 1) * ch, :] = (y * jax.nn.sigmoid(y)).astype(BF16)


def _conv(u, w, b, lg, lb, tc=256, ch=64):
    n = u.shape[0]
    hb = tc // HALO
    nh = n // HALO
    full = lambda i: (0, 0)
    return pl.pallas_call(
        functools.partial(_conv_kernel, tc=tc, ch=ch),
        grid=(n // tc,),
        in_specs=[pl.BlockSpec((HALO, D_CONV), lambda i: (jnp.maximum(i * hb - 1, 0), 0)),
                  pl.BlockSpec((tc, D_CONV), lambda i: (i, 0)),
                  pl.BlockSpec((HALO, D_CONV), lambda i: (jnp.minimum((i + 1) * hb, nh - 1), 0)),
                  pl.BlockSpec((CONV_WIDTH, D_CONV), full), pl.BlockSpec((1, D_CONV), full),
                  pl.BlockSpec((1, D_CONV), full), pl.BlockSpec((1, D_CONV), full)],
        out_specs=pl.BlockSpec((tc, D_CONV), lambda i: (i, 0)),
        out_shape=jax.ShapeDtypeStruct((n, D_CONV), BF16),
        scratch_shapes=[pltpu.VMEM((tc + 2 * HALO, D_CONV), F32),
                        pltpu.VMEM((SUBLANES, tc + 2 * HALO - SUBLANES, D_CONV), F32)],
        compiler_params=_cparams("parallel"),
        name="conv",
    )(u, u, u, w, b, lg, lb)


def _attn_kernel(q_ref, kp_ref, kc_ref, kn_ref, vp_ref, vc_ref, vn_ref, tb_ref, o_ref, *, rows):
    i = pl.program_id(0)
    nkey = 3 * ATT_BLK
    lrow = lax.broadcasted_iota(I32, (1, nkey), 1) >> 6
    lane = lax.broadcasted_iota(I32, (1, 2 * HEAD_DIM), 1)
    masks = []
    for j in range(ATT_ROWS):
        r = i * ATT_ROWS + j
        rs = jnp.clip(r - WIN_H // 2, 0, rows - WIN_H)
        lo = rs - (i - 1) * ATT_ROWS
        masks.append(jnp.where((lrow >= lo) & (lrow < lo + WIN_H), 0.0, NEG))
    dn = (((1,), (1,)), ((), ()))
    for p in range(N_HEADS // 2):
        sl = slice(2 * HEAD_DIM * p, 2 * HEAD_DIM * (p + 1))
        q2 = q_ref[:, sl]
        ks = (kp_ref[:, sl], kc_ref[:, sl], kn_ref[:, sl])
        vs = (vp_ref[:, sl], vc_ref[:, sl], vn_ref[:, sl])
        out_pair = None
        for half in range(2):
            h = 2 * p + half
            in_half = (lane >= HEAD_DIM * half) & (lane < HEAD_DIM * (half + 1))
            qm = jnp.where(in_half, q2, jnp.zeros_like(q2))
            s = jnp.concatenate([lax.dot_general(qm, kk, dn, preferred_element_type=F32) for kk in ks], axis=1)
            parts = []
            for j in range(ATT_ROWS):
                bias = jnp.concatenate([tb_ref[h, 2 * m - j + 3] for m in range(nkey // (2 * GRID_W))], axis=1)
                parts.append(s[j * GRID_W:(j + 1) * GRID_W, :] + bias + masks[j])
            s = jnp.concatenate(parts, axis=0)
            mx = jnp.max(s, axis=-1, keepdims=True)
            e = jnp.exp2(s - mx)
            den = jnp.sum(e, axis=-1, keepdims=True)
            pb = e.astype(BF16)
            o = sum(jnp.dot(pb[:, ATT_BLK * t:ATT_BLK * (t + 1)], vs[t], preferred_element_type=F32)
                    for t in range(3))
            o = o / den
            out_pair = o if out_pair is None else jnp.where(in_half, o, out_pair)
        o_ref[:, sl] = out_pair.astype(BF16)


def _attn(q, k, v, tb):
    n = q.shape[0]
    rows = n // GRID_W
    nb = n // ATT_BLK
    cur = lambda i: (i, 0)
    prev = lambda i: (jnp.maximum(i - 1, 0), 0)
    nxt = lambda i: (jnp.minimum(i + 1, nb - 1), 0)
    blk = (ATT_BLK, D_ATTN)
    return pl.pallas_call(
        functools.partial(_attn_kernel, rows=rows),
        grid=(nb,),
        in_specs=[pl.BlockSpec(blk, cur), pl.BlockSpec(blk, prev), pl.BlockSpec(blk, cur), pl.BlockSpec(blk, nxt),
                  pl.BlockSpec(blk, prev), pl.BlockSpec(blk, cur), pl.BlockSpec(blk, nxt),
                  pl.BlockSpec(tb.shape, lambda i: (0, 0, 0, 0))],
        out_specs=pl.BlockSpec(blk, cur),
        out_shape=jax.ShapeDtypeStruct((n, D_ATTN), BF16),
        compiler_params=_cparams("parallel"),
        name="attn",
    )(q, k, k, k, v, v, v, tb)


def _bias_table(rpb):
    cols = jnp.arange(GRID_W)
    start = jnp.clip(cols - WIN_W // 2, 0, GRID_W - WIN_W)
    kc = cols[None, :]
    inwin = (kc >= start[:, None]) & (kc < start[:, None] + WIN_W)
    off = GRID_W - WIN_W
    padded = jnp.pad(rpb, ((0, 0), (0, 0), (off, off)))
    rel = jnp.stack([padded[:, :, GRID_W - 1 - c:2 * GRID_W - 1 - c] for c in range(GRID_W)], axis=2)
    t = jnp.where(inwin[None, None], rel * LOG2E, NEG)
    t = jnp.concatenate([t, jnp.full_like(t[:, :1], NEG)], axis=1)
    t_next = jnp.concatenate([t[:, 1:], t[:, -1:]], axis=1)
    return jnp.concatenate([t, t_next], axis=-1).astype(F32)


def _mix_route_kernel(x_ref, uc_ref, at_ref, g1_ref, wg_ref, wco_ref, wao_ref, wo_ref, g2_ref,
                      wrh_ref, wrl_ref, rb_ref, tri_ref, wsgu_ref, wsd_ref,
                      xo_ref, hp_ref, idx_ref, rank_ref, wt_ref, cnt_ref, carry_ref, *, tm):
    i = pl.program_id(0)

    @pl.when(i == 0)
    def _():
        carry_ref[...] = jnp.zeros_like(carry_ref)

    x = x_ref[...]
    h = _rms(x, g1_ref[...]).astype(BF16)
    gates = jax.nn.sigmoid(jnp.dot(h, wg_ref[...], preferred_element_type=F32))
    y_conv = jnp.dot(uc_ref[...], wco_ref[...], preferred_element_type=F32)
    y_attn = jnp.dot(at_ref[...], wao_ref[...], preferred_element_type=F32)
    merged = gates[:, :D_MODEL] * y_conv + gates[:, D_MODEL:] * y_attn
    x1 = x + jnp.dot(merged.astype(BF16), wo_ref[...], preferred_element_type=F32)
    h2 = _rms(x1, g2_ref[...])
    h2b = h2.astype(BF16)
    hp_ref[...] = _pack_bf16_pair(h2[:, :D_MODEL // 2], h2[:, D_MODEL // 2:])

    gu = jnp.dot(h2b, wsgu_ref[...], preferred_element_type=F32)
    mid = (jax.nn.silu(gu[:, :D_SHARED]) * gu[:, D_SHARED:]).astype(BF16)
    xo_ref[...] = x1 + jnp.dot(mid, wsd_ref[...], preferred_element_type=F32)

    h2l = (h2 - h2b.astype(F32)).astype(BF16)
    dn = (((1,), (1,)), ((), ()))
    logits = (lax.dot_general(wrh_ref[...], h2b, dn, preferred_element_type=F32)
              + lax.dot_general(wrl_ref[...], h2b, dn, preferred_element_type=F32)
              + lax.dot_general(wrh_ref[...], h2l, dn, preferred_element_type=F32))
    scores = jax.nn.sigmoid(logits)
    biased = scores + rb_ref[...]

    sub = lax.broadcasted_iota(I32, (GROUP_SIZE, tm), 0).astype(F32)
    groups, gscore = [], []
    for g in range(N_GROUPS):
        bg = biased[g * GROUP_SIZE:(g + 1) * GROUP_SIZE, :]
        m1 = jnp.max(bg, axis=0, keepdims=True)
        first = jnp.min(jnp.where(bg == m1, sub, float(GROUP_SIZE)), axis=0, keepdims=True)
        m2 = jnp.max(jnp.where(sub == first, -jnp.inf, bg), axis=0, keepdims=True)
        groups.append(bg)
        gscore.append(m1 + m2)
    masked = []
    for g in range(N_GROUPS):
        beaten = jnp.zeros((1, tm), F32)
        for o in range(N_GROUPS):
            if o == g:
                continue
            wins = (gscore[o] >= gscore[g]) if o < g else (gscore[o] > gscore[g])
            beaten = beaten + jnp.where(wins, 1.0, 0.0)
        masked.append(jnp.where(beaten < TOPK_GROUPS, groups[g], -jnp.inf))
    cur = jnp.concatenate(masked, axis=0)

    eid = lax.broadcasted_iota(I32, (N_EXPERTS, tm), 0).astype(F32)
    idx_rows, sc_rows = [], []
    sel = jnp.zeros((N_EXPERTS, tm), F32)
    for _ in range(TOP_K):
        m = jnp.max(cur, axis=0, keepdims=True)
        idx = jnp.min(jnp.where(cur == m, eid, float(N_EXPERTS)), axis=0, keepdims=True)
        hit = eid == idx
        sel = jnp.where(hit, 1.0, sel)
        cur = jnp.where(hit, -jnp.inf, cur)
        idx_rows.append(idx)
        sc_rows.append(jnp.sum(jnp.where(hit, scores, 0.0), axis=0, keepdims=True))
    wsum = sc_rows[0]
    for k in range(1, TOP_K):
        wsum = wsum + sc_rows[k]

    prefix = jnp.dot(sel.astype(BF16), tri_ref[...], preferred_element_type=F32)
    rank = carry_ref[:, 0:1] + prefix
    rank_rows = [jnp.sum(jnp.where(eid == idx_rows[k], rank, 0.0), axis=0, keepdims=True) for k in range(TOP_K)]
    carry_ref[...] = carry_ref[...] + jnp.sum(sel, axis=1, keepdims=True)
    cnt_ref[...] = carry_ref[...]

    idx_ref[...] = jnp.concatenate(idx_rows, axis=0).astype(I32)
    rank_ref[...] = jnp.concatenate(rank_rows, axis=0).astype(I32)
    wt_ref[...] = jnp.concatenate([s / wsum * ROUTE_SCALE for s in sc_rows], axis=0)


def _mix_route(x, uc, at, g1, wg, wco, wao, wo, g2, wrh, wrl, rb, wsgu, wsd, tm=512):
    n = x.shape[0]
    tri = (jnp.arange(tm)[:, None] < jnp.arange(tm)[None, :]).astype(BF16)
    row = lambda i: (i, 0)
    col = lambda i: (0, i)
    full = lambda i: (0, 0)
    ins = [x, uc, at, g1, wg, wco, wao, wo, g2, wrh, wrl, rb, tri, wsgu, wsd]
    in_specs = [pl.BlockSpec((tm, D_MODEL), row), pl.BlockSpec((tm, D_CONV), row), pl.BlockSpec((tm, D_ATTN), row)]
    in_specs += [pl.BlockSpec(a.shape, full) for a in ins[3:]]
    return pl.pallas_call(
        functools.partial(_mix_route_kernel, tm=tm),
        grid=(n // tm,),
        in_specs=in_specs,
        out_specs=[pl.BlockSpec((tm, D_MODEL), row), pl.BlockSpec((tm, D_MODEL // 2), row),
                   pl.BlockSpec((TOP_K, tm), col), pl.BlockSpec((TOP_K, tm), col), pl.BlockSpec((TOP_K, tm), col),
                   pl.BlockSpec((N_EXPERTS, 128), full)],
        out_shape=[jax.ShapeDtypeStruct((n, D_MODEL), F32), jax.ShapeDtypeStruct((n, D_MODEL // 2), U32),
                   jax.ShapeDtypeStruct((TOP_K, n), I32), jax.ShapeDtypeStruct((TOP_K, n), I32),
                   jax.ShapeDtypeStruct((TOP_K, n), F32), jax.ShapeDtypeStruct((N_EXPERTS, 128), F32)],
        scratch_shapes=[pltpu.VMEM((N_EXPERTS, 128), F32)],
        compiler_params=_cparams("arbitrary"),
        name="mix_route",
    )(*ins)


def _sc_workers():
    info = plsc.get_sparse_core_info()
    return info.num_cores, info.num_subcores


def _sc_mesh():
    return plsc.VectorSubcoreMesh(core_axis_name="c", subcore_axis_name="s")


def _worker_id(n_cores):
    return lax.axis_index("s") * n_cores + lax.axis_index("c")


def _dispatch(slots3, hp, n_slots):
    n_chunks = slots3.shape[0]
    n_cores, n_sub = _sc_workers()
    per_worker = n_chunks // (n_cores * n_sub)
    assert per_worker * n_cores * n_sub == n_chunks

    @functools.partial(
        pl.kernel, mesh=_sc_mesh(),
        out_type=jax.ShapeDtypeStruct((n_slots, D_MODEL // 2), U32),
        scratch_types=[pltpu.VMEM((TOP_K, SC_CHUNK), I32), pltpu.VMEM((SC_CHUNK, D_MODEL // 2), U32),
                       pltpu.SemaphoreType.DMA],
        name="dispatch")
    def run(slots_hbm, hp_hbm, xs_hbm, idx_v, rows_v, sem):
        first = _worker_id(n_cores) * per_worker

        @pl.loop(0, per_worker)
        def _(j):
            c = first + j
            pltpu.sync_copy(slots_hbm.at[c], idx_v)
            pltpu.sync_copy(hp_hbm.at[pl.ds(c * SC_CHUNK, SC_CHUNK)], rows_v)
            copies = [pltpu.async_copy(rows_v, xs_hbm.at[idx_v.at[k]], sem) for k in range(TOP_K)]
            for cp in copies:
                cp.wait()

    return run(slots3, hp)


def _experts_kernel(bexp_ref, bidx_ref, bval_ref, nused_ref, next_ref, wslot_ref,
                    xs_ref, wg_ref, wu_ref, wd_ref, y_ref, wg_buf, wu_buf, wd_buf, wgu_s, wd_s, wsem):
    b = pl.program_id(0)

    def weight_copies(e, slot):
        return [pltpu.make_async_copy(wg_ref.at[e], wg_buf.at[slot], wsem.at[slot]),
                pltpu.make_async_copy(wu_ref.at[e], wu_buf.at[slot], wsem.at[slot]),
                pltpu.make_async_copy(wd_ref.at[e], wd_buf.at[slot], wsem.at[slot])]

    @pl.when(b < nused_ref[0])
    def _():
        e = bexp_ref[b]
        slot = wslot_ref[e]

        @pl.when(b == 0)
        def _():
            for cp in weight_copies(e, slot):
                cp.start()

        @pl.when((b == 0) | (bexp_ref[jnp.maximum(b - 1, 0)] != e))
        def _():
            for cp in weight_copies(e, slot):
                cp.wait()
            wgu_s[:, :D_EXPERT] = wg_buf[slot].astype(BF16)
            wgu_s[:, D_EXPERT:] = wu_buf[slot].astype(BF16)
            wd_s[...] = wd_buf[slot].astype(BF16)
            nxt = next_ref[e]

            @pl.when(nxt >= 0)
            def _():
                for cp in weight_copies(nxt, 1 - slot):
                    cp.start()

        srow = lax.broadcasted_iota(I32, (SLOT_BLK, 1), 0)
        xw = jnp.where(srow < bval_ref[b], xs_ref[...], jnp.uint32(0))
        xa, xb = _unpack_bf16_pair(xw)
        x = jnp.concatenate([xa.astype(BF16), xb.astype(BF16)], axis=1)
        gu = jnp.dot(x, wgu_s[...], preferred_element_type=F32)
        mid = (jax.nn.silu(gu[:, :D_EXPERT]) * gu[:, D_EXPERT:]).astype(BF16)
        y = jnp.dot(mid, wd_s[...], preferred_element_type=F32)
        y_ref[...] = _pack_bf16_pair(y[:, :D_MODEL // 2], y[:, D_MODEL // 2:])


def _experts(bexp, bidx, bval, nused, next_exp, wslot, xs, wg, wu, wd):
    n_blocks = bexp.shape[0]
    xmap = lambda b, be, bi, bv, nu, nx, ws: (bi[b], 0)
    grid_spec = pltpu.PrefetchScalarGridSpec(
        num_scalar_prefetch=6,
        grid=(n_blocks,),
        in_specs=[pl.BlockSpec((SLOT_BLK, D_MODEL // 2), xmap),
                  pl.BlockSpec(memory_space=pl.ANY), pl.BlockSpec(memory_space=pl.ANY),
                  pl.BlockSpec(memory_space=pl.ANY)],
        out_specs=pl.BlockSpec((SLOT_BLK, D_MODEL // 2), xmap),
        scratch_shapes=[pltpu.VMEM((2, D_MODEL, D_EXPERT), F32), pltpu.VMEM((2, D_MODEL, D_EXPERT), F32),
                        pltpu.VMEM((2, D_EXPERT, D_MODEL), F32),
                        pltpu.VMEM((D_MODEL, 2 * D_EXPERT), BF16), pltpu.VMEM((D_EXPERT, D_MODEL), BF16),
                        pltpu.SemaphoreType.DMA((2,))],
    )
    return pl.pallas_call(
        _experts_kernel,
        grid_spec=grid_spec,
        out_shape=jax.ShapeDtypeStruct(xs.shape, U32),
        compiler_params=_cparams("arbitrary"),
        name="experts",
    )(bexp, bidx, bval, nused, next_exp, wslot, xs, wg, wu, wd)


def _gather(slots3, y, n):
    n_chunks = slots3.shape[0]
    n_cores, n_sub = _sc_workers()
    per_worker = n_chunks // (n_cores * n_sub)
    assert per_worker * n_cores * n_sub == n_chunks

    @functools.partial(
        pl.kernel, mesh=_sc_mesh(),
        out_type=jax.ShapeDtypeStruct((TOP_K, n, D_MODEL // 2), U32),
        scratch_types=[pltpu.VMEM((TOP_K, SC_CHUNK), I32),
                       pltpu.VMEM((SC_CHUNK, D_MODEL // 2), U32), pltpu.VMEM((SC_CHUNK, D_MODEL // 2), U32),
                       pltpu.SemaphoreType.DMA, pltpu.SemaphoreType.DMA, pltpu.SemaphoreType.DMA],
        name="gather")
    def run(slots_hbm, y_hbm, g_hbm, idx_v, buf0, buf1, gsem, wsem0, wsem1):
        first = _worker_id(n_cores) * per_worker
        bufs = (buf0, buf1)
        wsems = (wsem0, wsem1)

        @pl.loop(0, per_worker)
        def _(j):
            c = first + j
            pltpu.sync_copy(slots_hbm.at[c], idx_v)
            gathers = [None] * TOP_K
            writes = [None] * TOP_K
            gathers[0] = pltpu.async_copy(y_hbm.at[idx_v.at[0]], bufs[0], gsem)
            for k in range(TOP_K):
                gathers[k].wait()
                if k >= 1:
                    writes[k - 1].wait()
                if k + 1 < TOP_K:
                    gathers[k + 1] = pltpu.async_copy(y_hbm.at[idx_v.at[k + 1]], bufs[(k + 1) % 2], gsem)
                writes[k] = pltpu.async_copy(bufs[k % 2], g_hbm.at[k, pl.ds(c * SC_CHUNK, SC_CHUNK)],
                                             wsems[k % 2])
            writes[TOP_K - 1].wait()

    return run(slots3, y)


def _combine_kernel(wt_ref, xo_ref, g_ref, o_ref):
    half = D_MODEL // 2
    acc_a = xo_ref[:, :half]
    acc_b = xo_ref[:, half:]
    for k in range(TOP_K):
        ya, yb = _unpack_bf16_pair(g_ref[k])
        w = wt_ref[:, k:k + 1]
        acc_a = acc_a + w * ya
        acc_b = acc_b + w * yb
    o_ref[:, :half] = acc_a
    o_ref[:, half:] = acc_b


def _combine(wt, xo, g, tm=256):
    n = xo.shape[0]
    return pl.pallas_call(
        _combine_kernel,
        grid=(n // tm,),
        in_specs=[pl.BlockSpec((tm, TOP_K), lambda i: (i, 0)),
                  pl.BlockSpec((tm, D_MODEL), lambda i: (i, 0)),
                  pl.BlockSpec((TOP_K, tm, D_MODEL // 2), lambda i: (0, i, 0))],
        out_specs=pl.BlockSpec((tm, D_MODEL), lambda i: (i, 0)),
        out_shape=jax.ShapeDtypeStruct((n, D_MODEL), F32),
        compiler_params=_cparams("parallel"),
        name="combine",
    )(wt, xo, g)


def _layer(x, norm_mix_g, w_in, q_norm_g, k_norm_g, rpb, conv_dw_w, conv_dw_b, conv_ln_g, conv_ln_b,
           w_conv_out, w_attn_out, w_o, norm_ffn_g, w_router, router_bias, w_exp_gate, w_exp_up,
           w_exp_down, w_sh_gate, w_sh_up, w_sh_down):
    n = x.shape[0]
    c_qkv = 2 * D_CONV + 3 * D_ATTN
    w_in_b = w_in.astype(BF16)
    head_of = jnp.arange(D_ATTN) // HEAD_DIM
    bsum = (head_of[:, None] == head_of[None, :]).astype(BF16)
    row = lambda v: v.reshape(1, -1).astype(F32)

    u, q, k, v = _inproj(x, row(norm_mix_g), w_in_b[:, :c_qkv], bsum,
                         row(jnp.tile(q_norm_g, N_HEADS)), row(jnp.tile(k_norm_g, N_HEADS)))
    uc = _conv(u, conv_dw_w.reshape(CONV_WIDTH, D_CONV), row(conv_dw_b), row(conv_ln_g), row(conv_ln_b))
    at = _attn(q, k, v, _bias_table(rpb))

    wr_t = w_router.T
    wr_hi = wr_t.astype(BF16)
    wr_lo = (wr_t - wr_hi.astype(F32)).astype(BF16)
    wsgu = jnp.concatenate([w_sh_gate, w_sh_up], axis=1).astype(BF16)
    xo, hp, idx_t, rank_t, wt_t, cnt = _mix_route(
        x, uc, at, row(norm_mix_g), w_in_b[:, c_qkv:], w_conv_out.astype(BF16), w_attn_out.astype(BF16),
        w_o.astype(BF16), row(norm_ffn_g), wr_hi, wr_lo, router_bias.reshape(N_EXPERTS, 1).astype(F32),
        wsgu, w_sh_down.astype(BF16))

    counts = cnt[:, 0].astype(I32)
    padded = (counts + SLOT_BLK - 1) // SLOT_BLK * SLOT_BLK
    pad_end = jnp.cumsum(padded)
    pad_start = pad_end - padded
    n_blocks = (n * TOP_K + N_EXPERTS * (SLOT_BLK - 1)) // SLOT_BLK
    n_used = pad_end[-1] // SLOT_BLK
    bidx = jnp.minimum(jnp.arange(n_blocks, dtype=I32), n_used - 1)
    is_done = (pad_end[None, :] <= (bidx * SLOT_BLK)[:, None]).astype(I32)
    bexp = jnp.minimum(jnp.sum(is_done, axis=1), N_EXPERTS - 1)
    owner = (bexp[:, None] == jnp.arange(N_EXPERTS)[None, :]).astype(I32)
    bval = jnp.sum(owner * (pad_start + counts)[None, :], axis=1) - bidx * SLOT_BLK
    bval = jnp.clip(bval, 0, SLOT_BLK).astype(I32)
    eids = jnp.arange(N_EXPERTS, dtype=I32)
    later = (eids[None, :] > eids[:, None]) & (counts[None, :] > 0)
    next_exp = jnp.min(jnp.where(later, eids[None, :], N_EXPERTS), axis=1)
    next_exp = jnp.where(next_exp == N_EXPERTS, -1, next_exp).astype(I32)
    wslot = ((jnp.cumsum((counts > 0).astype(I32)) - 1) & 1).astype(I32)

    onehot = (idx_t[:, :, None] == jnp.arange(N_EXPERTS)[None, None, :]).astype(I32)
    slots = jnp.sum(onehot * pad_start[None, None, :], axis=2) + rank_t
    slots3 = slots.reshape(TOP_K, n // SC_CHUNK, SC_CHUNK).transpose(1, 0, 2)

    xs = _dispatch(slots3, hp, n_blocks * SLOT_BLK)
    y = _experts(bexp, bidx, bval, n_used.reshape(1).astype(I32), next_exp, wslot, xs,
                 w_exp_gate, w_exp_up, w_exp_down)
    return _combine(wt_t.T, xo, _gather(slots3, y, n))


def kernel(x, norm_mix_g, w_in, q_norm_g, k_norm_g, rpb, conv_dw_w, conv_dw_b, conv_ln_g, conv_ln_b, w_conv_out, w_attn_out, w_o, norm_ffn_g, w_router, router_bias, w_exp_gate, w_exp_up, w_exp_down, w_sh_gate, w_sh_up, w_sh_down):
    b, s, d = x.shape
    assert b == 1 and d == D_MODEL and s % (2 * ATT_BLK) == 0, x.shape
    xf = x.reshape(b * s, d)
    depth = norm_mix_g.shape[0]
    for l in range(depth):
        xf = _layer(xf, norm_mix_g[l], w_in[l], q_norm_g[l], k_norm_g[l], rpb[l], conv_dw_w[l], conv_dw_b[l],
                    conv_ln_g[l], conv_ln_b[l], w_conv_out[l], w_attn_out[l], w_o[l], norm_ffn_g[l],
                    w_router[l], router_bias[l], w_exp_gate[l], w_exp_up[l], w_exp_down[l], w_sh_gate[l],
                    w_sh_up[l], w_sh_down[l])
    return xf.reshape(b, s, d)
```

```python
import functools

import jax
import jax.numpy as jnp
from jax import lax
from jax.experimental import pallas as pl
from jax.experimental.pallas import tpu as pltpu
from jax.experimental.pallas import tpu_sc as plsc

D_MODEL = 1024
GRID_W = 64
D_CONV = 512
CONV_WIDTH = 31
CONV_PAD = CONV_WIDTH // 2
N_HEADS = 8
HEAD_DIM = 64
D_ATTN = N_HEADS * HEAD_DIM
WIN_H = 8
WIN_W = 16
N_EXPERTS = 64
TOP_K = 8
N_GROUPS = 8
GROUP_SIZE = N_EXPERTS // N_GROUPS
TOPK_GROUPS = 4
D_EXPERT = 256
D_SHARED = 256
ROUTE_SCALE = 2.5
EPS = 1e-6
LOG2E = 1.4426950408889634

SUBLANES = 8
NEG = -1e30
HALO = 16
ATT_ROWS = 4
ATT_BLK = ATT_ROWS * GRID_W
SLOT_BLK = 512
XS_AHEAD = 2
XS_SLOTS = XS_AHEAD + 1
SC_CHUNK = 64
VMEM_LIMIT = 56 * 1024 * 1024

F32 = jnp.float32
BF16 = jnp.bfloat16
I32 = jnp.int32
U32 = jnp.uint32


def _cparams(*sem):
    return pltpu.CompilerParams(dimension_semantics=sem, vmem_limit_bytes=VMEM_LIMIT)


def _rms(x, g):
    return x * lax.rsqrt(jnp.mean(x * x, axis=-1, keepdims=True) + EPS) * g


def _pack_bf16_pair(a, b):
    def bf16_bits(v):
        return lax.bitcast_convert_type(v.astype(BF16).astype(F32), U32)
    return bf16_bits(a) | (bf16_bits(b) >> 16)


def _unpack_bf16_pair(w):
    a = lax.bitcast_convert_type(w & jnp.uint32(0xFFFF0000), F32)
    b = lax.bitcast_convert_type(w << 16, F32)
    return a, b


def _inproj_kernel(x_ref, g_ref, w_ref, bsum_ref, qg_ref, kg_ref, u_ref, q_ref, k_ref, v_ref):
    h = _rms(x_ref[...], g_ref[...]).astype(BF16)
    ag = jnp.dot(h, w_ref[:, 0:2 * D_CONV], preferred_element_type=F32)
    u_ref[...] = ag[:, :D_CONV] * jax.nn.sigmoid(ag[:, D_CONV:])
    bsum = bsum_ref[...]

    def head_norm(z, g):
        ss = jnp.dot((z * z).astype(BF16), bsum, preferred_element_type=F32)
        return z * lax.rsqrt(ss * (1.0 / HEAD_DIM) + EPS) * g

    c0 = 2 * D_CONV
    q = jnp.dot(h, w_ref[:, c0:c0 + D_ATTN], preferred_element_type=F32)
    q_ref[...] = (head_norm(q, qg_ref[...]) * (HEAD_DIM ** -0.5 * LOG2E)).astype(BF16)
    k = jnp.dot(h, w_ref[:, c0 + D_ATTN:c0 + 2 * D_ATTN], preferred_element_type=F32)
    k_ref[...] = head_norm(k, kg_ref[...]).astype(BF16)
    v = jnp.dot(h, w_ref[:, c0 + 2 * D_ATTN:c0 + 3 * D_ATTN], preferred_element_type=F32)
    v_ref[...] = v.astype(BF16)


def _inproj(x, g, w, bsum, qg, kg, tm=512):
    n = x.shape[0]
    wc = w.shape[1]
    full = lambda i: (0, 0)
    row = lambda i: (i, 0)
    return pl.pallas_call(
        _inproj_kernel,
        grid=(n // tm,),
        in_specs=[pl.BlockSpec((tm, D_MODEL), row), pl.BlockSpec((1, D_MODEL), full),
                  pl.BlockSpec((D_MODEL, wc), full), pl.BlockSpec((D_ATTN, D_ATTN), full),
                  pl.BlockSpec((1, D_ATTN), full), pl.BlockSpec((1, D_ATTN), full)],
        out_specs=[pl.BlockSpec((tm, D_CONV), row), pl.BlockSpec((tm, D_ATTN), row),
                   pl.BlockSpec((tm, D_ATTN), row), pl.BlockSpec((tm, D_ATTN), row)],
        out_shape=[jax.ShapeDtypeStruct((n, D_CONV), F32), jax.ShapeDtypeStruct((n, D_ATTN), BF16),
                   jax.ShapeDtypeStruct((n, D_ATTN), BF16), jax.ShapeDtypeStruct((n, D_ATTN), BF16)],
        compiler_params=_cparams("parallel"),
        name="inproj",
    )(x, g, w, bsum, qg, kg)


def _conv_kernel(up_ref, uc_ref, un_ref, w_ref, b_ref, lg_ref, lb_ref, o_ref, ext_ref, sh_ref, *, tc, ch):
    i = pl.program_id(0)
    last = pl.num_programs(0) - 1
    ext_ref[0:HALO, :] = jnp.where(i > 0, up_ref[...], 0.0)
    ext_ref[HALO:HALO + tc, :] = uc_ref[...]
    ext_ref[HALO + tc:2 * HALO + tc, :] = jnp.where(i < last, un_ref[...], 0.0)
    span = sh_ref.shape[1]
    for b in range(SUBLANES):
        sh_ref[b] = ext_ref[b:b + span, :]
    bias = b_ref[...]
    lg = lg_ref[...]
    lb = lb_ref[...]
    for c in range(tc // ch):
        acc = jnp.broadcast_to(bias, (ch, D_CONV))
        for j in range(CONV_WIDTH):
            shift = HALO - CONV_PAD + j
            row0 = c * ch + shift // SUBLANES * SUBLANES
            acc = acc + sh_ref[shift % SUBLANES, row0:row0 + ch, :] * w_ref[j:j + 1, :]
        mu = jnp.mean(acc, axis=-1, keepdims=True)
        d = acc - mu
        var = jnp.mean(d * d, axis=-1, keepdims=True)
        y = d * lax.rsqrt(var + EPS) * lg + lb
        o_ref[c * ch:(c + 1) * ch, :] = (y * jax.nn.sigmoid(y)).astype(BF16)


def _conv(u, w, b, lg, lb, tc=256, ch=64):
    n = u.shape[0]
    hb = tc // HALO
    nh = n // HALO
    full = lambda i: (0, 0)
    return pl.pallas_call(
        functools.partial(_conv_kernel, tc=tc, ch=ch),
        grid=(n // tc,),
        in_specs=[pl.BlockSpec((HALO, D_CONV), lambda i: (jnp.maximum(i * hb - 1, 0), 0)),
                  pl.BlockSpec((tc, D_CONV), lambda i: (i, 0)),
                  pl.BlockSpec((HALO, D_CONV), lambda i: (jnp.minimum((i + 1) * hb, nh - 1), 0)),
                  pl.BlockSpec((CONV_WIDTH, D_CONV), full), pl.BlockSpec((1, D_CONV), full),
                  pl.BlockSpec((1, D_CONV), full), pl.BlockSpec((1, D_CONV), full)],
        out_specs=pl.BlockSpec((tc, D_CONV), lambda i: (i, 0)),
        out_shape=jax.ShapeDtypeStruct((n, D_CONV), BF16),
        scratch_shapes=[pltpu.VMEM((tc + 2 * HALO, D_CONV), F32),
                        pltpu.VMEM((SUBLANES, tc + 2 * HALO - SUBLANES, D_CONV), F32)],
        compiler_params=_cparams("parallel"),
        name="conv",
    )(u, u, u, w, b, lg, lb)


def _attn_kernel(q_ref, kp_ref, kc_ref, kn_ref, vp_ref, vc_ref, vn_ref, tb_ref, o_ref, *, rows):
    i = pl.program_id(0)
    nkey = 3 * ATT_BLK
    lrow = lax.broadcasted_iota(I32, (1, nkey), 1) >> 6
    lane = lax.broadcasted_iota(I32, (1, 2 * HEAD_DIM), 1)
    masks = []
    for j in range(ATT_ROWS):
        r = i * ATT_ROWS + j
        rs = jnp.clip(r - WIN_H // 2, 0, rows - WIN_H)
        lo = rs - (i - 1) * ATT_ROWS
        masks.append(jnp.where((lrow >= lo) & (lrow < lo + WIN_H), 0.0, NEG))
    dn = (((1,), (1,)), ((), ()))
    for p in range(N_HEADS // 2):
        sl = slice(2 * HEAD_DIM * p, 2 * HEAD_DIM * (p + 1))
        q2 = q_ref[:, sl]
        ks = (kp_ref[:, sl], kc_ref[:, sl], kn_ref[:, sl])
        vs = (vp_ref[:, sl], vc_ref[:, sl], vn_ref[:, sl])
        out_pair = None
        for half in range(2):
            h = 2 * p + half
            in_half = (lane >= HEAD_DIM * half) & (lane < HEAD_DIM * (half + 1))
            qm = jnp.where(in_half, q2, jnp.zeros_like(q2))
            s = jnp.concatenate([lax.dot_general(qm, kk, dn, preferred_element_type=F32) for kk in ks], axis=1)
            parts = []
            for j in range(ATT_ROWS):
                bias = jnp.concatenate([tb_ref[h, 2 * m - j + 3] for m in range(nkey // (2 * GRID_W))], axis=1)
                parts.append(s[j * GRID_W:(j + 1) * GRID_W, :] + bias + masks[j])
            s = jnp.concatenate(parts, axis=0)
            mx = jnp.max(s, axis=-1, keepdims=True)
            e = jnp.exp2(s - mx)
            den = jnp.sum(e, axis=-1, keepdims=True)
            pb = e.astype(BF16)
            o = sum(jnp.dot(pb[:, ATT_BLK * t:ATT_BLK * (t + 1)], vs[t], preferred_element_type=F32)
                    for t in range(3))
            o = o / den
            out_pair = o if out_pair is None else jnp.where(in_half, o, out_pair)
        o_ref[:, sl] = out_pair.astype(BF16)


def _attn(q, k, v, tb):
    n = q.shape[0]
    rows = n // GRID_W
    nb = n // ATT_BLK
    cur = lambda i: (i, 0)
    prev = lambda i: (jnp.maximum(i - 1, 0), 0)
    nxt = lambda i: (jnp.minimum(i + 1, nb - 1), 0)
    blk = (ATT_BLK, D_ATTN)
    return pl.pallas_call(
        functools.partial(_attn_kernel, rows=rows),
        grid=(nb,),
        in_specs=[pl.BlockSpec(blk, cur), pl.BlockSpec(blk, prev), pl.BlockSpec(blk, cur), pl.BlockSpec(blk, nxt),
                  pl.BlockSpec(blk, prev), pl.BlockSpec(blk, cur), pl.BlockSpec(blk, nxt),
                  pl.BlockSpec(tb.shape, lambda i: (0, 0, 0, 0))],
        out_specs=pl.BlockSpec(blk, cur),
        out_shape=jax.ShapeDtypeStruct((n, D_ATTN), BF16),
        compiler_params=_cparams("parallel"),
        name="attn",
    )(q, k, k, k, v, v, v, tb)


def _bias_table(rpb):
    cols = jnp.arange(GRID_W)
    start = jnp.clip(cols - WIN_W // 2, 0, GRID_W - WIN_W)
    kc = cols[None, :]
    inwin = (kc >= start[:, None]) & (kc < start[:, None] + WIN_W)
    off = GRID_W - WIN_W
    padded = jnp.pad(rpb, ((0, 0), (0, 0), (off, off)))
    rel = jnp.stack([padded[:, :, GRID_W - 1 - c:2 * GRID_W - 1 - c] for c in range(GRID_W)], axis=2)
    t = jnp.where(inwin[None, None], rel * LOG2E, NEG)
    t = jnp.concatenate([t, jnp.full_like(t[:, :1], NEG)], axis=1)
    t_next = jnp.concatenate([t[:, 1:], t[:, -1:]], axis=1)
    return jnp.concatenate([t, t_next], axis=-1).astype(F32)


def _mix_route_kernel(x_ref, uc_ref, at_ref, g1_ref, wg_ref, wco_ref, wao_ref, wo_ref, g2_ref,
                      wrh_ref, wrl_ref, rb_ref, tri_ref, wsgu_ref, wsd_ref,
                      xo_ref, hp_ref, idx_ref, rank_ref, wt_ref, cnt_ref, carry_ref, *, tm):
    i = pl.program_id(0)

    @pl.when(i == 0)
    def _():
        carry_ref[...] = jnp.zeros_like(carry_ref)

    x = x_ref[...]
    h = _rms(x, g1_ref[...]).astype(BF16)
    gates = jax.nn.sigmoid(jnp.dot(h, wg_ref[...], preferred_element_type=F32))
    y_conv = jnp.dot(uc_ref[...], wco_ref[...], preferred_element_type=F32)
    y_attn = jnp.dot(at_ref[...], wao_ref[...], preferred_element_type=F32)
    merged = gates[:, :D_MODEL] * y_conv + gates[:, D_MODEL:] * y_attn
    x1 = x + jnp.dot(merged.astype(BF16), wo_ref[...], preferred_element_type=F32)
    h2 = _rms(x1, g2_ref[...])
    h2b = h2.astype(BF16)
    hp_ref[...] = _pack_bf16_pair(h2[:, :D_MODEL // 2], h2[:, D_MODEL // 2:])

    gu = jnp.dot(h2b, wsgu_ref[...], preferred_element_type=F32)
    mid = (jax.nn.silu(gu[:, :D_SHARED]) * gu[:, D_SHARED:]).astype(BF16)
    xo_ref[...] = x1 + jnp.dot(mid, wsd_ref[...], preferred_element_type=F32)

    h2l = (h2 - h2b.astype(F32)).astype(BF16)
    dn = (((1,), (1,)), ((), ()))
    logits = (lax.dot_general(wrh_ref[...], h2b, dn, preferred_element_type=F32)
              + lax.dot_general(wrl_ref[...], h2b, dn, preferred_element_type=F32)
              + lax.dot_general(wrh_ref[...], h2l, dn, preferred_element_type=F32))
    scores = jax.nn.sigmoid(logits)
    biased = scores + rb_ref[...]

    sub = lax.broadcasted_iota(I32, (GROUP_SIZE, tm), 0).astype(F32)
    groups, gscore = [], []
    for g in range(N_GROUPS):
        bg = biased[g * GROUP_SIZE:(g + 1) * GROUP_SIZE, :]
        m1 = jnp.max(bg, axis=0, keepdims=True)
        first = jnp.min(jnp.where(bg == m1, sub, float(GROUP_SIZE)), axis=0, keepdims=True)
        m2 = jnp.max(jnp.where(sub == first, -jnp.inf, bg), axis=0, keepdims=True)
        groups.append(bg)
        gscore.append(m1 + m2)
    masked = []
    for g in range(N_GROUPS):
        beaten = jnp.zeros((1, tm), F32)
        for o in range(N_GROUPS):
            if o == g:
                continue
            wins = (gscore[o] >= gscore[g]) if o < g else (gscore[o] > gscore[g])
            beaten = beaten + jnp.where(wins, 1.0, 0.0)
        masked.append(jnp.where(beaten < TOPK_GROUPS, groups[g], -jnp.inf))
    cur = jnp.concatenate(masked, axis=0)

    eid = lax.broadcasted_iota(I32, (N_EXPERTS, tm), 0).astype(F32)
    idx_rows, sc_rows = [], []
    sel = jnp.zeros((N_EXPERTS, tm), F32)
    for _ in range(TOP_K):
        m = jnp.max(cur, axis=0, keepdims=True)
        idx = jnp.min(jnp.where(cur == m, eid, float(N_EXPERTS)), axis=0, keepdims=True)
        hit = eid == idx
        sel = jnp.where(hit, 1.0, sel)
        cur = jnp.where(hit, -jnp.inf, cur)
        idx_rows.append(idx)
        sc_rows.append(jnp.sum(jnp.where(hit, scores, 0.0), axis=0, keepdims=True))
    wsum = sc_rows[0]
    for k in range(1, TOP_K):
        wsum = wsum + sc_rows[k]

    prefix = jnp.dot(sel.astype(BF16), tri_ref[...], preferred_element_type=F32)
    rank = carry_ref[:, 0:1] + prefix
    rank_rows = [jnp.sum(jnp.where(eid == idx_rows[k], rank, 0.0), axis=0, keepdims=True) for k in range(TOP_K)]
    carry_ref[...] = carry_ref[...] + jnp.sum(sel, axis=1, keepdims=True)
    cnt_ref[...] = carry_ref[...]

    idx_ref[...] = jnp.concatenate(idx_rows, axis=0).astype(I32)
    rank_ref[...] = jnp.concatenate(rank_rows, axis=0).astype(I32)
    wt_ref[...] = jnp.concatenate([s / wsum * ROUTE_SCALE for s in sc_rows], axis=0)


def _mix_route(x, uc, at, g1, wg, wco, wao, wo, g2, wrh, wrl, rb, wsgu, wsd, tm=512):
    n = x.shape[0]
    tri = (jnp.arange(tm)[:, None] < jnp.arange(tm)[None, :]).astype(BF16)
    row = lambda i: (i, 0)
    col = lambda i: (0, i)
    full = lambda i: (0, 0)
    ins = [x, uc, at, g1, wg, wco, wao, wo, g2, wrh, wrl, rb, tri, wsgu, wsd]
    in_specs = [pl.BlockSpec((tm, D_MODEL), row), pl.BlockSpec((tm, D_CONV), row), pl.BlockSpec((tm, D_ATTN), row)]
    in_specs += [pl.BlockSpec(a.shape, full) for a in ins[3:]]
    return pl.pallas_call(
        functools.partial(_mix_route_kernel, tm=tm),
        grid=(n // tm,),
        in_specs=in_specs,
        out_specs=[pl.BlockSpec((tm, D_MODEL), row), pl.BlockSpec((tm, D_MODEL // 2), row),
                   pl.BlockSpec((TOP_K, tm), col), pl.BlockSpec((TOP_K, tm), col), pl.BlockSpec((TOP_K, tm), col),
                   pl.BlockSpec((N_EXPERTS, 128), full)],
        out_shape=[jax.ShapeDtypeStruct((n, D_MODEL), F32), jax.ShapeDtypeStruct((n, D_MODEL // 2), U32),
                   jax.ShapeDtypeStruct((TOP_K, n), I32), jax.ShapeDtypeStruct((TOP_K, n), I32),
                   jax.ShapeDtypeStruct((TOP_K, n), F32), jax.ShapeDtypeStruct((N_EXPERTS, 128), F32)],
        scratch_shapes=[pltpu.VMEM((N_EXPERTS, 128), F32)],
        compiler_params=_cparams("arbitrary"),
        name="mix_route",
    )(*ins)


def _sc_workers():
    info = plsc.get_sparse_core_info()
    return info.num_cores, info.num_subcores


def _sc_mesh():
    return plsc.VectorSubcoreMesh(core_axis_name="c", subcore_axis_name="s")


def _worker_id(n_cores):
    return lax.axis_index("s") * n_cores + lax.axis_index("c")


def _dispatch(slots3, hp, n_slots):
    n_chunks = slots3.shape[0]
    n_cores, n_sub = _sc_workers()
    per_worker = n_chunks // (n_cores * n_sub)
    assert per_worker * n_cores * n_sub == n_chunks

    @functools.partial(
        pl.kernel, mesh=_sc_mesh(),
        out_type=jax.ShapeDtypeStruct((n_slots, D_MODEL // 2), U32),
        scratch_types=[pltpu.VMEM((TOP_K, SC_CHUNK), I32), pltpu.VMEM((SC_CHUNK, D_MODEL // 2), U32),
                       pltpu.SemaphoreType.DMA],
        name="dispatch")
    def run(slots_hbm, hp_hbm, xs_hbm, idx_v, rows_v, sem):
        first = _worker_id(n_cores) * per_worker

        @pl.loop(0, per_worker)
        def _(j):
            c = first + j
            pltpu.sync_copy(slots_hbm.at[c], idx_v)
            pltpu.sync_copy(hp_hbm.at[pl.ds(c * SC_CHUNK, SC_CHUNK)], rows_v)
            copies = [pltpu.async_copy(rows_v, xs_hbm.at[idx_v.at[k]], sem) for k in range(TOP_K)]
            for cp in copies:
                cp.wait()

    return run(slots3, hp)


def _experts_kernel(bexp_ref, bidx_ref, bval_ref, nused_ref, next_ref, wslot_ref,
                    xs_ref, wg_ref, wu_ref, wd_ref, y_ref, xbuf, wg_buf, wu_buf, wd_buf, wgu_s, wd_s, xsem, wsem):
    b = pl.program_id(0)
    n_used = nused_ref[0]

    def block_copy(blk):
        slot = blk % XS_SLOTS
        rows = pl.ds(pl.multiple_of(blk * SLOT_BLK, SLOT_BLK), SLOT_BLK)
        return pltpu.make_async_copy(xs_ref.at[rows], xbuf.at[slot], xsem.at[slot])

    @pl.when(b == 0)
    def _():
        for blk in range(XS_AHEAD):
            @pl.when(blk < n_used)
            def _():
                block_copy(blk).start()

    @pl.when(b + XS_AHEAD < n_used)
    def _():
        block_copy(b + XS_AHEAD).start()

    def weight_copies(e, slot):
        return [pltpu.make_async_copy(wg_ref.at[e], wg_buf.at[slot], wsem.at[slot]),
                pltpu.make_async_copy(wu_ref.at[e], wu_buf.at[slot], wsem.at[slot]),
                pltpu.make_async_copy(wd_ref.at[e], wd_buf.at[slot], wsem.at[slot])]

    @pl.when(b < n_used)
    def _():
        e = bexp_ref[b]
        slot = wslot_ref[e]

        @pl.when(b == 0)
        def _():
            for cp in weight_copies(e, slot):
                cp.start()

        @pl.when((b == 0) | (bexp_ref[jnp.maximum(b - 1, 0)] != e))
        def _():
            for cp in weight_copies(e, slot):
                cp.wait()
            wgu_s[:, :D_EXPERT] = wg_buf[slot].astype(BF16)
            wgu_s[:, D_EXPERT:] = wu_buf[slot].astype(BF16)
            wd_s[...] = wd_buf[slot].astype(BF16)
            nxt = next_ref[e]

            @pl.when(nxt >= 0)
            def _():
                for cp in weight_copies(nxt, 1 - slot):
                    cp.start()

        srow = lax.broadcasted_iota(I32, (SLOT_BLK, 1), 0)
        block_copy(b).wait()
        xw = jnp.where(srow < bval_ref[b], xbuf[b % XS_SLOTS], jnp.uint32(0))
        xa, xb = _unpack_bf16_pair(xw)
        x = jnp.concatenate([xa.astype(BF16), xb.astype(BF16)], axis=1)
        gu = jnp.dot(x, wgu_s[...], preferred_element_type=F32)
        mid = (jax.nn.silu(gu[:, :D_EXPERT]) * gu[:, D_EXPERT:]).astype(BF16)
        y = jnp.dot(mid, wd_s[...], preferred_element_type=F32)
        y_ref[...] = _pack_bf16_pair(y[:, :D_MODEL // 2], y[:, D_MODEL // 2:])


def _experts(bexp, bidx, bval, nused, next_exp, wslot, xs, wg, wu, wd):
    n_blocks = bexp.shape[0]
    xmap = lambda b, be, bi, bv, nu, nx, ws: (bi[b], 0)
    grid_spec = pltpu.PrefetchScalarGridSpec(
        num_scalar_prefetch=6,
        grid=(n_blocks,),
        in_specs=[pl.BlockSpec(memory_space=pl.ANY),
                  pl.BlockSpec(memory_space=pl.ANY), pl.BlockSpec(memory_space=pl.ANY),
                  pl.BlockSpec(memory_space=pl.ANY)],
        out_specs=pl.BlockSpec((SLOT_BLK, D_MODEL // 2), xmap),
        scratch_shapes=[pltpu.VMEM((XS_SLOTS, SLOT_BLK, D_MODEL // 2), U32),
                        pltpu.VMEM((2, D_MODEL, D_EXPERT), F32), pltpu.VMEM((2, D_MODEL, D_EXPERT), F32),
                        pltpu.VMEM((2, D_EXPERT, D_MODEL), F32),
                        pltpu.VMEM((D_MODEL, 2 * D_EXPERT), BF16), pltpu.VMEM((D_EXPERT, D_MODEL), BF16),
                        pltpu.SemaphoreType.DMA((XS_SLOTS,)), pltpu.SemaphoreType.DMA((2,))],
    )
    return pl.pallas_call(
        _experts_kernel,
        grid_spec=grid_spec,
        out_shape=jax.ShapeDtypeStruct(xs.shape, U32),
        compiler_params=_cparams("arbitrary"),
        name="experts",
    )(bexp, bidx, bval, nused, next_exp, wslot, xs, wg, wu, wd)


def _gather(slots3, y, n):
    n_chunks = slots3.shape[0]
    n_cores, n_sub = _sc_workers()
    per_worker = n_chunks // (n_cores * n_sub)
    assert per_worker * n_cores * n_sub == n_chunks

    @functools.partial(
        pl.kernel, mesh=_sc_mesh(),
        out_type=jax.ShapeDtypeStruct((TOP_K, n, D_MODEL // 2), U32),
        scratch_types=[pltpu.VMEM((TOP_K, SC_CHUNK), I32),
                       pltpu.VMEM((SC_CHUNK, D_MODEL // 2), U32), pltpu.VMEM((SC_CHUNK, D_MODEL // 2), U32),
                       pltpu.SemaphoreType.DMA, pltpu.SemaphoreType.DMA, pltpu.SemaphoreType.DMA],
        name="gather")
    def run(slots_hbm, y_hbm, g_hbm, idx_v, buf0, buf1, gsem, wsem0, wsem1):
        first = _worker_id(n_cores) * per_worker
        bufs = (buf0, buf1)
        wsems = (wsem0, wsem1)

        @pl.loop(0, per_worker)
        def _(j):
            c = first + j
            pltpu.sync_copy(slots_hbm.at[c], idx_v)
            gathers = [None] * TOP_K
            writes = [None] * TOP_K
            gathers[0] = pltpu.async_copy(y_hbm.at[idx_v.at[0]], bufs[0], gsem)
            for k in range(TOP_K):
                gathers[k].wait()
                if k >= 1:
                    writes[k - 1].wait()
                if k + 1 < TOP_K:
                    gathers[k + 1] = pltpu.async_copy(y_hbm.at[idx_v.at[k + 1]], bufs[(k + 1) % 2], gsem)
                writes[k] = pltpu.async_copy(bufs[k % 2], g_hbm.at[k, pl.ds(c * SC_CHUNK, SC_CHUNK)],
                                             wsems[k % 2])
            writes[TOP_K - 1].wait()

    return run(slots3, y)


def _combine_kernel(wt_ref, xo_ref, g_ref, o_ref):
    half = D_MODEL // 2
    acc_a = xo_ref[:, :half]
    acc_b = xo_ref[:, half:]
    for k in range(TOP_K):
        ya, yb = _unpack_bf16_pair(g_ref[k])
        w = wt_ref[:, k:k + 1]
        acc_a = acc_a + w * ya
        acc_b = acc_b + w * yb
    o_ref[:, :half] = acc_a
    o_ref[:, half:] = acc_b


def _combine(wt, xo, g, tm=256):
    n = xo.shape[0]
    return pl.pallas_call(
        _combine_kernel,
        grid=(n // tm,),
        in_specs=[pl.BlockSpec((tm, TOP_K), lambda i: (i, 0)),
                  pl.BlockSpec((tm, D_MODEL), lambda i: (i, 0)),
                  pl.BlockSpec((TOP_K, tm, D_MODEL // 2), lambda i: (0, i, 0))],
        out_specs=pl.BlockSpec((tm, D_MODEL), lambda i: (i, 0)),
        out_shape=jax.ShapeDtypeStruct((n, D_MODEL), F32),
        compiler_params=_cparams("parallel"),
        name="combine",
    )(wt, xo, g)


def _layer(x, norm_mix_g, w_in, q_norm_g, k_norm_g, rpb, conv_dw_w, conv_dw_b, conv_ln_g, conv_ln_b,
           w_conv_out, w_attn_out, w_o, norm_ffn_g, w_router, router_bias, w_exp_gate, w_exp_up,
           w_exp_down, w_sh_gate, w_sh_up, w_sh_down):
    n = x.shape[0]
    c_qkv = 2 * D_CONV + 3 * D_ATTN
    w_in_b = w_in.astype(BF16)
    head_of = jnp.arange(D_ATTN) // HEAD_DIM
    bsum = (head_of[:, None] == head_of[None, :]).astype(BF16)
    row = lambda v: v.reshape(1, -1).astype(F32)

    u, q, k, v = _inproj(x, row(norm_mix_g), w_in_b[:, :c_qkv], bsum,
                         row(jnp.tile(q_norm_g, N_HEADS)), row(jnp.tile(k_norm_g, N_HEADS)))
    uc = _conv(u, conv_dw_w.reshape(CONV_WIDTH, D_CONV), row(conv_dw_b), row(conv_ln_g), row(conv_ln_b))
    at = _attn(q, k, v, _bias_table(rpb))

    wr_t = w_router.T
    wr_hi = wr_t.astype(BF16)
    wr_lo = (wr_t - wr_hi.astype(F32)).astype(BF16)
    wsgu = jnp.concatenate([w_sh_gate, w_sh_up], axis=1).astype(BF16)
    xo, hp, idx_t, rank_t, wt_t, cnt = _mix_route(
        x, uc, at, row(norm_mix_g), w_in_b[:, c_qkv:], w_conv_out.astype(BF16), w_attn_out.astype(BF16),
        w_o.astype(BF16), row(norm_ffn_g), wr_hi, wr_lo, router_bias.reshape(N_EXPERTS, 1).astype(F32),
        wsgu, w_sh_down.astype(BF16))

    counts = cnt[:, 0].astype(I32)
    padded = (counts + SLOT_BLK - 1) // SLOT_BLK * SLOT_BLK
    pad_end = jnp.cumsum(padded)
    pad_start = pad_end - padded
    n_blocks = (n * TOP_K + N_EXPERTS * (SLOT_BLK - 1)) // SLOT_BLK
    n_used = pad_end[-1] // SLOT_BLK
    bidx = jnp.minimum(jnp.arange(n_blocks, dtype=I32), n_used - 1)
    is_done = (pad_end[None, :] <= (bidx * SLOT_BLK)[:, None]).astype(I32)
    bexp = jnp.minimum(jnp.sum(is_done, axis=1), N_EXPERTS - 1)
    owner = (bexp[:, None] == jnp.arange(N_EXPERTS)[None, :]).astype(I32)
    bval = jnp.sum(owner * (pad_start + counts)[None, :], axis=1) - bidx * SLOT_BLK
    bval = jnp.clip(bval, 0, SLOT_BLK).astype(I32)
    eids = jnp.arange(N_EXPERTS, dtype=I32)
    later = (eids[None, :] > eids[:, None]) & (counts[None, :] > 0)
    next_exp = jnp.min(jnp.where(later, eids[None, :], N_EXPERTS), axis=1)
    next_exp = jnp.where(next_exp == N_EXPERTS, -1, next_exp).astype(I32)
    wslot = ((jnp.cumsum((counts > 0).astype(I32)) - 1) & 1).astype(I32)

    onehot = (idx_t[:, :, None] == jnp.arange(N_EXPERTS)[None, None, :]).astype(I32)
    slots = jnp.sum(onehot * pad_start[None, None, :], axis=2) + rank_t
    slots3 = slots.reshape(TOP_K, n // SC_CHUNK, SC_CHUNK).transpose(1, 0, 2)

    xs = _dispatch(slots3, hp, n_blocks * SLOT_BLK)
    y = _experts(bexp, bidx, bval, n_used.reshape(1).astype(I32), next_exp, wslot, xs,
                 w_exp_gate, w_exp_up, w_exp_down)
    return _combine(wt_t.T, xo, _gather(slots3, y, n))


def kernel(x, norm_mix_g, w_in, q_norm_g, k_norm_g, rpb, conv_dw_w, conv_dw_b, conv_ln_g, conv_ln_b, w_conv_out, w_attn_out, w_o, norm_ffn_g, w_router, router_bias, w_exp_gate, w_exp_up, w_exp_down, w_sh_gate, w_sh_up, w_sh_down):
    b, s, d = x.shape
    assert b == 1 and d == D_MODEL and s % (2 * ATT_BLK) == 0, x.shape
    xf = x.reshape(b * s, d)
    depth = norm_mix_g.shape[0]
    for l in range(depth):
        xf = _layer(xf, norm_mix_g[l], w_in[l], q_norm_g[l], k_norm_g[l], rpb[l], conv_dw_w[l], conv_dw_b[l],
                    conv_ln_g[l], conv_ln_b[l], w_conv_out[l], w_attn_out[l], w_o[l], norm_ffn_g[l],
                    w_router[l], router_bias[l], w_exp_gate[l], w_exp_up[l], w_exp_down[l], w_sh_gate[l],
                    w_sh_up[l], w_sh_down[l])
    return xf.reshape(b, s, d)
```

```python
import functools

import jax
import jax.numpy as jnp
from jax import lax
from jax.experimental import pallas as pl
from jax.experimental.pallas import tpu as pltpu
from jax.experimental.pallas import tpu_sc as plsc

D_MODEL = 1024
GRID_W = 64
D_CONV = 512
CONV_WIDTH = 31
CONV_PAD = CONV_WIDTH // 2
N_HEADS = 8
HEAD_DIM = 64
D_ATTN = N_HEADS * HEAD_DIM
WIN_H = 8
WIN_W = 16
N_EXPERTS = 64
TOP_K = 8
N_GROUPS = 8
GROUP_SIZE = N_EXPERTS // N_GROUPS
TOPK_GROUPS = 4
D_EXPERT = 256
D_SHARED = 256
ROUTE_SCALE = 2.5
EPS = 1e-6
LOG2E = 1.4426950408889634

SUBLANES = 8
NEG = -1e30
HALO = 16
ATT_ROWS = 4
ATT_BLK = ATT_ROWS * GRID_W
SLOT_BLK = 512
XS_AHEAD = 2
XS_SLOTS = XS_AHEAD + 1
SC_CHUNK = 64
VMEM_LIMIT = 56 * 1024 * 1024

F32 = jnp.float32
BF16 = jnp.bfloat16
I32 = jnp.int32
U32 = jnp.uint32


def _cparams(*sem):
    return pltpu.CompilerParams(dimension_semantics=sem, vmem_limit_bytes=VMEM_LIMIT)


def _rms(x, g):
    return x * lax.rsqrt(jnp.mean(x * x, axis=-1, keepdims=True) + EPS) * g


def _pack_bf16_pair(a, b):
    def bf16_bits(v):
        return lax.bitcast_convert_type(v.astype(BF16).astype(F32), U32)
    return bf16_bits(a) | (bf16_bits(b) >> 16)


def _unpack_bf16_pair(w):
    a = lax.bitcast_convert_type(w & jnp.uint32(0xFFFF0000), F32)
    b = lax.bitcast_convert_type(w << 16, F32)
    return a, b


def _inproj_kernel(x_ref, g_ref, w_ref, bsum_ref, qg_ref, kg_ref, u_ref, q_ref, k_ref, v_ref):
    h = _rms(x_ref[...], g_ref[...]).astype(BF16)
    ag = jnp.dot(h, w_ref[:, 0:2 * D_CONV], preferred_element_type=F32)
    u_ref[...] = ag[:, :D_CONV] * jax.nn.sigmoid(ag[:, D_CONV:])
    bsum = bsum_ref[...]

    def head_norm(z, g):
        ss = jnp.dot((z * z).astype(BF16), bsum, preferred_element_type=F32)
        return z * lax.rsqrt(ss * (1.0 / HEAD_DIM) + EPS) * g

    c0 = 2 * D_CONV
    q = jnp.dot(h, w_ref[:, c0:c0 + D_ATTN], preferred_element_type=F32)
    q_ref[...] = (head_norm(q, qg_ref[...]) * (HEAD_DIM ** -0.5 * LOG2E)).astype(BF16)
    k = jnp.dot(h, w_ref[:, c0 + D_ATTN:c0 + 2 * D_ATTN], preferred_element_type=F32)
    k_ref[...] = head_norm(k, kg_ref[...]).astype(BF16)
    v = jnp.dot(h, w_ref[:, c0 + 2 * D_ATTN:c0 + 3 * D_ATTN], preferred_element_type=F32)
    v_ref[...] = v.astype(BF16)


def _inproj(x, g, w, bsum, qg, kg, tm=512):
    n = x.shape[0]
    wc = w.shape[1]
    full = lambda i: (0, 0)
    row = lambda i: (i, 0)
    return pl.pallas_call(
        _inproj_kernel,
        grid=(n // tm,),
        in_specs=[pl.BlockSpec((tm, D_MODEL), row), pl.BlockSpec((1, D_MODEL), full),
                  pl.BlockSpec((D_MODEL, wc), full), pl.BlockSpec((D_ATTN, D_ATTN), full),
                  pl.BlockSpec((1, D_ATTN), full), pl.BlockSpec((1, D_ATTN), full)],
        out_specs=[pl.BlockSpec((tm, D_CONV), row), pl.BlockSpec((tm, D_ATTN), row),
                   pl.BlockSpec((tm, D_ATTN), row), pl.BlockSpec((tm, D_ATTN), row)],
        out_shape=[jax.ShapeDtypeStruct((n, D_CONV), F32), jax.ShapeDtypeStruct((n, D_ATTN), BF16),
                   jax.ShapeDtypeStruct((n, D_ATTN), BF16), jax.ShapeDtypeStruct((n, D_ATTN), BF16)],
        compiler_params=_cparams("parallel"),
        name="inproj",
    )(x, g, w, bsum, qg, kg)


def _conv_kernel(up_ref, uc_ref, un_ref, w_ref, b_ref, lg_ref, lb_ref, o_ref, ext_ref, sh_ref, *, tc, ch):
    i = pl.program_id(0)
    last = pl.num_programs(0) - 1
    ext_ref[0:HALO, :] = jnp.where(i > 0, up_ref[...], 0.0)
    ext_ref[HALO:HALO + tc, :] = uc_ref[...]
    ext_ref[HALO + tc:2 * HALO + tc, :] = jnp.where(i < last, un_ref[...], 0.0)
    span = sh_ref.shape[1]
    for b in range(SUBLANES):
        sh_ref[b] = ext_ref[b:b + span, :]
    bias = b_ref[...]
    lg = lg_ref[...]
    lb = lb_ref[...]
    for c in range(tc // ch):
        acc = jnp.broadcast_to(bias, (ch, D_CONV))
        for j in range(CONV_WIDTH):
            shift = HALO - CONV_PAD + j
            row0 = c * ch + shift // SUBLANES * SUBLANES
            acc = acc + sh_ref[shift % SUBLANES, row0:row0 + ch, :] * w_ref[j:j + 1, :]
        mu = jnp.mean(acc, axis=-1, keepdims=True)
        d = acc - mu
        var = jnp.mean(d * d, axis=-1, keepdims=True)
        y = d * lax.rsqrt(var + EPS) * lg + lb
        o_ref[c * ch:(c + 1) * ch, :] = (y * jax.nn.sigmoid(y)).astype(BF16)


def _conv(u, w, b, lg, lb, tc=256, ch=64):
    n = u.shape[0]
    hb = tc // HALO
    nh = n // HALO
    full = lambda i: (0, 0)
    return pl.pallas_call(
        functools.partial(_conv_kernel, tc=tc, ch=ch),
        grid=(n // tc,),
        in_specs=[pl.BlockSpec((HALO, D_CONV), lambda i: (jnp.maximum(i * hb - 1, 0), 0)),
                  pl.BlockSpec((tc, D_CONV), lambda i: (i, 0)),
                  pl.BlockSpec((HALO, D_CONV), lambda i: (jnp.minimum((i + 1) * hb, nh - 1), 0)),
                  pl.BlockSpec((CONV_WIDTH, D_CONV), full), pl.BlockSpec((1, D_CONV), full),
                  pl.BlockSpec((1, D_CONV), full), pl.BlockSpec((1, D_CONV), full)],
        out_specs=pl.BlockSpec((tc, D_CONV), lambda i: (i, 0)),
        out_shape=jax.ShapeDtypeStruct((n, D_CONV), BF16),
        scratch_shapes=[pltpu.VMEM((tc + 2 * HALO, D_CONV), F32),
                        pltpu.VMEM((SUBLANES, tc + 2 * HALO - SUBLANES, D_CONV), F32)],
        compiler_params=_cparams("parallel"),
        name="conv",
    )(u, u, u, w, b, lg, lb)


def _attn_kernel(q_ref, kp_ref, kc_ref, kn_ref, vp_ref, vc_ref, vn_ref, tb_ref, o_ref, *, rows):
    i = pl.program_id(0)
    nkey = 3 * ATT_BLK
    lrow = lax.broadcasted_iota(I32, (1, nkey), 1) >> 6
    lane = lax.broadcasted_iota(I32, (1, 2 * HEAD_DIM), 1)
    masks = []
    for j in range(ATT_ROWS):
        r = i * ATT_ROWS + j
        rs = jnp.clip(r - WIN_H // 2, 0, rows - WIN_H)
        lo = rs - (i - 1) * ATT_ROWS
        masks.append(jnp.where((lrow >= lo) & (lrow < lo + WIN_H), 0.0, NEG))
    dn = (((1,), (1,)), ((), ()))

    def pair_slice(h):
        return slice(2 * HEAD_DIM * (h // 2), 2 * HEAD_DIM * (h // 2 + 1))

    def in_half(h):
        return (lane >= HEAD_DIM * (h % 2)) & (lane < HEAD_DIM * (h % 2 + 1))

    def scores(h):
        sl = pair_slice(h)
        q2 = q_ref[:, sl]
        qm = jnp.where(in_half(h), q2, jnp.zeros_like(q2))
        s = jnp.concatenate([lax.dot_general(qm, kk[:, sl], dn, preferred_element_type=F32)
                             for kk in (kp_ref, kc_ref, kn_ref)], axis=1)
        parts = []
        for j in range(ATT_ROWS):
            bias = jnp.concatenate([tb_ref[h, 2 * m - j + 3] for m in range(nkey // (2 * GRID_W))], axis=1)
            parts.append(s[j * GRID_W:(j + 1) * GRID_W, :] + bias + masks[j])
        return jnp.concatenate(parts, axis=0)

    def softmax(s):
        mx = jnp.max(s, axis=-1, keepdims=True)
        e = jnp.exp2(s - mx)
        return e.astype(BF16), jnp.sum(e, axis=-1, keepdims=True)

    def values(h, pb, den):
        sl = pair_slice(h)
        o = sum(jnp.dot(pb[:, ATT_BLK * t:ATT_BLK * (t + 1)], vv[:, sl], preferred_element_type=F32)
                for t, vv in enumerate((vp_ref, vc_ref, vn_ref)))
        return o / den

    stage_s = {h: scores(h) for h in range(2)}
    stage_p = {0: softmax(stage_s.pop(0))}
    out_pair = None
    for h in range(N_HEADS):
        if h + 2 < N_HEADS:
            stage_s[h + 2] = scores(h + 2)
        if h + 1 < N_HEADS:
            stage_p[h + 1] = softmax(stage_s.pop(h + 1))
        o = values(h, *stage_p.pop(h))
        if h % 2 == 0:
            out_pair = o
        else:
            o_ref[:, pair_slice(h)] = jnp.where(in_half(h), o, out_pair).astype(BF16)


def _attn(q, k, v, tb):
    n = q.shape[0]
    rows = n // GRID_W
    nb = n // ATT_BLK
    cur = lambda i: (i, 0)
    prev = lambda i: (jnp.maximum(i - 1, 0), 0)
    nxt = lambda i: (jnp.minimum(i + 1, nb - 1), 0)
    blk = (ATT_BLK, D_ATTN)
    return pl.pallas_call(
        functools.partial(_attn_kernel, rows=rows),
        grid=(nb,),
        in_specs=[pl.BlockSpec(blk, cur), pl.BlockSpec(blk, prev), pl.BlockSpec(blk, cur), pl.BlockSpec(blk, nxt),
                  pl.BlockSpec(blk, prev), pl.BlockSpec(blk, cur), pl.BlockSpec(blk, nxt),
                  pl.BlockSpec(tb.shape, lambda i: (0, 0, 0, 0))],
        out_specs=pl.BlockSpec(blk, cur),
        out_shape=jax.ShapeDtypeStruct((n, D_ATTN), BF16),
        compiler_params=_cparams("parallel"),
        name="attn",
    )(q, k, k, k, v, v, v, tb)


def _bias_table(rpb):
    cols = jnp.arange(GRID_W)
    start = jnp.clip(cols - WIN_W // 2, 0, GRID_W - WIN_W)
    kc = cols[None, :]
    inwin = (kc >= start[:, None]) & (kc < start[:, None] + WIN_W)
    off = GRID_W - WIN_W
    padded = jnp.pad(rpb, ((0, 0), (0, 0), (off, off)))
    rel = jnp.stack([padded[:, :, GRID_W - 1 - c:2 * GRID_W - 1 - c] for c in range(GRID_W)], axis=2)
    t = jnp.where(inwin[None, None], rel * LOG2E, NEG)
    t = jnp.concatenate([t, jnp.full_like(t[:, :1], NEG)], axis=1)
    t_next = jnp.concatenate([t[:, 1:], t[:, -1:]], axis=1)
    return jnp.concatenate([t, t_next], axis=-1).astype(F32)


def _mix_route_kernel(x_ref, uc_ref, at_ref, g1_ref, wg_ref, wco_ref, wao_ref, wo_ref, g2_ref,
                      wrh_ref, wrl_ref, rb_ref, tri_ref, wsgu_ref, wsd_ref,
                      xo_ref, hp_ref, idx_ref, rank_ref, wt_ref, cnt_ref, carry_ref, *, tm):
    i = pl.program_id(0)

    @pl.when(i == 0)
    def _():
        carry_ref[...] = jnp.zeros_like(carry_ref)

    x = x_ref[...]
    h = _rms(x, g1_ref[...]).astype(BF16)
    gates = jax.nn.sigmoid(jnp.dot(h, wg_ref[...], preferred_element_type=F32))
    y_conv = jnp.dot(uc_ref[...], wco_ref[...], preferred_element_type=F32)
    y_attn = jnp.dot(at_ref[...], wao_ref[...], preferred_element_type=F32)
    merged = gates[:, :D_MODEL] * y_conv + gates[:, D_MODEL:] * y_attn
    x1 = x + jnp.dot(merged.astype(BF16), wo_ref[...], preferred_element_type=F32)
    h2 = _rms(x1, g2_ref[...])
    h2b = h2.astype(BF16)
    hp_ref[...] = _pack_bf16_pair(h2[:, :D_MODEL // 2], h2[:, D_MODEL // 2:])

    gu = jnp.dot(h2b, wsgu_ref[...], preferred_element_type=F32)
    mid = (jax.nn.silu(gu[:, :D_SHARED]) * gu[:, D_SHARED:]).astype(BF16)
    xo_ref[...] = x1 + jnp.dot(mid, wsd_ref[...], preferred_element_type=F32)

    h2l = (h2 - h2b.astype(F32)).astype(BF16)
    dn = (((1,), (1,)), ((), ()))
    logits = (lax.dot_general(wrh_ref[...], h2b, dn, preferred_element_type=F32)
              + lax.dot_general(wrl_ref[...], h2b, dn, preferred_element_type=F32)
              + lax.dot_general(wrh_ref[...], h2l, dn, preferred_element_type=F32))
    scores = jax.nn.sigmoid(logits)
    biased = scores + rb_ref[...]

    sub = lax.broadcasted_iota(I32, (GROUP_SIZE, tm), 0).astype(F32)
    groups, gscore = [], []
    for g in range(N_GROUPS):
        bg = biased[g * GROUP_SIZE:(g + 1) * GROUP_SIZE, :]
        m1 = jnp.max(bg, axis=0, keepdims=True)
        first = jnp.min(jnp.where(bg == m1, sub, float(GROUP_SIZE)), axis=0, keepdims=True)
        m2 = jnp.max(jnp.where(sub == first, -jnp.inf, bg), axis=0, keepdims=True)
        groups.append(bg)
        gscore.append(m1 + m2)
    masked = []
    for g in range(N_GROUPS):
        beaten = jnp.zeros((1, tm), F32)
        for o in range(N_GROUPS):
            if o == g:
                continue
            wins = (gscore[o] >= gscore[g]) if o < g else (gscore[o] > gscore[g])
            beaten = beaten + jnp.where(wins, 1.0, 0.0)
        masked.append(jnp.where(beaten < TOPK_GROUPS, groups[g], -jnp.inf))
    cur = jnp.concatenate(masked, axis=0)

    eid = lax.broadcasted_iota(I32, (N_EXPERTS, tm), 0).astype(F32)
    idx_rows, sc_rows = [], []
    sel = jnp.zeros((N_EXPERTS, tm), F32)
    for _ in range(TOP_K):
        m = jnp.max(cur, axis=0, keepdims=True)
        idx = jnp.min(jnp.where(cur == m, eid, float(N_EXPERTS)), axis=0, keepdims=True)
        hit = eid == idx
        sel = jnp.where(hit, 1.0, sel)
        cur = jnp.where(hit, -jnp.inf, cur)
        idx_rows.append(idx)
        sc_rows.append(jnp.sum(jnp.where(hit, scores, 0.0), axis=0, keepdims=True))
    wsum = sc_rows[0]
    for k in range(1, TOP_K):
        wsum = wsum + sc_rows[k]

    prefix = jnp.dot(sel.astype(BF16), tri_ref[...], preferred_element_type=F32)
    rank = carry_ref[:, 0:1] + prefix
    rank_rows = [jnp.sum(jnp.where(eid == idx_rows[k], rank, 0.0), axis=0, keepdims=True) for k in range(TOP_K)]
    carry_ref[...] = carry_ref[...] + jnp.sum(sel, axis=1, keepdims=True)
    cnt_ref[...] = carry_ref[...]

    idx_ref[...] = jnp.concatenate(idx_rows, axis=0).astype(I32)
    rank_ref[...] = jnp.concatenate(rank_rows, axis=0).astype(I32)
    wt_ref[...] = jnp.concatenate([s / wsum * ROUTE_SCALE for s in sc_rows], axis=0)


def _mix_route(x, uc, at, g1, wg, wco, wao, wo, g2, wrh, wrl, rb, wsgu, wsd, tm=512):
    n = x.shape[0]
    tri = (jnp.arange(tm)[:, None] < jnp.arange(tm)[None, :]).astype(BF16)
    row = lambda i: (i, 0)
    col = lambda i: (0, i)
    full = lambda i: (0, 0)
    ins = [x, uc, at, g1, wg, wco, wao, wo, g2, wrh, wrl, rb, tri, wsgu, wsd]
    in_specs = [pl.BlockSpec((tm, D_MODEL), row), pl.BlockSpec((tm, D_CONV), row), pl.BlockSpec((tm, D_ATTN), row)]
    in_specs += [pl.BlockSpec(a.shape, full) for a in ins[3:]]
    return pl.pallas_call(
        functools.partial(_mix_route_kernel, tm=tm),
        grid=(n // tm,),
        in_specs=in_specs,
        out_specs=[pl.BlockSpec((tm, D_MODEL), row), pl.BlockSpec((tm, D_MODEL // 2), row),
                   pl.BlockSpec((TOP_K, tm), col), pl.BlockSpec((TOP_K, tm), col), pl.BlockSpec((TOP_K, tm), col),
                   pl.BlockSpec((N_EXPERTS, 128), full)],
        out_shape=[jax.ShapeDtypeStruct((n, D_MODEL), F32), jax.ShapeDtypeStruct((n, D_MODEL // 2), U32),
                   jax.ShapeDtypeStruct((TOP_K, n), I32), jax.ShapeDtypeStruct((TOP_K, n), I32),
                   jax.ShapeDtypeStruct((TOP_K, n), F32), jax.ShapeDtypeStruct((N_EXPERTS, 128), F32)],
        scratch_shapes=[pltpu.VMEM((N_EXPERTS, 128), F32)],
        compiler_params=_cparams("arbitrary"),
        name="mix_route",
    )(*ins)


def _sc_workers():
    info = plsc.get_sparse_core_info()
    return info.num_cores, info.num_subcores


def _sc_mesh():
    return plsc.VectorSubcoreMesh(core_axis_name="c", subcore_axis_name="s")


def _worker_id(n_cores):
    return lax.axis_index("s") * n_cores + lax.axis_index("c")


def _dispatch(slots3, hp, n_slots):
    n_chunks = slots3.shape[0]
    n_cores, n_sub = _sc_workers()
    per_worker = n_chunks // (n_cores * n_sub)
    assert per_worker * n_cores * n_sub == n_chunks

    @functools.partial(
        pl.kernel, mesh=_sc_mesh(),
        out_type=jax.ShapeDtypeStruct((n_slots, D_MODEL // 2), U32),
        scratch_types=[pltpu.VMEM((TOP_K, SC_CHUNK), I32), pltpu.VMEM((SC_CHUNK, D_MODEL // 2), U32),
                       pltpu.SemaphoreType.DMA],
        name="dispatch")
    def run(slots_hbm, hp_hbm, xs_hbm, idx_v, rows_v, sem):
        first = _worker_id(n_cores) * per_worker

        @pl.loop(0, per_worker)
        def _(j):
            c = first + j
            pltpu.sync_copy(slots_hbm.at[c], idx_v)
            pltpu.sync_copy(hp_hbm.at[pl.ds(c * SC_CHUNK, SC_CHUNK)], rows_v)
            copies = [pltpu.async_copy(rows_v, xs_hbm.at[idx_v.at[k]], sem) for k in range(TOP_K)]
            for cp in copies:
                cp.wait()

    return run(slots3, hp)


def _experts_kernel(bexp_ref, bidx_ref, bval_ref, nused_ref, next_ref, wslot_ref,
                    xs_ref, wg_ref, wu_ref, wd_ref, y_ref, xbuf, wg_buf, wu_buf, wd_buf, wgu_s, wd_s, xsem, wsem):
    b = pl.program_id(0)
    n_used = nused_ref[0]

    def block_copy(blk):
        slot = blk % XS_SLOTS
        rows = pl.ds(pl.multiple_of(blk * SLOT_BLK, SLOT_BLK), SLOT_BLK)
        return pltpu.make_async_copy(xs_ref.at[rows], xbuf.at[slot], xsem.at[slot])

    @pl.when(b == 0)
    def _():
        for blk in range(XS_AHEAD):
            @pl.when(blk < n_used)
            def _():
                block_copy(blk).start()

    @pl.when(b + XS_AHEAD < n_used)
    def _():
        block_copy(b + XS_AHEAD).start()

    def weight_copies(e, slot):
        return [pltpu.make_async_copy(wg_ref.at[e], wg_buf.at[slot], wsem.at[slot]),
                pltpu.make_async_copy(wu_ref.at[e], wu_buf.at[slot], wsem.at[slot]),
                pltpu.make_async_copy(wd_ref.at[e], wd_buf.at[slot], wsem.at[slot])]

    @pl.when(b < n_used)
    def _():
        e = bexp_ref[b]
        slot = wslot_ref[e]

        @pl.when(b == 0)
        def _():
            for cp in weight_copies(e, slot):
                cp.start()

        @pl.when((b == 0) | (bexp_ref[jnp.maximum(b - 1, 0)] != e))
        def _():
            for cp in weight_copies(e, slot):
                cp.wait()
            wgu_s[:, :D_EXPERT] = wg_buf[slot].astype(BF16)
            wgu_s[:, D_EXPERT:] = wu_buf[slot].astype(BF16)
            wd_s[...] = wd_buf[slot].astype(BF16)
            nxt = next_ref[e]

            @pl.when(nxt >= 0)
            def _():
                for cp in weight_copies(nxt, 1 - slot):
                    cp.start()

        srow = lax.broadcasted_iota(I32, (SLOT_BLK, 1), 0)
        block_copy(b).wait()
        xw = jnp.where(srow < bval_ref[b], xbuf[b % XS_SLOTS], jnp.uint32(0))
        xa, xb = _unpack_bf16_pair(xw)
        x = jnp.concatenate([xa.astype(BF16), xb.astype(BF16)], axis=1)
        gu = jnp.dot(x, wgu_s[...], preferred_element_type=F32)
        mid = (jax.nn.silu(gu[:, :D_EXPERT]) * gu[:, D_EXPERT:]).astype(BF16)
        y = jnp.dot(mid, wd_s[...], preferred_element_type=F32)
        y_ref[...] = _pack_bf16_pair(y[:, :D_MODEL // 2], y[:, D_MODEL // 2:])


def _experts(bexp, bidx, bval, nused, next_exp, wslot, xs, wg, wu, wd):
    n_blocks = bexp.shape[0]
    xmap = lambda b, be, bi, bv, nu, nx, ws: (bi[b], 0)
    grid_spec = pltpu.PrefetchScalarGridSpec(
        num_scalar_prefetch=6,
        grid=(n_blocks,),
        in_specs=[pl.BlockSpec(memory_space=pl.ANY),
                  pl.BlockSpec(memory_space=pl.ANY), pl.BlockSpec(memory_space=pl.ANY),
                  pl.BlockSpec(memory_space=pl.ANY)],
        out_specs=pl.BlockSpec((SLOT_BLK, D_MODEL // 2), xmap),
        scratch_shapes=[pltpu.VMEM((XS_SLOTS, SLOT_BLK, D_MODEL // 2), U32),
                        pltpu.VMEM((2, D_MODEL, D_EXPERT), F32), pltpu.VMEM((2, D_MODEL, D_EXPERT), F32),
                        pltpu.VMEM((2, D_EXPERT, D_MODEL), F32),
                        pltpu.VMEM((D_MODEL, 2 * D_EXPERT), BF16), pltpu.VMEM((D_EXPERT, D_MODEL), BF16),
                        pltpu.SemaphoreType.DMA((XS_SLOTS,)), pltpu.SemaphoreType.DMA((2,))],
    )
    return pl.pallas_call(
        _experts_kernel,
        grid_spec=grid_spec,
        out_shape=jax.ShapeDtypeStruct(xs.shape, U32),
        compiler_params=_cparams("arbitrary"),
        name="experts",
    )(bexp, bidx, bval, nused, next_exp, wslot, xs, wg, wu, wd)


def _gather(slots3, y, n):
    n_chunks = slots3.shape[0]
    n_cores, n_sub = _sc_workers()
    per_worker = n_chunks // (n_cores * n_sub)
    assert per_worker * n_cores * n_sub == n_chunks

    @functools.partial(
        pl.kernel, mesh=_sc_mesh(),
        out_type=jax.ShapeDtypeStruct((TOP_K, n, D_MODEL // 2), U32),
        scratch_types=[pltpu.VMEM((TOP_K, SC_CHUNK), I32),
                       pltpu.VMEM((SC_CHUNK, D_MODEL // 2), U32), pltpu.VMEM((SC_CHUNK, D_MODEL // 2), U32),
                       pltpu.SemaphoreType.DMA, pltpu.SemaphoreType.DMA, pltpu.SemaphoreType.DMA],
        name="gather")
    def run(slots_hbm, y_hbm, g_hbm, idx_v, buf0, buf1, gsem, wsem0, wsem1):
        first = _worker_id(n_cores) * per_worker
        bufs = (buf0, buf1)
        wsems = (wsem0, wsem1)

        @pl.loop(0, per_worker)
        def _(j):
            c = first + j
            pltpu.sync_copy(slots_hbm.at[c], idx_v)
            gathers = [None] * TOP_K
            writes = [None] * TOP_K
            gathers[0] = pltpu.async_copy(y_hbm.at[idx_v.at[0]], bufs[0], gsem)
            for k in range(TOP_K):
                gathers[k].wait()
                if k >= 1:
                    writes[k - 1].wait()
                if k + 1 < TOP_K:
                    gathers[k + 1] = pltpu.async_copy(y_hbm.at[idx_v.at[k + 1]], bufs[(k + 1) % 2], gsem)
                writes[k] = pltpu.async_copy(bufs[k % 2], g_hbm.at[k, pl.ds(c * SC_CHUNK, SC_CHUNK)],
                                             wsems[k % 2])
            writes[TOP_K - 1].wait()

    return run(slots3, y)


def _combine_kernel(wt_ref, xo_ref, g_ref, o_ref):
    half = D_MODEL // 2
    acc_a = xo_ref[:, :half]
    acc_b = xo_ref[:, half:]
    for k in range(TOP_K):
        ya, yb = _unpack_bf16_pair(g_ref[k])
        w = wt_ref[:, k:k + 1]
        acc_a = acc_a + w * ya
        acc_b = acc_b + w * yb
    o_ref[:, :half] = acc_a
    o_ref[:, half:] = acc_b


def _combine(wt, xo, g, tm=256):
    n = xo.shape[0]
    return pl.pallas_call(
        _combine_kernel,
        grid=(n // tm,),
        in_specs=[pl.BlockSpec((tm, TOP_K), lambda i: (i, 0)),
                  pl.BlockSpec((tm, D_MODEL), lambda i: (i, 0)),
                  pl.BlockSpec((TOP_K, tm, D_MODEL // 2), lambda i: (0, i, 0))],
        out_specs=pl.BlockSpec((tm, D_MODEL), lambda i: (i, 0)),
        out_shape=jax.ShapeDtypeStruct((n, D_MODEL), F32),
        compiler_params=_cparams("parallel"),
        name="combine",
    )(wt, xo, g)


def _layer(x, norm_mix_g, w_in, q_norm_g, k_norm_g, rpb, conv_dw_w, conv_dw_b, conv_ln_g, conv_ln_b,
           w_conv_out, w_attn_out, w_o, norm_ffn_g, w_router, router_bias, w_exp_gate, w_exp_up,
           w_exp_down, w_sh_gate, w_sh_up, w_sh_down):
    n = x.shape[0]
    c_qkv = 2 * D_CONV + 3 * D_ATTN
    w_in_b = w_in.astype(BF16)
    head_of = jnp.arange(D_ATTN) // HEAD_DIM
    bsum = (head_of[:, None] == head_of[None, :]).astype(BF16)
    row = lambda v: v.reshape(1, -1).astype(F32)

    u, q, k, v = _inproj(x, row(norm_mix_g), w_in_b[:, :c_qkv], bsum,
                         row(jnp.tile(q_norm_g, N_HEADS)), row(jnp.tile(k_norm_g, N_HEADS)))
    uc = _conv(u, conv_dw_w.reshape(CONV_WIDTH, D_CONV), row(conv_dw_b), row(conv_ln_g), row(conv_ln_b))
    at = _attn(q, k, v, _bias_table(rpb))

    wr_t = w_router.T
    wr_hi = wr_t.astype(BF16)
    wr_lo = (wr_t - wr_hi.astype(F32)).astype(BF16)
    wsgu = jnp.concatenate([w_sh_gate, w_sh_up], axis=1).astype(BF16)
    xo, hp, idx_t, rank_t, wt_t, cnt = _mix_route(
        x, uc, at, row(norm_mix_g), w_in_b[:, c_qkv:], w_conv_out.astype(BF16), w_attn_out.astype(BF16),
        w_o.astype(BF16), row(norm_ffn_g), wr_hi, wr_lo, router_bias.reshape(N_EXPERTS, 1).astype(F32),
        wsgu, w_sh_down.astype(BF16))

    counts = cnt[:, 0].astype(I32)
    padded = (counts + SLOT_BLK - 1) // SLOT_BLK * SLOT_BLK
    pad_end = jnp.cumsum(padded)
    pad_start = pad_end - padded
    n_blocks = (n * TOP_K + N_EXPERTS * (SLOT_BLK - 1)) // SLOT_BLK
    n_used = pad_end[-1] // SLOT_BLK
    bidx = jnp.minimum(jnp.arange(n_blocks, dtype=I32), n_used - 1)
    is_done = (pad_end[None, :] <= (bidx * SLOT_BLK)[:, None]).astype(I32)
    bexp = jnp.minimum(jnp.sum(is_done, axis=1), N_EXPERTS - 1)
    owner = (bexp[:, None] == jnp.arange(N_EXPERTS)[None, :]).astype(I32)
    bval = jnp.sum(owner * (pad_start + counts)[None, :], axis=1) - bidx * SLOT_BLK
    bval = jnp.clip(bval, 0, SLOT_BLK).astype(I32)
    eids = jnp.arange(N_EXPERTS, dtype=I32)
    later = (eids[None, :] > eids[:, None]) & (counts[None, :] > 0)
    next_exp = jnp.min(jnp.where(later, eids[None, :], N_EXPERTS), axis=1)
    next_exp = jnp.where(next_exp == N_EXPERTS, -1, next_exp).astype(I32)
    wslot = ((jnp.cumsum((counts > 0).astype(I32)) - 1) & 1).astype(I32)

    onehot = (idx_t[:, :, None] == jnp.arange(N_EXPERTS)[None, None, :]).astype(I32)
    slots = jnp.sum(onehot * pad_start[None, None, :], axis=2) + rank_t
    slots3 = slots.reshape(TOP_K, n // SC_CHUNK, SC_CHUNK).transpose(1, 0, 2)

    xs = _dispatch(slots3, hp, n_blocks * SLOT_BLK)
    y = _experts(bexp, bidx, bval, n_used.reshape(1).astype(I32), next_exp, wslot, xs,
                 w_exp_gate, w_exp_up, w_exp_down)
    return _combine(wt_t.T, xo, _gather(slots3, y, n))


def kernel(x, norm_mix_g, w_in, q_norm_g, k_norm_g, rpb, conv_dw_w, conv_dw_b, conv_ln_g, conv_ln_b, w_conv_out, w_attn_out, w_o, norm_ffn_g, w_router, router_bias, w_exp_gate, w_exp_up, w_exp_down, w_sh_gate, w_sh_up, w_sh_down):
    b, s, d = x.shape
    assert b == 1 and d == D_MODEL and s % (2 * ATT_BLK) == 0, x.shape
    xf = x.reshape(b * s, d)
    depth = norm_mix_g.shape[0]
    for l in range(depth):
        xf = _layer(xf, norm_mix_g[l], w_in[l], q_norm_g[l], k_norm_g[l], rpb[l], conv_dw_w[l], conv_dw_b[l],
                    conv_ln_g[l], conv_ln_b[l], w_conv_out[l], w_attn_out[l], w_o[l], norm_ffn_g[l],
                    w_router[l], router_bias[l], w_exp_gate[l], w_exp_up[l], w_exp_down[l], w_sh_gate[l],
                    w_sh_up[l], w_sh_down[l])
    return xf.reshape(b, s, d)
```

```python
import functools

import jax
import jax.numpy as jnp
from jax import lax
from jax.experimental import pallas as pl
from jax.experimental.pallas import tpu as pltpu
from jax.experimental.pallas import tpu_sc as plsc

D_MODEL = 1024
GRID_W = 64
D_CONV = 512
CONV_WIDTH = 31
CONV_PAD = CONV_WIDTH // 2
N_HEADS = 8
HEAD_DIM = 64
D_ATTN = N_HEADS * HEAD_DIM
WIN_H = 8
WIN_W = 16
N_EXPERTS = 64
TOP_K = 8
N_GROUPS = 8
GROUP_SIZE = N_EXPERTS // N_GROUPS
TOPK_GROUPS = 4
D_EXPERT = 256
D_SHARED = 256
ROUTE_SCALE = 2.5
EPS = 1e-6
LOG2E = 1.4426950408889634

SUBLANES = 8
NEG = -1e30
HALO = 16
ATT_ROWS = 4
ATT_BLK = ATT_ROWS * GRID_W
SLOT_BLK = 512
MIX_SPLIT = 2
XS_AHEAD = 2
XS_SLOTS = XS_AHEAD + 1
SC_LANES = 16
SC_TOK = 8
SC_CHUNK = 64
VMEM_LIMIT = 56 * 1024 * 1024

F32 = jnp.float32
BF16 = jnp.bfloat16
I32 = jnp.int32
U32 = jnp.uint32


def _cparams(*sem):
    return pltpu.CompilerParams(dimension_semantics=sem, vmem_limit_bytes=VMEM_LIMIT)


def _rms(x, g):
    return x * lax.rsqrt(jnp.mean(x * x, axis=-1, keepdims=True) + EPS) * g


def _pack_bf16_pair(a, b):
    def bf16_bits(v):
        return lax.bitcast_convert_type(v.astype(BF16).astype(F32), U32)
    return bf16_bits(a) | (bf16_bits(b) >> 16)


def _unpack_bf16_pair(w):
    a = lax.bitcast_convert_type(w & jnp.uint32(0xFFFF0000), F32)
    b = lax.bitcast_convert_type(w << 16, F32)
    return a, b


def _inproj_kernel(x_ref, g_ref, w_ref, bsum_ref, qg_ref, kg_ref, u_ref, q_ref, k_ref, v_ref):
    h = _rms(x_ref[...], g_ref[...]).astype(BF16)
    ag = jnp.dot(h, w_ref[:, 0:2 * D_CONV], preferred_element_type=F32)
    u_ref[...] = ag[:, :D_CONV] * jax.nn.sigmoid(ag[:, D_CONV:])
    bsum = bsum_ref[...]

    def head_norm(z, g):
        ss = jnp.dot((z * z).astype(BF16), bsum, preferred_element_type=F32)
        return z * lax.rsqrt(ss * (1.0 / HEAD_DIM) + EPS) * g

    c0 = 2 * D_CONV
    q = jnp.dot(h, w_ref[:, c0:c0 + D_ATTN], preferred_element_type=F32)
    q_ref[...] = (head_norm(q, qg_ref[...]) * (HEAD_DIM ** -0.5 * LOG2E)).astype(BF16)
    k = jnp.dot(h, w_ref[:, c0 + D_ATTN:c0 + 2 * D_ATTN], preferred_element_type=F32)
    k_ref[...] = head_norm(k, kg_ref[...]).astype(BF16)
    v = jnp.dot(h, w_ref[:, c0 + 2 * D_ATTN:c0 + 3 * D_ATTN], preferred_element_type=F32)
    v_ref[...] = v.astype(BF16)


def _inproj(x, g, w, bsum, qg, kg, tm=512):
    n = x.shape[0]
    wc = w.shape[1]
    full = lambda i: (0, 0)
    row = lambda i: (i, 0)
    return pl.pallas_call(
        _inproj_kernel,
        grid=(n // tm,),
        in_specs=[pl.BlockSpec((tm, D_MODEL), row), pl.BlockSpec((1, D_MODEL), full),
                  pl.BlockSpec((D_MODEL, wc), full), pl.BlockSpec((D_ATTN, D_ATTN), full),
                  pl.BlockSpec((1, D_ATTN), full), pl.BlockSpec((1, D_ATTN), full)],
        out_specs=[pl.BlockSpec((tm, D_CONV), row), pl.BlockSpec((tm, D_ATTN), row),
                   pl.BlockSpec((tm, D_ATTN), row), pl.BlockSpec((tm, D_ATTN), row)],
        out_shape=[jax.ShapeDtypeStruct((n, D_CONV), F32), jax.ShapeDtypeStruct((n, D_ATTN), BF16),
                   jax.ShapeDtypeStruct((n, D_ATTN), BF16), jax.ShapeDtypeStruct((n, D_ATTN), BF16)],
        compiler_params=_cparams("parallel"),
        name="inproj",
    )(x, g, w, bsum, qg, kg)


def _conv_kernel(up_ref, uc_ref, un_ref, w_ref, b_ref, lg_ref, lb_ref, o_ref, ext_ref, sh_ref, *, tc, ch):
    i = pl.program_id(0)
    last = pl.num_programs(0) - 1
    ext_ref[0:HALO, :] = jnp.where(i > 0, up_ref[...], 0.0)
    ext_ref[HALO:HALO + tc, :] = uc_ref[...]
    ext_ref[HALO + tc:2 * HALO + tc, :] = jnp.where(i < last, un_ref[...], 0.0)
    span = sh_ref.shape[1]
    for b in range(SUBLANES):
        sh_ref[b] = ext_ref[b:b + span, :]
    bias = b_ref[...]
    lg = lg_ref[...]
    lb = lb_ref[...]
    for c in range(tc // ch):
        acc = jnp.broadcast_to(bias, (ch, D_CONV))
        for j in range(CONV_WIDTH):
            shift = HALO - CONV_PAD + j
            row0 = c * ch + shift // SUBLANES * SUBLANES
            acc = acc + sh_ref[shift % SUBLANES, row0:row0 + ch, :] * w_ref[j:j + 1, :]
        mu = jnp.mean(acc, axis=-1, keepdims=True)
        d = acc - mu
        var = jnp.mean(d * d, axis=-1, keepdims=True)
        y = d * lax.rsqrt(var + EPS) * lg + lb
        o_ref[c * ch:(c + 1) * ch, :] = (y * jax.nn.sigmoid(y)).astype(BF16)


def _conv(u, w, b, lg, lb, tc=256, ch=64):
    n = u.shape[0]
    hb = tc // HALO
    nh = n // HALO
    full = lambda i: (0, 0)
    return pl.pallas_call(
        functools.partial(_conv_kernel, tc=tc, ch=ch),
        grid=(n // tc,),
        in_specs=[pl.BlockSpec((HALO, D_CONV), lambda i: (jnp.maximum(i * hb - 1, 0), 0)),
                  pl.BlockSpec((tc, D_CONV), lambda i: (i, 0)),
                  pl.BlockSpec((HALO, D_CONV), lambda i: (jnp.minimum((i + 1) * hb, nh - 1), 0)),
                  pl.BlockSpec((CONV_WIDTH, D_CONV), full), pl.BlockSpec((1, D_CONV), full),
                  pl.BlockSpec((1, D_CONV), full), pl.BlockSpec((1, D_CONV), full)],
        out_specs=pl.BlockSpec((tc, D_CONV), lambda i: (i, 0)),
        out_shape=jax.ShapeDtypeStruct((n, D_CONV), BF16),
        scratch_shapes=[pltpu.VMEM((tc + 2 * HALO, D_CONV), F32),
                        pltpu.VMEM((SUBLANES, tc + 2 * HALO - SUBLANES, D_CONV), F32)],
        compiler_params=_cparams("parallel"),
        name="conv",
    )(u, u, u, w, b, lg, lb)


def _attn_kernel(q_ref, kp_ref, kc_ref, kn_ref, vp_ref, vc_ref, vn_ref, tb_ref, o_ref, *, rows):
    i = pl.program_id(0)
    nkey = 3 * ATT_BLK
    lrow = lax.broadcasted_iota(I32, (1, nkey), 1) >> 6
    lane = lax.broadcasted_iota(I32, (1, 2 * HEAD_DIM), 1)
    masks = []
    for j in range(ATT_ROWS):
        r = i * ATT_ROWS + j
        rs = jnp.clip(r - WIN_H // 2, 0, rows - WIN_H)
        lo = rs - (i - 1) * ATT_ROWS
        masks.append(jnp.where((lrow >= lo) & (lrow < lo + WIN_H), 0.0, NEG))
    dn = (((1,), (1,)), ((), ()))

    def pair_slice(h):
        return slice(2 * HEAD_DIM * (h // 2), 2 * HEAD_DIM * (h // 2 + 1))

    def in_half(h):
        return (lane >= HEAD_DIM * (h % 2)) & (lane < HEAD_DIM * (h % 2 + 1))

    def scores(h):
        sl = pair_slice(h)
        q2 = q_ref[:, sl]
        qm = jnp.where(in_half(h), q2, jnp.zeros_like(q2))
        s = jnp.concatenate([lax.dot_general(qm, kk[:, sl], dn, preferred_element_type=F32)
                             for kk in (kp_ref, kc_ref, kn_ref)], axis=1)
        parts = []
        for j in range(ATT_ROWS):
            bias = jnp.concatenate([tb_ref[h, 2 * m - j + 3] for m in range(nkey // (2 * GRID_W))], axis=1)
            parts.append(s[j * GRID_W:(j + 1) * GRID_W, :] + bias + masks[j])
        return jnp.concatenate(parts, axis=0)

    def softmax(s):
        mx = jnp.max(s, axis=-1, keepdims=True)
        e = jnp.exp2(s - mx)
        return e.astype(BF16), jnp.sum(e, axis=-1, keepdims=True)

    def values(h, pb, den):
        sl = pair_slice(h)
        o = sum(jnp.dot(pb[:, ATT_BLK * t:ATT_BLK * (t + 1)], vv[:, sl], preferred_element_type=F32)
                for t, vv in enumerate((vp_ref, vc_ref, vn_ref)))
        return o / den

    stage_s = {h: scores(h) for h in range(2)}
    stage_p = {0: softmax(stage_s.pop(0))}
    out_pair = None
    for h in range(N_HEADS):
        if h + 2 < N_HEADS:
            stage_s[h + 2] = scores(h + 2)
        if h + 1 < N_HEADS:
            stage_p[h + 1] = softmax(stage_s.pop(h + 1))
        o = values(h, *stage_p.pop(h))
        if h % 2 == 0:
            out_pair = o
        else:
            o_ref[:, pair_slice(h)] = jnp.where(in_half(h), o, out_pair).astype(BF16)


def _attn(q, k, v, tb):
    n = q.shape[0]
    rows = n // GRID_W
    nb = n // ATT_BLK
    cur = lambda i: (i, 0)
    prev = lambda i: (jnp.maximum(i - 1, 0), 0)
    nxt = lambda i: (jnp.minimum(i + 1, nb - 1), 0)
    blk = (ATT_BLK, D_ATTN)
    return pl.pallas_call(
        functools.partial(_attn_kernel, rows=rows),
        grid=(nb,),
        in_specs=[pl.BlockSpec(blk, cur), pl.BlockSpec(blk, prev), pl.BlockSpec(blk, cur), pl.BlockSpec(blk, nxt),
                  pl.BlockSpec(blk, prev), pl.BlockSpec(blk, cur), pl.BlockSpec(blk, nxt),
                  pl.BlockSpec(tb.shape, lambda i: (0, 0, 0, 0))],
        out_specs=pl.BlockSpec(blk, cur),
        out_shape=jax.ShapeDtypeStruct((n, D_ATTN), BF16),
        compiler_params=_cparams("parallel"),
        name="attn",
    )(q, k, k, k, v, v, v, tb)


def _bias_table(rpb):
    cols = jnp.arange(GRID_W)
    start = jnp.clip(cols - WIN_W // 2, 0, GRID_W - WIN_W)
    kc = cols[None, :]
    inwin = (kc >= start[:, None]) & (kc < start[:, None] + WIN_W)
    off = GRID_W - WIN_W
    period = 2 * GRID_W
    padded = jnp.pad(rpb, ((0, 0), (0, 0), (off, off + 1)))
    tiled = jnp.tile(padded, (1, 1, GRID_W))[:, :, :GRID_W * (period - 1)]
    rel = tiled.reshape(rpb.shape[0], rpb.shape[1], GRID_W, period - 1)[..., GRID_W - 1:]
    t = jnp.where(inwin[None, None], rel * LOG2E, NEG)
    t = jnp.concatenate([t, jnp.full_like(t[:, :1], NEG)], axis=1)
    t_next = jnp.concatenate([t[:, 1:], t[:, -1:]], axis=1)
    return jnp.concatenate([t, t_next], axis=-1).astype(F32)


def _mix_route_kernel(x_ref, uc_ref, at_ref, g1_ref, wg_ref, wco_ref, wao_ref, wo_ref, g2_ref,
                      wrh_ref, wrl_ref, rb_ref, tri_ref, wsgu_ref, wsd_ref,
                      xo_ref, hp_ref, idx_ref, rank_ref, wt_ref, cnt_ref, carry_ref, *, tm):
    i = pl.program_id(0)

    @pl.when(i == 0)
    def _():
        carry_ref[...] = jnp.zeros_like(carry_ref)

    th = tm // MIX_SPLIT
    dn = (((1,), (1,)), ((), ()))

    def mix(p):
        rs = slice(p * th, (p + 1) * th)
        x = x_ref[rs, :]
        h = _rms(x, g1_ref[...]).astype(BF16)
        gates = jax.nn.sigmoid(jnp.dot(h, wg_ref[...], preferred_element_type=F32))
        y_conv = jnp.dot(uc_ref[rs, :], wco_ref[...], preferred_element_type=F32)
        y_attn = jnp.dot(at_ref[rs, :], wao_ref[...], preferred_element_type=F32)
        merged = gates[:, :D_MODEL] * y_conv + gates[:, D_MODEL:] * y_attn
        x1 = x + jnp.dot(merged.astype(BF16), wo_ref[...], preferred_element_type=F32)
        h2 = _rms(x1, g2_ref[...])
        hp_ref[rs, :] = _pack_bf16_pair(h2[:, :D_MODEL // 2], h2[:, D_MODEL // 2:])
        return x1, h2

    def shared_and_logits(p, x1, h2):
        rs = slice(p * th, (p + 1) * th)
        h2b = h2.astype(BF16)
        gu = jnp.dot(h2b, wsgu_ref[...], preferred_element_type=F32)
        mid = (jax.nn.silu(gu[:, :D_SHARED]) * gu[:, D_SHARED:]).astype(BF16)
        xo_ref[rs, :] = x1 + jnp.dot(mid, wsd_ref[...], preferred_element_type=F32)
        h2l = (h2 - h2b.astype(F32)).astype(BF16)
        return (lax.dot_general(wrh_ref[...], h2b, dn, preferred_element_type=F32)
                + lax.dot_general(wrl_ref[...], h2b, dn, preferred_element_type=F32)
                + lax.dot_general(wrh_ref[...], h2l, dn, preferred_element_type=F32))

    def route(p, logits, carry):
        cs = slice(p * th, (p + 1) * th)
        scores = jax.nn.sigmoid(logits)
        biased = scores + rb_ref[...]

        sub = lax.broadcasted_iota(I32, (GROUP_SIZE, th), 0).astype(F32)
        groups, gscore = [], []
        for g in range(N_GROUPS):
            bg = biased[g * GROUP_SIZE:(g + 1) * GROUP_SIZE, :]
            m1 = jnp.max(bg, axis=0, keepdims=True)
            first = jnp.min(jnp.where(bg == m1, sub, float(GROUP_SIZE)), axis=0, keepdims=True)
            m2 = jnp.max(jnp.where(sub == first, -jnp.inf, bg), axis=0, keepdims=True)
            groups.append(bg)
            gscore.append(m1 + m2)
        masked = []
        for g in range(N_GROUPS):
            beaten = jnp.zeros((1, th), F32)
            for o in range(N_GROUPS):
                if o == g:
                    continue
                wins = (gscore[o] >= gscore[g]) if o < g else (gscore[o] > gscore[g])
                beaten = beaten + jnp.where(wins, 1.0, 0.0)
            masked.append(jnp.where(beaten < TOPK_GROUPS, groups[g], -jnp.inf))
        cur = jnp.concatenate(masked, axis=0)

        eid = lax.broadcasted_iota(I32, (N_EXPERTS, th), 0).astype(F32)
        idx_rows, sc_rows = [], []
        sel = jnp.zeros((N_EXPERTS, th), F32)
        for _ in range(TOP_K):
            m = jnp.max(cur, axis=0, keepdims=True)
            idx = jnp.min(jnp.where(cur == m, eid, float(N_EXPERTS)), axis=0, keepdims=True)
            hit = eid == idx
            sel = jnp.where(hit, 1.0, sel)
            cur = jnp.where(hit, -jnp.inf, cur)
            idx_rows.append(idx)
            sc_rows.append(jnp.sum(jnp.where(hit, scores, 0.0), axis=0, keepdims=True))
        wsum = sc_rows[0]
        for k in range(1, TOP_K):
            wsum = wsum + sc_rows[k]

        prefix = jnp.dot(sel.astype(BF16), tri_ref[...], preferred_element_type=F32)
        rank = carry + prefix
        rank_rows = [jnp.sum(jnp.where(eid == idx_rows[k], rank, 0.0), axis=0, keepdims=True)
                     for k in range(TOP_K)]
        idx_ref[:, cs] = jnp.concatenate(idx_rows, axis=0).astype(I32)
        rank_ref[:, cs] = jnp.concatenate(rank_rows, axis=0).astype(I32)
        wt_ref[:, cs] = jnp.concatenate([s / wsum * ROUTE_SCALE for s in sc_rows], axis=0)
        return carry + jnp.sum(sel, axis=1, keepdims=True)

    carry = carry_ref[:, 0:1]
    logits = shared_and_logits(0, *mix(0))
    for p in range(MIX_SPLIT):
        if p + 1 < MIX_SPLIT:
            mixed_next = mix(p + 1)
        carry = route(p, logits, carry)
        if p + 1 < MIX_SPLIT:
            logits = shared_and_logits(p + 1, *mixed_next)
    carry_ref[...] = jnp.broadcast_to(carry, carry_ref.shape)
    cnt_ref[...] = carry_ref[...]


def _mix_route(x, uc, at, g1, wg, wco, wao, wo, g2, wrh, wrl, rb, wsgu, wsd, tm=1024):
    n = x.shape[0]
    th = tm // MIX_SPLIT
    tri = (jnp.arange(th)[:, None] < jnp.arange(th)[None, :]).astype(BF16)
    row = lambda i: (i, 0)
    col = lambda i: (0, i)
    full = lambda i: (0, 0)
    ins = [x, uc, at, g1, wg, wco, wao, wo, g2, wrh, wrl, rb, tri, wsgu, wsd]
    in_specs = [pl.BlockSpec((tm, D_MODEL), row), pl.BlockSpec((tm, D_CONV), row), pl.BlockSpec((tm, D_ATTN), row)]
    in_specs += [pl.BlockSpec(a.shape, full) for a in ins[3:]]
    return pl.pallas_call(
        functools.partial(_mix_route_kernel, tm=tm),
        grid=(n // tm,),
        in_specs=in_specs,
        out_specs=[pl.BlockSpec((tm, D_MODEL), row), pl.BlockSpec((tm, D_MODEL // 2), row),
                   pl.BlockSpec((TOP_K, tm), col), pl.BlockSpec((TOP_K, tm), col), pl.BlockSpec((TOP_K, tm), col),
                   pl.BlockSpec((N_EXPERTS, 128), full)],
        out_shape=[jax.ShapeDtypeStruct((n, D_MODEL), F32), jax.ShapeDtypeStruct((n, D_MODEL // 2), U32),
                   jax.ShapeDtypeStruct((TOP_K, n), I32), jax.ShapeDtypeStruct((TOP_K, n), I32),
                   jax.ShapeDtypeStruct((TOP_K, n), F32), jax.ShapeDtypeStruct((N_EXPERTS, 128), F32)],
        scratch_shapes=[pltpu.VMEM((N_EXPERTS, 128), F32)],
        compiler_params=_cparams("arbitrary"),
        name="mix_route",
    )(*ins)


def _sc_workers():
    info = plsc.get_sparse_core_info()
    return info.num_cores, info.num_subcores


def _sc_mesh():
    return plsc.VectorSubcoreMesh(core_axis_name="c", subcore_axis_name="s")


def _worker_id(n_cores):
    return lax.axis_index("s") * n_cores + lax.axis_index("c")


def _dispatch(slots3, hp, n_slots):
    n_chunks = slots3.shape[0]
    n_cores, n_sub = _sc_workers()
    per_worker = n_chunks // (n_cores * n_sub)
    assert per_worker * n_cores * n_sub == n_chunks

    @functools.partial(
        pl.kernel, mesh=_sc_mesh(),
        out_type=jax.ShapeDtypeStruct((n_slots, D_MODEL // 2), U32),
        scratch_types=[pltpu.VMEM((TOP_K, SC_CHUNK), I32), pltpu.VMEM((SC_CHUNK, D_MODEL // 2), U32),
                       pltpu.SemaphoreType.DMA],
        name="dispatch")
    def run(slots_hbm, hp_hbm, xs_hbm, idx_v, rows_v, sem):
        first = _worker_id(n_cores) * per_worker

        @pl.loop(0, per_worker)
        def _(j):
            c = first + j
            pltpu.sync_copy(slots_hbm.at[c], idx_v)
            pltpu.sync_copy(hp_hbm.at[pl.ds(c * SC_CHUNK, SC_CHUNK)], rows_v)
            copies = [pltpu.async_copy(rows_v, xs_hbm.at[idx_v.at[k]], sem) for k in range(TOP_K)]
            for cp in copies:
                cp.wait()

    return run(slots3, hp)


def _experts_kernel(bexp_ref, bidx_ref, bval_ref, nused_ref, next_ref, wslot_ref,
                    xs_ref, wg_ref, wu_ref, wd_ref, y_ref, xbuf, wg_buf, wu_buf, wd_buf, wgu_s, wd_s, xsem, wsem):
    b = pl.program_id(0)
    n_used = nused_ref[0]

    def block_copy(blk):
        slot = blk % XS_SLOTS
        rows = pl.ds(pl.multiple_of(blk * SLOT_BLK, SLOT_BLK), SLOT_BLK)
        return pltpu.make_async_copy(xs_ref.at[rows], xbuf.at[slot], xsem.at[slot])

    @pl.when(b == 0)
    def _():
        for blk in range(XS_AHEAD):
            @pl.when(blk < n_used)
            def _():
                block_copy(blk).start()

    @pl.when(b + XS_AHEAD < n_used)
    def _():
        block_copy(b + XS_AHEAD).start()

    def weight_copies(e, slot):
        return [pltpu.make_async_copy(wg_ref.at[e], wg_buf.at[slot], wsem.at[slot]),
                pltpu.make_async_copy(wu_ref.at[e], wu_buf.at[slot], wsem.at[slot]),
                pltpu.make_async_copy(wd_ref.at[e], wd_buf.at[slot], wsem.at[slot])]

    @pl.when(b < n_used)
    def _():
        e = bexp_ref[b]
        slot = wslot_ref[e]

        @pl.when(b == 0)
        def _():
            for cp in weight_copies(e, slot):
                cp.start()

        @pl.when((b == 0) | (bexp_ref[jnp.maximum(b - 1, 0)] != e))
        def _():
            for cp in weight_copies(e, slot):
                cp.wait()
            wgu_s[:, :D_EXPERT] = wg_buf[slot].astype(BF16)
            wgu_s[:, D_EXPERT:] = wu_buf[slot].astype(BF16)
            wd_s[...] = wd_buf[slot].astype(BF16)
            nxt = next_ref[e]

            @pl.when(nxt >= 0)
            def _():
                for cp in weight_copies(nxt, 1 - slot):
                    cp.start()

        srow = lax.broadcasted_iota(I32, (SLOT_BLK, 1), 0)
        block_copy(b).wait()
        xw = jnp.where(srow < bval_ref[b], xbuf[b % XS_SLOTS], jnp.uint32(0))
        xa, xb = _unpack_bf16_pair(xw)
        x = jnp.concatenate([xa.astype(BF16), xb.astype(BF16)], axis=1)
        gu = jnp.dot(x, wgu_s[...], preferred_element_type=F32)
        mid = (jax.nn.silu(gu[:, :D_EXPERT]) * gu[:, D_EXPERT:]).astype(BF16)
        y = jnp.dot(mid, wd_s[...], preferred_element_type=F32)
        y_ref[...] = _pack_bf16_pair(y[:, :D_MODEL // 2], y[:, D_MODEL // 2:])


def _experts(bexp, bidx, bval, nused, next_exp, wslot, xs, wg, wu, wd):
    n_blocks = bexp.shape[0]
    xmap = lambda b, be, bi, bv, nu, nx, ws: (bi[b], 0)
    grid_spec = pltpu.PrefetchScalarGridSpec(
        num_scalar_prefetch=6,
        grid=(n_blocks,),
        in_specs=[pl.BlockSpec(memory_space=pl.ANY),
                  pl.BlockSpec(memory_space=pl.ANY), pl.BlockSpec(memory_space=pl.ANY),
                  pl.BlockSpec(memory_space=pl.ANY)],
        out_specs=pl.BlockSpec((SLOT_BLK, D_MODEL // 2), xmap),
        scratch_shapes=[pltpu.VMEM((XS_SLOTS, SLOT_BLK, D_MODEL // 2), U32),
                        pltpu.VMEM((2, D_MODEL, D_EXPERT), F32), pltpu.VMEM((2, D_MODEL, D_EXPERT), F32),
                        pltpu.VMEM((2, D_EXPERT, D_MODEL), F32),
                        pltpu.VMEM((D_MODEL, 2 * D_EXPERT), BF16), pltpu.VMEM((D_EXPERT, D_MODEL), BF16),
                        pltpu.SemaphoreType.DMA((XS_SLOTS,)), pltpu.SemaphoreType.DMA((2,))],
    )
    return pl.pallas_call(
        _experts_kernel,
        grid_spec=grid_spec,
        out_shape=jax.ShapeDtypeStruct(xs.shape, U32),
        compiler_params=_cparams("arbitrary"),
        name="experts",
    )(bexp, bidx, bval, nused, next_exp, wslot, xs, wg, wu, wd)


def _gather(slots3, y, n):
    n_chunks = slots3.shape[0]
    n_cores, n_sub = _sc_workers()
    per_worker = n_chunks // (n_cores * n_sub)
    assert per_worker * n_cores * n_sub == n_chunks

    @functools.partial(
        pl.kernel, mesh=_sc_mesh(),
        out_type=jax.ShapeDtypeStruct((TOP_K, n, D_MODEL // 2), U32),
        scratch_types=[pltpu.VMEM((TOP_K, SC_CHUNK), I32),
                       pltpu.VMEM((SC_CHUNK, D_MODEL // 2), U32), pltpu.VMEM((SC_CHUNK, D_MODEL // 2), U32),
                       pltpu.SemaphoreType.DMA, pltpu.SemaphoreType.DMA, pltpu.SemaphoreType.DMA],
        name="gather")
    def run(slots_hbm, y_hbm, g_hbm, idx_v, buf0, buf1, gsem, wsem0, wsem1):
        first = _worker_id(n_cores) * per_worker
        bufs = (buf0, buf1)
        wsems = (wsem0, wsem1)

        @pl.loop(0, per_worker)
        def _(j):
            c = first + j
            pltpu.sync_copy(slots_hbm.at[c], idx_v)
            gathers = [None] * TOP_K
            writes = [None] * TOP_K
            gathers[0] = pltpu.async_copy(y_hbm.at[idx_v.at[0]], bufs[0], gsem)
            for k in range(TOP_K):
                gathers[k].wait()
                if k >= 1:
                    writes[k - 1].wait()
                if k + 1 < TOP_K:
                    gathers[k + 1] = pltpu.async_copy(y_hbm.at[idx_v.at[k + 1]], bufs[(k + 1) % 2], gsem)
                writes[k] = pltpu.async_copy(bufs[k % 2], g_hbm.at[k, pl.ds(c * SC_CHUNK, SC_CHUNK)],
                                             wsems[k % 2])
            writes[TOP_K - 1].wait()

    return run(slots3, y)


def _gather_combine(slots_c, wsplat, xo, y):
    n = xo.shape[0]
    n_groups = slots_c.shape[0]
    n_cores, n_sub = _sc_workers()
    per_worker = n_groups // (n_cores * n_sub)
    assert per_worker * n_cores * n_sub == n_groups
    half = D_MODEL // 2

    @functools.partial(
        pl.kernel, mesh=_sc_mesh(),
        out_type=jax.ShapeDtypeStruct((n, D_MODEL), F32),
        scratch_types=[pltpu.VMEM((TOP_K * SC_TOK,), I32), pltpu.VMEM((TOP_K * SC_TOK, half), U32),
                       pltpu.VMEM((SC_TOK, TOP_K, SC_LANES), F32), pltpu.VMEM((SC_TOK, D_MODEL), F32),
                       pltpu.VMEM((SC_TOK, D_MODEL), F32), pltpu.SemaphoreType.DMA],
        compiler_params=pltpu.CompilerParams(needs_layout_passes=False),
        name="gather_combine")
    def run(slots_hbm, w_hbm, xo_hbm, y_hbm, o_hbm, idx_v, rows_v, w_v, xo_v, out_v, sem):
        first = _worker_id(n_cores) * per_worker

        @pl.loop(0, per_worker)
        def _(j):
            grp = first + j
            tok = pl.ds(grp * SC_TOK, SC_TOK)
            pltpu.sync_copy(slots_hbm.at[grp], idx_v)
            gather = pltpu.async_copy(y_hbm.at[idx_v], rows_v, sem)
            pltpu.sync_copy(w_hbm.at[tok], w_v)
            pltpu.sync_copy(xo_hbm.at[tok], xo_v)
            gather.wait()

            @pl.loop(0, SC_TOK)
            def _(t):
                ws = [w_v[t, k, :] for k in range(TOP_K)]

                @pl.loop(0, half // SC_LANES)
                def _(g):
                    lo = pl.ds(pl.multiple_of(g * SC_LANES, SC_LANES), SC_LANES)
                    hi = pl.ds(pl.multiple_of(half + g * SC_LANES, SC_LANES), SC_LANES)
                    acc_a = xo_v[t, lo]
                    acc_b = xo_v[t, hi]
                    for k in range(TOP_K):
                        word = rows_v[k * SC_TOK + t, lo]
                        ya = plsc.bitcast(word & jnp.uint32(0xFFFF0000), F32)
                        yb = plsc.bitcast(word << 16, F32)
                        acc_a = acc_a + ws[k] * ya
                        acc_b = acc_b + ws[k] * yb
                    out_v[t, lo] = acc_a
                    out_v[t, hi] = acc_b

            pltpu.sync_copy(out_v, o_hbm.at[tok])

    return run(slots_c, wsplat, xo, y)


def _combine_kernel(wt_ref, xo_ref, g_ref, o_ref):
    half = D_MODEL // 2
    acc_a = xo_ref[:, :half]
    acc_b = xo_ref[:, half:]
    for k in range(TOP_K):
        ya, yb = _unpack_bf16_pair(g_ref[k])
        w = wt_ref[:, k:k + 1]
        acc_a = acc_a + w * ya
        acc_b = acc_b + w * yb
    o_ref[:, :half] = acc_a
    o_ref[:, half:] = acc_b


def _combine(wt, xo, g, tm=256):
    n = xo.shape[0]
    return pl.pallas_call(
        _combine_kernel,
        grid=(n // tm,),
        in_specs=[pl.BlockSpec((tm, TOP_K), lambda i: (i, 0)),
                  pl.BlockSpec((tm, D_MODEL), lambda i: (i, 0)),
                  pl.BlockSpec((TOP_K, tm, D_MODEL // 2), lambda i: (0, i, 0))],
        out_specs=pl.BlockSpec((tm, D_MODEL), lambda i: (i, 0)),
        out_shape=jax.ShapeDtypeStruct((n, D_MODEL), F32),
        compiler_params=_cparams("parallel"),
        name="combine",
    )(wt, xo, g)


def _layer(x, norm_mix_g, w_in, q_norm_g, k_norm_g, rpb, conv_dw_w, conv_dw_b, conv_ln_g, conv_ln_b,
           w_conv_out, w_attn_out, w_o, norm_ffn_g, w_router, router_bias, w_exp_gate, w_exp_up,
           w_exp_down, w_sh_gate, w_sh_up, w_sh_down):
    n = x.shape[0]
    c_qkv = 2 * D_CONV + 3 * D_ATTN
    w_in_b = w_in.astype(BF16)
    head_of = jnp.arange(D_ATTN) // HEAD_DIM
    bsum = (head_of[:, None] == head_of[None, :]).astype(BF16)
    row = lambda v: v.reshape(1, -1).astype(F32)

    u, q, k, v = _inproj(x, row(norm_mix_g), w_in_b[:, :c_qkv], bsum,
                         row(jnp.tile(q_norm_g, N_HEADS)), row(jnp.tile(k_norm_g, N_HEADS)))
    uc = _conv(u, conv_dw_w.reshape(CONV_WIDTH, D_CONV), row(conv_dw_b), row(conv_ln_g), row(conv_ln_b))
    at = _attn(q, k, v, _bias_table(rpb))

    wr_t = w_router.T
    wr_hi = wr_t.astype(BF16)
    wr_lo = (wr_t - wr_hi.astype(F32)).astype(BF16)
    wsgu = jnp.concatenate([w_sh_gate, w_sh_up], axis=1).astype(BF16)
    xo, hp, idx_t, rank_t, wt_t, cnt = _mix_route(
        x, uc, at, row(norm_mix_g), w_in_b[:, c_qkv:], w_conv_out.astype(BF16), w_attn_out.astype(BF16),
        w_o.astype(BF16), row(norm_ffn_g), wr_hi, wr_lo, router_bias.reshape(N_EXPERTS, 1).astype(F32),
        wsgu, w_sh_down.astype(BF16))

    counts = cnt[:, 0].astype(I32)
    padded = (counts + SLOT_BLK - 1) // SLOT_BLK * SLOT_BLK
    pad_end = jnp.cumsum(padded)
    pad_start = pad_end - padded
    n_blocks = (n * TOP_K + N_EXPERTS * (SLOT_BLK - 1)) // SLOT_BLK
    n_used = pad_end[-1] // SLOT_BLK
    bidx = jnp.minimum(jnp.arange(n_blocks, dtype=I32), n_used - 1)
    is_done = (pad_end[None, :] <= (bidx * SLOT_BLK)[:, None]).astype(I32)
    bexp = jnp.minimum(jnp.sum(is_done, axis=1), N_EXPERTS - 1)
    owner = (bexp[:, None] == jnp.arange(N_EXPERTS)[None, :]).astype(I32)
    bval = jnp.sum(owner * (pad_start + counts)[None, :], axis=1) - bidx * SLOT_BLK
    bval = jnp.clip(bval, 0, SLOT_BLK).astype(I32)
    eids = jnp.arange(N_EXPERTS, dtype=I32)
    later = (eids[None, :] > eids[:, None]) & (counts[None, :] > 0)
    next_exp = jnp.min(jnp.where(later, eids[None, :], N_EXPERTS), axis=1)
    next_exp = jnp.where(next_exp == N_EXPERTS, -1, next_exp).astype(I32)
    wslot = ((jnp.cumsum((counts > 0).astype(I32)) - 1) & 1).astype(I32)

    onehot = (idx_t[:, :, None] == jnp.arange(N_EXPERTS)[None, None, :]).astype(I32)
    slots = jnp.sum(onehot * pad_start[None, None, :], axis=2) + rank_t
    slots3 = slots.reshape(TOP_K, n // SC_CHUNK, SC_CHUNK).transpose(1, 0, 2)

    xs = _dispatch(slots3, hp, n_blocks * SLOT_BLK)
    y = _experts(bexp, bidx, bval, n_used.reshape(1).astype(I32), next_exp, wslot, xs,
                 w_exp_gate, w_exp_up, w_exp_down)
    slots_c = slots.reshape(TOP_K, n // SC_TOK, SC_TOK).transpose(1, 0, 2).reshape(n // SC_TOK, TOP_K * SC_TOK)
    wsplat = jnp.broadcast_to(wt_t.T[:, :, None], (n, TOP_K, SC_LANES))
    return _gather_combine(slots_c, wsplat, xo, y)


def kernel(x, norm_mix_g, w_in, q_norm_g, k_norm_g, rpb, conv_dw_w, conv_dw_b, conv_ln_g, conv_ln_b, w_conv_out, w_attn_out, w_o, norm_ffn_g, w_router, router_bias, w_exp_gate, w_exp_up, w_exp_down, w_sh_gate, w_sh_up, w_sh_down):
    b, s, d = x.shape
    assert b == 1 and d == D_MODEL and s % (2 * ATT_BLK) == 0, x.shape
    xf = x.reshape(b * s, d)
    depth = norm_mix_g.shape[0]
    for l in range(depth):
        xf = _layer(xf, norm_mix_g[l], w_in[l], q_norm_g[l], k_norm_g[l], rpb[l], conv_dw_w[l], conv_dw_b[l],
                    conv_ln_g[l], conv_ln_b[l], w_conv_out[l], w_attn_out[l], w_o[l], norm_ffn_g[l],
                    w_router[l], router_bias[l], w_exp_gate[l], w_exp_up[l], w_exp_down[l], w_sh_gate[l],
                    w_sh_up[l], w_sh_down[l])
    return xf.reshape(b, s, d)
```

```python
import functools

import jax
import jax.numpy as jnp
from jax import lax
from jax.experimental import pallas as pl
from jax.experimental.pallas import tpu as pltpu
from jax.experimental.pallas import tpu_sc as plsc

D_MODEL = 1024
GRID_W = 64
D_CONV = 512
CONV_WIDTH = 31
CONV_PAD = CONV_WIDTH // 2
N_HEADS = 8
HEAD_DIM = 64
D_ATTN = N_HEADS * HEAD_DIM
WIN_H = 8
WIN_W = 16
N_EXPERTS = 64
TOP_K = 8
N_GROUPS = 8
GROUP_SIZE = N_EXPERTS // N_GROUPS
TOPK_GROUPS = 4
D_EXPERT = 256
D_SHARED = 256
ROUTE_SCALE = 2.5
EPS = 1e-6
LOG2E = 1.4426950408889634

SUBLANES = 8
NEG = -1e30
HALO = 16
ATT_ROWS = 4
ATT_BLK = ATT_ROWS * GRID_W
SLOT_BLK = 512
MIX_SPLIT = 2
XS_AHEAD = 2
XS_SLOTS = XS_AHEAD + 1
COMBINE_PARTS = 4
SC_CHUNK = 64
VMEM_LIMIT = 56 * 1024 * 1024

F32 = jnp.float32
BF16 = jnp.bfloat16
I32 = jnp.int32
U32 = jnp.uint32


def _cparams(*sem):
    return pltpu.CompilerParams(dimension_semantics=sem, vmem_limit_bytes=VMEM_LIMIT)


def _rms(x, g):
    return x * lax.rsqrt(jnp.mean(x * x, axis=-1, keepdims=True) + EPS) * g


def _pack_bf16_pair(a, b):
    def bf16_bits(v):
        return lax.bitcast_convert_type(v.astype(BF16).astype(F32), U32)
    return bf16_bits(a) | (bf16_bits(b) >> 16)


def _unpack_bf16_pair(w):
    a = lax.bitcast_convert_type(w & jnp.uint32(0xFFFF0000), F32)
    b = lax.bitcast_convert_type(w << 16, F32)
    return a, b


def _inproj_kernel(x_ref, g_ref, w_ref, bsum_ref, qg_ref, kg_ref, u_ref, q_ref, k_ref, v_ref):
    h = _rms(x_ref[...], g_ref[...]).astype(BF16)
    ag = jnp.dot(h, w_ref[:, 0:2 * D_CONV], preferred_element_type=F32)
    u_ref[...] = ag[:, :D_CONV] * jax.nn.sigmoid(ag[:, D_CONV:])
    bsum = bsum_ref[...]

    def head_norm(z, g):
        ss = jnp.dot((z * z).astype(BF16), bsum, preferred_element_type=F32)
        return z * lax.rsqrt(ss * (1.0 / HEAD_DIM) + EPS) * g

    c0 = 2 * D_CONV
    q = jnp.dot(h, w_ref[:, c0:c0 + D_ATTN], preferred_element_type=F32)
    q_ref[...] = (head_norm(q, qg_ref[...]) * (HEAD_DIM ** -0.5 * LOG2E)).astype(BF16)
    k = jnp.dot(h, w_ref[:, c0 + D_ATTN:c0 + 2 * D_ATTN], preferred_element_type=F32)
    k_ref[...] = head_norm(k, kg_ref[...]).astype(BF16)
    v = jnp.dot(h, w_ref[:, c0 + 2 * D_ATTN:c0 + 3 * D_ATTN], preferred_element_type=F32)
    v_ref[...] = v.astype(BF16)


def _inproj(x, g, w, bsum, qg, kg, tm=512):
    n = x.shape[0]
    wc = w.shape[1]
    full = lambda i: (0, 0)
    row = lambda i: (i, 0)
    return pl.pallas_call(
        _inproj_kernel,
        grid=(n // tm,),
        in_specs=[pl.BlockSpec((tm, D_MODEL), row), pl.BlockSpec((1, D_MODEL), full),
                  pl.BlockSpec((D_MODEL, wc), full), pl.BlockSpec((D_ATTN, D_ATTN), full),
                  pl.BlockSpec((1, D_ATTN), full), pl.BlockSpec((1, D_ATTN), full)],
        out_specs=[pl.BlockSpec((tm, D_CONV), row), pl.BlockSpec((tm, D_ATTN), row),
                   pl.BlockSpec((tm, D_ATTN), row), pl.BlockSpec((tm, D_ATTN), row)],
        out_shape=[jax.ShapeDtypeStruct((n, D_CONV), F32), jax.ShapeDtypeStruct((n, D_ATTN), BF16),
                   jax.ShapeDtypeStruct((n, D_ATTN), BF16), jax.ShapeDtypeStruct((n, D_ATTN), BF16)],
        compiler_params=_cparams("parallel"),
        name="inproj",
    )(x, g, w, bsum, qg, kg)


def _conv_kernel(up_ref, uc_ref, un_ref, w_ref, b_ref, lg_ref, lb_ref, o_ref, ext_ref, sh_ref, *, tc, ch):
    i = pl.program_id(0)
    last = pl.num_programs(0) - 1
    ext_ref[0:HALO, :] = jnp.where(i > 0, up_ref[...], 0.0)
    ext_ref[HALO:HALO + tc, :] = uc_ref[...]
    ext_ref[HALO + tc:2 * HALO + tc, :] = jnp.where(i < last, un_ref[...], 0.0)
    span = sh_ref.shape[1]
    for b in range(SUBLANES):
        sh_ref[b] = ext_ref[b:b + span, :]
    bias = b_ref[...]
    lg = lg_ref[...]
    lb = lb_ref[...]
    for c in range(tc // ch):
        acc = jnp.broadcast_to(bias, (ch, D_CONV))
        for j in range(CONV_WIDTH):
            shift = HALO - CONV_PAD + j
            row0 = c * ch + shift // SUBLANES * SUBLANES
            acc = acc + sh_ref[shift % SUBLANES, row0:row0 + ch, :] * w_ref[j:j + 1, :]
        mu = jnp.mean(acc, axis=-1, keepdims=True)
        d = acc - mu
        var = jnp.mean(d * d, axis=-1, keepdims=True)
        y = d * lax.rsqrt(var + EPS) * lg + lb
        o_ref[c * ch:(c + 1) * ch, :] = (y * jax.nn.sigmoid(y)).astype(BF16)


def _conv(u, w, b, lg, lb, tc=256, ch=64):
    n = u.shape[0]
    hb = tc // HALO
    nh = n // HALO
    full = lambda i: (0, 0)
    return pl.pallas_call(
        functools.partial(_conv_kernel, tc=tc, ch=ch),
        grid=(n // tc,),
        in_specs=[pl.BlockSpec((HALO, D_CONV), lambda i: (jnp.maximum(i * hb - 1, 0), 0)),
                  pl.BlockSpec((tc, D_CONV), lambda i: (i, 0)),
                  pl.BlockSpec((HALO, D_CONV), lambda i: (jnp.minimum((i + 1) * hb, nh - 1), 0)),
                  pl.BlockSpec((CONV_WIDTH, D_CONV), full), pl.BlockSpec((1, D_CONV), full),
                  pl.BlockSpec((1, D_CONV), full), pl.BlockSpec((1, D_CONV), full)],
        out_specs=pl.BlockSpec((tc, D_CONV), lambda i: (i, 0)),
        out_shape=jax.ShapeDtypeStruct((n, D_CONV), BF16),
        scratch_shapes=[pltpu.VMEM((tc + 2 * HALO, D_CONV), F32),
                        pltpu.VMEM((SUBLANES, tc + 2 * HALO - SUBLANES, D_CONV), F32)],
        compiler_params=_cparams("parallel"),
        name="conv",
    )(u, u, u, w, b, lg, lb)


def _attn_kernel(q_ref, kp_ref, kc_ref, kn_ref, vp_ref, vc_ref, vn_ref, tb_ref, o_ref, *, rows):
    i = pl.program_id(0)
    nkey = 3 * ATT_BLK
    lrow = lax.broadcasted_iota(I32, (1, nkey), 1) >> 6
    lane = lax.broadcasted_iota(I32, (1, 2 * HEAD_DIM), 1)
    masks = []
    for j in range(ATT_ROWS):
        r = i * ATT_ROWS + j
        rs = jnp.clip(r - WIN_H // 2, 0, rows - WIN_H)
        lo = rs - (i - 1) * ATT_ROWS
        masks.append(jnp.where((lrow >= lo) & (lrow < lo + WIN_H), 0.0, NEG))
    dn = (((1,), (1,)), ((), ()))

    def pair_slice(h):
        return slice(2 * HEAD_DIM * (h // 2), 2 * HEAD_DIM * (h // 2 + 1))

    def in_half(h):
        return (lane >= HEAD_DIM * (h % 2)) & (lane < HEAD_DIM * (h % 2 + 1))

    def scores(h):
        sl = pair_slice(h)
        q2 = q_ref[:, sl]
        qm = jnp.where(in_half(h), q2, jnp.zeros_like(q2))
        s = jnp.concatenate([lax.dot_general(qm, kk[:, sl], dn, preferred_element_type=F32)
                             for kk in (kp_ref, kc_ref, kn_ref)], axis=1)
        parts = []
        for j in range(ATT_ROWS):
            bias = jnp.concatenate([tb_ref[h, 2 * m - j + 3] for m in range(nkey // (2 * GRID_W))], axis=1)
            parts.append(s[j * GRID_W:(j + 1) * GRID_W, :] + bias + masks[j])
        return jnp.concatenate(parts, axis=0)

    def softmax(s):
        mx = jnp.max(s, axis=-1, keepdims=True)
        e = jnp.exp2(s - mx)
        return e.astype(BF16), jnp.sum(e, axis=-1, keepdims=True)

    def values(h, pb, den):
        sl = pair_slice(h)
        o = sum(jnp.dot(pb[:, ATT_BLK * t:ATT_BLK * (t + 1)], vv[:, sl], preferred_element_type=F32)
                for t, vv in enumerate((vp_ref, vc_ref, vn_ref)))
        return o / den

    stage_s = {h: scores(h) for h in range(2)}
    stage_p = {0: softmax(stage_s.pop(0))}
    out_pair = None
    for h in range(N_HEADS):
        if h + 2 < N_HEADS:
            stage_s[h + 2] = scores(h + 2)
        if h + 1 < N_HEADS:
            stage_p[h + 1] = softmax(stage_s.pop(h + 1))
        o = values(h, *stage_p.pop(h))
        if h % 2 == 0:
            out_pair = o
        else:
            o_ref[:, pair_slice(h)] = jnp.where(in_half(h), o, out_pair).astype(BF16)


def _attn(q, k, v, tb):
    n = q.shape[0]
    rows = n // GRID_W
    nb = n // ATT_BLK
    cur = lambda i: (i, 0)
    prev = lambda i: (jnp.maximum(i - 1, 0), 0)
    nxt = lambda i: (jnp.minimum(i + 1, nb - 1), 0)
    blk = (ATT_BLK, D_ATTN)
    return pl.pallas_call(
        functools.partial(_attn_kernel, rows=rows),
        grid=(nb,),
        in_specs=[pl.BlockSpec(blk, cur), pl.BlockSpec(blk, prev), pl.BlockSpec(blk, cur), pl.BlockSpec(blk, nxt),
                  pl.BlockSpec(blk, prev), pl.BlockSpec(blk, cur), pl.BlockSpec(blk, nxt),
                  pl.BlockSpec(tb.shape, lambda i: (0, 0, 0, 0))],
        out_specs=pl.BlockSpec(blk, cur),
        out_shape=jax.ShapeDtypeStruct((n, D_ATTN), BF16),
        compiler_params=_cparams("parallel"),
        name="attn",
    )(q, k, k, k, v, v, v, tb)


def _bias_table(rpb):
    cols = jnp.arange(GRID_W)
    start = jnp.clip(cols - WIN_W // 2, 0, GRID_W - WIN_W)
    kc = cols[None, :]
    inwin = (kc >= start[:, None]) & (kc < start[:, None] + WIN_W)
    off = GRID_W - WIN_W
    period = 2 * GRID_W
    padded = jnp.pad(rpb, ((0, 0), (0, 0), (off, off + 1)))
    tiled = jnp.tile(padded, (1, 1, GRID_W))[:, :, :GRID_W * (period - 1)]
    rel = tiled.reshape(rpb.shape[0], rpb.shape[1], GRID_W, period - 1)[..., GRID_W - 1:]
    t = jnp.where(inwin[None, None], rel * LOG2E, NEG)
    t = jnp.concatenate([t, jnp.full_like(t[:, :1], NEG)], axis=1)
    t_next = jnp.concatenate([t[:, 1:], t[:, -1:]], axis=1)
    return jnp.concatenate([t, t_next], axis=-1).astype(F32)


def _mix_route_kernel(x_ref, uc_ref, at_ref, g1_ref, wg_ref, wco_ref, wao_ref, wo_ref, g2_ref,
                      wrh_ref, wrl_ref, rb_ref, tri_ref, wsgu_ref, wsd_ref,
                      xo_ref, hp_ref, idx_ref, rank_ref, wt_ref, cnt_ref, carry_ref, *, tm):
    i = pl.program_id(0)

    @pl.when(i == 0)
    def _():
        carry_ref[...] = jnp.zeros_like(carry_ref)

    th = tm // MIX_SPLIT
    dn = (((1,), (1,)), ((), ()))

    def mix(p):
        rs = slice(p * th, (p + 1) * th)
        x = x_ref[rs, :]
        h = _rms(x, g1_ref[...]).astype(BF16)
        gates = jax.nn.sigmoid(jnp.dot(h, wg_ref[...], preferred_element_type=F32))
        y_conv = jnp.dot(uc_ref[rs, :], wco_ref[...], preferred_element_type=F32)
        y_attn = jnp.dot(at_ref[rs, :], wao_ref[...], preferred_element_type=F32)
        merged = gates[:, :D_MODEL] * y_conv + gates[:, D_MODEL:] * y_attn
        x1 = x + jnp.dot(merged.astype(BF16), wo_ref[...], preferred_element_type=F32)
        h2 = _rms(x1, g2_ref[...])
        hp_ref[rs, :] = _pack_bf16_pair(h2[:, :D_MODEL // 2], h2[:, D_MODEL // 2:])
        return x1, h2

    def shared_and_logits(p, x1, h2):
        rs = slice(p * th, (p + 1) * th)
        h2b = h2.astype(BF16)
        gu = jnp.dot(h2b, wsgu_ref[...], preferred_element_type=F32)
        mid = (jax.nn.silu(gu[:, :D_SHARED]) * gu[:, D_SHARED:]).astype(BF16)
        xo_ref[rs, :] = x1 + jnp.dot(mid, wsd_ref[...], preferred_element_type=F32)
        h2l = (h2 - h2b.astype(F32)).astype(BF16)
        return (lax.dot_general(wrh_ref[...], h2b, dn, preferred_element_type=F32)
                + lax.dot_general(wrl_ref[...], h2b, dn, preferred_element_type=F32)
                + lax.dot_general(wrh_ref[...], h2l, dn, preferred_element_type=F32))

    def route(p, logits, carry):
        cs = slice(p * th, (p + 1) * th)
        scores = jax.nn.sigmoid(logits)
        biased = scores + rb_ref[...]

        sub = lax.broadcasted_iota(I32, (GROUP_SIZE, th), 0).astype(F32)
        groups, gscore = [], []
        for g in range(N_GROUPS):
            bg = biased[g * GROUP_SIZE:(g + 1) * GROUP_SIZE, :]
            m1 = jnp.max(bg, axis=0, keepdims=True)
            first = jnp.min(jnp.where(bg == m1, sub, float(GROUP_SIZE)), axis=0, keepdims=True)
            m2 = jnp.max(jnp.where(sub == first, -jnp.inf, bg), axis=0, keepdims=True)
            groups.append(bg)
            gscore.append(m1 + m2)
        masked = []
        for g in range(N_GROUPS):
            beaten = jnp.zeros((1, th), F32)
            for o in range(N_GROUPS):
                if o == g:
                    continue
                wins = (gscore[o] >= gscore[g]) if o < g else (gscore[o] > gscore[g])
                beaten = beaten + jnp.where(wins, 1.0, 0.0)
            masked.append(jnp.where(beaten < TOPK_GROUPS, groups[g], -jnp.inf))
        cur = jnp.concatenate(masked, axis=0)

        eid = lax.broadcasted_iota(I32, (N_EXPERTS, th), 0).astype(F32)
        idx_rows, sc_rows = [], []
        sel = jnp.zeros((N_EXPERTS, th), F32)
        for _ in range(TOP_K):
            m = jnp.max(cur, axis=0, keepdims=True)
            idx = jnp.min(jnp.where(cur == m, eid, float(N_EXPERTS)), axis=0, keepdims=True)
            hit = eid == idx
            sel = jnp.where(hit, 1.0, sel)
            cur = jnp.where(hit, -jnp.inf, cur)
            idx_rows.append(idx)
            sc_rows.append(jnp.sum(jnp.where(hit, scores, 0.0), axis=0, keepdims=True))
        wsum = sc_rows[0]
        for k in range(1, TOP_K):
            wsum = wsum + sc_rows[k]

        prefix = jnp.dot(sel.astype(BF16), tri_ref[...], preferred_element_type=F32)
        rank = carry + prefix
        rank_rows = [jnp.sum(jnp.where(eid == idx_rows[k], rank, 0.0), axis=0, keepdims=True)
                     for k in range(TOP_K)]
        idx_ref[:, cs] = jnp.concatenate(idx_rows, axis=0).astype(I32)
        rank_ref[:, cs] = jnp.concatenate(rank_rows, axis=0).astype(I32)
        wt_ref[:, cs] = jnp.concatenate([s / wsum * ROUTE_SCALE for s in sc_rows], axis=0)
        return carry + jnp.sum(sel, axis=1, keepdims=True)

    carry = carry_ref[:, 0:1]
    logits = shared_and_logits(0, *mix(0))
    for p in range(MIX_SPLIT):
        if p + 1 < MIX_SPLIT:
            mixed_next = mix(p + 1)
        carry = route(p, logits, carry)
        if p + 1 < MIX_SPLIT:
            logits = shared_and_logits(p + 1, *mixed_next)
    carry_ref[...] = jnp.broadcast_to(carry, carry_ref.shape)
    cnt_ref[...] = carry_ref[...]


def _mix_route(x, uc, at, g1, wg, wco, wao, wo, g2, wrh, wrl, rb, wsgu, wsd, tm=1024):
    n = x.shape[0]
    th = tm // MIX_SPLIT
    tri = (jnp.arange(th)[:, None] < jnp.arange(th)[None, :]).astype(BF16)
    row = lambda i: (i, 0)
    col = lambda i: (0, i)
    full = lambda i: (0, 0)
    ins = [x, uc, at, g1, wg, wco, wao, wo, g2, wrh, wrl, rb, tri, wsgu, wsd]
    in_specs = [pl.BlockSpec((tm, D_MODEL), row), pl.BlockSpec((tm, D_CONV), row), pl.BlockSpec((tm, D_ATTN), row)]
    in_specs += [pl.BlockSpec(a.shape, full) for a in ins[3:]]
    return pl.pallas_call(
        functools.partial(_mix_route_kernel, tm=tm),
        grid=(n // tm,),
        in_specs=in_specs,
        out_specs=[pl.BlockSpec((tm, D_MODEL), row), pl.BlockSpec((tm, D_MODEL // 2), row),
                   pl.BlockSpec((TOP_K, tm), col), pl.BlockSpec((TOP_K, tm), col), pl.BlockSpec((TOP_K, tm), col),
                   pl.BlockSpec((N_EXPERTS, 128), full)],
        out_shape=[jax.ShapeDtypeStruct((n, D_MODEL), F32), jax.ShapeDtypeStruct((n, D_MODEL // 2), U32),
                   jax.ShapeDtypeStruct((TOP_K, n), I32), jax.ShapeDtypeStruct((TOP_K, n), I32),
                   jax.ShapeDtypeStruct((TOP_K, n), F32), jax.ShapeDtypeStruct((N_EXPERTS, 128), F32)],
        scratch_shapes=[pltpu.VMEM((N_EXPERTS, 128), F32)],
        compiler_params=_cparams("arbitrary"),
        name="mix_route",
    )(*ins)


def _sc_workers():
    info = plsc.get_sparse_core_info()
    return info.num_cores, info.num_subcores


def _sc_mesh():
    return plsc.VectorSubcoreMesh(core_axis_name="c", subcore_axis_name="s")


def _worker_id(n_cores):
    return lax.axis_index("s") * n_cores + lax.axis_index("c")


def _dispatch(slots3, hp, n_slots):
    n_chunks = slots3.shape[0]
    n_cores, n_sub = _sc_workers()
    per_worker = n_chunks // (n_cores * n_sub)
    assert per_worker * n_cores * n_sub == n_chunks

    @functools.partial(
        pl.kernel, mesh=_sc_mesh(),
        out_type=jax.ShapeDtypeStruct((n_slots, D_MODEL // 2), U32),
        scratch_types=[pltpu.VMEM((TOP_K, SC_CHUNK), I32), pltpu.VMEM((SC_CHUNK, D_MODEL // 2), U32),
                       pltpu.SemaphoreType.DMA],
        name="dispatch")
    def run(slots_hbm, hp_hbm, xs_hbm, idx_v, rows_v, sem):
        first = _worker_id(n_cores) * per_worker

        @pl.loop(0, per_worker)
        def _(j):
            c = first + j
            pltpu.sync_copy(slots_hbm.at[c], idx_v)
            pltpu.sync_copy(hp_hbm.at[pl.ds(c * SC_CHUNK, SC_CHUNK)], rows_v)
            copies = [pltpu.async_copy(rows_v, xs_hbm.at[idx_v.at[k]], sem) for k in range(TOP_K)]
            for cp in copies:
                cp.wait()

    return run(slots3, hp)


def _experts_kernel(bexp_ref, bidx_ref, bval_ref, nused_ref, next_ref, wslot_ref,
                    xs_ref, wg_ref, wu_ref, wd_ref, y_ref, xbuf, wg_buf, wu_buf, wd_buf, wgu_s, wd_s, xsem, wsem):
    b = pl.program_id(0)
    n_used = nused_ref[0]

    def block_copy(blk):
        slot = blk % XS_SLOTS
        rows = pl.ds(pl.multiple_of(blk * SLOT_BLK, SLOT_BLK), SLOT_BLK)
        return pltpu.make_async_copy(xs_ref.at[rows], xbuf.at[slot], xsem.at[slot])

    @pl.when(b == 0)
    def _():
        for blk in range(XS_AHEAD):
            @pl.when(blk < n_used)
            def _():
                block_copy(blk).start()

    @pl.when(b + XS_AHEAD < n_used)
    def _():
        block_copy(b + XS_AHEAD).start()

    def weight_copies(e, slot):
        return [pltpu.make_async_copy(wg_ref.at[e], wg_buf.at[slot], wsem.at[slot]),
                pltpu.make_async_copy(wu_ref.at[e], wu_buf.at[slot], wsem.at[slot]),
                pltpu.make_async_copy(wd_ref.at[e], wd_buf.at[slot], wsem.at[slot])]

    @pl.when(b < n_used)
    def _():
        e = bexp_ref[b]
        slot = wslot_ref[e]

        @pl.when(b == 0)
        def _():
            for cp in weight_copies(e, slot):
                cp.start()

        @pl.when((b == 0) | (bexp_ref[jnp.maximum(b - 1, 0)] != e))
        def _():
            for cp in weight_copies(e, slot):
                cp.wait()
            wgu_s[:, :D_EXPERT] = wg_buf[slot].astype(BF16)
            wgu_s[:, D_EXPERT:] = wu_buf[slot].astype(BF16)
            wd_s[...] = wd_buf[slot].astype(BF16)
            nxt = next_ref[e]

            @pl.when(nxt >= 0)
            def _():
                for cp in weight_copies(nxt, 1 - slot):
                    cp.start()

        srow = lax.broadcasted_iota(I32, (SLOT_BLK, 1), 0)
        block_copy(b).wait()
        xw = jnp.where(srow < bval_ref[b], xbuf[b % XS_SLOTS], jnp.uint32(0))
        xa, xb = _unpack_bf16_pair(xw)
        x = jnp.concatenate([xa.astype(BF16), xb.astype(BF16)], axis=1)
        gu = jnp.dot(x, wgu_s[...], preferred_element_type=F32)
        mid = (jax.nn.silu(gu[:, :D_EXPERT]) * gu[:, D_EXPERT:]).astype(BF16)
        y = jnp.dot(mid, wd_s[...], preferred_element_type=F32)
        y_ref[...] = _pack_bf16_pair(y[:, :D_MODEL // 2], y[:, D_MODEL // 2:])


def _experts(bexp, bidx, bval, nused, next_exp, wslot, xs, wg, wu, wd):
    n_blocks = bexp.shape[0]
    xmap = lambda b, be, bi, bv, nu, nx, ws: (bi[b], 0)
    grid_spec = pltpu.PrefetchScalarGridSpec(
        num_scalar_prefetch=6,
        grid=(n_blocks,),
        in_specs=[pl.BlockSpec(memory_space=pl.ANY),
                  pl.BlockSpec(memory_space=pl.ANY), pl.BlockSpec(memory_space=pl.ANY),
                  pl.BlockSpec(memory_space=pl.ANY)],
        out_specs=pl.BlockSpec((SLOT_BLK, D_MODEL // 2), xmap),
        scratch_shapes=[pltpu.VMEM((XS_SLOTS, SLOT_BLK, D_MODEL // 2), U32),
                        pltpu.VMEM((2, D_MODEL, D_EXPERT), F32), pltpu.VMEM((2, D_MODEL, D_EXPERT), F32),
                        pltpu.VMEM((2, D_EXPERT, D_MODEL), F32),
                        pltpu.VMEM((D_MODEL, 2 * D_EXPERT), BF16), pltpu.VMEM((D_EXPERT, D_MODEL), BF16),
                        pltpu.SemaphoreType.DMA((XS_SLOTS,)), pltpu.SemaphoreType.DMA((2,))],
    )
    return pl.pallas_call(
        _experts_kernel,
        grid_spec=grid_spec,
        out_shape=jax.ShapeDtypeStruct(xs.shape, U32),
        compiler_params=_cparams("arbitrary"),
        name="experts",
    )(bexp, bidx, bval, nused, next_exp, wslot, xs, wg, wu, wd)


def _gather(slots3, y, n):
    n_chunks = slots3.shape[0]
    n_cores, n_sub = _sc_workers()
    per_worker = n_chunks // (n_cores * n_sub)
    assert per_worker * n_cores * n_sub == n_chunks

    @functools.partial(
        pl.kernel, mesh=_sc_mesh(),
        out_type=jax.ShapeDtypeStruct((TOP_K, n, D_MODEL // 2), U32),
        scratch_types=[pltpu.VMEM((TOP_K, SC_CHUNK), I32),
                       pltpu.VMEM((SC_CHUNK, D_MODEL // 2), U32), pltpu.VMEM((SC_CHUNK, D_MODEL // 2), U32),
                       pltpu.SemaphoreType.DMA, pltpu.SemaphoreType.DMA, pltpu.SemaphoreType.DMA],
        name="gather")
    def run(slots_hbm, y_hbm, g_hbm, idx_v, buf0, buf1, gsem, wsem0, wsem1):
        first = _worker_id(n_cores) * per_worker
        bufs = (buf0, buf1)
        wsems = (wsem0, wsem1)

        @pl.loop(0, per_worker)
        def _(j):
            c = first + j
            pltpu.sync_copy(slots_hbm.at[c], idx_v)
            gathers = [None] * TOP_K
            writes = [None] * TOP_K
            gathers[0] = pltpu.async_copy(y_hbm.at[idx_v.at[0]], bufs[0], gsem)
            for k in range(TOP_K):
                gathers[k].wait()
                if k >= 1:
                    writes[k - 1].wait()
                if k + 1 < TOP_K:
                    gathers[k + 1] = pltpu.async_copy(y_hbm.at[idx_v.at[k + 1]], bufs[(k + 1) % 2], gsem)
                writes[k] = pltpu.async_copy(bufs[k % 2], g_hbm.at[k, pl.ds(c * SC_CHUNK, SC_CHUNK)],
                                             wsems[k % 2])
            writes[TOP_K - 1].wait()

    return run(slots3, y)


def _combine_kernel(wt_ref, xo_ref, g_ref, *rest):
    o_ref = rest[-1]
    half = D_MODEL // 2
    acc_a = xo_ref[:, :half]
    acc_b = xo_ref[:, half:]
    for k in range(TOP_K):
        ya, yb = _unpack_bf16_pair(g_ref[k])
        w = wt_ref[:, k:k + 1]
        acc_a = acc_a + w * ya
        acc_b = acc_b + w * yb
    o_ref[:, :half] = acc_a
    o_ref[:, half:] = acc_b


def _combine(wt, xo, g, part, out_so_far, tm=256):
    n = xo.shape[0]
    steps = g.shape[1] // tm
    row = lambda i: (part * steps + i, 0)
    in_specs = [pl.BlockSpec((tm, TOP_K), row), pl.BlockSpec((tm, D_MODEL), row),
                pl.BlockSpec((TOP_K, tm, D_MODEL // 2), lambda i: (0, i, 0))]
    args = [wt, xo, g]
    aliases = {}
    if out_so_far is not None:
        in_specs.append(pl.BlockSpec(memory_space=pl.ANY))
        args.append(out_so_far)
        aliases = {len(args) - 1: 0}
    return pl.pallas_call(
        _combine_kernel,
        grid=(steps,),
        in_specs=in_specs,
        out_specs=pl.BlockSpec((tm, D_MODEL), row),
        out_shape=jax.ShapeDtypeStruct((n, D_MODEL), F32),
        input_output_aliases=aliases,
        compiler_params=_cparams("parallel"),
        name="combine",
    )(*args)


def _layer(x, norm_mix_g, w_in, q_norm_g, k_norm_g, rpb, conv_dw_w, conv_dw_b, conv_ln_g, conv_ln_b,
           w_conv_out, w_attn_out, w_o, norm_ffn_g, w_router, router_bias, w_exp_gate, w_exp_up,
           w_exp_down, w_sh_gate, w_sh_up, w_sh_down):
    n = x.shape[0]
    c_qkv = 2 * D_CONV + 3 * D_ATTN
    w_in_b = w_in.astype(BF16)
    head_of = jnp.arange(D_ATTN) // HEAD_DIM
    bsum = (head_of[:, None] == head_of[None, :]).astype(BF16)
    row = lambda v: v.reshape(1, -1).astype(F32)

    u, q, k, v = _inproj(x, row(norm_mix_g), w_in_b[:, :c_qkv], bsum,
                         row(jnp.tile(q_norm_g, N_HEADS)), row(jnp.tile(k_norm_g, N_HEADS)))
    uc = _conv(u, conv_dw_w.reshape(CONV_WIDTH, D_CONV), row(conv_dw_b), row(conv_ln_g), row(conv_ln_b))
    at = _attn(q, k, v, _bias_table(rpb))

    wr_t = w_router.T
    wr_hi = wr_t.astype(BF16)
    wr_lo = (wr_t - wr_hi.astype(F32)).astype(BF16)
    wsgu = jnp.concatenate([w_sh_gate, w_sh_up], axis=1).astype(BF16)
    xo, hp, idx_t, rank_t, wt_t, cnt = _mix_route(
        x, uc, at, row(norm_mix_g), w_in_b[:, c_qkv:], w_conv_out.astype(BF16), w_attn_out.astype(BF16),
        w_o.astype(BF16), row(norm_ffn_g), wr_hi, wr_lo, router_bias.reshape(N_EXPERTS, 1).astype(F32),
        wsgu, w_sh_down.astype(BF16))

    counts = cnt[:, 0].astype(I32)
    padded = (counts + SLOT_BLK - 1) // SLOT_BLK * SLOT_BLK
    pad_end = jnp.cumsum(padded)
    pad_start = pad_end - padded
    n_blocks = (n * TOP_K + N_EXPERTS * (SLOT_BLK - 1)) // SLOT_BLK
    n_used = pad_end[-1] // SLOT_BLK
    bidx = jnp.minimum(jnp.arange(n_blocks, dtype=I32), n_used - 1)
    is_done = (pad_end[None, :] <= (bidx * SLOT_BLK)[:, None]).astype(I32)
    bexp = jnp.minimum(jnp.sum(is_done, axis=1), N_EXPERTS - 1)
    owner = (bexp[:, None] == jnp.arange(N_EXPERTS)[None, :]).astype(I32)
    bval = jnp.sum(owner * (pad_start + counts)[None, :], axis=1) - bidx * SLOT_BLK
    bval = jnp.clip(bval, 0, SLOT_BLK).astype(I32)
    eids = jnp.arange(N_EXPERTS, dtype=I32)
    later = (eids[None, :] > eids[:, None]) & (counts[None, :] > 0)
    next_exp = jnp.min(jnp.where(later, eids[None, :], N_EXPERTS), axis=1)
    next_exp = jnp.where(next_exp == N_EXPERTS, -1, next_exp).astype(I32)
    wslot = ((jnp.cumsum((counts > 0).astype(I32)) - 1) & 1).astype(I32)

    onehot = (idx_t[:, :, None] == jnp.arange(N_EXPERTS)[None, None, :]).astype(I32)
    slots = jnp.sum(onehot * pad_start[None, None, :], axis=2) + rank_t
    slots3 = slots.reshape(TOP_K, n // SC_CHUNK, SC_CHUNK).transpose(1, 0, 2)

    xs = _dispatch(slots3, hp, n_blocks * SLOT_BLK)
    y = _experts(bexp, bidx, bval, n_used.reshape(1).astype(I32), next_exp, wslot, xs,
                 w_exp_gate, w_exp_up, w_exp_down)
    wt = wt_t.T
    part_chunks = slots3.shape[0] // COMBINE_PARTS
    out = None
    for part in range(COMBINE_PARTS):
        g = _gather(slots3[part * part_chunks:(part + 1) * part_chunks], y, n // COMBINE_PARTS)
        out = _combine(wt, xo, g, part, out)
    return out


def kernel(x, norm_mix_g, w_in, q_norm_g, k_norm_g, rpb, conv_dw_w, conv_dw_b, conv_ln_g, conv_ln_b, w_conv_out, w_attn_out, w_o, norm_ffn_g, w_router, router_bias, w_exp_gate, w_exp_up, w_exp_down, w_sh_gate, w_sh_up, w_sh_down):
    b, s, d = x.shape
    assert b == 1 and d == D_MODEL and s % (2 * ATT_BLK) == 0, x.shape
    xf = x.reshape(b * s, d)
    depth = norm_mix_g.shape[0]
    for l in range(depth):
        xf = _layer(xf, norm_mix_g[l], w_in[l], q_norm_g[l], k_norm_g[l], rpb[l], conv_dw_w[l], conv_dw_b[l],
                    conv_ln_g[l], conv_ln_b[l], w_conv_out[l], w_attn_out[l], w_o[l], norm_ffn_g[l],
                    w_router[l], router_bias[l], w_exp_gate[l], w_exp_up[l], w_exp_down[l], w_sh_gate[l],
                    w_sh_up[l], w_sh_down[l])
    return xf.reshape(b, s, d)
```

```python
import functools

import jax
import jax.numpy as jnp
from jax import lax
from jax.experimental import pallas as pl
from jax.experimental.pallas import tpu as pltpu
from jax.experimental.pallas import tpu_sc as plsc

D_MODEL = 1024
GRID_W = 64
D_CONV = 512
CONV_WIDTH = 31
CONV_PAD = CONV_WIDTH // 2
N_HEADS = 8
HEAD_DIM = 64
D_ATTN = N_HEADS * HEAD_DIM
WIN_H = 8
WIN_W = 16
N_EXPERTS = 64
TOP_K = 8
N_GROUPS = 8
GROUP_SIZE = N_EXPERTS // N_GROUPS
TOPK_GROUPS = 4
D_EXPERT = 256
D_SHARED = 256
ROUTE_SCALE = 2.5
EPS = 1e-6
LOG2E = 1.4426950408889634

SUBLANES = 8
NEG = -1e30
HALO = 16
ATT_ROWS = 4
ATT_BLK = ATT_ROWS * GRID_W
SLOT_BLK = 512
MIX_SPLIT = 2
XS_AHEAD = 2
XS_SLOTS = XS_AHEAD + 1
COMBINE_PARTS = 4
SC_CHUNK = 64
VMEM_LIMIT = 56 * 1024 * 1024

F32 = jnp.float32
BF16 = jnp.bfloat16
I32 = jnp.int32
U32 = jnp.uint32


def _cparams(*sem):
    return pltpu.CompilerParams(dimension_semantics=sem, vmem_limit_bytes=VMEM_LIMIT)


def _rms(x, g):
    return x * lax.rsqrt(jnp.mean(x * x, axis=-1, keepdims=True) + EPS) * g


def _pack_bf16_pair(a, b):
    def bf16_bits(v):
        return lax.bitcast_convert_type(v.astype(BF16).astype(F32), U32)
    return bf16_bits(a) | (bf16_bits(b) >> 16)


def _unpack_bf16_pair(w):
    a = lax.bitcast_convert_type(w & jnp.uint32(0xFFFF0000), F32)
    b = lax.bitcast_convert_type(w << 16, F32)
    return a, b


def _inproj_kernel(x_ref, g_ref, w_ref, bsum_ref, qg_ref, kg_ref, u_ref, q_ref, k_ref, v_ref):
    h = _rms(x_ref[...], g_ref[...]).astype(BF16)
    ag = jnp.dot(h, w_ref[:, 0:2 * D_CONV], preferred_element_type=F32)
    u_ref[...] = ag[:, :D_CONV] * jax.nn.sigmoid(ag[:, D_CONV:])
    bsum = bsum_ref[...]

    def head_norm(z, g):
        zz = (z * z).astype(BF16)
        hw = D_ATTN // 2
        ss = jnp.concatenate([jnp.dot(zz[:, :hw], bsum, preferred_element_type=F32),
                              jnp.dot(zz[:, hw:], bsum, preferred_element_type=F32)], axis=1)
        return z * lax.rsqrt(ss * (1.0 / HEAD_DIM) + EPS) * g

    c0 = 2 * D_CONV
    q = jnp.dot(h, w_ref[:, c0:c0 + D_ATTN], preferred_element_type=F32)
    q_ref[...] = (head_norm(q, qg_ref[...]) * (HEAD_DIM ** -0.5 * LOG2E)).astype(BF16)
    k = jnp.dot(h, w_ref[:, c0 + D_ATTN:c0 + 2 * D_ATTN], preferred_element_type=F32)
    k_ref[...] = head_norm(k, kg_ref[...]).astype(BF16)
    v = jnp.dot(h, w_ref[:, c0 + 2 * D_ATTN:c0 + 3 * D_ATTN], preferred_element_type=F32)
    v_ref[...] = v.astype(BF16)


def _inproj(x, g, w, bsum, qg, kg, tm=512):
    n = x.shape[0]
    wc = w.shape[1]
    full = lambda i: (0, 0)
    row = lambda i: (i, 0)
    return pl.pallas_call(
        _inproj_kernel,
        grid=(n // tm,),
        in_specs=[pl.BlockSpec((tm, D_MODEL), row), pl.BlockSpec((1, D_MODEL), full),
                  pl.BlockSpec((D_MODEL, wc), full), pl.BlockSpec(bsum.shape, full),
                  pl.BlockSpec((1, D_ATTN), full), pl.BlockSpec((1, D_ATTN), full)],
        out_specs=[pl.BlockSpec((tm, D_CONV), row), pl.BlockSpec((tm, D_ATTN), row),
                   pl.BlockSpec((tm, D_ATTN), row), pl.BlockSpec((tm, D_ATTN), row)],
        out_shape=[jax.ShapeDtypeStruct((n, D_CONV), F32), jax.ShapeDtypeStruct((n, D_ATTN), BF16),
                   jax.ShapeDtypeStruct((n, D_ATTN), BF16), jax.ShapeDtypeStruct((n, D_ATTN), BF16)],
        compiler_params=_cparams("parallel"),
        name="inproj",
    )(x, g, w, bsum, qg, kg)


def _conv_kernel(up_ref, uc_ref, un_ref, w_ref, b_ref, lg_ref, lb_ref, o_ref, ext_ref, sh_ref, *, tc, ch):
    i = pl.program_id(0)
    last = pl.num_programs(0) - 1
    ext_ref[0:HALO, :] = jnp.where(i > 0, up_ref[...], 0.0)
    ext_ref[HALO:HALO + tc, :] = uc_ref[...]
    ext_ref[HALO + tc:2 * HALO + tc, :] = jnp.where(i < last, un_ref[...], 0.0)
    span = sh_ref.shape[1]
    for b in range(SUBLANES):
        sh_ref[b] = ext_ref[b:b + span, :]
    bias = b_ref[...]
    lg = lg_ref[...]
    lb = lb_ref[...]
    for c in range(tc // ch):
        acc = jnp.broadcast_to(bias, (ch, D_CONV))
        for j in range(CONV_WIDTH):
            shift = HALO - CONV_PAD + j
            row0 = c * ch + shift // SUBLANES * SUBLANES
            acc = acc + sh_ref[shift % SUBLANES, row0:row0 + ch, :] * w_ref[j:j + 1, :]
        mu = jnp.mean(acc, axis=-1, keepdims=True)
        d = acc - mu
        var = jnp.mean(d * d, axis=-1, keepdims=True)
        y = d * lax.rsqrt(var + EPS) * lg + lb
        o_ref[c * ch:(c + 1) * ch, :] = (y * jax.nn.sigmoid(y)).astype(BF16)


def _conv(u, w, b, lg, lb, tc=256, ch=64):
    n = u.shape[0]
    hb = tc // HALO
    nh = n // HALO
    full = lambda i: (0, 0)
    return pl.pallas_call(
        functools.partial(_conv_kernel, tc=tc, ch=ch),
        grid=(n // tc,),
        in_specs=[pl.BlockSpec((HALO, D_CONV), lambda i: (jnp.maximum(i * hb - 1, 0), 0)),
                  pl.BlockSpec((tc, D_CONV), lambda i: (i, 0)),
                  pl.BlockSpec((HALO, D_CONV), lambda i: (jnp.minimum((i + 1) * hb, nh - 1), 0)),
                  pl.BlockSpec((CONV_WIDTH, D_CONV), full), pl.BlockSpec((1, D_CONV), full),
                  pl.BlockSpec((1, D_CONV), full), pl.BlockSpec((1, D_CONV), full)],
        out_specs=pl.BlockSpec((tc, D_CONV), lambda i: (i, 0)),
        out_shape=jax.ShapeDtypeStruct((n, D_CONV), BF16),
        scratch_shapes=[pltpu.VMEM((tc + 2 * HALO, D_CONV), F32),
                        pltpu.VMEM((SUBLANES, tc + 2 * HALO - SUBLANES, D_CONV), F32)],
        compiler_params=_cparams("parallel"),
        name="conv",
    )(u, u, u, w, b, lg, lb)


def _attn_kernel(q_ref, kp_ref, kc_ref, kn_ref, vp_ref, vc_ref, vn_ref, tb_ref, o_ref, *, rows):
    i = pl.program_id(0)
    nkey = 3 * ATT_BLK
    lrow = lax.broadcasted_iota(I32, (1, nkey), 1) >> 6
    lane = lax.broadcasted_iota(I32, (1, 2 * HEAD_DIM), 1)
    masks = []
    for j in range(ATT_ROWS):
        r = i * ATT_ROWS + j
        rs = jnp.clip(r - WIN_H // 2, 0, rows - WIN_H)
        lo = rs - (i - 1) * ATT_ROWS
        masks.append(jnp.where((lrow >= lo) & (lrow < lo + WIN_H), 0.0, NEG))
    dn = (((1,), (1,)), ((), ()))

    def pair_slice(h):
        return slice(2 * HEAD_DIM * (h // 2), 2 * HEAD_DIM * (h // 2 + 1))

    def in_half(h):
        return (lane >= HEAD_DIM * (h % 2)) & (lane < HEAD_DIM * (h % 2 + 1))

    def scores(h):
        sl = pair_slice(h)
        q2 = q_ref[:, sl]
        qm = jnp.where(in_half(h), q2, jnp.zeros_like(q2))
        s = jnp.concatenate([lax.dot_general(qm, kk[:, sl], dn, preferred_element_type=F32)
                             for kk in (kp_ref, kc_ref, kn_ref)], axis=1)
        parts = []
        for j in range(ATT_ROWS):
            bias = jnp.concatenate([tb_ref[h, 2 * m - j + 3] for m in range(nkey // (2 * GRID_W))], axis=1)
            parts.append(s[j * GRID_W:(j + 1) * GRID_W, :] + bias + masks[j])
        return jnp.concatenate(parts, axis=0)

    def softmax(s):
        mx = jnp.max(s, axis=-1, keepdims=True)
        e = jnp.exp2(s - mx)
        return e.astype(BF16), jnp.sum(e, axis=-1, keepdims=True)

    def values(h, pb, den):
        sl = pair_slice(h)
        o = sum(jnp.dot(pb[:, ATT_BLK * t:ATT_BLK * (t + 1)], vv[:, sl], preferred_element_type=F32)
                for t, vv in enumerate((vp_ref, vc_ref, vn_ref)))
        return o / den

    stage_s = {h: scores(h) for h in range(2)}
    stage_p = {0: softmax(stage_s.pop(0))}
    out_pair = None
    for h in range(N_HEADS):
        if h + 2 < N_HEADS:
            stage_s[h + 2] = scores(h + 2)
        if h + 1 < N_HEADS:
            stage_p[h + 1] = softmax(stage_s.pop(h + 1))
        o = values(h, *stage_p.pop(h))
        if h % 2 == 0:
            out_pair = o
        else:
            o_ref[:, pair_slice(h)] = jnp.where(in_half(h), o, out_pair).astype(BF16)


def _attn(q, k, v, tb):
    n = q.shape[0]
    rows = n // GRID_W
    nb = n // ATT_BLK
    cur = lambda i: (i, 0)
    prev = lambda i: (jnp.maximum(i - 1, 0), 0)
    nxt = lambda i: (jnp.minimum(i + 1, nb - 1), 0)
    blk = (ATT_BLK, D_ATTN)
    return pl.pallas_call(
        functools.partial(_attn_kernel, rows=rows),
        grid=(nb,),
        in_specs=[pl.BlockSpec(blk, cur), pl.BlockSpec(blk, prev), pl.BlockSpec(blk, cur), pl.BlockSpec(blk, nxt),
                  pl.BlockSpec(blk, prev), pl.BlockSpec(blk, cur), pl.BlockSpec(blk, nxt),
                  pl.BlockSpec(tb.shape, lambda i: (0, 0, 0, 0))],
        out_specs=pl.BlockSpec(blk, cur),
        out_shape=jax.ShapeDtypeStruct((n, D_ATTN), BF16),
        compiler_params=_cparams("parallel"),
        name="attn",
    )(q, k, k, k, v, v, v, tb)


def _bias_table(rpb):
    cols = jnp.arange(GRID_W)
    start = jnp.clip(cols - WIN_W // 2, 0, GRID_W - WIN_W)
    kc = cols[None, :]
    inwin = (kc >= start[:, None]) & (kc < start[:, None] + WIN_W)
    rel_idx = kc - cols[:, None] + (WIN_W - 1)
    pick = ((rel_idx[None] == jnp.arange(2 * WIN_W - 1)[:, None, None]) & inwin[None]).astype(F32)
    rel = jnp.einsum('hdj,jck->hdck', rpb.astype(F32), pick, precision=lax.Precision.HIGHEST)
    t = jnp.where(inwin[None, None], rel * LOG2E, NEG)
    t = jnp.concatenate([t, jnp.full_like(t[:, :1], NEG)], axis=1)
    t_next = jnp.concatenate([t[:, 1:], t[:, -1:]], axis=1)
    return jnp.concatenate([t, t_next], axis=-1).astype(F32)


def _mix_route_kernel(x_ref, uc_ref, at_ref, g1_ref, wg_ref, wco_ref, wao_ref, wo_ref, g2_ref,
                      wrh_ref, wrl_ref, rb_ref, tri_ref, wsgu_ref, wsd_ref,
                      xo_ref, hp_ref, idx_ref, rank_ref, wt_ref, cnt_ref, carry_ref, *, tm):
    i = pl.program_id(0)

    @pl.when(i == 0)
    def _():
        carry_ref[...] = jnp.zeros_like(carry_ref)

    th = tm // MIX_SPLIT
    dn = (((1,), (1,)), ((), ()))

    def mix(p):
        rs = slice(p * th, (p + 1) * th)
        x = x_ref[rs, :]
        h = _rms(x, g1_ref[...]).astype(BF16)
        gates = jax.nn.sigmoid(jnp.dot(h, wg_ref[...], preferred_element_type=F32))
        y_conv = jnp.dot(uc_ref[rs, :], wco_ref[...], preferred_element_type=F32)
        y_attn = jnp.dot(at_ref[rs, :], wao_ref[...], preferred_element_type=F32)
        merged = gates[:, :D_MODEL] * y_conv + gates[:, D_MODEL:] * y_attn
        x1 = x + jnp.dot(merged.astype(BF16), wo_ref[...], preferred_element_type=F32)
        h2 = _rms(x1, g2_ref[...])
        hp_ref[rs, :] = _pack_bf16_pair(h2[:, :D_MODEL // 2], h2[:, D_MODEL // 2:])
        return x1, h2

    def shared_and_logits(p, x1, h2):
        rs = slice(p * th, (p + 1) * th)
        h2b = h2.astype(BF16)
        gu = jnp.dot(h2b, wsgu_ref[...], preferred_element_type=F32)
        mid = (jax.nn.silu(gu[:, :D_SHARED]) * gu[:, D_SHARED:]).astype(BF16)
        xo_ref[rs, :] = x1 + jnp.dot(mid, wsd_ref[...], preferred_element_type=F32)
        h2l = (h2 - h2b.astype(F32)).astype(BF16)
        return (lax.dot_general(wrh_ref[...], h2b, dn, preferred_element_type=F32)
                + lax.dot_general(wrl_ref[...], h2b, dn, preferred_element_type=F32)
                + lax.dot_general(wrh_ref[...], h2l, dn, preferred_element_type=F32))

    def route(p, logits, carry):
        cs = slice(p * th, (p + 1) * th)
        scores = jax.nn.sigmoid(logits)
        biased = scores + rb_ref[...]

        sub = lax.broadcasted_iota(I32, (GROUP_SIZE, th), 0).astype(F32)
        groups, gscore = [], []
        for g in range(N_GROUPS):
            bg = biased[g * GROUP_SIZE:(g + 1) * GROUP_SIZE, :]
            m1 = jnp.max(bg, axis=0, keepdims=True)
            first = jnp.min(jnp.where(bg == m1, sub, float(GROUP_SIZE)), axis=0, keepdims=True)
            m2 = jnp.max(jnp.where(sub == first, -jnp.inf, bg), axis=0, keepdims=True)
            groups.append(bg)
            gscore.append(m1 + m2)
        masked = []
        for g in range(N_GROUPS):
            beaten = jnp.zeros((1, th), F32)
            for o in range(N_GROUPS):
                if o == g:
                    continue
                wins = (gscore[o] >= gscore[g]) if o < g else (gscore[o] > gscore[g])
                beaten = beaten + jnp.where(wins, 1.0, 0.0)
            masked.append(jnp.where(beaten < TOPK_GROUPS, groups[g], -jnp.inf))
        cur = jnp.concatenate(masked, axis=0)

        eid = lax.broadcasted_iota(I32, (N_EXPERTS, th), 0).astype(F32)
        idx_rows, sc_rows = [], []
        sel = jnp.zeros((N_EXPERTS, th), F32)
        for _ in range(TOP_K):
            m = jnp.max(cur, axis=0, keepdims=True)
            idx = jnp.min(jnp.where(cur == m, eid, float(N_EXPERTS)), axis=0, keepdims=True)
            hit = eid == idx
            sel = jnp.where(hit, 1.0, sel)
            cur = jnp.where(hit, -jnp.inf, cur)
            idx_rows.append(idx)
            sc_rows.append(jnp.sum(jnp.where(hit, scores, 0.0), axis=0, keepdims=True))
        wsum = sc_rows[0]
        for k in range(1, TOP_K):
            wsum = wsum + sc_rows[k]

        prefix = jnp.dot(sel.astype(BF16), tri_ref[...], preferred_element_type=F32)
        rank = carry + prefix
        rank_rows = [jnp.sum(jnp.where(eid == idx_rows[k], rank, 0.0), axis=0, keepdims=True)
                     for k in range(TOP_K)]
        idx_ref[:, cs] = jnp.concatenate(idx_rows, axis=0).astype(I32)
        rank_ref[:, cs] = jnp.concatenate(rank_rows, axis=0).astype(I32)
        wt_ref[:, cs] = jnp.concatenate([s / wsum * ROUTE_SCALE for s in sc_rows], axis=0)
        return carry + jnp.sum(sel, axis=1, keepdims=True)

    carry = carry_ref[:, 0:1]
    logits = shared_and_logits(0, *mix(0))
    for p in range(MIX_SPLIT):
        if p + 1 < MIX_SPLIT:
            mixed_next = mix(p + 1)
        carry = route(p, logits, carry)
        if p + 1 < MIX_SPLIT:
            logits = shared_and_logits(p + 1, *mixed_next)
    carry_ref[...] = jnp.broadcast_to(carry, carry_ref.shape)
    cnt_ref[...] = carry_ref[...]


def _mix_route(x, uc, at, g1, wg, wco, wao, wo, g2, wrh, wrl, rb, wsgu, wsd, tm=1024):
    n = x.shape[0]
    th = tm // MIX_SPLIT
    tri = (jnp.arange(th)[:, None] < jnp.arange(th)[None, :]).astype(BF16)
    row = lambda i: (i, 0)
    col = lambda i: (0, i)
    full = lambda i: (0, 0)
    ins = [x, uc, at, g1, wg, wco, wao, wo, g2, wrh, wrl, rb, tri, wsgu, wsd]
    in_specs = [pl.BlockSpec((tm, D_MODEL), row), pl.BlockSpec((tm, D_CONV), row), pl.BlockSpec((tm, D_ATTN), row)]
    in_specs += [pl.BlockSpec(a.shape, full) for a in ins[3:]]
    return pl.pallas_call(
        functools.partial(_mix_route_kernel, tm=tm),
        grid=(n // tm,),
        in_specs=in_specs,
        out_specs=[pl.BlockSpec((tm, D_MODEL), row), pl.BlockSpec((tm, D_MODEL // 2), row),
                   pl.BlockSpec((TOP_K, tm), col), pl.BlockSpec((TOP_K, tm), col), pl.BlockSpec((TOP_K, tm), col),
                   pl.BlockSpec((N_EXPERTS, 128), full)],
        out_shape=[jax.ShapeDtypeStruct((n, D_MODEL), F32), jax.ShapeDtypeStruct((n, D_MODEL // 2), U32),
                   jax.ShapeDtypeStruct((TOP_K, n), I32), jax.ShapeDtypeStruct((TOP_K, n), I32),
                   jax.ShapeDtypeStruct((TOP_K, n), F32), jax.ShapeDtypeStruct((N_EXPERTS, 128), F32)],
        scratch_shapes=[pltpu.VMEM((N_EXPERTS, 128), F32)],
        compiler_params=_cparams("arbitrary"),
        name="mix_route",
    )(*ins)


def _sc_workers():
    info = plsc.get_sparse_core_info()
    return info.num_cores, info.num_subcores


def _sc_mesh():
    return plsc.VectorSubcoreMesh(core_axis_name="c", subcore_axis_name="s")


def _worker_id(n_cores):
    return lax.axis_index("s") * n_cores + lax.axis_index("c")


def _dispatch(slots3, hp, n_slots):
    n_chunks = slots3.shape[0]
    n_cores, n_sub = _sc_workers()
    per_worker = n_chunks // (n_cores * n_sub)
    assert per_worker * n_cores * n_sub == n_chunks

    @functools.partial(
        pl.kernel, mesh=_sc_mesh(),
        out_type=jax.ShapeDtypeStruct((n_slots, D_MODEL // 2), U32),
        scratch_types=[pltpu.VMEM((TOP_K, SC_CHUNK), I32), pltpu.VMEM((SC_CHUNK, D_MODEL // 2), U32),
                       pltpu.SemaphoreType.DMA],
        name="dispatch")
    def run(slots_hbm, hp_hbm, xs_hbm, idx_v, rows_v, sem):
        first = _worker_id(n_cores) * per_worker

        @pl.loop(0, per_worker)
        def _(j):
            c = first + j
            pltpu.sync_copy(slots_hbm.at[c], idx_v)
            pltpu.sync_copy(hp_hbm.at[pl.ds(c * SC_CHUNK, SC_CHUNK)], rows_v)
            copies = [pltpu.async_copy(rows_v, xs_hbm.at[idx_v.at[k]], sem) for k in range(TOP_K)]
            for cp in copies:
                cp.wait()

    return run(slots3, hp)


def _experts_kernel(bexp_ref, bidx_ref, bval_ref, nused_ref, next_ref, wslot_ref,
                    xs_ref, wg_ref, wu_ref, wd_ref, y_ref, xbuf, wg_buf, wu_buf, wd_buf, wgu_s, wd_s, xsem, wsem):
    b = pl.program_id(0)
    n_used = nused_ref[0]

    def block_copy(blk):
        slot = blk % XS_SLOTS
        rows = pl.ds(pl.multiple_of(blk * SLOT_BLK, SLOT_BLK), SLOT_BLK)
        return pltpu.make_async_copy(xs_ref.at[rows], xbuf.at[slot], xsem.at[slot])

    @pl.when(b == 0)
    def _():
        for blk in range(XS_AHEAD):
            @pl.when(blk < n_used)
            def _():
                block_copy(blk).start()

    @pl.when(b + XS_AHEAD < n_used)
    def _():
        block_copy(b + XS_AHEAD).start()

    def weight_copies(e, slot):
        return [pltpu.make_async_copy(wg_ref.at[e], wg_buf.at[slot], wsem.at[slot]),
                pltpu.make_async_copy(wu_ref.at[e], wu_buf.at[slot], wsem.at[slot]),
                pltpu.make_async_copy(wd_ref.at[e], wd_buf.at[slot], wsem.at[slot])]

    @pl.when(b < n_used)
    def _():
        e = bexp_ref[b]
        slot = wslot_ref[e]

        @pl.when(b == 0)
        def _():
            for cp in weight_copies(e, slot):
                cp.start()

        @pl.when((b == 0) | (bexp_ref[jnp.maximum(b - 1, 0)] != e))
        def _():
            for cp in weight_copies(e, slot):
                cp.wait()
            wgu_s[:, :D_EXPERT] = wg_buf[slot].astype(BF16)
            wgu_s[:, D_EXPERT:] = wu_buf[slot].astype(BF16)
            wd_s[...] = wd_buf[slot].astype(BF16)
            nxt = next_ref[e]

            @pl.when(nxt >= 0)
            def _():
                for cp in weight_copies(nxt, 1 - slot):
                    cp.start()

        srow = lax.broadcasted_iota(I32, (SLOT_BLK, 1), 0)
        block_copy(b).wait()
        xw = jnp.where(srow < bval_ref[b], xbuf[b % XS_SLOTS], jnp.uint32(0))
        xa, xb = _unpack_bf16_pair(xw)
        x = jnp.concatenate([xa.astype(BF16), xb.astype(BF16)], axis=1)
        gu = jnp.dot(x, wgu_s[...], preferred_element_type=F32)
        mid = (jax.nn.silu(gu[:, :D_EXPERT]) * gu[:, D_EXPERT:]).astype(BF16)
        y = jnp.dot(mid, wd_s[...], preferred_element_type=F32)
        y_ref[...] = _pack_bf16_pair(y[:, :D_MODEL // 2], y[:, D_MODEL // 2:])


def _experts(bexp, bidx, bval, nused, next_exp, wslot, xs, wg, wu, wd):
    n_blocks = bexp.shape[0]
    xmap = lambda b, be, bi, bv, nu, nx, ws: (bi[b], 0)
    grid_spec = pltpu.PrefetchScalarGridSpec(
        num_scalar_prefetch=6,
        grid=(n_blocks,),
        in_specs=[pl.BlockSpec(memory_space=pl.ANY),
                  pl.BlockSpec(memory_space=pl.ANY), pl.BlockSpec(memory_space=pl.ANY),
                  pl.BlockSpec(memory_space=pl.ANY)],
        out_specs=pl.BlockSpec((SLOT_BLK, D_MODEL // 2), xmap),
        scratch_shapes=[pltpu.VMEM((XS_SLOTS, SLOT_BLK, D_MODEL // 2), U32),
                        pltpu.VMEM((2, D_MODEL, D_EXPERT), F32), pltpu.VMEM((2, D_MODEL, D_EXPERT), F32),
                        pltpu.VMEM((2, D_EXPERT, D_MODEL), F32),
                        pltpu.VMEM((D_MODEL, 2 * D_EXPERT), BF16), pltpu.VMEM((D_EXPERT, D_MODEL), BF16),
                        pltpu.SemaphoreType.DMA((XS_SLOTS,)), pltpu.SemaphoreType.DMA((2,))],
    )
    return pl.pallas_call(
        _experts_kernel,
        grid_spec=grid_spec,
        out_shape=jax.ShapeDtypeStruct(xs.shape, U32),
        compiler_params=_cparams("arbitrary"),
        name="experts",
    )(bexp, bidx, bval, nused, next_exp, wslot, xs, wg, wu, wd)


def _gather(slots3, y, n):
    n_chunks = slots3.shape[0]
    n_cores, n_sub = _sc_workers()
    per_worker = n_chunks // (n_cores * n_sub)
    assert per_worker * n_cores * n_sub == n_chunks

    @functools.partial(
        pl.kernel, mesh=_sc_mesh(),
        out_type=jax.ShapeDtypeStruct((TOP_K, n, D_MODEL // 2), U32),
        scratch_types=[pltpu.VMEM((TOP_K, SC_CHUNK), I32),
                       pltpu.VMEM((SC_CHUNK, D_MODEL // 2), U32), pltpu.VMEM((SC_CHUNK, D_MODEL // 2), U32),
                       pltpu.SemaphoreType.DMA, pltpu.SemaphoreType.DMA, pltpu.SemaphoreType.DMA],
        name="gather")
    def run(slots_hbm, y_hbm, g_hbm, idx_v, buf0, buf1, gsem, wsem0, wsem1):
        first = _worker_id(n_cores) * per_worker
        bufs = (buf0, buf1)
        wsems = (wsem0, wsem1)

        @pl.loop(0, per_worker)
        def _(j):
            c = first + j
            pltpu.sync_copy(slots_hbm.at[c], idx_v)
            gathers = [None] * TOP_K
            writes = [None] * TOP_K
            gathers[0] = pltpu.async_copy(y_hbm.at[idx_v.at[0]], bufs[0], gsem)
            for k in range(TOP_K):
                gathers[k].wait()
                if k >= 1:
                    writes[k - 1].wait()
                if k + 1 < TOP_K:
                    gathers[k + 1] = pltpu.async_copy(y_hbm.at[idx_v.at[k + 1]], bufs[(k + 1) % 2], gsem)
                writes[k] = pltpu.async_copy(bufs[k % 2], g_hbm.at[k, pl.ds(c * SC_CHUNK, SC_CHUNK)],
                                             wsems[k % 2])
            writes[TOP_K - 1].wait()

    return run(slots3, y)


def _combine_kernel(wt_ref, xo_ref, g_ref, *rest):
    o_ref = rest[-1]
    half = D_MODEL // 2
    acc_a = xo_ref[:, :half]
    acc_b = xo_ref[:, half:]
    w_all = wt_ref[...].T
    for k in range(TOP_K):
        ya, yb = _unpack_bf16_pair(g_ref[k])
        w = w_all[:, k:k + 1]
        acc_a = acc_a + w * ya
        acc_b = acc_b + w * yb
    o_ref[:, :half] = acc_a
    o_ref[:, half:] = acc_b


def _combine(wt, xo, g, part, out_so_far, tm=256):
    n = xo.shape[0]
    steps = g.shape[1] // tm
    row = lambda i: (part * steps + i, 0)
    in_specs = [pl.BlockSpec((TOP_K, tm), lambda i: (0, part * steps + i)), pl.BlockSpec((tm, D_MODEL), row),
                pl.BlockSpec((TOP_K, tm, D_MODEL // 2), lambda i: (0, i, 0))]
    args = [wt, xo, g]
    aliases = {}
    if out_so_far is not None:
        in_specs.append(pl.BlockSpec(memory_space=pl.ANY))
        args.append(out_so_far)
        aliases = {len(args) - 1: 0}
    return pl.pallas_call(
        _combine_kernel,
        grid=(steps,),
        in_specs=in_specs,
        out_specs=pl.BlockSpec((tm, D_MODEL), row),
        out_shape=jax.ShapeDtypeStruct((n, D_MODEL), F32),
        input_output_aliases=aliases,
        compiler_params=_cparams("parallel"),
        name="combine",
    )(*args)


def _layer(x, norm_mix_g, w_in, q_norm_g, k_norm_g, rpb, conv_dw_w, conv_dw_b, conv_ln_g, conv_ln_b,
           w_conv_out, w_attn_out, w_o, norm_ffn_g, w_router, router_bias, w_exp_gate, w_exp_up,
           w_exp_down, w_sh_gate, w_sh_up, w_sh_down):
    n = x.shape[0]
    c_qkv = 2 * D_CONV + 3 * D_ATTN
    w_qkv = w_in[:, :c_qkv].astype(BF16)
    w_gates = w_in[:, c_qkv:].astype(BF16)
    head_of = jnp.arange(D_ATTN // 2) // HEAD_DIM
    bsum = (head_of[:, None] == head_of[None, :]).astype(BF16)
    row = lambda v: v.reshape(1, -1).astype(F32)

    u, q, k, v = _inproj(x, row(norm_mix_g), w_qkv, bsum,
                         row(jnp.tile(q_norm_g, N_HEADS)), row(jnp.tile(k_norm_g, N_HEADS)))
    uc = _conv(u, conv_dw_w.reshape(CONV_WIDTH, D_CONV), row(conv_dw_b), row(conv_ln_g), row(conv_ln_b))
    at = _attn(q, k, v, _bias_table(rpb))

    wr_t = w_router.T
    wr_hi = wr_t.astype(BF16)
    wr_lo = (wr_t - wr_hi.astype(F32)).astype(BF16)
    wsgu = jnp.concatenate([w_sh_gate, w_sh_up], axis=1).astype(BF16)
    xo, hp, idx_t, rank_t, wt_t, cnt = _mix_route(
        x, uc, at, row(norm_mix_g), w_gates, w_conv_out.astype(BF16), w_attn_out.astype(BF16),
        w_o.astype(BF16), row(norm_ffn_g), wr_hi, wr_lo, router_bias.reshape(N_EXPERTS, 1).astype(F32),
        wsgu, w_sh_down.astype(BF16))

    counts = cnt[:, 0].astype(I32)
    padded = (counts + SLOT_BLK - 1) // SLOT_BLK * SLOT_BLK
    pad_end = jnp.cumsum(padded)
    pad_start = pad_end - padded
    n_blocks = (n * TOP_K + N_EXPERTS * (SLOT_BLK - 1)) // SLOT_BLK
    n_used = pad_end[-1] // SLOT_BLK
    bidx = jnp.minimum(jnp.arange(n_blocks, dtype=I32), n_used - 1)
    is_done = (pad_end[None, :] <= (bidx * SLOT_BLK)[:, None]).astype(I32)
    bexp = jnp.minimum(jnp.sum(is_done, axis=1), N_EXPERTS - 1)
    owner = (bexp[:, None] == jnp.arange(N_EXPERTS)[None, :]).astype(I32)
    bval = jnp.sum(owner * (pad_start + counts)[None, :], axis=1) - bidx * SLOT_BLK
    bval = jnp.clip(bval, 0, SLOT_BLK).astype(I32)
    eids = jnp.arange(N_EXPERTS, dtype=I32)
    later = (eids[None, :] > eids[:, None]) & (counts[None, :] > 0)
    next_exp = jnp.min(jnp.where(later, eids[None, :], N_EXPERTS), axis=1)
    next_exp = jnp.where(next_exp == N_EXPERTS, -1, next_exp).astype(I32)
    wslot = ((jnp.cumsum((counts > 0).astype(I32)) - 1) & 1).astype(I32)

    onehot = (idx_t[:, :, None] == jnp.arange(N_EXPERTS)[None, None, :]).astype(I32)
    slots = jnp.sum(onehot * pad_start[None, None, :], axis=2) + rank_t
    slots3 = slots.reshape(TOP_K, n // SC_CHUNK, SC_CHUNK).transpose(1, 0, 2)

    xs = _dispatch(slots3, hp, n_blocks * SLOT_BLK)
    y = _experts(bexp, bidx, bval, n_used.reshape(1).astype(I32), next_exp, wslot, xs,
                 w_exp_gate, w_exp_up, w_exp_down)
    part_chunks = slots3.shape[0] // COMBINE_PARTS
    out = None
    for part in range(COMBINE_PARTS):
        g = _gather(slots3[part * part_chunks:(part + 1) * part_chunks], y, n // COMBINE_PARTS)
        out = _combine(wt_t, xo, g, part, out)
    return out


def kernel(x, norm_mix_g, w_in, q_norm_g, k_norm_g, rpb, conv_dw_w, conv_dw_b, conv_ln_g, conv_ln_b, w_conv_out, w_attn_out, w_o, norm_ffn_g, w_router, router_bias, w_exp_gate, w_exp_up, w_exp_down, w_sh_gate, w_sh_up, w_sh_down):
    b, s, d = x.shape
    assert b == 1 and d == D_MODEL and s % (2 * ATT_BLK) == 0, x.shape
    xf = x.reshape(b * s, d)
    depth = norm_mix_g.shape[0]
    for l in range(depth):
        xf = _layer(xf, norm_mix_g[l], w_in[l], q_norm_g[l], k_norm_g[l], rpb[l], conv_dw_w[l], conv_dw_b[l],
                    conv_ln_g[l], conv_ln_b[l], w_conv_out[l], w_attn_out[l], w_o[l], norm_ffn_g[l],
                    w_router[l], router_bias[l], w_exp_gate[l], w_exp_up[l], w_exp_down[l], w_sh_gate[l],
                    w_sh_up[l], w_sh_down[l])
    return xf.reshape(b, s, d)
```

```python
import functools

import jax
import jax.numpy as jnp
from jax import lax
from jax.experimental import pallas as pl
from jax.experimental.pallas import tpu as pltpu
from jax.experimental.pallas import tpu_sc as plsc

D_MODEL = 1024
GRID_W = 64
D_CONV = 512
CONV_WIDTH = 31
CONV_PAD = CONV_WIDTH // 2
N_HEADS = 8
HEAD_DIM = 64
D_ATTN = N_HEADS * HEAD_DIM
WIN_H = 8
WIN_W = 16
N_EXPERTS = 64
TOP_K = 8
N_GROUPS = 8
GROUP_SIZE = N_EXPERTS // N_GROUPS
TOPK_GROUPS = 4
D_EXPERT = 256
D_SHARED = 256
ROUTE_SCALE = 2.5
EPS = 1e-6
LOG2E = 1.4426950408889634

SUBLANES = 8
NEG = -1e30
HALO = 16
ATT_ROWS = 4
ATT_BLK = ATT_ROWS * GRID_W
SLOT_BLK = 512
MIX_SPLIT = 2
XS_AHEAD = 2
XS_SLOTS = XS_AHEAD + 1
TOKEN_PARTS = 4
SC_CHUNK = 64
VMEM_LIMIT = 56 * 1024 * 1024

F32 = jnp.float32
BF16 = jnp.bfloat16
I32 = jnp.int32
U32 = jnp.uint32


def _cparams(*sem):
    return pltpu.CompilerParams(dimension_semantics=sem, vmem_limit_bytes=VMEM_LIMIT)


def _rms(x, g):
    return x * lax.rsqrt(jnp.mean(x * x, axis=-1, keepdims=True) + EPS) * g


def _pack_bf16_pair(a, b):
    def bf16_bits(v):
        return lax.bitcast_convert_type(v.astype(BF16).astype(F32), U32)
    return bf16_bits(a) | (bf16_bits(b) >> 16)


def _unpack_bf16_pair(w):
    a = lax.bitcast_convert_type(w & jnp.uint32(0xFFFF0000), F32)
    b = lax.bitcast_convert_type(w << 16, F32)
    return a, b


def _inproj_kernel(x_ref, g_ref, w_ref, bsum_ref, qg_ref, kg_ref, u_ref, q_ref, k_ref, v_ref):
    h = _rms(x_ref[...], g_ref[...]).astype(BF16)
    ag = jnp.dot(h, w_ref[:, 0:2 * D_CONV], preferred_element_type=F32)
    u_ref[...] = ag[:, :D_CONV] * jax.nn.sigmoid(ag[:, D_CONV:])
    bsum = bsum_ref[...]

    def head_norm(z, g):
        zz = (z * z).astype(BF16)
        hw = D_ATTN // 2
        ss = jnp.concatenate([jnp.dot(zz[:, :hw], bsum, preferred_element_type=F32),
                              jnp.dot(zz[:, hw:], bsum, preferred_element_type=F32)], axis=1)
        return z * lax.rsqrt(ss * (1.0 / HEAD_DIM) + EPS) * g

    c0 = 2 * D_CONV
    q = jnp.dot(h, w_ref[:, c0:c0 + D_ATTN], preferred_element_type=F32)
    q_ref[...] = (head_norm(q, qg_ref[...]) * (HEAD_DIM ** -0.5 * LOG2E)).astype(BF16)
    k = jnp.dot(h, w_ref[:, c0 + D_ATTN:c0 + 2 * D_ATTN], preferred_element_type=F32)
    k_ref[...] = head_norm(k, kg_ref[...]).astype(BF16)
    v = jnp.dot(h, w_ref[:, c0 + 2 * D_ATTN:c0 + 3 * D_ATTN], preferred_element_type=F32)
    v_ref[...] = v.astype(BF16)


def _inproj(x, g, w, bsum, qg, kg, tm=512):
    n = x.shape[0]
    wc = w.shape[1]
    full = lambda i: (0, 0)
    row = lambda i: (i, 0)
    return pl.pallas_call(
        _inproj_kernel,
        grid=(n // tm,),
        in_specs=[pl.BlockSpec((tm, D_MODEL), row), pl.BlockSpec((1, D_MODEL), full),
                  pl.BlockSpec((D_MODEL, wc), full), pl.BlockSpec(bsum.shape, full),
                  pl.BlockSpec((1, D_ATTN), full), pl.BlockSpec((1, D_ATTN), full)],
        out_specs=[pl.BlockSpec((tm, D_CONV), row), pl.BlockSpec((tm, D_ATTN), row),
                   pl.BlockSpec((tm, D_ATTN), row), pl.BlockSpec((tm, D_ATTN), row)],
        out_shape=[jax.ShapeDtypeStruct((n, D_CONV), F32), jax.ShapeDtypeStruct((n, D_ATTN), BF16),
                   jax.ShapeDtypeStruct((n, D_ATTN), BF16), jax.ShapeDtypeStruct((n, D_ATTN), BF16)],
        compiler_params=_cparams("parallel"),
        name="inproj",
    )(x, g, w, bsum, qg, kg)


def _conv_kernel(up_ref, uc_ref, un_ref, w_ref, b_ref, lg_ref, lb_ref, o_ref, ext_ref, sh_ref, *, tc, ch):
    i = pl.program_id(0)
    last = pl.num_programs(0) - 1
    ext_ref[0:HALO, :] = jnp.where(i > 0, up_ref[...], 0.0)
    ext_ref[HALO:HALO + tc, :] = uc_ref[...]
    ext_ref[HALO + tc:2 * HALO + tc, :] = jnp.where(i < last, un_ref[...], 0.0)
    span = sh_ref.shape[1]
    for b in range(SUBLANES):
        sh_ref[b] = ext_ref[b:b + span, :]
    bias = b_ref[...]
    lg = lg_ref[...]
    lb = lb_ref[...]
    for c in range(tc // ch):
        acc = jnp.broadcast_to(bias, (ch, D_CONV))
        for j in range(CONV_WIDTH):
            shift = HALO - CONV_PAD + j
            row0 = c * ch + shift // SUBLANES * SUBLANES
            acc = acc + sh_ref[shift % SUBLANES, row0:row0 + ch, :] * w_ref[j:j + 1, :]
        mu = jnp.mean(acc, axis=-1, keepdims=True)
        d = acc - mu
        var = jnp.mean(d * d, axis=-1, keepdims=True)
        y = d * lax.rsqrt(var + EPS) * lg + lb
        o_ref[c * ch:(c + 1) * ch, :] = (y * jax.nn.sigmoid(y)).astype(BF16)


def _conv(u, w, b, lg, lb, tc=256, ch=64):
    n = u.shape[0]
    hb = tc // HALO
    nh = n // HALO
    full = lambda i: (0, 0)
    return pl.pallas_call(
        functools.partial(_conv_kernel, tc=tc, ch=ch),
        grid=(n // tc,),
        in_specs=[pl.BlockSpec((HALO, D_CONV), lambda i: (jnp.maximum(i * hb - 1, 0), 0)),
                  pl.BlockSpec((tc, D_CONV), lambda i: (i, 0)),
                  pl.BlockSpec((HALO, D_CONV), lambda i: (jnp.minimum((i + 1) * hb, nh - 1), 0)),
                  pl.BlockSpec((CONV_WIDTH, D_CONV), full), pl.BlockSpec((1, D_CONV), full),
                  pl.BlockSpec((1, D_CONV), full), pl.BlockSpec((1, D_CONV), full)],
        out_specs=pl.BlockSpec((tc, D_CONV), lambda i: (i, 0)),
        out_shape=jax.ShapeDtypeStruct((n, D_CONV), BF16),
        scratch_shapes=[pltpu.VMEM((tc + 2 * HALO, D_CONV), F32),
                        pltpu.VMEM((SUBLANES, tc + 2 * HALO - SUBLANES, D_CONV), F32)],
        compiler_params=_cparams("parallel"),
        name="conv",
    )(u, u, u, w, b, lg, lb)


def _attn_kernel(q_ref, kp_ref, kc_ref, kn_ref, vp_ref, vc_ref, vn_ref, tb_ref, o_ref, *, rows):
    i = pl.program_id(0)
    nkey = 3 * ATT_BLK
    lrow = lax.broadcasted_iota(I32, (1, nkey), 1) >> 6
    lane = lax.broadcasted_iota(I32, (1, 2 * HEAD_DIM), 1)
    masks = []
    for j in range(ATT_ROWS):
        r = i * ATT_ROWS + j
        rs = jnp.clip(r - WIN_H // 2, 0, rows - WIN_H)
        lo = rs - (i - 1) * ATT_ROWS
        masks.append(jnp.where((lrow >= lo) & (lrow < lo + WIN_H), 0.0, NEG))
    dn = (((1,), (1,)), ((), ()))

    def pair_slice(h):
        return slice(2 * HEAD_DIM * (h // 2), 2 * HEAD_DIM * (h // 2 + 1))

    def in_half(h):
        return (lane >= HEAD_DIM * (h % 2)) & (lane < HEAD_DIM * (h % 2 + 1))

    def scores(h):
        sl = pair_slice(h)
        q2 = q_ref[:, sl]
        qm = jnp.where(in_half(h), q2, jnp.zeros_like(q2))
        s = jnp.concatenate([lax.dot_general(qm, kk[:, sl], dn, preferred_element_type=F32)
                             for kk in (kp_ref, kc_ref, kn_ref)], axis=1)
        parts = []
        for j in range(ATT_ROWS):
            bias = jnp.concatenate([tb_ref[h, 2 * m - j + 3] for m in range(nkey // (2 * GRID_W))], axis=1)
            parts.append(s[j * GRID_W:(j + 1) * GRID_W, :] + bias + masks[j])
        return jnp.concatenate(parts, axis=0)

    def softmax(s):
        mx = jnp.max(s, axis=-1, keepdims=True)
        e = jnp.exp2(s - mx)
        return e.astype(BF16), jnp.sum(e, axis=-1, keepdims=True)

    def values(h, pb, den):
        sl = pair_slice(h)
        o = sum(jnp.dot(pb[:, ATT_BLK * t:ATT_BLK * (t + 1)], vv[:, sl], preferred_element_type=F32)
                for t, vv in enumerate((vp_ref, vc_ref, vn_ref)))
        return o / den

    stage_s = {h: scores(h) for h in range(2)}
    stage_p = {0: softmax(stage_s.pop(0))}
    out_pair = None
    for h in range(N_HEADS):
        if h + 2 < N_HEADS:
            stage_s[h + 2] = scores(h + 2)
        if h + 1 < N_HEADS:
            stage_p[h + 1] = softmax(stage_s.pop(h + 1))
        o = values(h, *stage_p.pop(h))
        if h % 2 == 0:
            out_pair = o
        else:
            o_ref[:, pair_slice(h)] = jnp.where(in_half(h), o, out_pair).astype(BF16)


def _attn(q, k, v, tb):
    n = q.shape[0]
    rows = n // GRID_W
    nb = n // ATT_BLK
    cur = lambda i: (i, 0)
    prev = lambda i: (jnp.maximum(i - 1, 0), 0)
    nxt = lambda i: (jnp.minimum(i + 1, nb - 1), 0)
    blk = (ATT_BLK, D_ATTN)
    return pl.pallas_call(
        functools.partial(_attn_kernel, rows=rows),
        grid=(nb,),
        in_specs=[pl.BlockSpec(blk, cur), pl.BlockSpec(blk, prev), pl.BlockSpec(blk, cur), pl.BlockSpec(blk, nxt),
                  pl.BlockSpec(blk, prev), pl.BlockSpec(blk, cur), pl.BlockSpec(blk, nxt),
                  pl.BlockSpec(tb.shape, lambda i: (0, 0, 0, 0))],
        out_specs=pl.BlockSpec(blk, cur),
        out_shape=jax.ShapeDtypeStruct((n, D_ATTN), BF16),
        compiler_params=_cparams("parallel"),
        name="attn",
    )(q, k, k, k, v, v, v, tb)


def _bias_table(rpb):
    cols = jnp.arange(GRID_W)
    start = jnp.clip(cols - WIN_W // 2, 0, GRID_W - WIN_W)
    kc = cols[None, :]
    inwin = (kc >= start[:, None]) & (kc < start[:, None] + WIN_W)
    rel_idx = kc - cols[:, None] + (WIN_W - 1)
    pick = ((rel_idx[None] == jnp.arange(2 * WIN_W - 1)[:, None, None]) & inwin[None]).astype(F32)
    rel = jnp.einsum('hdj,jck->hdck', rpb.astype(F32), pick, precision=lax.Precision.HIGHEST)
    t = jnp.where(inwin[None, None], rel * LOG2E, NEG)
    t = jnp.concatenate([t, jnp.full_like(t[:, :1], NEG)], axis=1)
    t_next = jnp.concatenate([t[:, 1:], t[:, -1:]], axis=1)
    return jnp.concatenate([t, t_next], axis=-1).astype(F32)


def _mix_route_kernel(x_ref, uc_ref, at_ref, g1_ref, wg_ref, wco_ref, wao_ref, wo_ref, g2_ref,
                      wrh_ref, wrl_ref, rb_ref, tri_ref, wsgu_ref, wsd_ref, cin_ref,
                      xo_ref, hp_ref, idx_ref, rank_ref, wt_ref, cnt_ref, carry_ref, *, tm):
    i = pl.program_id(0)

    @pl.when(i == 0)
    def _():
        carry_ref[...] = cin_ref[...]

    th = tm // MIX_SPLIT
    dn = (((1,), (1,)), ((), ()))

    def mix(p):
        rs = slice(p * th, (p + 1) * th)
        x = x_ref[rs, :]
        h = _rms(x, g1_ref[...]).astype(BF16)
        gates = jax.nn.sigmoid(jnp.dot(h, wg_ref[...], preferred_element_type=F32))
        y_conv = jnp.dot(uc_ref[rs, :], wco_ref[...], preferred_element_type=F32)
        y_attn = jnp.dot(at_ref[rs, :], wao_ref[...], preferred_element_type=F32)
        merged = gates[:, :D_MODEL] * y_conv + gates[:, D_MODEL:] * y_attn
        x1 = x + jnp.dot(merged.astype(BF16), wo_ref[...], preferred_element_type=F32)
        h2 = _rms(x1, g2_ref[...])
        hp_ref[rs, :] = _pack_bf16_pair(h2[:, :D_MODEL // 2], h2[:, D_MODEL // 2:])
        return x1, h2

    def shared_and_logits(p, x1, h2):
        rs = slice(p * th, (p + 1) * th)
        h2b = h2.astype(BF16)
        gu = jnp.dot(h2b, wsgu_ref[...], preferred_element_type=F32)
        mid = (jax.nn.silu(gu[:, :D_SHARED]) * gu[:, D_SHARED:]).astype(BF16)
        xo_ref[rs, :] = x1 + jnp.dot(mid, wsd_ref[...], preferred_element_type=F32)
        h2l = (h2 - h2b.astype(F32)).astype(BF16)
        return (lax.dot_general(wrh_ref[...], h2b, dn, preferred_element_type=F32)
                + lax.dot_general(wrl_ref[...], h2b, dn, preferred_element_type=F32)
                + lax.dot_general(wrh_ref[...], h2l, dn, preferred_element_type=F32))

    def route(p, logits, carry):
        cs = slice(p * th, (p + 1) * th)
        scores = jax.nn.sigmoid(logits)
        biased = scores + rb_ref[...]

        sub = lax.broadcasted_iota(I32, (GROUP_SIZE, th), 0).astype(F32)
        groups, gscore = [], []
        for g in range(N_GROUPS):
            bg = biased[g * GROUP_SIZE:(g + 1) * GROUP_SIZE, :]
            m1 = jnp.max(bg, axis=0, keepdims=True)
            first = jnp.min(jnp.where(bg == m1, sub, float(GROUP_SIZE)), axis=0, keepdims=True)
            m2 = jnp.max(jnp.where(sub == first, -jnp.inf, bg), axis=0, keepdims=True)
            groups.append(bg)
            gscore.append(m1 + m2)
        masked = []
        for g in range(N_GROUPS):
            beaten = jnp.zeros((1, th), F32)
            for o in range(N_GROUPS):
                if o == g:
                    continue
                wins = (gscore[o] >= gscore[g]) if o < g else (gscore[o] > gscore[g])
                beaten = beaten + jnp.where(wins, 1.0, 0.0)
            masked.append(jnp.where(beaten < TOPK_GROUPS, groups[g], -jnp.inf))
        cur = jnp.concatenate(masked, axis=0)

        eid = lax.broadcasted_iota(I32, (N_EXPERTS, th), 0).astype(F32)
        idx_rows, sc_rows = [], []
        sel = jnp.zeros((N_EXPERTS, th), F32)
        for _ in range(TOP_K):
            m = jnp.max(cur, axis=0, keepdims=True)
            idx = jnp.min(jnp.where(cur == m, eid, float(N_EXPERTS)), axis=0, keepdims=True)
            hit = eid == idx
            sel = jnp.where(hit, 1.0, sel)
            cur = jnp.where(hit, -jnp.inf, cur)
            idx_rows.append(idx)
            sc_rows.append(jnp.sum(jnp.where(hit, scores, 0.0), axis=0, keepdims=True))
        wsum = sc_rows[0]
        for k in range(1, TOP_K):
            wsum = wsum + sc_rows[k]

        prefix = jnp.dot(sel.astype(BF16), tri_ref[...], preferred_element_type=F32)
        rank = carry + prefix
        rank_rows = [jnp.sum(jnp.where(eid == idx_rows[k], rank, 0.0), axis=0, keepdims=True)
                     for k in range(TOP_K)]
        idx_ref[:, cs] = jnp.concatenate(idx_rows, axis=0).astype(I32)
        rank_ref[:, cs] = jnp.concatenate(rank_rows, axis=0).astype(I32)
        wt_ref[:, cs] = jnp.concatenate([s / wsum * ROUTE_SCALE for s in sc_rows], axis=0)
        return carry + jnp.sum(sel, axis=1, keepdims=True)

    carry = carry_ref[:, 0:1]
    logits = shared_and_logits(0, *mix(0))
    for p in range(MIX_SPLIT):
        if p + 1 < MIX_SPLIT:
            mixed_next = mix(p + 1)
        carry = route(p, logits, carry)
        if p + 1 < MIX_SPLIT:
            logits = shared_and_logits(p + 1, *mixed_next)
    carry_ref[...] = jnp.broadcast_to(carry, carry_ref.shape)
    cnt_ref[...] = carry_ref[...]


def _mix_route(x, uc, at, g1, wg, wco, wao, wo, g2, wrh, wrl, rb, wsgu, wsd, counts_in, part, tm=1024):
    n = x.shape[0] // TOKEN_PARTS
    steps = n // tm
    th = tm // MIX_SPLIT
    tri = (jnp.arange(th)[:, None] < jnp.arange(th)[None, :]).astype(BF16)
    src = lambda i: (part * steps + i, 0)
    row = lambda i: (i, 0)
    col = lambda i: (0, i)
    full = lambda i: (0, 0)
    ins = [x, uc, at, g1, wg, wco, wao, wo, g2, wrh, wrl, rb, tri, wsgu, wsd, counts_in]
    in_specs = [pl.BlockSpec((tm, D_MODEL), src), pl.BlockSpec((tm, D_CONV), src), pl.BlockSpec((tm, D_ATTN), src)]
    in_specs += [pl.BlockSpec(a.shape, full) for a in ins[3:]]
    return pl.pallas_call(
        functools.partial(_mix_route_kernel, tm=tm),
        grid=(steps,),
        in_specs=in_specs,
        out_specs=[pl.BlockSpec((tm, D_MODEL), row), pl.BlockSpec((tm, D_MODEL // 2), row),
                   pl.BlockSpec((TOP_K, tm), col), pl.BlockSpec((TOP_K, tm), col), pl.BlockSpec((TOP_K, tm), col),
                   pl.BlockSpec((N_EXPERTS, 128), full)],
        out_shape=[jax.ShapeDtypeStruct((n, D_MODEL), F32), jax.ShapeDtypeStruct((n, D_MODEL // 2), U32),
                   jax.ShapeDtypeStruct((TOP_K, n), I32), jax.ShapeDtypeStruct((TOP_K, n), I32),
                   jax.ShapeDtypeStruct((TOP_K, n), F32), jax.ShapeDtypeStruct((N_EXPERTS, 128), F32)],
        scratch_shapes=[pltpu.VMEM((N_EXPERTS, 128), F32)],
        compiler_params=_cparams("arbitrary"),
        name="mix_route",
    )(*ins)


def _sc_workers():
    info = plsc.get_sparse_core_info()
    return info.num_cores, info.num_subcores


def _sc_mesh():
    return plsc.VectorSubcoreMesh(core_axis_name="c", subcore_axis_name="s")


def _worker_id(n_cores):
    return lax.axis_index("s") * n_cores + lax.axis_index("c")


def _dispatch(slots3, hp, xs_ref):
    n_chunks = slots3.shape[0]
    n_cores, n_sub = _sc_workers()
    per_worker = n_chunks // (n_cores * n_sub)
    assert per_worker * n_cores * n_sub == n_chunks

    @functools.partial(
        pl.kernel, mesh=_sc_mesh(), out_type=(),
        scratch_types=[pltpu.VMEM((TOP_K, SC_CHUNK), I32), pltpu.VMEM((SC_CHUNK, D_MODEL // 2), U32),
                       pltpu.SemaphoreType.DMA],
        name="dispatch")
    def run(slots_hbm, hp_hbm, xs_hbm, idx_v, rows_v, sem):
        first = _worker_id(n_cores) * per_worker

        @pl.loop(0, per_worker)
        def _(j):
            c = first + j
            pltpu.sync_copy(slots_hbm.at[c], idx_v)
            pltpu.sync_copy(hp_hbm.at[pl.ds(c * SC_CHUNK, SC_CHUNK)], rows_v)
            copies = [pltpu.async_copy(rows_v, xs_hbm.at[idx_v.at[k]], sem) for k in range(TOP_K)]
            for cp in copies:
                cp.wait()

    run(slots3, hp, xs_ref)


def _experts_kernel(bexp_ref, bidx_ref, bval_ref, nused_ref, next_ref, wslot_ref,
                    xs_ref, wg_ref, wu_ref, wd_ref, y_ref, xbuf, wg_buf, wu_buf, wd_buf, wgu_s, wd_s, xsem, wsem):
    b = pl.program_id(0)
    n_used = nused_ref[0]

    def block_copy(blk):
        slot = blk % XS_SLOTS
        rows = pl.ds(pl.multiple_of(bidx_ref[blk] * SLOT_BLK, SLOT_BLK), SLOT_BLK)
        return pltpu.make_async_copy(xs_ref.at[rows], xbuf.at[slot], xsem.at[slot])

    @pl.when(b == 0)
    def _():
        for blk in range(XS_AHEAD):
            @pl.when(blk < n_used)
            def _():
                block_copy(blk).start()

    @pl.when(b + XS_AHEAD < n_used)
    def _():
        block_copy(b + XS_AHEAD).start()

    def weight_copies(e, slot):
        return [pltpu.make_async_copy(wg_ref.at[e], wg_buf.at[slot], wsem.at[slot]),
                pltpu.make_async_copy(wu_ref.at[e], wu_buf.at[slot], wsem.at[slot]),
                pltpu.make_async_copy(wd_ref.at[e], wd_buf.at[slot], wsem.at[slot])]

    @pl.when(b < n_used)
    def _():
        e = bexp_ref[b]
        slot = wslot_ref[e]

        @pl.when(b == 0)
        def _():
            for cp in weight_copies(e, slot):
                cp.start()

        @pl.when((b == 0) | (bexp_ref[jnp.maximum(b - 1, 0)] != e))
        def _():
            for cp in weight_copies(e, slot):
                cp.wait()
            wgu_s[:, :D_EXPERT] = wg_buf[slot].astype(BF16)
            wgu_s[:, D_EXPERT:] = wu_buf[slot].astype(BF16)
            wd_s[...] = wd_buf[slot].astype(BF16)
            nxt = next_ref[e]

            @pl.when(nxt >= 0)
            def _():
                for cp in weight_copies(nxt, 1 - slot):
                    cp.start()

        srow = lax.broadcasted_iota(I32, (SLOT_BLK, 1), 0)
        block_copy(b).wait()
        xw = jnp.where(srow < bval_ref[b], xbuf[b % XS_SLOTS], jnp.uint32(0))
        xa, xb = _unpack_bf16_pair(xw)
        x = jnp.concatenate([xa.astype(BF16), xb.astype(BF16)], axis=1)
        gu = jnp.dot(x, wgu_s[...], preferred_element_type=F32)
        mid = (jax.nn.silu(gu[:, :D_EXPERT]) * gu[:, D_EXPERT:]).astype(BF16)
        y = jnp.dot(mid, wd_s[...], preferred_element_type=F32)
        y_ref[...] = _pack_bf16_pair(y[:, :D_MODEL // 2], y[:, D_MODEL // 2:])


def _experts(bexp, bidx, bval, nused, next_exp, wslot, xs, wg, wu, wd):
    n_blocks = bexp.shape[0]
    xmap = lambda b, be, bi, bv, nu, nx, ws: (bi[b], 0)
    grid_spec = pltpu.PrefetchScalarGridSpec(
        num_scalar_prefetch=6,
        grid=(n_blocks,),
        in_specs=[pl.BlockSpec(memory_space=pl.ANY),
                  pl.BlockSpec(memory_space=pl.ANY), pl.BlockSpec(memory_space=pl.ANY),
                  pl.BlockSpec(memory_space=pl.ANY)],
        out_specs=pl.BlockSpec((SLOT_BLK, D_MODEL // 2), xmap),
        scratch_shapes=[pltpu.VMEM((XS_SLOTS, SLOT_BLK, D_MODEL // 2), U32),
                        pltpu.VMEM((2, D_MODEL, D_EXPERT), F32), pltpu.VMEM((2, D_MODEL, D_EXPERT), F32),
                        pltpu.VMEM((2, D_EXPERT, D_MODEL), F32),
                        pltpu.VMEM((D_MODEL, 2 * D_EXPERT), BF16), pltpu.VMEM((D_EXPERT, D_MODEL), BF16),
                        pltpu.SemaphoreType.DMA((XS_SLOTS,)), pltpu.SemaphoreType.DMA((2,))],
    )
    return pl.pallas_call(
        _experts_kernel,
        grid_spec=grid_spec,
        out_shape=jax.ShapeDtypeStruct(xs.shape, U32),
        compiler_params=_cparams("arbitrary"),
        name="experts",
    )(bexp, bidx, bval, nused, next_exp, wslot, xs, wg, wu, wd)


def _gather(slots3, y, n):
    n_chunks = slots3.shape[0]
    n_cores, n_sub = _sc_workers()
    per_worker = n_chunks // (n_cores * n_sub)
    assert per_worker * n_cores * n_sub == n_chunks

    @functools.partial(
        pl.kernel, mesh=_sc_mesh(),
        out_type=jax.ShapeDtypeStruct((TOP_K, n, D_MODEL // 2), U32),
        scratch_types=[pltpu.VMEM((TOP_K, SC_CHUNK), I32),
                       pltpu.VMEM((SC_CHUNK, D_MODEL // 2), U32), pltpu.VMEM((SC_CHUNK, D_MODEL // 2), U32),
                       pltpu.SemaphoreType.DMA, pltpu.SemaphoreType.DMA, pltpu.SemaphoreType.DMA],
        name="gather")
    def run(slots_hbm, y_hbm, g_hbm, idx_v, buf0, buf1, gsem, wsem0, wsem1):
        first = _worker_id(n_cores) * per_worker
        bufs = (buf0, buf1)
        wsems = (wsem0, wsem1)

        @pl.loop(0, per_worker)
        def _(j):
            c = first + j
            pltpu.sync_copy(slots_hbm.at[c], idx_v)
            gathers = [None] * TOP_K
            writes = [None] * TOP_K
            gathers[0] = pltpu.async_copy(y_hbm.at[idx_v.at[0]], bufs[0], gsem)
            for k in range(TOP_K):
                gathers[k].wait()
                if k >= 1:
                    writes[k - 1].wait()
                if k + 1 < TOP_K:
                    gathers[k + 1] = pltpu.async_copy(y_hbm.at[idx_v.at[k + 1]], bufs[(k + 1) % 2], gsem)
                writes[k] = pltpu.async_copy(bufs[k % 2], g_hbm.at[k, pl.ds(c * SC_CHUNK, SC_CHUNK)],
                                             wsems[k % 2])
            writes[TOP_K - 1].wait()

    return run(slots3, y)


def _combine_kernel(wt_ref, xo_ref, g_ref, *rest):
    o_ref = rest[-1]
    half = D_MODEL // 2
    acc_a = xo_ref[:, :half]
    acc_b = xo_ref[:, half:]
    w_all = wt_ref[...].T
    for k in range(TOP_K):
        ya, yb = _unpack_bf16_pair(g_ref[k])
        w = w_all[:, k:k + 1]
        acc_a = acc_a + w * ya
        acc_b = acc_b + w * yb
    o_ref[:, :half] = acc_a
    o_ref[:, half:] = acc_b


def _combine(wt, xo, g, part, n, out_so_far, tm=256):
    steps = g.shape[1] // tm
    in_specs = [pl.BlockSpec((TOP_K, tm), lambda i: (0, i)), pl.BlockSpec((tm, D_MODEL), lambda i: (i, 0)),
                pl.BlockSpec((TOP_K, tm, D_MODEL // 2), lambda i: (0, i, 0))]
    args = [wt, xo, g]
    aliases = {}
    if out_so_far is not None:
        in_specs.append(pl.BlockSpec(memory_space=pl.ANY))
        args.append(out_so_far)
        aliases = {len(args) - 1: 0}
    return pl.pallas_call(
        _combine_kernel,
        grid=(steps,),
        in_specs=in_specs,
        out_specs=pl.BlockSpec((tm, D_MODEL), lambda i: (part * steps + i, 0)),
        out_shape=jax.ShapeDtypeStruct((n, D_MODEL), F32),
        input_output_aliases=aliases,
        compiler_params=_cparams("parallel"),
        name="combine",
    )(*args)


def _layer(x, norm_mix_g, w_in, q_norm_g, k_norm_g, rpb, conv_dw_w, conv_dw_b, conv_ln_g, conv_ln_b,
           w_conv_out, w_attn_out, w_o, norm_ffn_g, w_router, router_bias, w_exp_gate, w_exp_up,
           w_exp_down, w_sh_gate, w_sh_up, w_sh_down):
    n = x.shape[0]
    c_qkv = 2 * D_CONV + 3 * D_ATTN
    w_qkv = w_in[:, :c_qkv].astype(BF16)
    w_gates = w_in[:, c_qkv:].astype(BF16)
    head_of = jnp.arange(D_ATTN // 2) // HEAD_DIM
    bsum = (head_of[:, None] == head_of[None, :]).astype(BF16)
    row = lambda v: v.reshape(1, -1).astype(F32)

    u, q, k, v = _inproj(x, row(norm_mix_g), w_qkv, bsum,
                         row(jnp.tile(q_norm_g, N_HEADS)), row(jnp.tile(k_norm_g, N_HEADS)))
    uc = _conv(u, conv_dw_w.reshape(CONV_WIDTH, D_CONV), row(conv_dw_b), row(conv_ln_g), row(conv_ln_b))
    at = _attn(q, k, v, _bias_table(rpb))

    wr_t = w_router.T
    wr_hi = wr_t.astype(BF16)
    wr_lo = (wr_t - wr_hi.astype(F32)).astype(BF16)
    wsgu = jnp.concatenate([w_sh_gate, w_sh_up], axis=1).astype(BF16)
    mix_weights = (row(norm_mix_g), w_gates, w_conv_out.astype(BF16), w_attn_out.astype(BF16), w_o.astype(BF16),
                   row(norm_ffn_g), wr_hi, wr_lo, router_bias.reshape(N_EXPERTS, 1).astype(F32), wsgu,
                   w_sh_down.astype(BF16))

    cap = n
    n_part = n // TOKEN_PARTS
    xs_ref = jax.empty_ref(jax.ShapeDtypeStruct((N_EXPERTS * cap, D_MODEL // 2), U32))
    cnt = jnp.zeros((N_EXPERTS, 128), F32)
    parts = []
    for part in range(TOKEN_PARTS):
        xo_p, hp_p, idx_p, rank_p, wt_p, cnt = _mix_route(x, uc, at, *mix_weights, cnt, part)
        slots_p = (idx_p * cap + rank_p).reshape(TOP_K, n_part // SC_CHUNK, SC_CHUNK).transpose(1, 0, 2)
        _dispatch(slots_p, hp_p, xs_ref)
        parts.append((xo_p, wt_p, slots_p))
    xs = jax.freeze(xs_ref)

    counts = cnt[:, 0].astype(I32)
    blocks_of = (counts + SLOT_BLK - 1) // SLOT_BLK
    blocks_end = jnp.cumsum(blocks_of)
    n_blocks = (n * TOP_K + N_EXPERTS * (SLOT_BLK - 1)) // SLOT_BLK
    n_used = blocks_end[-1]
    bseq = jnp.minimum(jnp.arange(n_blocks, dtype=I32), n_used - 1)
    bexp = jnp.minimum(jnp.sum((blocks_end[None, :] <= bseq[:, None]).astype(I32), axis=1), N_EXPERTS - 1)
    owner = (bexp[:, None] == jnp.arange(N_EXPERTS)[None, :]).astype(I32)
    local = bseq - jnp.sum(owner * (blocks_end - blocks_of)[None, :], axis=1)
    bidx = (bexp * (cap // SLOT_BLK) + local).astype(I32)
    bval = jnp.clip(jnp.sum(owner * counts[None, :], axis=1) - local * SLOT_BLK, 0, SLOT_BLK).astype(I32)
    eids = jnp.arange(N_EXPERTS, dtype=I32)
    later = (eids[None, :] > eids[:, None]) & (counts[None, :] > 0)
    next_exp = jnp.min(jnp.where(later, eids[None, :], N_EXPERTS), axis=1)
    next_exp = jnp.where(next_exp == N_EXPERTS, -1, next_exp).astype(I32)
    wslot = ((jnp.cumsum((counts > 0).astype(I32)) - 1) & 1).astype(I32)

    y = _experts(bexp, bidx, bval, n_used.reshape(1).astype(I32), next_exp, wslot, xs,
                 w_exp_gate, w_exp_up, w_exp_down)
    out = None
    for part, (xo_p, wt_p, slots_p) in enumerate(parts):
        out = _combine(wt_p, xo_p, _gather(slots_p, y, n_part), part, n, out)
    return out


def kernel(x, norm_mix_g, w_in, q_norm_g, k_norm_g, rpb, conv_dw_w, conv_dw_b, conv_ln_g, conv_ln_b, w_conv_out, w_attn_out, w_o, norm_ffn_g, w_router, router_bias, w_exp_gate, w_exp_up, w_exp_down, w_sh_gate, w_sh_up, w_sh_down):
    b, s, d = x.shape
    assert b == 1 and d == D_MODEL and s % (2 * ATT_BLK) == 0, x.shape
    xf = x.reshape(b * s, d)
    depth = norm_mix_g.shape[0]
    for l in range(depth):
        xf = _layer(xf, norm_mix_g[l], w_in[l], q_norm_g[l], k_norm_g[l], rpb[l], conv_dw_w[l], conv_dw_b[l],
                    conv_ln_g[l], conv_ln_b[l], w_conv_out[l], w_attn_out[l], w_o[l], norm_ffn_g[l],
                    w_router[l], router_bias[l], w_exp_gate[l], w_exp_up[l], w_exp_down[l], w_sh_gate[l],
                    w_sh_up[l], w_sh_down[l])
    return xf.reshape(b, s, d)
```

```python
import functools

import jax
import jax.numpy as jnp
from jax import lax
from jax.experimental import pallas as pl
from jax.experimental.pallas import tpu as pltpu
from jax.experimental.pallas import tpu_sc as plsc

D_MODEL = 1024
GRID_W = 64
D_CONV = 512
CONV_WIDTH = 31
CONV_PAD = CONV_WIDTH // 2
N_HEADS = 8
HEAD_DIM = 64
D_ATTN = N_HEADS * HEAD_DIM
WIN_H = 8
WIN_W = 16
N_EXPERTS = 64
TOP_K = 8
N_GROUPS = 8
GROUP_SIZE = N_EXPERTS // N_GROUPS
TOPK_GROUPS = 4
D_EXPERT = 256
D_SHARED = 256
ROUTE_SCALE = 2.5
EPS = 1e-6
LOG2E = 1.4426950408889634

SUBLANES = 8
NEG = -1e30
HALO = 16
ATT_ROWS = 4
ATT_BLK = ATT_ROWS * GRID_W
SLOT_BLK = 512
MIX_SPLIT = 2
XS_AHEAD = 2
XS_SLOTS = XS_AHEAD + 1
TOKEN_PARTS = 8
SC_CHUNK = 64
VMEM_LIMIT = 56 * 1024 * 1024

F32 = jnp.float32
BF16 = jnp.bfloat16
I32 = jnp.int32
U32 = jnp.uint32


def _cparams(*sem):
    return pltpu.CompilerParams(dimension_semantics=sem, vmem_limit_bytes=VMEM_LIMIT)


def _rms(x, g):
    return x * lax.rsqrt(jnp.mean(x * x, axis=-1, keepdims=True) + EPS) * g


def _pack_bf16_pair(a, b):
    def bf16_bits(v):
        return lax.bitcast_convert_type(v.astype(BF16).astype(F32), U32)
    return bf16_bits(a) | (bf16_bits(b) >> 16)


def _unpack_bf16_pair(w):
    a = lax.bitcast_convert_type(w & jnp.uint32(0xFFFF0000), F32)
    b = lax.bitcast_convert_type(w << 16, F32)
    return a, b


def _inproj_kernel(x_ref, g_ref, w_ref, bsum_ref, qg_ref, kg_ref, u_ref, q_ref, k_ref, v_ref):
    h = _rms(x_ref[...], g_ref[...]).astype(BF16)
    ag = jnp.dot(h, w_ref[:, 0:2 * D_CONV], preferred_element_type=F32)
    u_ref[...] = ag[:, :D_CONV] * jax.nn.sigmoid(ag[:, D_CONV:])
    bsum = bsum_ref[...]

    def head_norm(z, g):
        zz = (z * z).astype(BF16)
        hw = D_ATTN // 2
        ss = jnp.concatenate([jnp.dot(zz[:, :hw], bsum, preferred_element_type=F32),
                              jnp.dot(zz[:, hw:], bsum, preferred_element_type=F32)], axis=1)
        return z * lax.rsqrt(ss * (1.0 / HEAD_DIM) + EPS) * g

    c0 = 2 * D_CONV
    q = jnp.dot(h, w_ref[:, c0:c0 + D_ATTN], preferred_element_type=F32)
    q_ref[...] = (head_norm(q, qg_ref[...]) * (HEAD_DIM ** -0.5 * LOG2E)).astype(BF16)
    k = jnp.dot(h, w_ref[:, c0 + D_ATTN:c0 + 2 * D_ATTN], preferred_element_type=F32)
    k_ref[...] = head_norm(k, kg_ref[...]).astype(BF16)
    v = jnp.dot(h, w_ref[:, c0 + 2 * D_ATTN:c0 + 3 * D_ATTN], preferred_element_type=F32)
    v_ref[...] = v.astype(BF16)


def _inproj(x, g, w, bsum, qg, kg, tm=512):
    n = x.shape[0]
    wc = w.shape[1]
    full = lambda i: (0, 0)
    row = lambda i: (i, 0)
    return pl.pallas_call(
        _inproj_kernel,
        grid=(n // tm,),
        in_specs=[pl.BlockSpec((tm, D_MODEL), row), pl.BlockSpec((1, D_MODEL), full),
                  pl.BlockSpec((D_MODEL, wc), full), pl.BlockSpec(bsum.shape, full),
                  pl.BlockSpec((1, D_ATTN), full), pl.BlockSpec((1, D_ATTN), full)],
        out_specs=[pl.BlockSpec((tm, D_CONV), row), pl.BlockSpec((tm, D_ATTN), row),
                   pl.BlockSpec((tm, D_ATTN), row), pl.BlockSpec((tm, D_ATTN), row)],
        out_shape=[jax.ShapeDtypeStruct((n, D_CONV), F32), jax.ShapeDtypeStruct((n, D_ATTN), BF16),
                   jax.ShapeDtypeStruct((n, D_ATTN), BF16), jax.ShapeDtypeStruct((n, D_ATTN), BF16)],
        compiler_params=_cparams("parallel"),
        name="inproj",
    )(x, g, w, bsum, qg, kg)


def _conv_kernel(up_ref, uc_ref, un_ref, w_ref, b_ref, lg_ref, lb_ref, o_ref, ext_ref, sh_ref, *, tc, ch):
    i = pl.program_id(0)
    last = pl.num_programs(0) - 1
    ext_ref[0:HALO, :] = jnp.where(i > 0, up_ref[...], 0.0)
    ext_ref[HALO:HALO + tc, :] = uc_ref[...]
    ext_ref[HALO + tc:2 * HALO + tc, :] = jnp.where(i < last, un_ref[...], 0.0)
    span = sh_ref.shape[1]
    for b in range(SUBLANES):
        sh_ref[b] = ext_ref[b:b + span, :]
    bias = b_ref[...]
    lg = lg_ref[...]
    lb = lb_ref[...]
    for c in range(tc // ch):
        acc = jnp.broadcast_to(bias, (ch, D_CONV))
        for j in range(CONV_WIDTH):
            shift = HALO - CONV_PAD + j
            row0 = c * ch + shift // SUBLANES * SUBLANES
            acc = acc + sh_ref[shift % SUBLANES, row0:row0 + ch, :] * w_ref[j:j + 1, :]
        mu = jnp.mean(acc, axis=-1, keepdims=True)
        d = acc - mu
        var = jnp.mean(d * d, axis=-1, keepdims=True)
        y = d * lax.rsqrt(var + EPS) * lg + lb
        o_ref[c * ch:(c + 1) * ch, :] = (y * jax.nn.sigmoid(y)).astype(BF16)


def _conv(u, w, b, lg, lb, tc=256, ch=64):
    n = u.shape[0]
    hb = tc // HALO
    nh = n // HALO
    full = lambda i: (0, 0)
    return pl.pallas_call(
        functools.partial(_conv_kernel, tc=tc, ch=ch),
        grid=(n // tc,),
        in_specs=[pl.BlockSpec((HALO, D_CONV), lambda i: (jnp.maximum(i * hb - 1, 0), 0)),
                  pl.BlockSpec((tc, D_CONV), lambda i: (i, 0)),
                  pl.BlockSpec((HALO, D_CONV), lambda i: (jnp.minimum((i + 1) * hb, nh - 1), 0)),
                  pl.BlockSpec((CONV_WIDTH, D_CONV), full), pl.BlockSpec((1, D_CONV), full),
                  pl.BlockSpec((1, D_CONV), full), pl.BlockSpec((1, D_CONV), full)],
        out_specs=pl.BlockSpec((tc, D_CONV), lambda i: (i, 0)),
        out_shape=jax.ShapeDtypeStruct((n, D_CONV), BF16),
        scratch_shapes=[pltpu.VMEM((tc + 2 * HALO, D_CONV), F32),
                        pltpu.VMEM((SUBLANES, tc + 2 * HALO - SUBLANES, D_CONV), F32)],
        compiler_params=_cparams("parallel"),
        name="conv",
    )(u, u, u, w, b, lg, lb)


def _attn_kernel(q_ref, kp_ref, kc_ref, kn_ref, vp_ref, vc_ref, vn_ref, tb_ref, o_ref, *, rows):
    i = pl.program_id(0)
    nkey = 3 * ATT_BLK
    lrow = lax.broadcasted_iota(I32, (1, nkey), 1) >> 6
    lane = lax.broadcasted_iota(I32, (1, 2 * HEAD_DIM), 1)
    masks = []
    for j in range(ATT_ROWS):
        r = i * ATT_ROWS + j
        rs = jnp.clip(r - WIN_H // 2, 0, rows - WIN_H)
        lo = rs - (i - 1) * ATT_ROWS
        masks.append(jnp.where((lrow >= lo) & (lrow < lo + WIN_H), 0.0, NEG))
    dn = (((1,), (1,)), ((), ()))

    def pair_slice(h):
        return slice(2 * HEAD_DIM * (h // 2), 2 * HEAD_DIM * (h // 2 + 1))

    def in_half(h):
        return (lane >= HEAD_DIM * (h % 2)) & (lane < HEAD_DIM * (h % 2 + 1))

    def scores(h):
        sl = pair_slice(h)
        q2 = q_ref[:, sl]
        qm = jnp.where(in_half(h), q2, jnp.zeros_like(q2))
        s = jnp.concatenate([lax.dot_general(qm, kk[:, sl], dn, preferred_element_type=F32)
                             for kk in (kp_ref, kc_ref, kn_ref)], axis=1)
        parts = []
        for j in range(ATT_ROWS):
            bias = jnp.concatenate([tb_ref[h, 2 * m - j + 3] for m in range(nkey // (2 * GRID_W))], axis=1)
            parts.append(s[j * GRID_W:(j + 1) * GRID_W, :] + bias + masks[j])
        return jnp.concatenate(parts, axis=0)

    def softmax(s):
        mx = jnp.max(s, axis=-1, keepdims=True)
        e = jnp.exp2(s - mx)
        return e.astype(BF16), jnp.sum(e, axis=-1, keepdims=True)

    def values(h, pb, den):
        sl = pair_slice(h)
        o = sum(jnp.dot(pb[:, ATT_BLK * t:ATT_BLK * (t + 1)], vv[:, sl], preferred_element_type=F32)
                for t, vv in enumerate((vp_ref, vc_ref, vn_ref)))
        return o / den

    stage_s = {h: scores(h) for h in range(2)}
    stage_p = {0: softmax(stage_s.pop(0))}
    out_pair = None
    for h in range(N_HEADS):
        if h + 2 < N_HEADS:
            stage_s[h + 2] = scores(h + 2)
        if h + 1 < N_HEADS:
            stage_p[h + 1] = softmax(stage_s.pop(h + 1))
        o = values(h, *stage_p.pop(h))
        if h % 2 == 0:
            out_pair = o
        else:
            o_ref[:, pair_slice(h)] = jnp.where(in_half(h), o, out_pair).astype(BF16)


def _attn(q, k, v, tb):
    n = q.shape[0]
    rows = n // GRID_W
    nb = n // ATT_BLK
    cur = lambda i: (i, 0)
    prev = lambda i: (jnp.maximum(i - 1, 0), 0)
    nxt = lambda i: (jnp.minimum(i + 1, nb - 1), 0)
    blk = (ATT_BLK, D_ATTN)
    return pl.pallas_call(
        functools.partial(_attn_kernel, rows=rows),
        grid=(nb,),
        in_specs=[pl.BlockSpec(blk, cur), pl.BlockSpec(blk, prev), pl.BlockSpec(blk, cur), pl.BlockSpec(blk, nxt),
                  pl.BlockSpec(blk, prev), pl.BlockSpec(blk, cur), pl.BlockSpec(blk, nxt),
                  pl.BlockSpec(tb.shape, lambda i: (0, 0, 0, 0))],
        out_specs=pl.BlockSpec(blk, cur),
        out_shape=jax.ShapeDtypeStruct((n, D_ATTN), BF16),
        compiler_params=_cparams("parallel"),
        name="attn",
    )(q, k, k, k, v, v, v, tb)


def _bias_table(rpb):
    cols = jnp.arange(GRID_W)
    start = jnp.clip(cols - WIN_W // 2, 0, GRID_W - WIN_W)
    kc = cols[None, :]
    inwin = (kc >= start[:, None]) & (kc < start[:, None] + WIN_W)
    rel_idx = kc - cols[:, None] + (WIN_W - 1)
    pick = ((rel_idx[None] == jnp.arange(2 * WIN_W - 1)[:, None, None]) & inwin[None]).astype(F32)
    rel = jnp.einsum('hdj,jck->hdck', rpb.astype(F32), pick, precision=lax.Precision.HIGHEST)
    t = jnp.where(inwin[None, None], rel * LOG2E, NEG)
    t = jnp.concatenate([t, jnp.full_like(t[:, :1], NEG)], axis=1)
    t_next = jnp.concatenate([t[:, 1:], t[:, -1:]], axis=1)
    return jnp.concatenate([t, t_next], axis=-1).astype(F32)


def _mix_route_kernel(x_ref, uc_ref, at_ref, g1_ref, wg_ref, wco_ref, wao_ref, wo_ref, g2_ref,
                      wrh_ref, wrl_ref, rb_ref, tri_ref, wsgu_ref, wsd_ref, cin_ref,
                      xo_ref, hp_ref, idx_ref, rank_ref, wt_ref, cnt_ref, carry_ref, *, tm):
    i = pl.program_id(0)

    @pl.when(i == 0)
    def _():
        carry_ref[...] = cin_ref[...]

    th = tm // MIX_SPLIT
    dn = (((1,), (1,)), ((), ()))

    def mix(p):
        rs = slice(p * th, (p + 1) * th)
        x = x_ref[rs, :]
        h = _rms(x, g1_ref[...]).astype(BF16)
        gates = jax.nn.sigmoid(jnp.dot(h, wg_ref[...], preferred_element_type=F32))
        y_conv = jnp.dot(uc_ref[rs, :], wco_ref[...], preferred_element_type=F32)
        y_attn = jnp.dot(at_ref[rs, :], wao_ref[...], preferred_element_type=F32)
        merged = gates[:, :D_MODEL] * y_conv + gates[:, D_MODEL:] * y_attn
        x1 = x + jnp.dot(merged.astype(BF16), wo_ref[...], preferred_element_type=F32)
        h2 = _rms(x1, g2_ref[...])
        hp_ref[rs, :] = _pack_bf16_pair(h2[:, :D_MODEL // 2], h2[:, D_MODEL // 2:])
        return x1, h2

    def shared_and_logits(p, x1, h2):
        rs = slice(p * th, (p + 1) * th)
        h2b = h2.astype(BF16)
        gu = jnp.dot(h2b, wsgu_ref[...], preferred_element_type=F32)
        mid = (jax.nn.silu(gu[:, :D_SHARED]) * gu[:, D_SHARED:]).astype(BF16)
        xo_ref[rs, :] = x1 + jnp.dot(mid, wsd_ref[...], preferred_element_type=F32)
        h2l = (h2 - h2b.astype(F32)).astype(BF16)
        return (lax.dot_general(wrh_ref[...], h2b, dn, preferred_element_type=F32)
                + lax.dot_general(wrl_ref[...], h2b, dn, preferred_element_type=F32)
                + lax.dot_general(wrh_ref[...], h2l, dn, preferred_element_type=F32))

    def route(p, logits, carry):
        cs = slice(p * th, (p + 1) * th)
        scores = jax.nn.sigmoid(logits)
        biased = scores + rb_ref[...]

        sub = lax.broadcasted_iota(I32, (GROUP_SIZE, th), 0).astype(F32)
        groups, gscore = [], []
        for g in range(N_GROUPS):
            bg = biased[g * GROUP_SIZE:(g + 1) * GROUP_SIZE, :]
            m1 = jnp.max(bg, axis=0, keepdims=True)
            first = jnp.min(jnp.where(bg == m1, sub, float(GROUP_SIZE)), axis=0, keepdims=True)
            m2 = jnp.max(jnp.where(sub == first, -jnp.inf, bg), axis=0, keepdims=True)
            groups.append(bg)
            gscore.append(m1 + m2)
        masked = []
        for g in range(N_GROUPS):
            beaten = jnp.zeros((1, th), F32)
            for o in range(N_GROUPS):
                if o == g:
                    continue
                wins = (gscore[o] >= gscore[g]) if o < g else (gscore[o] > gscore[g])
                beaten = beaten + jnp.where(wins, 1.0, 0.0)
            masked.append(jnp.where(beaten < TOPK_GROUPS, groups[g], -jnp.inf))
        cur = jnp.concatenate(masked, axis=0)

        eid = lax.broadcasted_iota(I32, (N_EXPERTS, th), 0).astype(F32)
        idx_rows, sc_rows = [], []
        sel = jnp.zeros((N_EXPERTS, th), F32)
        for _ in range(TOP_K):
            m = jnp.max(cur, axis=0, keepdims=True)
            idx = jnp.min(jnp.where(cur == m, eid, float(N_EXPERTS)), axis=0, keepdims=True)
            hit = eid == idx
            sel = jnp.where(hit, 1.0, sel)
            cur = jnp.where(hit, -jnp.inf, cur)
            idx_rows.append(idx)
            sc_rows.append(jnp.sum(jnp.where(hit, scores, 0.0), axis=0, keepdims=True))
        wsum = sc_rows[0]
        for k in range(1, TOP_K):
            wsum = wsum + sc_rows[k]

        prefix = jnp.dot(sel.astype(BF16), tri_ref[...], preferred_element_type=F32)
        rank = carry + prefix
        rank_rows = [jnp.sum(jnp.where(eid == idx_rows[k], rank, 0.0), axis=0, keepdims=True)
                     for k in range(TOP_K)]
        idx_ref[:, cs] = jnp.concatenate(idx_rows, axis=0).astype(I32)
        rank_ref[:, cs] = jnp.concatenate(rank_rows, axis=0).astype(I32)
        wt_ref[:, cs] = jnp.concatenate([s / wsum * ROUTE_SCALE for s in sc_rows], axis=0)
        return carry + jnp.sum(sel, axis=1, keepdims=True)

    carry = carry_ref[:, 0:1]
    logits = shared_and_logits(0, *mix(0))
    for p in range(MIX_SPLIT):
        if p + 1 < MIX_SPLIT:
            mixed_next = mix(p + 1)
        carry = route(p, logits, carry)
        if p + 1 < MIX_SPLIT:
            logits = shared_and_logits(p + 1, *mixed_next)
    carry_ref[...] = jnp.broadcast_to(carry, carry_ref.shape)
    cnt_ref[...] = carry_ref[...]


def _mix_route(x, uc, at, g1, wg, wco, wao, wo, g2, wrh, wrl, rb, wsgu, wsd, counts_in, part, tm=1024):
    n = x.shape[0] // TOKEN_PARTS
    steps = n // tm
    th = tm // MIX_SPLIT
    tri = (jnp.arange(th)[:, None] < jnp.arange(th)[None, :]).astype(BF16)
    src = lambda i: (part * steps + i, 0)
    row = lambda i: (i, 0)
    col = lambda i: (0, i)
    full = lambda i: (0, 0)
    ins = [x, uc, at, g1, wg, wco, wao, wo, g2, wrh, wrl, rb, tri, wsgu, wsd, counts_in]
    in_specs = [pl.BlockSpec((tm, D_MODEL), src), pl.BlockSpec((tm, D_CONV), src), pl.BlockSpec((tm, D_ATTN), src)]
    in_specs += [pl.BlockSpec(a.shape, full) for a in ins[3:]]
    return pl.pallas_call(
        functools.partial(_mix_route_kernel, tm=tm),
        grid=(steps,),
        in_specs=in_specs,
        out_specs=[pl.BlockSpec((tm, D_MODEL), row), pl.BlockSpec((tm, D_MODEL // 2), row),
                   pl.BlockSpec((TOP_K, tm), col), pl.BlockSpec((TOP_K, tm), col), pl.BlockSpec((TOP_K, tm), col),
                   pl.BlockSpec((N_EXPERTS, 128), full)],
        out_shape=[jax.ShapeDtypeStruct((n, D_MODEL), F32), jax.ShapeDtypeStruct((n, D_MODEL // 2), U32),
                   jax.ShapeDtypeStruct((TOP_K, n), I32), jax.ShapeDtypeStruct((TOP_K, n), I32),
                   jax.ShapeDtypeStruct((TOP_K, n), F32), jax.ShapeDtypeStruct((N_EXPERTS, 128), F32)],
        scratch_shapes=[pltpu.VMEM((N_EXPERTS, 128), F32)],
        compiler_params=_cparams("arbitrary"),
        name="mix_route",
    )(*ins)


def _sc_workers():
    info = plsc.get_sparse_core_info()
    return info.num_cores, info.num_subcores


def _sc_mesh():
    return plsc.VectorSubcoreMesh(core_axis_name="c", subcore_axis_name="s")


def _worker_id(n_cores):
    return lax.axis_index("s") * n_cores + lax.axis_index("c")


def _dispatch(slots3, hp, xs_ref):
    n_chunks = slots3.shape[0]
    n_cores, n_sub = _sc_workers()
    per_worker = n_chunks // (n_cores * n_sub)
    assert per_worker * n_cores * n_sub == n_chunks

    @functools.partial(
        pl.kernel, mesh=_sc_mesh(), out_type=(),
        scratch_types=[pltpu.VMEM((TOP_K, SC_CHUNK), I32), pltpu.VMEM((SC_CHUNK, D_MODEL // 2), U32),
                       pltpu.SemaphoreType.DMA],
        name="dispatch")
    def run(slots_hbm, hp_hbm, xs_hbm, idx_v, rows_v, sem):
        first = _worker_id(n_cores) * per_worker

        @pl.loop(0, per_worker)
        def _(j):
            c = first + j
            pltpu.sync_copy(slots_hbm.at[c], idx_v)
            pltpu.sync_copy(hp_hbm.at[pl.ds(c * SC_CHUNK, SC_CHUNK)], rows_v)
            copies = [pltpu.async_copy(rows_v, xs_hbm.at[idx_v.at[k]], sem) for k in range(TOP_K)]
            for cp in copies:
                cp.wait()

    run(slots3, hp, xs_ref)


def _experts_kernel(bexp_ref, bidx_ref, bval_ref, nused_ref, next_ref, wslot_ref,
                    xs_ref, wg_ref, wu_ref, wd_ref, y_ref, xbuf, wg_buf, wu_buf, wd_buf, wgu_s, wd_s, xsem, wsem):
    b = pl.program_id(0)
    n_used = nused_ref[0]

    def block_copy(blk):
        slot = blk % XS_SLOTS
        rows = pl.ds(pl.multiple_of(bidx_ref[blk] * SLOT_BLK, SLOT_BLK), SLOT_BLK)
        return pltpu.make_async_copy(xs_ref.at[rows], xbuf.at[slot], xsem.at[slot])

    @pl.when(b == 0)
    def _():
        for blk in range(XS_AHEAD):
            @pl.when(blk < n_used)
            def _():
                block_copy(blk).start()

    @pl.when(b + XS_AHEAD < n_used)
    def _():
        block_copy(b + XS_AHEAD).start()

    def weight_copies(e, slot):
        return [pltpu.make_async_copy(wg_ref.at[e], wg_buf.at[slot], wsem.at[slot]),
                pltpu.make_async_copy(wu_ref.at[e], wu_buf.at[slot], wsem.at[slot]),
                pltpu.make_async_copy(wd_ref.at[e], wd_buf.at[slot], wsem.at[slot])]

    @pl.when(b < n_used)
    def _():
        e = bexp_ref[b]
        slot = wslot_ref[e]

        @pl.when(b == 0)
        def _():
            for cp in weight_copies(e, slot):
                cp.start()

        @pl.when((b == 0) | (bexp_ref[jnp.maximum(b - 1, 0)] != e))
        def _():
            for cp in weight_copies(e, slot):
                cp.wait()
            wgu_s[:, :D_EXPERT] = wg_buf[slot].astype(BF16)
            wgu_s[:, D_EXPERT:] = wu_buf[slot].astype(BF16)
            wd_s[...] = wd_buf[slot].astype(BF16)
            nxt = next_ref[e]

            @pl.when(nxt >= 0)
            def _():
                for cp in weight_copies(nxt, 1 - slot):
                    cp.start()

        srow = lax.broadcasted_iota(I32, (SLOT_BLK, 1), 0)
        block_copy(b).wait()
        xw = jnp.where(srow < bval_ref[b], xbuf[b % XS_SLOTS], jnp.uint32(0))
        xa, xb = _unpack_bf16_pair(xw)
        x = jnp.concatenate([xa.astype(BF16), xb.astype(BF16)], axis=1)
        gu = jnp.dot(x, wgu_s[...], preferred_element_type=F32)
        mid = (jax.nn.silu(gu[:, :D_EXPERT]) * gu[:, D_EXPERT:]).astype(BF16)
        y = jnp.dot(mid, wd_s[...], preferred_element_type=F32)
        y_ref[...] = _pack_bf16_pair(y[:, :D_MODEL // 2], y[:, D_MODEL // 2:])


def _experts(bexp, bidx, bval, nused, next_exp, wslot, xs, wg, wu, wd):
    n_blocks = bexp.shape[0]
    xmap = lambda b, be, bi, bv, nu, nx, ws: (bi[b], 0)
    grid_spec = pltpu.PrefetchScalarGridSpec(
        num_scalar_prefetch=6,
        grid=(n_blocks,),
        in_specs=[pl.BlockSpec(memory_space=pl.ANY),
                  pl.BlockSpec(memory_space=pl.ANY), pl.BlockSpec(memory_space=pl.ANY),
                  pl.BlockSpec(memory_space=pl.ANY)],
        out_specs=pl.BlockSpec((SLOT_BLK, D_MODEL // 2), xmap),
        scratch_shapes=[pltpu.VMEM((XS_SLOTS, SLOT_BLK, D_MODEL // 2), U32),
                        pltpu.VMEM((2, D_MODEL, D_EXPERT), F32), pltpu.VMEM((2, D_MODEL, D_EXPERT), F32),
                        pltpu.VMEM((2, D_EXPERT, D_MODEL), F32),
                        pltpu.VMEM((D_MODEL, 2 * D_EXPERT), BF16), pltpu.VMEM((D_EXPERT, D_MODEL), BF16),
                        pltpu.SemaphoreType.DMA((XS_SLOTS,)), pltpu.SemaphoreType.DMA((2,))],
    )
    return pl.pallas_call(
        _experts_kernel,
        grid_spec=grid_spec,
        out_shape=jax.ShapeDtypeStruct(xs.shape, U32),
        compiler_params=_cparams("arbitrary"),
        name="experts",
    )(bexp, bidx, bval, nused, next_exp, wslot, xs, wg, wu, wd)


def _gather(slots3, y, n):
    n_chunks = slots3.shape[0]
    n_cores, n_sub = _sc_workers()
    per_worker = n_chunks // (n_cores * n_sub)
    assert per_worker * n_cores * n_sub == n_chunks

    @functools.partial(
        pl.kernel, mesh=_sc_mesh(),
        out_type=jax.ShapeDtypeStruct((TOP_K, n, D_MODEL // 2), U32),
        scratch_types=[pltpu.VMEM((TOP_K, SC_CHUNK), I32),
                       pltpu.VMEM((SC_CHUNK, D_MODEL // 2), U32), pltpu.VMEM((SC_CHUNK, D_MODEL // 2), U32),
                       pltpu.SemaphoreType.DMA, pltpu.SemaphoreType.DMA, pltpu.SemaphoreType.DMA],
        name="gather")
    def run(slots_hbm, y_hbm, g_hbm, idx_v, buf0, buf1, gsem, wsem0, wsem1):
        first = _worker_id(n_cores) * per_worker
        bufs = (buf0, buf1)
        wsems = (wsem0, wsem1)

        @pl.loop(0, per_worker)
        def _(j):
            c = first + j
            pltpu.sync_copy(slots_hbm.at[c], idx_v)
            gathers = [None] * TOP_K
            writes = [None] * TOP_K
            gathers[0] = pltpu.async_copy(y_hbm.at[idx_v.at[0]], bufs[0], gsem)
            for k in range(TOP_K):
                gathers[k].wait()
                if k >= 1:
                    writes[k - 1].wait()
                if k + 1 < TOP_K:
                    gathers[k + 1] = pltpu.async_copy(y_hbm.at[idx_v.at[k + 1]], bufs[(k + 1) % 2], gsem)
                writes[k] = pltpu.async_copy(bufs[k % 2], g_hbm.at[k, pl.ds(c * SC_CHUNK, SC_CHUNK)],
                                             wsems[k % 2])
            writes[TOP_K - 1].wait()

    return run(slots3, y)


def _combine_kernel(wt_ref, xo_ref, g_ref, *rest):
    o_ref = rest[-1]
    half = D_MODEL // 2
    acc_a = xo_ref[:, :half]
    acc_b = xo_ref[:, half:]
    w_all = wt_ref[...].T
    for k in range(TOP_K):
        ya, yb = _unpack_bf16_pair(g_ref[k])
        w = w_all[:, k:k + 1]
        acc_a = acc_a + w * ya
        acc_b = acc_b + w * yb
    o_ref[:, :half] = acc_a
    o_ref[:, half:] = acc_b


def _combine(wt, xo, g, part, n, out_so_far, tm=256):
    steps = g.shape[1] // tm
    in_specs = [pl.BlockSpec((TOP_K, tm), lambda i: (0, i)), pl.BlockSpec((tm, D_MODEL), lambda i: (i, 0)),
                pl.BlockSpec((TOP_K, tm, D_MODEL // 2), lambda i: (0, i, 0))]
    args = [wt, xo, g]
    aliases = {}
    if out_so_far is not None:
        in_specs.append(pl.BlockSpec(memory_space=pl.ANY))
        args.append(out_so_far)
        aliases = {len(args) - 1: 0}
    return pl.pallas_call(
        _combine_kernel,
        grid=(steps,),
        in_specs=in_specs,
        out_specs=pl.BlockSpec((tm, D_MODEL), lambda i: (part * steps + i, 0)),
        out_shape=jax.ShapeDtypeStruct((n, D_MODEL), F32),
        input_output_aliases=aliases,
        compiler_params=_cparams("parallel"),
        name="combine",
    )(*args)


def _layer(x, norm_mix_g, w_in, q_norm_g, k_norm_g, rpb, conv_dw_w, conv_dw_b, conv_ln_g, conv_ln_b,
           w_conv_out, w_attn_out, w_o, norm_ffn_g, w_router, router_bias, w_exp_gate, w_exp_up,
           w_exp_down, w_sh_gate, w_sh_up, w_sh_down):
    n = x.shape[0]
    c_qkv = 2 * D_CONV + 3 * D_ATTN
    w_qkv = w_in[:, :c_qkv].astype(BF16)
    w_gates = w_in[:, c_qkv:].astype(BF16)
    head_of = jnp.arange(D_ATTN // 2) // HEAD_DIM
    bsum = (head_of[:, None] == head_of[None, :]).astype(BF16)
    row = lambda v: v.reshape(1, -1).astype(F32)

    u, q, k, v = _inproj(x, row(norm_mix_g), w_qkv, bsum,
                         row(jnp.tile(q_norm_g, N_HEADS)), row(jnp.tile(k_norm_g, N_HEADS)))
    uc = _conv(u, conv_dw_w.reshape(CONV_WIDTH, D_CONV), row(conv_dw_b), row(conv_ln_g), row(conv_ln_b))
    at = _attn(q, k, v, _bias_table(rpb))

    wr_t = w_router.T
    wr_hi = wr_t.astype(BF16)
    wr_lo = (wr_t - wr_hi.astype(F32)).astype(BF16)
    wsgu = jnp.concatenate([w_sh_gate, w_sh_up], axis=1).astype(BF16)
    mix_weights = (row(norm_mix_g), w_gates, w_conv_out.astype(BF16), w_attn_out.astype(BF16), w_o.astype(BF16),
                   row(norm_ffn_g), wr_hi, wr_lo, router_bias.reshape(N_EXPERTS, 1).astype(F32), wsgu,
                   w_sh_down.astype(BF16))

    cap = n
    n_part = n // TOKEN_PARTS
    xs_ref = jax.empty_ref(jax.ShapeDtypeStruct((N_EXPERTS * cap, D_MODEL // 2), U32))
    cnt = jnp.zeros((N_EXPERTS, 128), F32)
    parts = []
    for part in range(TOKEN_PARTS):
        xo_p, hp_p, idx_p, rank_p, wt_p, cnt = _mix_route(x, uc, at, *mix_weights, cnt, part)
        slots_p = (idx_p * cap + rank_p).reshape(TOP_K, n_part // SC_CHUNK, SC_CHUNK).transpose(1, 0, 2)
        _dispatch(slots_p, hp_p, xs_ref)
        parts.append((xo_p, wt_p, slots_p))
    xs = jax.freeze(xs_ref)

    counts = cnt[:, 0].astype(I32)
    blocks_of = (counts + SLOT_BLK - 1) // SLOT_BLK
    blocks_end = jnp.cumsum(blocks_of)
    n_blocks = (n * TOP_K + N_EXPERTS * (SLOT_BLK - 1)) // SLOT_BLK
    n_used = blocks_end[-1]
    bseq = jnp.minimum(jnp.arange(n_blocks, dtype=I32), n_used - 1)
    bexp = jnp.minimum(jnp.sum((blocks_end[None, :] <= bseq[:, None]).astype(I32), axis=1), N_EXPERTS - 1)
    owner = (bexp[:, None] == jnp.arange(N_EXPERTS)[None, :]).astype(I32)
    local = bseq - jnp.sum(owner * (blocks_end - blocks_of)[None, :], axis=1)
    bidx = (bexp * (cap // SLOT_BLK) + local).astype(I32)
    bval = jnp.clip(jnp.sum(owner * counts[None, :], axis=1) - local * SLOT_BLK, 0, SLOT_BLK).astype(I32)
    eids = jnp.arange(N_EXPERTS, dtype=I32)
    later = (eids[None, :] > eids[:, None]) & (counts[None, :] > 0)
    next_exp = jnp.min(jnp.where(later, eids[None, :], N_EXPERTS), axis=1)
    next_exp = jnp.where(next_exp == N_EXPERTS, -1, next_exp).astype(I32)
    wslot = ((jnp.cumsum((counts > 0).astype(I32)) - 1) & 1).astype(I32)

    y = _experts(bexp, bidx, bval, n_used.reshape(1).astype(I32), next_exp, wslot, xs,
                 w_exp_gate, w_exp_up, w_exp_down)
    out = None
    for part, (xo_p, wt_p, slots_p) in enumerate(parts):
        out = _combine(wt_p, xo_p, _gather(slots_p, y, n_part), part, n, out)
    return out


def kernel(x, norm_mix_g, w_in, q_norm_g, k_norm_g, rpb, conv_dw_w, conv_dw_b, conv_ln_g, conv_ln_b, w_conv_out, w_attn_out, w_o, norm_ffn_g, w_router, router_bias, w_exp_gate, w_exp_up, w_exp_down, w_sh_gate, w_sh_up, w_sh_down):
    b, s, d = x.shape
    assert b == 1 and d == D_MODEL and s % (2 * ATT_BLK) == 0, x.shape
    xf = x.reshape(b * s, d)
    depth = norm_mix_g.shape[0]
    for l in range(depth):
        xf = _layer(xf, norm_mix_g[l], w_in[l], q_norm_g[l], k_norm_g[l], rpb[l], conv_dw_w[l], conv_dw_b[l],
                    conv_ln_g[l], conv_ln_b[l], w_conv_out[l], w_attn_out[l], w_o[l], norm_ffn_g[l],
                    w_router[l], router_bias[l], w_exp_gate[l], w_exp_up[l], w_exp_down[l], w_sh_gate[l],
                    w_sh_up[l], w_sh_down[l])
    return xf.reshape(b, s, d)
```

```python
import functools

import jax
import jax.numpy as jnp
from jax import lax
from jax.experimental import pallas as pl
from jax.experimental.pallas import tpu as pltpu
from jax.experimental.pallas import tpu_sc as plsc

D_MODEL = 1024
GRID_W = 64
D_CONV = 512
CONV_WIDTH = 31
CONV_PAD = CONV_WIDTH // 2
N_HEADS = 8
HEAD_DIM = 64
D_ATTN = N_HEADS * HEAD_DIM
WIN_H = 8
WIN_W = 16
N_EXPERTS = 64
TOP_K = 8
N_GROUPS = 8
GROUP_SIZE = N_EXPERTS // N_GROUPS
TOPK_GROUPS = 4
D_EXPERT = 256
D_SHARED = 256
ROUTE_SCALE = 2.5
EPS = 1e-6
LOG2E = 1.4426950408889634

SUBLANES = 8
NEG = -1e30
HALO = 16
ATT_ROWS = 4
ATT_BLK = ATT_ROWS * GRID_W
SLOT_BLK = 512
MIX_SPLIT = 2
XS_AHEAD = 2
XS_SLOTS = XS_AHEAD + 1
TOKEN_PARTS = 4
GATHER_SPLIT = 2
SC_CHUNK = 64
VMEM_LIMIT = 56 * 1024 * 1024

F32 = jnp.float32
BF16 = jnp.bfloat16
I32 = jnp.int32
U32 = jnp.uint32


def _cparams(*sem):
    return pltpu.CompilerParams(dimension_semantics=sem, vmem_limit_bytes=VMEM_LIMIT)


def _rms(x, g):
    return x * lax.rsqrt(jnp.mean(x * x, axis=-1, keepdims=True) + EPS) * g


def _pack_bf16_pair(a, b):
    def bf16_bits(v):
        return lax.bitcast_convert_type(v.astype(BF16).astype(F32), U32)
    return bf16_bits(a) | (bf16_bits(b) >> 16)


def _unpack_bf16_pair(w):
    a = lax.bitcast_convert_type(w & jnp.uint32(0xFFFF0000), F32)
    b = lax.bitcast_convert_type(w << 16, F32)
    return a, b


def _inproj_kernel(x_ref, g_ref, w_ref, bsum_ref, qg_ref, kg_ref, u_ref, q_ref, k_ref, v_ref):
    h = _rms(x_ref[...], g_ref[...]).astype(BF16)
    ag = jnp.dot(h, w_ref[:, 0:2 * D_CONV], preferred_element_type=F32)
    u_ref[...] = ag[:, :D_CONV] * jax.nn.sigmoid(ag[:, D_CONV:])
    bsum = bsum_ref[...]

    def head_norm(z, g):
        zz = (z * z).astype(BF16)
        hw = D_ATTN // 2
        ss = jnp.concatenate([jnp.dot(zz[:, :hw], bsum, preferred_element_type=F32),
                              jnp.dot(zz[:, hw:], bsum, preferred_element_type=F32)], axis=1)
        return z * lax.rsqrt(ss * (1.0 / HEAD_DIM) + EPS) * g

    c0 = 2 * D_CONV
    q = jnp.dot(h, w_ref[:, c0:c0 + D_ATTN], preferred_element_type=F32)
    q_ref[...] = (head_norm(q, qg_ref[...]) * (HEAD_DIM ** -0.5 * LOG2E)).astype(BF16)
    k = jnp.dot(h, w_ref[:, c0 + D_ATTN:c0 + 2 * D_ATTN], preferred_element_type=F32)
    k_ref[...] = head_norm(k, kg_ref[...]).astype(BF16)
    v = jnp.dot(h, w_ref[:, c0 + 2 * D_ATTN:c0 + 3 * D_ATTN], preferred_element_type=F32)
    v_ref[...] = v.astype(BF16)


def _inproj(x, g, w, bsum, qg, kg, tm=512):
    n = x.shape[0]
    wc = w.shape[1]
    full = lambda i: (0, 0)
    row = lambda i: (i, 0)
    return pl.pallas_call(
        _inproj_kernel,
        grid=(n // tm,),
        in_specs=[pl.BlockSpec((tm, D_MODEL), row), pl.BlockSpec((1, D_MODEL), full),
                  pl.BlockSpec((D_MODEL, wc), full), pl.BlockSpec(bsum.shape, full),
                  pl.BlockSpec((1, D_ATTN), full), pl.BlockSpec((1, D_ATTN), full)],
        out_specs=[pl.BlockSpec((tm, D_CONV), row), pl.BlockSpec((tm, D_ATTN), row),
                   pl.BlockSpec((tm, D_ATTN), row), pl.BlockSpec((tm, D_ATTN), row)],
        out_shape=[jax.ShapeDtypeStruct((n, D_CONV), F32), jax.ShapeDtypeStruct((n, D_ATTN), BF16),
                   jax.ShapeDtypeStruct((n, D_ATTN), BF16), jax.ShapeDtypeStruct((n, D_ATTN), BF16)],
        compiler_params=_cparams("parallel"),
        name="inproj",
    )(x, g, w, bsum, qg, kg)


def _conv_kernel(up_ref, uc_ref, un_ref, w_ref, b_ref, lg_ref, lb_ref, o_ref, ext_ref, sh_ref, *, tc, ch):
    i = pl.program_id(0)
    last = pl.num_programs(0) - 1
    ext_ref[0:HALO, :] = jnp.where(i > 0, up_ref[...], 0.0)
    ext_ref[HALO:HALO + tc, :] = uc_ref[...]
    ext_ref[HALO + tc:2 * HALO + tc, :] = jnp.where(i < last, un_ref[...], 0.0)
    span = sh_ref.shape[1]
    for b in range(SUBLANES):
        sh_ref[b] = ext_ref[b:b + span, :]
    bias = b_ref[...]
    lg = lg_ref[...]
    lb = lb_ref[...]
    for c in range(tc // ch):
        acc = jnp.broadcast_to(bias, (ch, D_CONV))
        for j in range(CONV_WIDTH):
            shift = HALO - CONV_PAD + j
            row0 = c * ch + shift // SUBLANES * SUBLANES
            acc = acc + sh_ref[shift % SUBLANES, row0:row0 + ch, :] * w_ref[j:j + 1, :]
        mu = jnp.mean(acc, axis=-1, keepdims=True)
        d = acc - mu
        var = jnp.mean(d * d, axis=-1, keepdims=True)
        y = d * lax.rsqrt(var + EPS) * lg + lb
        o_ref[c * ch:(c + 1) * ch, :] = (y * jax.nn.sigmoid(y)).astype(BF16)


def _conv(u, w, b, lg, lb, tc=256, ch=64):
    n = u.shape[0]
    hb = tc // HALO
    nh = n // HALO
    full = lambda i: (0, 0)
    return pl.pallas_call(
        functools.partial(_conv_kernel, tc=tc, ch=ch),
        grid=(n // tc,),
        in_specs=[pl.BlockSpec((HALO, D_CONV), lambda i: (jnp.maximum(i * hb - 1, 0), 0)),
                  pl.BlockSpec((tc, D_CONV), lambda i: (i, 0)),
                  pl.BlockSpec((HALO, D_CONV), lambda i: (jnp.minimum((i + 1) * hb, nh - 1), 0)),
                  pl.BlockSpec((CONV_WIDTH, D_CONV), full), pl.BlockSpec((1, D_CONV), full),
                  pl.BlockSpec((1, D_CONV), full), pl.BlockSpec((1, D_CONV), full)],
        out_specs=pl.BlockSpec((tc, D_CONV), lambda i: (i, 0)),
        out_shape=jax.ShapeDtypeStruct((n, D_CONV), BF16),
        scratch_shapes=[pltpu.VMEM((tc + 2 * HALO, D_CONV), F32),
                        pltpu.VMEM((SUBLANES, tc + 2 * HALO - SUBLANES, D_CONV), F32)],
        compiler_params=_cparams("parallel"),
        name="conv",
    )(u, u, u, w, b, lg, lb)


def _attn_kernel(q_ref, kp_ref, kc_ref, kn_ref, vp_ref, vc_ref, vn_ref, tb_ref, o_ref, *, rows):
    i = pl.program_id(0)
    nkey = 3 * ATT_BLK
    lrow = lax.broadcasted_iota(I32, (1, nkey), 1) >> 6
    lane = lax.broadcasted_iota(I32, (1, 2 * HEAD_DIM), 1)
    masks = []
    for j in range(ATT_ROWS):
        r = i * ATT_ROWS + j
        rs = jnp.clip(r - WIN_H // 2, 0, rows - WIN_H)
        lo = rs - (i - 1) * ATT_ROWS
        masks.append(jnp.where((lrow >= lo) & (lrow < lo + WIN_H), 0.0, NEG))
    dn = (((1,), (1,)), ((), ()))

    def pair_slice(h):
        return slice(2 * HEAD_DIM * (h // 2), 2 * HEAD_DIM * (h // 2 + 1))

    def in_half(h):
        return (lane >= HEAD_DIM * (h % 2)) & (lane < HEAD_DIM * (h % 2 + 1))

    def scores(h):
        sl = pair_slice(h)
        q2 = q_ref[:, sl]
        qm = jnp.where(in_half(h), q2, jnp.zeros_like(q2))
        s = jnp.concatenate([lax.dot_general(qm, kk[:, sl], dn, preferred_element_type=F32)
                             for kk in (kp_ref, kc_ref, kn_ref)], axis=1)
        parts = []
        for j in range(ATT_ROWS):
            bias = jnp.concatenate([tb_ref[h, 2 * m - j + 3] for m in range(nkey // (2 * GRID_W))], axis=1)
            parts.append(s[j * GRID_W:(j + 1) * GRID_W, :] + bias + masks[j])
        return jnp.concatenate(parts, axis=0)

    def softmax(s):
        mx = jnp.max(s, axis=-1, keepdims=True)
        e = jnp.exp2(s - mx)
        return e.astype(BF16), jnp.sum(e, axis=-1, keepdims=True)

    def values(h, pb, den):
        sl = pair_slice(h)
        o = sum(jnp.dot(pb[:, ATT_BLK * t:ATT_BLK * (t + 1)], vv[:, sl], preferred_element_type=F32)
                for t, vv in enumerate((vp_ref, vc_ref, vn_ref)))
        return o / den

    stage_s = {h: scores(h) for h in range(2)}
    stage_p = {0: softmax(stage_s.pop(0))}
    out_pair = None
    for h in range(N_HEADS):
        if h + 2 < N_HEADS:
            stage_s[h + 2] = scores(h + 2)
        if h + 1 < N_HEADS:
            stage_p[h + 1] = softmax(stage_s.pop(h + 1))
        o = values(h, *stage_p.pop(h))
        if h % 2 == 0:
            out_pair = o
        else:
            o_ref[:, pair_slice(h)] = jnp.where(in_half(h), o, out_pair).astype(BF16)


def _attn(q, k, v, tb):
    n = q.shape[0]
    rows = n // GRID_W
    nb = n // ATT_BLK
    cur = lambda i: (i, 0)
    prev = lambda i: (jnp.maximum(i - 1, 0), 0)
    nxt = lambda i: (jnp.minimum(i + 1, nb - 1), 0)
    blk = (ATT_BLK, D_ATTN)
    return pl.pallas_call(
        functools.partial(_attn_kernel, rows=rows),
        grid=(nb,),
        in_specs=[pl.BlockSpec(blk, cur), pl.BlockSpec(blk, prev), pl.BlockSpec(blk, cur), pl.BlockSpec(blk, nxt),
                  pl.BlockSpec(blk, prev), pl.BlockSpec(blk, cur), pl.BlockSpec(blk, nxt),
                  pl.BlockSpec(tb.shape, lambda i: (0, 0, 0, 0))],
        out_specs=pl.BlockSpec(blk, cur),
        out_shape=jax.ShapeDtypeStruct((n, D_ATTN), BF16),
        compiler_params=_cparams("parallel"),
        name="attn",
    )(q, k, k, k, v, v, v, tb)


def _bias_table(rpb):
    cols = jnp.arange(GRID_W)
    start = jnp.clip(cols - WIN_W // 2, 0, GRID_W - WIN_W)
    kc = cols[None, :]
    inwin = (kc >= start[:, None]) & (kc < start[:, None] + WIN_W)
    rel_idx = kc - cols[:, None] + (WIN_W - 1)
    pick = ((rel_idx[None] == jnp.arange(2 * WIN_W - 1)[:, None, None]) & inwin[None]).astype(F32)
    rel = jnp.einsum('hdj,jck->hdck', rpb.astype(F32), pick, precision=lax.Precision.HIGHEST)
    t = jnp.where(inwin[None, None], rel * LOG2E, NEG)
    t = jnp.concatenate([t, jnp.full_like(t[:, :1], NEG)], axis=1)
    t_next = jnp.concatenate([t[:, 1:], t[:, -1:]], axis=1)
    return jnp.concatenate([t, t_next], axis=-1).astype(F32)


def _mix_route_kernel(x_ref, uc_ref, at_ref, g1_ref, wg_ref, wco_ref, wao_ref, wo_ref, g2_ref,
                      wrh_ref, wrl_ref, rb_ref, tri_ref, wsgu_ref, wsd_ref, cin_ref,
                      xo_ref, hp_ref, idx_ref, rank_ref, wt_ref, cnt_ref, carry_ref, *, tm):
    i = pl.program_id(0)

    @pl.when(i == 0)
    def _():
        carry_ref[...] = cin_ref[...]

    th = tm // MIX_SPLIT
    dn = (((1,), (1,)), ((), ()))

    def mix(p):
        rs = slice(p * th, (p + 1) * th)
        x = x_ref[rs, :]
        h = _rms(x, g1_ref[...]).astype(BF16)
        gates = jax.nn.sigmoid(jnp.dot(h, wg_ref[...], preferred_element_type=F32))
        y_conv = jnp.dot(uc_ref[rs, :], wco_ref[...], preferred_element_type=F32)
        y_attn = jnp.dot(at_ref[rs, :], wao_ref[...], preferred_element_type=F32)
        merged = gates[:, :D_MODEL] * y_conv + gates[:, D_MODEL:] * y_attn
        x1 = x + jnp.dot(merged.astype(BF16), wo_ref[...], preferred_element_type=F32)
        h2 = _rms(x1, g2_ref[...])
        hp_ref[rs, :] = _pack_bf16_pair(h2[:, :D_MODEL // 2], h2[:, D_MODEL // 2:])
        return x1, h2

    def shared_and_logits(p, x1, h2):
        rs = slice(p * th, (p + 1) * th)
        h2b = h2.astype(BF16)
        gu = jnp.dot(h2b, wsgu_ref[...], preferred_element_type=F32)
        mid = (jax.nn.silu(gu[:, :D_SHARED]) * gu[:, D_SHARED:]).astype(BF16)
        xo_ref[rs, :] = x1 + jnp.dot(mid, wsd_ref[...], preferred_element_type=F32)
        h2l = (h2 - h2b.astype(F32)).astype(BF16)
        return (lax.dot_general(wrh_ref[...], h2b, dn, preferred_element_type=F32)
                + lax.dot_general(wrl_ref[...], h2b, dn, preferred_element_type=F32)
                + lax.dot_general(wrh_ref[...], h2l, dn, preferred_element_type=F32))

    def route(p, logits, carry):
        cs = slice(p * th, (p + 1) * th)
        scores = jax.nn.sigmoid(logits)
        biased = scores + rb_ref[...]

        sub = lax.broadcasted_iota(I32, (GROUP_SIZE, th), 0).astype(F32)
        groups, gscore = [], []
        for g in range(N_GROUPS):
            bg = biased[g * GROUP_SIZE:(g + 1) * GROUP_SIZE, :]
            m1 = jnp.max(bg, axis=0, keepdims=True)
            first = jnp.min(jnp.where(bg == m1, sub, float(GROUP_SIZE)), axis=0, keepdims=True)
            m2 = jnp.max(jnp.where(sub == first, -jnp.inf, bg), axis=0, keepdims=True)
            groups.append(bg)
            gscore.append(m1 + m2)
        masked = []
        for g in range(N_GROUPS):
            beaten = jnp.zeros((1, th), F32)
            for o in range(N_GROUPS):
                if o == g:
                    continue
                wins = (gscore[o] >= gscore[g]) if o < g else (gscore[o] > gscore[g])
                beaten = beaten + jnp.where(wins, 1.0, 0.0)
            masked.append(jnp.where(beaten < TOPK_GROUPS, groups[g], -jnp.inf))
        cur = jnp.concatenate(masked, axis=0)

        eid = lax.broadcasted_iota(I32, (N_EXPERTS, th), 0).astype(F32)
        idx_rows, sc_rows = [], []
        sel = jnp.zeros((N_EXPERTS, th), F32)
        for _ in range(TOP_K):
            m = jnp.max(cur, axis=0, keepdims=True)
            idx = jnp.min(jnp.where(cur == m, eid, float(N_EXPERTS)), axis=0, keepdims=True)
            hit = eid == idx
            sel = jnp.where(hit, 1.0, sel)
            cur = jnp.where(hit, -jnp.inf, cur)
            idx_rows.append(idx)
            sc_rows.append(jnp.sum(jnp.where(hit, scores, 0.0), axis=0, keepdims=True))
        wsum = sc_rows[0]
        for k in range(1, TOP_K):
            wsum = wsum + sc_rows[k]

        prefix = jnp.dot(sel.astype(BF16), tri_ref[...], preferred_element_type=F32)
        rank = carry + prefix
        rank_rows = [jnp.sum(jnp.where(eid == idx_rows[k], rank, 0.0), axis=0, keepdims=True)
                     for k in range(TOP_K)]
        idx_ref[:, cs] = jnp.concatenate(idx_rows, axis=0).astype(I32)
        rank_ref[:, cs] = jnp.concatenate(rank_rows, axis=0).astype(I32)
        wt_ref[:, cs] = jnp.concatenate([s / wsum * ROUTE_SCALE for s in sc_rows], axis=0)
        return carry + jnp.sum(sel, axis=1, keepdims=True)

    carry = carry_ref[:, 0:1]
    logits = shared_and_logits(0, *mix(0))
    for p in range(MIX_SPLIT):
        if p + 1 < MIX_SPLIT:
            mixed_next = mix(p + 1)
        carry = route(p, logits, carry)
        if p + 1 < MIX_SPLIT:
            logits = shared_and_logits(p + 1, *mixed_next)
    carry_ref[...] = jnp.broadcast_to(carry, carry_ref.shape)
    cnt_ref[...] = carry_ref[...]


def _mix_route(x, uc, at, g1, wg, wco, wao, wo, g2, wrh, wrl, rb, wsgu, wsd, counts_in, part, tm=1024):
    n = x.shape[0] // TOKEN_PARTS
    steps = n // tm
    th = tm // MIX_SPLIT
    tri = (jnp.arange(th)[:, None] < jnp.arange(th)[None, :]).astype(BF16)
    src = lambda i: (part * steps + i, 0)
    row = lambda i: (i, 0)
    col = lambda i: (0, i)
    full = lambda i: (0, 0)
    ins = [x, uc, at, g1, wg, wco, wao, wo, g2, wrh, wrl, rb, tri, wsgu, wsd, counts_in]
    in_specs = [pl.BlockSpec((tm, D_MODEL), src), pl.BlockSpec((tm, D_CONV), src), pl.BlockSpec((tm, D_ATTN), src)]
    in_specs += [pl.BlockSpec(a.shape, full) for a in ins[3:]]
    return pl.pallas_call(
        functools.partial(_mix_route_kernel, tm=tm),
        grid=(steps,),
        in_specs=in_specs,
        out_specs=[pl.BlockSpec((tm, D_MODEL), row), pl.BlockSpec((tm, D_MODEL // 2), row),
                   pl.BlockSpec((TOP_K, tm), col), pl.BlockSpec((TOP_K, tm), col), pl.BlockSpec((TOP_K, tm), col),
                   pl.BlockSpec((N_EXPERTS, 128), full)],
        out_shape=[jax.ShapeDtypeStruct((n, D_MODEL), F32), jax.ShapeDtypeStruct((n, D_MODEL // 2), U32),
                   jax.ShapeDtypeStruct((TOP_K, n), I32), jax.ShapeDtypeStruct((TOP_K, n), I32),
                   jax.ShapeDtypeStruct((TOP_K, n), F32), jax.ShapeDtypeStruct((N_EXPERTS, 128), F32)],
        scratch_shapes=[pltpu.VMEM((N_EXPERTS, 128), F32)],
        compiler_params=_cparams("arbitrary"),
        name="mix_route",
    )(*ins)


def _sc_workers():
    info = plsc.get_sparse_core_info()
    return info.num_cores, info.num_subcores


def _sc_mesh():
    return plsc.VectorSubcoreMesh(core_axis_name="c", subcore_axis_name="s")


def _worker_id(n_cores):
    return lax.axis_index("s") * n_cores + lax.axis_index("c")


def _dispatch(slots3, hp, xs_ref):
    n_chunks = slots3.shape[0]
    n_cores, n_sub = _sc_workers()
    per_worker = n_chunks // (n_cores * n_sub)
    assert per_worker * n_cores * n_sub == n_chunks

    @functools.partial(
        pl.kernel, mesh=_sc_mesh(), out_type=(),
        scratch_types=[pltpu.VMEM((TOP_K, SC_CHUNK), I32), pltpu.VMEM((SC_CHUNK, D_MODEL // 2), U32),
                       pltpu.SemaphoreType.DMA],
        name="dispatch")
    def run(slots_hbm, hp_hbm, xs_hbm, idx_v, rows_v, sem):
        first = _worker_id(n_cores) * per_worker

        @pl.loop(0, per_worker)
        def _(j):
            c = first + j
            pltpu.sync_copy(slots_hbm.at[c], idx_v)
            pltpu.sync_copy(hp_hbm.at[pl.ds(c * SC_CHUNK, SC_CHUNK)], rows_v)
            copies = [pltpu.async_copy(rows_v, xs_hbm.at[idx_v.at[k]], sem) for k in range(TOP_K)]
            for cp in copies:
                cp.wait()

    run(slots3, hp, xs_ref)


def _experts_kernel(bexp_ref, bidx_ref, bval_ref, nused_ref, next_ref, wslot_ref,
                    xs_ref, wg_ref, wu_ref, wd_ref, y_ref, xbuf, wg_buf, wu_buf, wd_buf, wgu_s, wd_s, xsem, wsem):
    b = pl.program_id(0)
    n_used = nused_ref[0]

    def block_copy(blk):
        slot = blk % XS_SLOTS
        rows = pl.ds(pl.multiple_of(bidx_ref[blk] * SLOT_BLK, SLOT_BLK), SLOT_BLK)
        return pltpu.make_async_copy(xs_ref.at[rows], xbuf.at[slot], xsem.at[slot])

    @pl.when(b == 0)
    def _():
        for blk in range(XS_AHEAD):
            @pl.when(blk < n_used)
            def _():
                block_copy(blk).start()

    @pl.when(b + XS_AHEAD < n_used)
    def _():
        block_copy(b + XS_AHEAD).start()

    def weight_copies(e, slot):
        return [pltpu.make_async_copy(wg_ref.at[e], wg_buf.at[slot], wsem.at[slot]),
                pltpu.make_async_copy(wu_ref.at[e], wu_buf.at[slot], wsem.at[slot]),
                pltpu.make_async_copy(wd_ref.at[e], wd_buf.at[slot], wsem.at[slot])]

    @pl.when(b < n_used)
    def _():
        e = bexp_ref[b]
        slot = wslot_ref[e]

        @pl.when(b == 0)
        def _():
            for cp in weight_copies(e, slot):
                cp.start()

        @pl.when((b == 0) | (bexp_ref[jnp.maximum(b - 1, 0)] != e))
        def _():
            for cp in weight_copies(e, slot):
                cp.wait()
            wgu_s[:, :D_EXPERT] = wg_buf[slot].astype(BF16)
            wgu_s[:, D_EXPERT:] = wu_buf[slot].astype(BF16)
            wd_s[...] = wd_buf[slot].astype(BF16)
            nxt = next_ref[e]

            @pl.when(nxt >= 0)
            def _():
                for cp in weight_copies(nxt, 1 - slot):
                    cp.start()

        srow = lax.broadcasted_iota(I32, (SLOT_BLK, 1), 0)
        block_copy(b).wait()
        xw = jnp.where(srow < bval_ref[b], xbuf[b % XS_SLOTS], jnp.uint32(0))
        xa, xb = _unpack_bf16_pair(xw)
        x = jnp.concatenate([xa.astype(BF16), xb.astype(BF16)], axis=1)
        gu = jnp.dot(x, wgu_s[...], preferred_element_type=F32)
        mid = (jax.nn.silu(gu[:, :D_EXPERT]) * gu[:, D_EXPERT:]).astype(BF16)
        y = jnp.dot(mid, wd_s[...], preferred_element_type=F32)
        y_ref[...] = _pack_bf16_pair(y[:, :D_MODEL // 2], y[:, D_MODEL // 2:])


def _experts(bexp, bidx, bval, nused, next_exp, wslot, xs, wg, wu, wd):
    n_blocks = bexp.shape[0]
    xmap = lambda b, be, bi, bv, nu, nx, ws: (bi[b], 0)
    grid_spec = pltpu.PrefetchScalarGridSpec(
        num_scalar_prefetch=6,
        grid=(n_blocks,),
        in_specs=[pl.BlockSpec(memory_space=pl.ANY),
                  pl.BlockSpec(memory_space=pl.ANY), pl.BlockSpec(memory_space=pl.ANY),
                  pl.BlockSpec(memory_space=pl.ANY)],
        out_specs=pl.BlockSpec((SLOT_BLK, D_MODEL // 2), xmap),
        scratch_shapes=[pltpu.VMEM((XS_SLOTS, SLOT_BLK, D_MODEL // 2), U32),
                        pltpu.VMEM((2, D_MODEL, D_EXPERT), F32), pltpu.VMEM((2, D_MODEL, D_EXPERT), F32),
                        pltpu.VMEM((2, D_EXPERT, D_MODEL), F32),
                        pltpu.VMEM((D_MODEL, 2 * D_EXPERT), BF16), pltpu.VMEM((D_EXPERT, D_MODEL), BF16),
                        pltpu.SemaphoreType.DMA((XS_SLOTS,)), pltpu.SemaphoreType.DMA((2,))],
    )
    return pl.pallas_call(
        _experts_kernel,
        grid_spec=grid_spec,
        out_shape=jax.ShapeDtypeStruct(xs.shape, U32),
        compiler_params=_cparams("arbitrary"),
        name="experts",
    )(bexp, bidx, bval, nused, next_exp, wslot, xs, wg, wu, wd)


def _gather(slots3, y, n):
    n_chunks = slots3.shape[0]
    n_cores, n_sub = _sc_workers()
    per_worker = n_chunks // (n_cores * n_sub)
    assert per_worker * n_cores * n_sub == n_chunks

    @functools.partial(
        pl.kernel, mesh=_sc_mesh(),
        out_type=jax.ShapeDtypeStruct((TOP_K, n, D_MODEL // 2), U32),
        scratch_types=[pltpu.VMEM((TOP_K, SC_CHUNK), I32),
                       pltpu.VMEM((SC_CHUNK, D_MODEL // 2), U32), pltpu.VMEM((SC_CHUNK, D_MODEL // 2), U32),
                       pltpu.SemaphoreType.DMA, pltpu.SemaphoreType.DMA, pltpu.SemaphoreType.DMA],
        name="gather")
    def run(slots_hbm, y_hbm, g_hbm, idx_v, buf0, buf1, gsem, wsem0, wsem1):
        first = _worker_id(n_cores) * per_worker
        bufs = (buf0, buf1)
        wsems = (wsem0, wsem1)

        @pl.loop(0, per_worker)
        def _(j):
            c = first + j
            pltpu.sync_copy(slots_hbm.at[c], idx_v)
            gathers = [None] * TOP_K
            writes = [None] * TOP_K
            gathers[0] = pltpu.async_copy(y_hbm.at[idx_v.at[0]], bufs[0], gsem)
            for k in range(TOP_K):
                gathers[k].wait()
                if k >= 1:
                    writes[k - 1].wait()
                if k + 1 < TOP_K:
                    gathers[k + 1] = pltpu.async_copy(y_hbm.at[idx_v.at[k + 1]], bufs[(k + 1) % 2], gsem)
                writes[k] = pltpu.async_copy(bufs[k % 2], g_hbm.at[k, pl.ds(c * SC_CHUNK, SC_CHUNK)],
                                             wsems[k % 2])
            writes[TOP_K - 1].wait()

    return run(slots3, y)


def _combine_kernel(wt_ref, xo_ref, g_ref, *rest):
    o_ref = rest[-1]
    half = D_MODEL // 2
    acc_a = xo_ref[:, :half]
    acc_b = xo_ref[:, half:]
    w_all = wt_ref[...].T
    for k in range(TOP_K):
        ya, yb = _unpack_bf16_pair(g_ref[k])
        w = w_all[:, k:k + 1]
        acc_a = acc_a + w * ya
        acc_b = acc_b + w * yb
    o_ref[:, :half] = acc_a
    o_ref[:, half:] = acc_b


def _combine(wt, xo, g, piece, part, n, out_so_far, tm=256):
    steps = g.shape[1] // tm
    in_specs = [pl.BlockSpec((TOP_K, tm), lambda i: (0, piece * steps + i)),
                pl.BlockSpec((tm, D_MODEL), lambda i: (piece * steps + i, 0)),
                pl.BlockSpec((TOP_K, tm, D_MODEL // 2), lambda i: (0, i, 0))]
    args = [wt, xo, g]
    aliases = {}
    if out_so_far is not None:
        in_specs.append(pl.BlockSpec(memory_space=pl.ANY))
        args.append(out_so_far)
        aliases = {len(args) - 1: 0}
    return pl.pallas_call(
        _combine_kernel,
        grid=(steps,),
        in_specs=in_specs,
        out_specs=pl.BlockSpec((tm, D_MODEL), lambda i: (part * steps + i, 0)),
        out_shape=jax.ShapeDtypeStruct((n, D_MODEL), F32),
        input_output_aliases=aliases,
        compiler_params=_cparams("parallel"),
        name="combine",
    )(*args)


def _layer(x, norm_mix_g, w_in, q_norm_g, k_norm_g, rpb, conv_dw_w, conv_dw_b, conv_ln_g, conv_ln_b,
           w_conv_out, w_attn_out, w_o, norm_ffn_g, w_router, router_bias, w_exp_gate, w_exp_up,
           w_exp_down, w_sh_gate, w_sh_up, w_sh_down):
    n = x.shape[0]
    c_qkv = 2 * D_CONV + 3 * D_ATTN
    w_qkv = w_in[:, :c_qkv].astype(BF16)
    w_gates = w_in[:, c_qkv:].astype(BF16)
    head_of = jnp.arange(D_ATTN // 2) // HEAD_DIM
    bsum = (head_of[:, None] == head_of[None, :]).astype(BF16)
    row = lambda v: v.reshape(1, -1).astype(F32)

    u, q, k, v = _inproj(x, row(norm_mix_g), w_qkv, bsum,
                         row(jnp.tile(q_norm_g, N_HEADS)), row(jnp.tile(k_norm_g, N_HEADS)))
    uc = _conv(u, conv_dw_w.reshape(CONV_WIDTH, D_CONV), row(conv_dw_b), row(conv_ln_g), row(conv_ln_b))
    at = _attn(q, k, v, _bias_table(rpb))

    wr_t = w_router.T
    wr_hi = wr_t.astype(BF16)
    wr_lo = (wr_t - wr_hi.astype(F32)).astype(BF16)
    wsgu = jnp.concatenate([w_sh_gate, w_sh_up], axis=1).astype(BF16)
    mix_weights = (row(norm_mix_g), w_gates, w_conv_out.astype(BF16), w_attn_out.astype(BF16), w_o.astype(BF16),
                   row(norm_ffn_g), wr_hi, wr_lo, router_bias.reshape(N_EXPERTS, 1).astype(F32), wsgu,
                   w_sh_down.astype(BF16))

    cap = n
    n_part = n // TOKEN_PARTS
    xs_ref = jax.empty_ref(jax.ShapeDtypeStruct((N_EXPERTS * cap, D_MODEL // 2), U32))
    cnt = jnp.zeros((N_EXPERTS, 128), F32)
    parts = []
    for part in range(TOKEN_PARTS):
        xo_p, hp_p, idx_p, rank_p, wt_p, cnt = _mix_route(x, uc, at, *mix_weights, cnt, part)
        slots_p = (idx_p * cap + rank_p).reshape(TOP_K, n_part // SC_CHUNK, SC_CHUNK).transpose(1, 0, 2)
        _dispatch(slots_p, hp_p, xs_ref)
        parts.append((xo_p, wt_p, slots_p))
    xs = jax.freeze(xs_ref)

    counts = cnt[:, 0].astype(I32)
    blocks_of = (counts + SLOT_BLK - 1) // SLOT_BLK
    blocks_end = jnp.cumsum(blocks_of)
    n_blocks = (n * TOP_K + N_EXPERTS * (SLOT_BLK - 1)) // SLOT_BLK
    n_used = blocks_end[-1]
    bseq = jnp.minimum(jnp.arange(n_blocks, dtype=I32), n_used - 1)
    bexp = jnp.minimum(jnp.sum((blocks_end[None, :] <= bseq[:, None]).astype(I32), axis=1), N_EXPERTS - 1)
    owner = (bexp[:, None] == jnp.arange(N_EXPERTS)[None, :]).astype(I32)
    local = bseq - jnp.sum(owner * (blocks_end - blocks_of)[None, :], axis=1)
    bidx = (bexp * (cap // SLOT_BLK) + local).astype(I32)
    bval = jnp.clip(jnp.sum(owner * counts[None, :], axis=1) - local * SLOT_BLK, 0, SLOT_BLK).astype(I32)
    eids = jnp.arange(N_EXPERTS, dtype=I32)
    later = (eids[None, :] > eids[:, None]) & (counts[None, :] > 0)
    next_exp = jnp.min(jnp.where(later, eids[None, :], N_EXPERTS), axis=1)
    next_exp = jnp.where(next_exp == N_EXPERTS, -1, next_exp).astype(I32)
    wslot = ((jnp.cumsum((counts > 0).astype(I32)) - 1) & 1).astype(I32)

    y = _experts(bexp, bidx, bval, n_used.reshape(1).astype(I32), next_exp, wslot, xs,
                 w_exp_gate, w_exp_up, w_exp_down)
    out = None
    piece_chunks = n_part // SC_CHUNK // GATHER_SPLIT
    for part, (xo_p, wt_p, slots_p) in enumerate(parts):
        for piece in range(GATHER_SPLIT):
            g = _gather(slots_p[piece * piece_chunks:(piece + 1) * piece_chunks], y, n_part // GATHER_SPLIT)
            out = _combine(wt_p, xo_p, g, piece, part * GATHER_SPLIT + piece, n, out)
    return out


def kernel(x, norm_mix_g, w_in, q_norm_g, k_norm_g, rpb, conv_dw_w, conv_dw_b, conv_ln_g, conv_ln_b, w_conv_out, w_attn_out, w_o, norm_ffn_g, w_router, router_bias, w_exp_gate, w_exp_up, w_exp_down, w_sh_gate, w_sh_up, w_sh_down):
    b, s, d = x.shape
    assert b == 1 and d == D_MODEL and s % (2 * ATT_BLK) == 0, x.shape
    xf = x.reshape(b * s, d)
    depth = norm_mix_g.shape[0]
    for l in range(depth):
        xf = _layer(xf, norm_mix_g[l], w_in[l], q_norm_g[l], k_norm_g[l], rpb[l], conv_dw_w[l], conv_dw_b[l],
                    conv_ln_g[l], conv_ln_b[l], w_conv_out[l], w_attn_out[l], w_o[l], norm_ffn_g[l],
                    w_router[l], router_bias[l], w_exp_gate[l], w_exp_up[l], w_exp_down[l], w_sh_gate[l],
                    w_sh_up[l], w_sh_down[l])
    return xf.reshape(b, s, d)
```

```python
import functools

import jax
import jax.numpy as jnp
from jax import lax
from jax.experimental import pallas as pl
from jax.experimental.pallas import tpu as pltpu
from jax.experimental.pallas import tpu_sc as plsc

D_MODEL = 1024
GRID_W = 64
D_CONV = 512
CONV_WIDTH = 31
CONV_PAD = CONV_WIDTH // 2
N_HEADS = 8
HEAD_DIM = 64
D_ATTN = N_HEADS * HEAD_DIM
WIN_H = 8
WIN_W = 16
N_EXPERTS = 64
TOP_K = 8
N_GROUPS = 8
GROUP_SIZE = N_EXPERTS // N_GROUPS
TOPK_GROUPS = 4
D_EXPERT = 256
D_SHARED = 256
ROUTE_SCALE = 2.5
EPS = 1e-6
LOG2E = 1.4426950408889634

SUBLANES = 8
NEG = -1e30
HALO = 16
ATT_ROWS = 4
ATT_BLK = ATT_ROWS * GRID_W
SLOT_BLK = 512
MIX_SPLIT = 2
XS_AHEAD = 3
XS_SLOTS = XS_AHEAD + 1
TOKEN_PARTS = 4
SC_CHUNK = 64
VMEM_LIMIT = 56 * 1024 * 1024

F32 = jnp.float32
BF16 = jnp.bfloat16
I32 = jnp.int32
U32 = jnp.uint32


def _cparams(*sem):
    return pltpu.CompilerParams(dimension_semantics=sem, vmem_limit_bytes=VMEM_LIMIT)


def _rms(x, g):
    return x * lax.rsqrt(jnp.mean(x * x, axis=-1, keepdims=True) + EPS) * g


def _pack_bf16_pair(a, b):
    def bf16_bits(v):
        return lax.bitcast_convert_type(v.astype(BF16).astype(F32), U32)
    return bf16_bits(a) | (bf16_bits(b) >> 16)


def _unpack_bf16_pair(w):
    a = lax.bitcast_convert_type(w & jnp.uint32(0xFFFF0000), F32)
    b = lax.bitcast_convert_type(w << 16, F32)
    return a, b


def _inproj_kernel(x_ref, g_ref, w_ref, bsum_ref, qg_ref, kg_ref, u_ref, q_ref, k_ref, v_ref):
    h = _rms(x_ref[...], g_ref[...]).astype(BF16)
    ag = jnp.dot(h, w_ref[:, 0:2 * D_CONV], preferred_element_type=F32)
    u_ref[...] = ag[:, :D_CONV] * jax.nn.sigmoid(ag[:, D_CONV:])
    bsum = bsum_ref[...]

    def head_norm(z, g):
        zz = (z * z).astype(BF16)
        hw = D_ATTN // 2
        ss = jnp.concatenate([jnp.dot(zz[:, :hw], bsum, preferred_element_type=F32),
                              jnp.dot(zz[:, hw:], bsum, preferred_element_type=F32)], axis=1)
        return z * lax.rsqrt(ss * (1.0 / HEAD_DIM) + EPS) * g

    c0 = 2 * D_CONV
    q = jnp.dot(h, w_ref[:, c0:c0 + D_ATTN], preferred_element_type=F32)
    q_ref[...] = (head_norm(q, qg_ref[...]) * (HEAD_DIM ** -0.5 * LOG2E)).astype(BF16)
    k = jnp.dot(h, w_ref[:, c0 + D_ATTN:c0 + 2 * D_ATTN], preferred_element_type=F32)
    k_ref[...] = head_norm(k, kg_ref[...]).astype(BF16)
    v = jnp.dot(h, w_ref[:, c0 + 2 * D_ATTN:c0 + 3 * D_ATTN], preferred_element_type=F32)
    v_ref[...] = v.astype(BF16)


def _inproj(x, g, w, bsum, qg, kg, tm=512):
    n = x.shape[0]
    wc = w.shape[1]
    full = lambda i: (0, 0)
    row = lambda i: (i, 0)
    return pl.pallas_call(
        _inproj_kernel,
        grid=(n // tm,),
        in_specs=[pl.BlockSpec((tm, D_MODEL), row), pl.BlockSpec((1, D_MODEL), full),
                  pl.BlockSpec((D_MODEL, wc), full), pl.BlockSpec(bsum.shape, full),
                  pl.BlockSpec((1, D_ATTN), full), pl.BlockSpec((1, D_ATTN), full)],
        out_specs=[pl.BlockSpec((tm, D_CONV), row), pl.BlockSpec((tm, D_ATTN), row),
                   pl.BlockSpec((tm, D_ATTN), row), pl.BlockSpec((tm, D_ATTN), row)],
        out_shape=[jax.ShapeDtypeStruct((n, D_CONV), F32), jax.ShapeDtypeStruct((n, D_ATTN), BF16),
                   jax.ShapeDtypeStruct((n, D_ATTN), BF16), jax.ShapeDtypeStruct((n, D_ATTN), BF16)],
        compiler_params=_cparams("parallel"),
        name="inproj",
    )(x, g, w, bsum, qg, kg)


def _conv_kernel(up_ref, uc_ref, un_ref, w_ref, b_ref, lg_ref, lb_ref, o_ref, ext_ref, sh_ref, *, tc, ch):
    i = pl.program_id(0)
    last = pl.num_programs(0) - 1
    ext_ref[0:HALO, :] = jnp.where(i > 0, up_ref[...], 0.0)
    ext_ref[HALO:HALO + tc, :] = uc_ref[...]
    ext_ref[HALO + tc:2 * HALO + tc, :] = jnp.where(i < last, un_ref[...], 0.0)
    span = sh_ref.shape[1]
    for b in range(SUBLANES):
        sh_ref[b] = ext_ref[b:b + span, :]
    bias = b_ref[...]
    lg = lg_ref[...]
    lb = lb_ref[...]
    for c in range(tc // ch):
        acc = jnp.broadcast_to(bias, (ch, D_CONV))
        for j in range(CONV_WIDTH):
            shift = HALO - CONV_PAD + j
            row0 = c * ch + shift // SUBLANES * SUBLANES
            acc = acc + sh_ref[shift % SUBLANES, row0:row0 + ch, :] * w_ref[j:j + 1, :]
        mu = jnp.mean(acc, axis=-1, keepdims=True)
        d = acc - mu
        var = jnp.mean(d * d, axis=-1, keepdims=True)
        y = d * lax.rsqrt(var + EPS) * lg + lb
        o_ref[c * ch:(c + 1) * ch, :] = (y * jax.nn.sigmoid(y)).astype(BF16)


def _conv(u, w, b, lg, lb, tc=256, ch=64):
    n = u.shape[0]
    hb = tc // HALO
    nh = n // HALO
    full = lambda i: (0, 0)
    return pl.pallas_call(
        functools.partial(_conv_kernel, tc=tc, ch=ch),
        grid=(n // tc,),
        in_specs=[pl.BlockSpec((HALO, D_CONV), lambda i: (jnp.maximum(i * hb - 1, 0), 0)),
                  pl.BlockSpec((tc, D_CONV), lambda i: (i, 0)),
                  pl.BlockSpec((HALO, D_CONV), lambda i: (jnp.minimum((i + 1) * hb, nh - 1), 0)),
                  pl.BlockSpec((CONV_WIDTH, D_CONV), full), pl.BlockSpec((1, D_CONV), full),
                  pl.BlockSpec((1, D_CONV), full), pl.BlockSpec((1, D_CONV), full)],
        out_specs=pl.BlockSpec((tc, D_CONV), lambda i: (i, 0)),
        out_shape=jax.ShapeDtypeStruct((n, D_CONV), BF16),
        scratch_shapes=[pltpu.VMEM((tc + 2 * HALO, D_CONV), F32),
                        pltpu.VMEM((SUBLANES, tc + 2 * HALO - SUBLANES, D_CONV), F32)],
        compiler_params=_cparams("parallel"),
        name="conv",
    )(u, u, u, w, b, lg, lb)


def _attn_kernel(q_ref, kp_ref, kc_ref, kn_ref, vp_ref, vc_ref, vn_ref, tb_ref, o_ref, *, rows):
    i = pl.program_id(0)
    nkey = 3 * ATT_BLK
    lrow = lax.broadcasted_iota(I32, (1, nkey), 1) >> 6
    lane = lax.broadcasted_iota(I32, (1, 2 * HEAD_DIM), 1)
    masks = []
    for j in range(ATT_ROWS):
        r = i * ATT_ROWS + j
        rs = jnp.clip(r - WIN_H // 2, 0, rows - WIN_H)
        lo = rs - (i - 1) * ATT_ROWS
        masks.append(jnp.where((lrow >= lo) & (lrow < lo + WIN_H), 0.0, NEG))
    dn = (((1,), (1,)), ((), ()))

    def pair_slice(h):
        return slice(2 * HEAD_DIM * (h // 2), 2 * HEAD_DIM * (h // 2 + 1))

    def in_half(h):
        return (lane >= HEAD_DIM * (h % 2)) & (lane < HEAD_DIM * (h % 2 + 1))

    def scores(h):
        sl = pair_slice(h)
        q2 = q_ref[:, sl]
        qm = jnp.where(in_half(h), q2, jnp.zeros_like(q2))
        s = jnp.concatenate([lax.dot_general(qm, kk[:, sl], dn, preferred_element_type=F32)
                             for kk in (kp_ref, kc_ref, kn_ref)], axis=1)
        parts = []
        for j in range(ATT_ROWS):
            bias = jnp.concatenate([tb_ref[h, 2 * m - j + 3] for m in range(nkey // (2 * GRID_W))], axis=1)
            parts.append(s[j * GRID_W:(j + 1) * GRID_W, :] + bias + masks[j])
        return jnp.concatenate(parts, axis=0)

    def softmax(s):
        mx = jnp.max(s, axis=-1, keepdims=True)
        e = jnp.exp2(s - mx)
        return e.astype(BF16), jnp.sum(e, axis=-1, keepdims=True)

    def values(h, pb, den):
        sl = pair_slice(h)
        o = sum(jnp.dot(pb[:, ATT_BLK * t:ATT_BLK * (t + 1)], vv[:, sl], preferred_element_type=F32)
                for t, vv in enumerate((vp_ref, vc_ref, vn_ref)))
        return o / den

    stage_s = {h: scores(h) for h in range(2)}
    stage_p = {0: softmax(stage_s.pop(0))}
    out_pair = None
    for h in range(N_HEADS):
        if h + 2 < N_HEADS:
            stage_s[h + 2] = scores(h + 2)
        if h + 1 < N_HEADS:
            stage_p[h + 1] = softmax(stage_s.pop(h + 1))
        o = values(h, *stage_p.pop(h))
        if h % 2 == 0:
            out_pair = o
        else:
            o_ref[:, pair_slice(h)] = jnp.where(in_half(h), o, out_pair).astype(BF16)


def _attn(q, k, v, tb):
    n = q.shape[0]
    rows = n // GRID_W
    nb = n // ATT_BLK
    cur = lambda i: (i, 0)
    prev = lambda i: (jnp.maximum(i - 1, 0), 0)
    nxt = lambda i: (jnp.minimum(i + 1, nb - 1), 0)
    blk = (ATT_BLK, D_ATTN)
    return pl.pallas_call(
        functools.partial(_attn_kernel, rows=rows),
        grid=(nb,),
        in_specs=[pl.BlockSpec(blk, cur), pl.BlockSpec(blk, prev), pl.BlockSpec(blk, cur), pl.BlockSpec(blk, nxt),
                  pl.BlockSpec(blk, prev), pl.BlockSpec(blk, cur), pl.BlockSpec(blk, nxt),
                  pl.BlockSpec(tb.shape, lambda i: (0, 0, 0, 0))],
        out_specs=pl.BlockSpec(blk, cur),
        out_shape=jax.ShapeDtypeStruct((n, D_ATTN), BF16),
        compiler_params=_cparams("parallel"),
        name="attn",
    )(q, k, k, k, v, v, v, tb)


def _bias_table(rpb):
    cols = jnp.arange(GRID_W)
    start = jnp.clip(cols - WIN_W // 2, 0, GRID_W - WIN_W)
    kc = cols[None, :]
    inwin = (kc >= start[:, None]) & (kc < start[:, None] + WIN_W)
    rel_idx = kc - cols[:, None] + (WIN_W - 1)
    pick = ((rel_idx[None] == jnp.arange(2 * WIN_W - 1)[:, None, None]) & inwin[None]).astype(F32)
    rel = jnp.einsum('hdj,jck->hdck', rpb.astype(F32), pick, precision=lax.Precision.HIGHEST)
    t = jnp.where(inwin[None, None], rel * LOG2E, NEG)
    t = jnp.concatenate([t, jnp.full_like(t[:, :1], NEG)], axis=1)
    t_next = jnp.concatenate([t[:, 1:], t[:, -1:]], axis=1)
    return jnp.concatenate([t, t_next], axis=-1).astype(F32)


def _mix_route_kernel(x_ref, uc_ref, at_ref, g1_ref, wg_ref, wco_ref, wao_ref, wo_ref, g2_ref,
                      wrh_ref, wrl_ref, rb_ref, tri_ref, wsgu_ref, wsd_ref, cin_ref,
                      xo_ref, hp_ref, idx_ref, rank_ref, wt_ref, cnt_ref, carry_ref, *, tm):
    i = pl.program_id(0)

    @pl.when(i == 0)
    def _():
        carry_ref[...] = cin_ref[...]

    th = tm // MIX_SPLIT
    dn = (((1,), (1,)), ((), ()))

    def mix(p):
        rs = slice(p * th, (p + 1) * th)
        x = x_ref[rs, :]
        h = _rms(x, g1_ref[...]).astype(BF16)
        gates = jax.nn.sigmoid(jnp.dot(h, wg_ref[...], preferred_element_type=F32))
        y_conv = jnp.dot(uc_ref[rs, :], wco_ref[...], preferred_element_type=F32)
        y_attn = jnp.dot(at_ref[rs, :], wao_ref[...], preferred_element_type=F32)
        merged = gates[:, :D_MODEL] * y_conv + gates[:, D_MODEL:] * y_attn
        x1 = x + jnp.dot(merged.astype(BF16), wo_ref[...], preferred_element_type=F32)
        h2 = _rms(x1, g2_ref[...])
        hp_ref[rs, :] = _pack_bf16_pair(h2[:, :D_MODEL // 2], h2[:, D_MODEL // 2:])
        return x1, h2

    def shared_and_logits(p, x1, h2):
        rs = slice(p * th, (p + 1) * th)
        h2b = h2.astype(BF16)
        gu = jnp.dot(h2b, wsgu_ref[...], preferred_element_type=F32)
        mid = (jax.nn.silu(gu[:, :D_SHARED]) * gu[:, D_SHARED:]).astype(BF16)
        xo_ref[rs, :] = x1 + jnp.dot(mid, wsd_ref[...], preferred_element_type=F32)
        h2l = (h2 - h2b.astype(F32)).astype(BF16)
        return (lax.dot_general(wrh_ref[...], h2b, dn, preferred_element_type=F32)
                + lax.dot_general(wrl_ref[...], h2b, dn, preferred_element_type=F32)
                + lax.dot_general(wrh_ref[...], h2l, dn, preferred_element_type=F32))

    def route(p, logits, carry):
        cs = slice(p * th, (p + 1) * th)
        scores = jax.nn.sigmoid(logits)
        biased = scores + rb_ref[...]

        sub = lax.broadcasted_iota(I32, (GROUP_SIZE, th), 0).astype(F32)
        groups, gscore = [], []
        for g in range(N_GROUPS):
            bg = biased[g * GROUP_SIZE:(g + 1) * GROUP_SIZE, :]
            m1 = jnp.max(bg, axis=0, keepdims=True)
            first = jnp.min(jnp.where(bg == m1, sub, float(GROUP_SIZE)), axis=0, keepdims=True)
            m2 = jnp.max(jnp.where(sub == first, -jnp.inf, bg), axis=0, keepdims=True)
            groups.append(bg)
            gscore.append(m1 + m2)
        masked = []
        for g in range(N_GROUPS):
            beaten = jnp.zeros((1, th), F32)
            for o in range(N_GROUPS):
                if o == g:
                    continue
                wins = (gscore[o] >= gscore[g]) if o < g else (gscore[o] > gscore[g])
                beaten = beaten + jnp.where(wins, 1.0, 0.0)
            masked.append(jnp.where(beaten < TOPK_GROUPS, groups[g], -jnp.inf))
        cur = jnp.concatenate(masked, axis=0)

        eid = lax.broadcasted_iota(I32, (N_EXPERTS, th), 0).astype(F32)
        idx_rows, sc_rows = [], []
        sel = jnp.zeros((N_EXPERTS, th), F32)
        for _ in range(TOP_K):
            m = jnp.max(cur, axis=0, keepdims=True)
            idx = jnp.min(jnp.where(cur == m, eid, float(N_EXPERTS)), axis=0, keepdims=True)
            hit = eid == idx
            sel = jnp.where(hit, 1.0, sel)
            cur = jnp.where(hit, -jnp.inf, cur)
            idx_rows.append(idx)
            sc_rows.append(jnp.sum(jnp.where(hit, scores, 0.0), axis=0, keepdims=True))
        wsum = sc_rows[0]
        for k in range(1, TOP_K):
            wsum = wsum + sc_rows[k]

        prefix = jnp.dot(sel.astype(BF16), tri_ref[...], preferred_element_type=F32)
        rank = carry + prefix
        rank_rows = [jnp.sum(jnp.where(eid == idx_rows[k], rank, 0.0), axis=0, keepdims=True)
                     for k in range(TOP_K)]
        idx_ref[:, cs] = jnp.concatenate(idx_rows, axis=0).astype(I32)
        rank_ref[:, cs] = jnp.concatenate(rank_rows, axis=0).astype(I32)
        wt_ref[:, cs] = jnp.concatenate([s / wsum * ROUTE_SCALE for s in sc_rows], axis=0)
        return carry + jnp.sum(sel, axis=1, keepdims=True)

    carry = carry_ref[:, 0:1]
    logits = shared_and_logits(0, *mix(0))
    for p in range(MIX_SPLIT):
        if p + 1 < MIX_SPLIT:
            mixed_next = mix(p + 1)
        carry = route(p, logits, carry)
        if p + 1 < MIX_SPLIT:
            logits = shared_and_logits(p + 1, *mixed_next)
    carry_ref[...] = jnp.broadcast_to(carry, carry_ref.shape)
    cnt_ref[...] = carry_ref[...]


def _mix_route(x, uc, at, g1, wg, wco, wao, wo, g2, wrh, wrl, rb, wsgu, wsd, counts_in, part, tm=1024):
    n = x.shape[0] // TOKEN_PARTS
    steps = n // tm
    th = tm // MIX_SPLIT
    tri = (jnp.arange(th)[:, None] < jnp.arange(th)[None, :]).astype(BF16)
    src = lambda i: (part * steps + i, 0)
    row = lambda i: (i, 0)
    col = lambda i: (0, i)
    full = lambda i: (0, 0)
    ins = [x, uc, at, g1, wg, wco, wao, wo, g2, wrh, wrl, rb, tri, wsgu, wsd, counts_in]
    in_specs = [pl.BlockSpec((tm, D_MODEL), src), pl.BlockSpec((tm, D_CONV), src), pl.BlockSpec((tm, D_ATTN), src)]
    in_specs += [pl.BlockSpec(a.shape, full) for a in ins[3:]]
    return pl.pallas_call(
        functools.partial(_mix_route_kernel, tm=tm),
        grid=(steps,),
        in_specs=in_specs,
        out_specs=[pl.BlockSpec((tm, D_MODEL), row), pl.BlockSpec((tm, D_MODEL // 2), row),
                   pl.BlockSpec((TOP_K, tm), col), pl.BlockSpec((TOP_K, tm), col), pl.BlockSpec((TOP_K, tm), col),
                   pl.BlockSpec((N_EXPERTS, 128), full)],
        out_shape=[jax.ShapeDtypeStruct((n, D_MODEL), F32), jax.ShapeDtypeStruct((n, D_MODEL // 2), U32),
                   jax.ShapeDtypeStruct((TOP_K, n), I32), jax.ShapeDtypeStruct((TOP_K, n), I32),
                   jax.ShapeDtypeStruct((TOP_K, n), F32), jax.ShapeDtypeStruct((N_EXPERTS, 128), F32)],
        scratch_shapes=[pltpu.VMEM((N_EXPERTS, 128), F32)],
        compiler_params=_cparams("arbitrary"),
        name="mix_route",
    )(*ins)


def _sc_workers():
    info = plsc.get_sparse_core_info()
    return info.num_cores, info.num_subcores


def _sc_mesh():
    return plsc.VectorSubcoreMesh(core_axis_name="c", subcore_axis_name="s")


def _worker_id(n_cores):
    return lax.axis_index("s") * n_cores + lax.axis_index("c")


def _dispatch(slots3, hp, xs_ref):
    n_chunks = slots3.shape[0]
    n_cores, n_sub = _sc_workers()
    per_worker = n_chunks // (n_cores * n_sub)
    assert per_worker * n_cores * n_sub == n_chunks

    @functools.partial(
        pl.kernel, mesh=_sc_mesh(), out_type=(),
        scratch_types=[pltpu.VMEM((TOP_K, SC_CHUNK), I32), pltpu.VMEM((SC_CHUNK, D_MODEL // 2), U32),
                       pltpu.SemaphoreType.DMA],
        name="dispatch")
    def run(slots_hbm, hp_hbm, xs_hbm, idx_v, rows_v, sem):
        first = _worker_id(n_cores) * per_worker

        @pl.loop(0, per_worker)
        def _(j):
            c = first + j
            pltpu.sync_copy(slots_hbm.at[c], idx_v)
            pltpu.sync_copy(hp_hbm.at[pl.ds(c * SC_CHUNK, SC_CHUNK)], rows_v)
            copies = [pltpu.async_copy(rows_v, xs_hbm.at[idx_v.at[k]], sem) for k in range(TOP_K)]
            for cp in copies:
                cp.wait()

    run(slots3, hp, xs_ref)


def _experts_kernel(bexp_ref, bidx_ref, bval_ref, nused_ref, next_ref, wslot_ref,
                    xs_ref, wg_ref, wu_ref, wd_ref, y_ref, xbuf, wg_buf, wu_buf, wd_buf, wgu_s, wd_s, xsem, wsem):
    b = pl.program_id(0)
    n_used = nused_ref[0]

    def block_copy(blk):
        slot = blk % XS_SLOTS
        rows = pl.ds(pl.multiple_of(bidx_ref[blk] * SLOT_BLK, SLOT_BLK), SLOT_BLK)
        return pltpu.make_async_copy(xs_ref.at[rows], xbuf.at[slot], xsem.at[slot])

    @pl.when(b == 0)
    def _():
        for blk in range(XS_AHEAD):
            @pl.when(blk < n_used)
            def _():
                block_copy(blk).start()

    @pl.when(b + XS_AHEAD < n_used)
    def _():
        block_copy(b + XS_AHEAD).start()

    def weight_copies(e, slot):
        return [pltpu.make_async_copy(wg_ref.at[e], wg_buf.at[slot], wsem.at[slot]),
                pltpu.make_async_copy(wu_ref.at[e], wu_buf.at[slot], wsem.at[slot]),
                pltpu.make_async_copy(wd_ref.at[e], wd_buf.at[slot], wsem.at[slot])]

    @pl.when(b < n_used)
    def _():
        e = bexp_ref[b]
        slot = wslot_ref[e]

        @pl.when(b == 0)
        def _():
            for cp in weight_copies(e, slot):
                cp.start()

        @pl.when((b == 0) | (bexp_ref[jnp.maximum(b - 1, 0)] != e))
        def _():
            for cp in weight_copies(e, slot):
                cp.wait()
            wgu_s[:, :D_EXPERT] = wg_buf[slot].astype(BF16)
            wgu_s[:, D_EXPERT:] = wu_buf[slot].astype(BF16)
            wd_s[...] = wd_buf[slot].astype(BF16)
            nxt = next_ref[e]

            @pl.when(nxt >= 0)
            def _():
                for cp in weight_copies(nxt, 1 - slot):
                    cp.start()

        srow = lax.broadcasted_iota(I32, (SLOT_BLK, 1), 0)
        block_copy(b).wait()
        xw = jnp.where(srow < bval_ref[b], xbuf[b % XS_SLOTS], jnp.uint32(0))
        xa, xb = _unpack_bf16_pair(xw)
        x = jnp.concatenate([xa.astype(BF16), xb.astype(BF16)], axis=1)
        gu = jnp.dot(x, wgu_s[...], preferred_element_type=F32)
        mid = (jax.nn.silu(gu[:, :D_EXPERT]) * gu[:, D_EXPERT:]).astype(BF16)
        y = jnp.dot(mid, wd_s[...], preferred_element_type=F32)
        y_ref[...] = _pack_bf16_pair(y[:, :D_MODEL // 2], y[:, D_MODEL // 2:])


def _experts(bexp, bidx, bval, nused, next_exp, wslot, xs, wg, wu, wd):
    n_blocks = bexp.shape[0]
    xmap = lambda b, be, bi, bv, nu, nx, ws: (bi[b], 0)
    grid_spec = pltpu.PrefetchScalarGridSpec(
        num_scalar_prefetch=6,
        grid=(n_blocks,),
        in_specs=[pl.BlockSpec(memory_space=pl.ANY),
                  pl.BlockSpec(memory_space=pl.ANY), pl.BlockSpec(memory_space=pl.ANY),
                  pl.BlockSpec(memory_space=pl.ANY)],
        out_specs=pl.BlockSpec((SLOT_BLK, D_MODEL // 2), xmap),
        scratch_shapes=[pltpu.VMEM((XS_SLOTS, SLOT_BLK, D_MODEL // 2), U32),
                        pltpu.VMEM((2, D_MODEL, D_EXPERT), F32), pltpu.VMEM((2, D_MODEL, D_EXPERT), F32),
                        pltpu.VMEM((2, D_EXPERT, D_MODEL), F32),
                        pltpu.VMEM((D_MODEL, 2 * D_EXPERT), BF16), pltpu.VMEM((D_EXPERT, D_MODEL), BF16),
                        pltpu.SemaphoreType.DMA((XS_SLOTS,)), pltpu.SemaphoreType.DMA((2,))],
    )
    return pl.pallas_call(
        _experts_kernel,
        grid_spec=grid_spec,
        out_shape=jax.ShapeDtypeStruct(xs.shape, U32),
        compiler_params=_cparams("arbitrary"),
        name="experts",
    )(bexp, bidx, bval, nused, next_exp, wslot, xs, wg, wu, wd)


def _gather(slots3, y, n):
    n_chunks = slots3.shape[0]
    n_cores, n_sub = _sc_workers()
    per_worker = n_chunks // (n_cores * n_sub)
    assert per_worker * n_cores * n_sub == n_chunks

    @functools.partial(
        pl.kernel, mesh=_sc_mesh(),
        out_type=jax.ShapeDtypeStruct((TOP_K, n, D_MODEL // 2), U32),
        scratch_types=[pltpu.VMEM((TOP_K, SC_CHUNK), I32),
                       pltpu.VMEM((SC_CHUNK, D_MODEL // 2), U32), pltpu.VMEM((SC_CHUNK, D_MODEL // 2), U32),
                       pltpu.SemaphoreType.DMA, pltpu.SemaphoreType.DMA, pltpu.SemaphoreType.DMA],
        name="gather")
    def run(slots_hbm, y_hbm, g_hbm, idx_v, buf0, buf1, gsem, wsem0, wsem1):
        first = _worker_id(n_cores) * per_worker
        bufs = (buf0, buf1)
        wsems = (wsem0, wsem1)

        @pl.loop(0, per_worker)
        def _(j):
            c = first + j
            pltpu.sync_copy(slots_hbm.at[c], idx_v)
            gathers = [None] * TOP_K
            writes = [None] * TOP_K
            gathers[0] = pltpu.async_copy(y_hbm.at[idx_v.at[0]], bufs[0], gsem)
            for k in range(TOP_K):
                gathers[k].wait()
                if k >= 1:
                    writes[k - 1].wait()
                if k + 1 < TOP_K:
                    gathers[k + 1] = pltpu.async_copy(y_hbm.at[idx_v.at[k + 1]], bufs[(k + 1) % 2], gsem)
                writes[k] = pltpu.async_copy(bufs[k % 2], g_hbm.at[k, pl.ds(c * SC_CHUNK, SC_CHUNK)],
                                             wsems[k % 2])
            writes[TOP_K - 1].wait()

    return run(slots3, y)


def _combine_kernel(wt_ref, xo_ref, g_ref, *rest):
    o_ref = rest[-1]
    half = D_MODEL // 2
    acc_a = xo_ref[:, :half]
    acc_b = xo_ref[:, half:]
    w_all = wt_ref[...].T
    for k in range(TOP_K):
        ya, yb = _unpack_bf16_pair(g_ref[k])
        w = w_all[:, k:k + 1]
        acc_a = acc_a + w * ya
        acc_b = acc_b + w * yb
    o_ref[:, :half] = acc_a
    o_ref[:, half:] = acc_b


def _combine(wt, xo, g, part, n, out_so_far, tm=256):
    steps = g.shape[1] // tm
    in_specs = [pl.BlockSpec((TOP_K, tm), lambda i: (0, i)), pl.BlockSpec((tm, D_MODEL), lambda i: (i, 0)),
                pl.BlockSpec((TOP_K, tm, D_MODEL // 2), lambda i: (0, i, 0))]
    args = [wt, xo, g]
    aliases = {}
    if out_so_far is not None:
        in_specs.append(pl.BlockSpec(memory_space=pl.ANY))
        args.append(out_so_far)
        aliases = {len(args) - 1: 0}
    return pl.pallas_call(
        _combine_kernel,
        grid=(steps,),
        in_specs=in_specs,
        out_specs=pl.BlockSpec((tm, D_MODEL), lambda i: (part * steps + i, 0)),
        out_shape=jax.ShapeDtypeStruct((n, D_MODEL), F32),
        input_output_aliases=aliases,
        compiler_params=_cparams("parallel"),
        name="combine",
    )(*args)


def _layer(x, norm_mix_g, w_in, q_norm_g, k_norm_g, rpb, conv_dw_w, conv_dw_b, conv_ln_g, conv_ln_b,
           w_conv_out, w_attn_out, w_o, norm_ffn_g, w_router, router_bias, w_exp_gate, w_exp_up,
           w_exp_down, w_sh_gate, w_sh_up, w_sh_down):
    n = x.shape[0]
    c_qkv = 2 * D_CONV + 3 * D_ATTN
    w_qkv = w_in[:, :c_qkv].astype(BF16)
    w_gates = w_in[:, c_qkv:].astype(BF16)
    head_of = jnp.arange(D_ATTN // 2) // HEAD_DIM
    bsum = (head_of[:, None] == head_of[None, :]).astype(BF16)
    row = lambda v: v.reshape(1, -1).astype(F32)

    u, q, k, v = _inproj(x, row(norm_mix_g), w_qkv, bsum,
                         row(jnp.tile(q_norm_g, N_HEADS)), row(jnp.tile(k_norm_g, N_HEADS)))
    uc = _conv(u, conv_dw_w.reshape(CONV_WIDTH, D_CONV), row(conv_dw_b), row(conv_ln_g), row(conv_ln_b))
    at = _attn(q, k, v, _bias_table(rpb))

    wr_t = w_router.T
    wr_hi = wr_t.astype(BF16)
    wr_lo = (wr_t - wr_hi.astype(F32)).astype(BF16)
    wsgu = jnp.concatenate([w_sh_gate, w_sh_up], axis=1).astype(BF16)
    mix_weights = (row(norm_mix_g), w_gates, w_conv_out.astype(BF16), w_attn_out.astype(BF16), w_o.astype(BF16),
                   row(norm_ffn_g), wr_hi, wr_lo, router_bias.reshape(N_EXPERTS, 1).astype(F32), wsgu,
                   w_sh_down.astype(BF16))

    cap = n
    n_part = n // TOKEN_PARTS
    xs_ref = jax.empty_ref(jax.ShapeDtypeStruct((N_EXPERTS * cap, D_MODEL // 2), U32))
    cnt = jnp.zeros((N_EXPERTS, 128), F32)
    parts = []
    for part in range(TOKEN_PARTS):
        xo_p, hp_p, idx_p, rank_p, wt_p, cnt = _mix_route(x, uc, at, *mix_weights, cnt, part)
        slots_p = (idx_p * cap + rank_p).reshape(TOP_K, n_part // SC_CHUNK, SC_CHUNK).transpose(1, 0, 2)
        _dispatch(slots_p, hp_p, xs_ref)
        parts.append((xo_p, wt_p, slots_p))
    xs = jax.freeze(xs_ref)

    counts = cnt[:, 0].astype(I32)
    blocks_of = (counts + SLOT_BLK - 1) // SLOT_BLK
    blocks_end = jnp.cumsum(blocks_of)
    n_blocks = (n * TOP_K + N_EXPERTS * (SLOT_BLK - 1)) // SLOT_BLK
    n_used = blocks_end[-1]
    bseq = jnp.minimum(jnp.arange(n_blocks, dtype=I32), n_used - 1)
    bexp = jnp.minimum(jnp.sum((blocks_end[None, :] <= bseq[:, None]).astype(I32), axis=1), N_EXPERTS - 1)
    owner = (bexp[:, None] == jnp.arange(N_EXPERTS)[None, :]).astype(I32)
    local = bseq - jnp.sum(owner * (blocks_end - blocks_of)[None, :], axis=1)
    bidx = (bexp * (cap // SLOT_BLK) + local).astype(I32)
    bval = jnp.clip(jnp.sum(owner * counts[None, :], axis=1) - local * SLOT_BLK, 0, SLOT_BLK).astype(I32)
    eids = jnp.arange(N_EXPERTS, dtype=I32)
    later = (eids[None, :] > eids[:, None]) & (counts[None, :] > 0)
    next_exp = jnp.min(jnp.where(later, eids[None, :], N_EXPERTS), axis=1)
    next_exp = jnp.where(next_exp == N_EXPERTS, -1, next_exp).astype(I32)
    wslot = ((jnp.cumsum((counts > 0).astype(I32)) - 1) & 1).astype(I32)

    y = _experts(bexp, bidx, bval, n_used.reshape(1).astype(I32), next_exp, wslot, xs,
                 w_exp_gate, w_exp_up, w_exp_down)
    out = None
    for part, (xo_p, wt_p, slots_p) in enumerate(parts):
        out = _combine(wt_p, xo_p, _gather(slots_p, y, n_part), part, n, out)
    return out


def kernel(x, norm_mix_g, w_in, q_norm_g, k_norm_g, rpb, conv_dw_w, conv_dw_b, conv_ln_g, conv_ln_b, w_conv_out, w_attn_out, w_o, norm_ffn_g, w_router, router_bias, w_exp_gate, w_exp_up, w_exp_down, w_sh_gate, w_sh_up, w_sh_down):
    b, s, d = x.shape
    assert b == 1 and d == D_MODEL and s % (TOKEN_PARTS * 1024) == 0, x.shape
    xf = x.reshape(b * s, d)
    depth = norm_mix_g.shape[0]
    for l in range(depth):
        xf = _layer(xf, norm_mix_g[l], w_in[l], q_norm_g[l], k_norm_g[l], rpb[l], conv_dw_w[l], conv_dw_b[l],
                    conv_ln_g[l], conv_ln_b[l], w_conv_out[l], w_attn_out[l], w_o[l], norm_ffn_g[l],
                    w_router[l], router_bias[l], w_exp_gate[l], w_exp_up[l], w_exp_down[l], w_sh_gate[l],
                    w_sh_up[l], w_sh_down[l])
    return xf.reshape(b, s, d)
```

```python
import functools

import jax
import jax.numpy as jnp
from jax import lax
from jax.experimental import pallas as pl
from jax.experimental.pallas import tpu as pltpu
from jax.experimental.pallas import tpu_sc as plsc

D_MODEL = 1024
GRID_W = 64
D_CONV = 512
CONV_WIDTH = 31
CONV_PAD = CONV_WIDTH // 2
N_HEADS = 8
HEAD_DIM = 64
D_ATTN = N_HEADS * HEAD_DIM
WIN_H = 8
WIN_W = 16
N_EXPERTS = 64
TOP_K = 8
N_GROUPS = 8
GROUP_SIZE = N_EXPERTS // N_GROUPS
TOPK_GROUPS = 4
D_EXPERT = 256
D_SHARED = 256
ROUTE_SCALE = 2.5
EPS = 1e-6
LOG2E = 1.4426950408889634

SUBLANES = 8
NEG = -1e30
HALO = 16
ATT_ROWS = 4
ATT_BLK = ATT_ROWS * GRID_W
SLOT_BLK = 512
MIX_SPLIT = 2
XS_AHEAD = 3
XS_SLOTS = XS_AHEAD + 1
TOKEN_PARTS = 4
SC_CHUNK = 64
VMEM_LIMIT = 56 * 1024 * 1024

F32 = jnp.float32
BF16 = jnp.bfloat16
I32 = jnp.int32
U32 = jnp.uint32


def _cparams(*sem):
    return pltpu.CompilerParams(dimension_semantics=sem, vmem_limit_bytes=VMEM_LIMIT)


def _rms(x, g):
    return x * lax.rsqrt(jnp.mean(x * x, axis=-1, keepdims=True) + EPS) * g


def _pack_bf16_pair(a, b):
    def bf16_bits(v):
        return lax.bitcast_convert_type(v.astype(BF16).astype(F32), U32)
    return bf16_bits(a) | (bf16_bits(b) >> 16)


def _unpack_bf16_pair(w):
    a = lax.bitcast_convert_type(w & jnp.uint32(0xFFFF0000), F32)
    b = lax.bitcast_convert_type(w << 16, F32)
    return a, b


def _inproj_kernel(x_ref, g_ref, w_ref, bsum_ref, qg_ref, kg_ref, u_ref, q_ref, k_ref, v_ref):
    h = _rms(x_ref[...], g_ref[...]).astype(BF16)
    ag = jnp.dot(h, w_ref[:, 0:2 * D_CONV], preferred_element_type=F32)
    u_ref[...] = ag[:, :D_CONV] * jax.nn.sigmoid(ag[:, D_CONV:])
    bsum = bsum_ref[...]

    def head_norm(z, g):
        zz = (z * z).astype(BF16)
        hw = D_ATTN // 2
        ss = jnp.concatenate([jnp.dot(zz[:, :hw], bsum, preferred_element_type=F32),
                              jnp.dot(zz[:, hw:], bsum, preferred_element_type=F32)], axis=1)
        return z * lax.rsqrt(ss * (1.0 / HEAD_DIM) + EPS) * g

    c0 = 2 * D_CONV
    q = jnp.dot(h, w_ref[:, c0:c0 + D_ATTN], preferred_element_type=F32)
    q_ref[...] = (head_norm(q, qg_ref[...]) * (HEAD_DIM ** -0.5 * LOG2E)).astype(BF16)
    k = jnp.dot(h, w_ref[:, c0 + D_ATTN:c0 + 2 * D_ATTN], preferred_element_type=F32)
    k_ref[...] = head_norm(k, kg_ref[...]).astype(BF16)
    v = jnp.dot(h, w_ref[:, c0 + 2 * D_ATTN:c0 + 3 * D_ATTN], preferred_element_type=F32)
    v_ref[...] = v.astype(BF16)


def _inproj(x, g, w, bsum, qg, kg, tm=512):
    n = x.shape[0]
    wc = w.shape[1]
    full = lambda i: (0, 0)
    row = lambda i: (i, 0)
    return pl.pallas_call(
        _inproj_kernel,
        grid=(n // tm,),
        in_specs=[pl.BlockSpec((tm, D_MODEL), row), pl.BlockSpec((1, D_MODEL), full),
                  pl.BlockSpec((D_MODEL, wc), full), pl.BlockSpec(bsum.shape, full),
                  pl.BlockSpec((1, D_ATTN), full), pl.BlockSpec((1, D_ATTN), full)],
        out_specs=[pl.BlockSpec((tm, D_CONV), row), pl.BlockSpec((tm, D_ATTN), row),
                   pl.BlockSpec((tm, D_ATTN), row), pl.BlockSpec((tm, D_ATTN), row)],
        out_shape=[jax.ShapeDtypeStruct((n, D_CONV), F32), jax.ShapeDtypeStruct((n, D_ATTN), BF16),
                   jax.ShapeDtypeStruct((n, D_ATTN), BF16), jax.ShapeDtypeStruct((n, D_ATTN), BF16)],
        compiler_params=_cparams("parallel"),
        name="inproj",
    )(x, g, w, bsum, qg, kg)


def _conv_kernel(up_ref, uc_ref, un_ref, w_ref, b_ref, lg_ref, lb_ref, o_ref, ext_ref, sh_ref, *, tc, ch):
    i = pl.program_id(0)
    last = pl.num_programs(0) - 1
    ext_ref[0:HALO, :] = jnp.where(i > 0, up_ref[...], 0.0)
    ext_ref[HALO:HALO + tc, :] = uc_ref[...]
    ext_ref[HALO + tc:2 * HALO + tc, :] = jnp.where(i < last, un_ref[...], 0.0)
    span = sh_ref.shape[1]
    for b in range(SUBLANES):
        sh_ref[b] = ext_ref[b:b + span, :]
    bias = b_ref[...]
    lg = lg_ref[...]
    lb = lb_ref[...]
    for c in range(tc // ch):
        acc = jnp.broadcast_to(bias, (ch, D_CONV))
        for j in range(CONV_WIDTH):
            shift = HALO - CONV_PAD + j
            row0 = c * ch + shift // SUBLANES * SUBLANES
            acc = acc + sh_ref[shift % SUBLANES, row0:row0 + ch, :] * w_ref[j:j + 1, :]
        mu = jnp.mean(acc, axis=-1, keepdims=True)
        d = acc - mu
        var = jnp.mean(d * d, axis=-1, keepdims=True)
        y = d * lax.rsqrt(var + EPS) * lg + lb
        o_ref[c * ch:(c + 1) * ch, :] = (y * jax.nn.sigmoid(y)).astype(BF16)


def _conv(u, w, b, lg, lb, tc=256, ch=64):
    n = u.shape[0]
    hb = tc // HALO
    nh = n // HALO
    full = lambda i: (0, 0)
    return pl.pallas_call(
        functools.partial(_conv_kernel, tc=tc, ch=ch),
        grid=(n // tc,),
        in_specs=[pl.BlockSpec((HALO, D_CONV), lambda i: (jnp.maximum(i * hb - 1, 0), 0)),
                  pl.BlockSpec((tc, D_CONV), lambda i: (i, 0)),
                  pl.BlockSpec((HALO, D_CONV), lambda i: (jnp.minimum((i + 1) * hb, nh - 1), 0)),
                  pl.BlockSpec((CONV_WIDTH, D_CONV), full), pl.BlockSpec((1, D_CONV), full),
                  pl.BlockSpec((1, D_CONV), full), pl.BlockSpec((1, D_CONV), full)],
        out_specs=pl.BlockSpec((tc, D_CONV), lambda i: (i, 0)),
        out_shape=jax.ShapeDtypeStruct((n, D_CONV), BF16),
        scratch_shapes=[pltpu.VMEM((tc + 2 * HALO, D_CONV), F32),
                        pltpu.VMEM((SUBLANES, tc + 2 * HALO - SUBLANES, D_CONV), F32)],
        compiler_params=_cparams("parallel"),
        name="conv",
    )(u, u, u, w, b, lg, lb)


def _attn_kernel(q_ref, kp_ref, kc_ref, kn_ref, vp_ref, vc_ref, vn_ref, tb_ref, o_ref, *, rows):
    i = pl.program_id(0)
    nkey = 3 * ATT_BLK
    lrow = lax.broadcasted_iota(I32, (1, nkey), 1) >> 6
    lane = lax.broadcasted_iota(I32, (1, 2 * HEAD_DIM), 1)
    masks = []
    for j in range(ATT_ROWS):
        r = i * ATT_ROWS + j
        rs = jnp.clip(r - WIN_H // 2, 0, rows - WIN_H)
        lo = rs - (i - 1) * ATT_ROWS
        masks.append(jnp.where((lrow >= lo) & (lrow < lo + WIN_H), 0.0, NEG))
    dn = (((1,), (1,)), ((), ()))

    def pair_slice(h):
        return slice(2 * HEAD_DIM * (h // 2), 2 * HEAD_DIM * (h // 2 + 1))

    def in_half(h):
        return (lane >= HEAD_DIM * (h % 2)) & (lane < HEAD_DIM * (h % 2 + 1))

    def scores(h):
        sl = pair_slice(h)
        q2 = q_ref[:, sl]
        qm = jnp.where(in_half(h), q2, jnp.zeros_like(q2))
        s = jnp.concatenate([lax.dot_general(qm, kk[:, sl], dn, preferred_element_type=F32)
                             for kk in (kp_ref, kc_ref, kn_ref)], axis=1)
        parts = []
        for j in range(ATT_ROWS):
            bias = jnp.concatenate([tb_ref[h, 2 * m - j + 3] for m in range(nkey // (2 * GRID_W))], axis=1)
            parts.append(s[j * GRID_W:(j + 1) * GRID_W, :] + bias + masks[j])
        return jnp.concatenate(parts, axis=0)

    def softmax(s):
        mx = jnp.max(s, axis=-1, keepdims=True)
        e = jnp.exp2(s - mx)
        return e.astype(BF16), jnp.sum(e, axis=-1, keepdims=True)

    def values(h, pb, den):
        sl = pair_slice(h)
        o = sum(jnp.dot(pb[:, ATT_BLK * t:ATT_BLK * (t + 1)], vv[:, sl], preferred_element_type=F32)
                for t, vv in enumerate((vp_ref, vc_ref, vn_ref)))
        return o / den

    stage_s = {h: scores(h) for h in range(2)}
    stage_p = {0: softmax(stage_s.pop(0))}
    out_pair = None
    for h in range(N_HEADS):
        if h + 2 < N_HEADS:
            stage_s[h + 2] = scores(h + 2)
        if h + 1 < N_HEADS:
            stage_p[h + 1] = softmax(stage_s.pop(h + 1))
        o = values(h, *stage_p.pop(h))
        if h % 2 == 0:
            out_pair = o
        else:
            o_ref[:, pair_slice(h)] = jnp.where(in_half(h), o, out_pair).astype(BF16)


def _attn(q, k, v, tb):
    n = q.shape[0]
    rows = n // GRID_W
    nb = n // ATT_BLK
    cur = lambda i: (i, 0)
    prev = lambda i: (jnp.maximum(i - 1, 0), 0)
    nxt = lambda i: (jnp.minimum(i + 1, nb - 1), 0)
    blk = (ATT_BLK, D_ATTN)
    return pl.pallas_call(
        functools.partial(_attn_kernel, rows=rows),
        grid=(nb,),
        in_specs=[pl.BlockSpec(blk, cur), pl.BlockSpec(blk, prev), pl.BlockSpec(blk, cur), pl.BlockSpec(blk, nxt),
                  pl.BlockSpec(blk, prev), pl.BlockSpec(blk, cur), pl.BlockSpec(blk, nxt),
                  pl.BlockSpec(tb.shape, lambda i: (0, 0, 0, 0))],
        out_specs=pl.BlockSpec(blk, cur),
        out_shape=jax.ShapeDtypeStruct((n, D_ATTN), BF16),
        compiler_params=_cparams("parallel"),
        name="attn",
    )(q, k, k, k, v, v, v, tb)


def _bias_table(rpb):
    cols = jnp.arange(GRID_W)
    start = jnp.clip(cols - WIN_W // 2, 0, GRID_W - WIN_W)
    kc = cols[None, :]
    inwin = (kc >= start[:, None]) & (kc < start[:, None] + WIN_W)
    rel_idx = kc - cols[:, None] + (WIN_W - 1)
    pick = ((rel_idx[None] == jnp.arange(2 * WIN_W - 1)[:, None, None]) & inwin[None]).astype(F32)
    rel = jnp.einsum('hdj,jck->hdck', rpb.astype(F32), pick, precision=lax.Precision.HIGHEST)
    t = jnp.where(inwin[None, None], rel * LOG2E, NEG)
    t = jnp.concatenate([t, jnp.full_like(t[:, :1], NEG)], axis=1)
    t_next = jnp.concatenate([t[:, 1:], t[:, -1:]], axis=1)
    return jnp.concatenate([t, t_next], axis=-1).astype(F32)


def _mix_route_kernel(x_ref, uc_ref, at_ref, g1_ref, wg_ref, wco_ref, wao_ref, wo_ref, g2_ref,
                      wrh_ref, wrl_ref, rb_ref, tri_ref, cin_ref,
                      xo_ref, hp_ref, idx_ref, rank_ref, wt_ref, cnt_ref, carry_ref, *, tm):
    i = pl.program_id(0)

    @pl.when(i == 0)
    def _():
        carry_ref[...] = cin_ref[...]

    th = tm // MIX_SPLIT
    dn = (((1,), (1,)), ((), ()))

    def mix(p):
        rs = slice(p * th, (p + 1) * th)
        x = x_ref[rs, :]
        h = _rms(x, g1_ref[...]).astype(BF16)
        gates = jax.nn.sigmoid(jnp.dot(h, wg_ref[...], preferred_element_type=F32))
        y_conv = jnp.dot(uc_ref[rs, :], wco_ref[...], preferred_element_type=F32)
        y_attn = jnp.dot(at_ref[rs, :], wao_ref[...], preferred_element_type=F32)
        merged = gates[:, :D_MODEL] * y_conv + gates[:, D_MODEL:] * y_attn
        x1 = x + jnp.dot(merged.astype(BF16), wo_ref[...], preferred_element_type=F32)
        h2 = _rms(x1, g2_ref[...])
        xo_ref[rs, :] = x1
        hp_ref[rs, :] = _pack_bf16_pair(h2[:, :D_MODEL // 2], h2[:, D_MODEL // 2:])
        return (h2,)

    def router_logits(p, h2):
        h2b = h2.astype(BF16)
        h2l = (h2 - h2b.astype(F32)).astype(BF16)
        return (lax.dot_general(wrh_ref[...], h2b, dn, preferred_element_type=F32)
                + lax.dot_general(wrl_ref[...], h2b, dn, preferred_element_type=F32)
                + lax.dot_general(wrh_ref[...], h2l, dn, preferred_element_type=F32))

    def route(p, logits, carry):
        cs = slice(p * th, (p + 1) * th)
        scores = jax.nn.sigmoid(logits)
        biased = scores + rb_ref[...]

        sub = lax.broadcasted_iota(I32, (GROUP_SIZE, th), 0).astype(F32)
        groups, gscore = [], []
        for g in range(N_GROUPS):
            bg = biased[g * GROUP_SIZE:(g + 1) * GROUP_SIZE, :]
            m1 = jnp.max(bg, axis=0, keepdims=True)
            first = jnp.min(jnp.where(bg == m1, sub, float(GROUP_SIZE)), axis=0, keepdims=True)
            m2 = jnp.max(jnp.where(sub == first, -jnp.inf, bg), axis=0, keepdims=True)
            groups.append(bg)
            gscore.append(m1 + m2)
        masked = []
        for g in range(N_GROUPS):
            beaten = jnp.zeros((1, th), F32)
            for o in range(N_GROUPS):
                if o == g:
                    continue
                wins = (gscore[o] >= gscore[g]) if o < g else (gscore[o] > gscore[g])
                beaten = beaten + jnp.where(wins, 1.0, 0.0)
            masked.append(jnp.where(beaten < TOPK_GROUPS, groups[g], -jnp.inf))
        cur = jnp.concatenate(masked, axis=0)

        eid = lax.broadcasted_iota(I32, (N_EXPERTS, th), 0).astype(F32)
        idx_rows, sc_rows = [], []
        sel = jnp.zeros((N_EXPERTS, th), F32)
        for _ in range(TOP_K):
            m = jnp.max(cur, axis=0, keepdims=True)
            idx = jnp.min(jnp.where(cur == m, eid, float(N_EXPERTS)), axis=0, keepdims=True)
            hit = eid == idx
            sel = jnp.where(hit, 1.0, sel)
            cur = jnp.where(hit, -jnp.inf, cur)
            idx_rows.append(idx)
            sc_rows.append(jnp.sum(jnp.where(hit, scores, 0.0), axis=0, keepdims=True))
        wsum = sc_rows[0]
        for k in range(1, TOP_K):
            wsum = wsum + sc_rows[k]

        prefix = jnp.dot(sel.astype(BF16), tri_ref[...], preferred_element_type=F32)
        rank = carry + prefix
        rank_rows = [jnp.sum(jnp.where(eid == idx_rows[k], rank, 0.0), axis=0, keepdims=True)
                     for k in range(TOP_K)]
        idx_ref[:, cs] = jnp.concatenate(idx_rows, axis=0).astype(I32)
        rank_ref[:, cs] = jnp.concatenate(rank_rows, axis=0).astype(I32)
        wt_ref[:, cs] = jnp.concatenate([s / wsum * ROUTE_SCALE for s in sc_rows], axis=0)
        return carry + jnp.sum(sel, axis=1, keepdims=True)

    carry = carry_ref[:, 0:1]
    logits = router_logits(0, *mix(0))
    for p in range(MIX_SPLIT):
        if p + 1 < MIX_SPLIT:
            mixed_next = mix(p + 1)
        carry = route(p, logits, carry)
        if p + 1 < MIX_SPLIT:
            logits = router_logits(p + 1, *mixed_next)
    carry_ref[...] = jnp.broadcast_to(carry, carry_ref.shape)
    cnt_ref[...] = carry_ref[...]


def _mix_route(x, uc, at, g1, wg, wco, wao, wo, g2, wrh, wrl, rb, counts_in, part, tm=1024):
    n = x.shape[0] // TOKEN_PARTS
    steps = n // tm
    th = tm // MIX_SPLIT
    tri = (jnp.arange(th)[:, None] < jnp.arange(th)[None, :]).astype(BF16)
    src = lambda i: (part * steps + i, 0)
    row = lambda i: (i, 0)
    col = lambda i: (0, i)
    full = lambda i: (0, 0)
    ins = [x, uc, at, g1, wg, wco, wao, wo, g2, wrh, wrl, rb, tri, counts_in]
    in_specs = [pl.BlockSpec((tm, D_MODEL), src), pl.BlockSpec((tm, D_CONV), src), pl.BlockSpec((tm, D_ATTN), src)]
    in_specs += [pl.BlockSpec(a.shape, full) for a in ins[3:]]
    return pl.pallas_call(
        functools.partial(_mix_route_kernel, tm=tm),
        grid=(steps,),
        in_specs=in_specs,
        out_specs=[pl.BlockSpec((tm, D_MODEL), row), pl.BlockSpec((tm, D_MODEL // 2), row),
                   pl.BlockSpec((TOP_K, tm), col), pl.BlockSpec((TOP_K, tm), col), pl.BlockSpec((TOP_K, tm), col),
                   pl.BlockSpec((N_EXPERTS, 128), full)],
        out_shape=[jax.ShapeDtypeStruct((n, D_MODEL), F32), jax.ShapeDtypeStruct((n, D_MODEL // 2), U32),
                   jax.ShapeDtypeStruct((TOP_K, n), I32), jax.ShapeDtypeStruct((TOP_K, n), I32),
                   jax.ShapeDtypeStruct((TOP_K, n), F32), jax.ShapeDtypeStruct((N_EXPERTS, 128), F32)],
        scratch_shapes=[pltpu.VMEM((N_EXPERTS, 128), F32)],
        compiler_params=_cparams("arbitrary"),
        name="mix_route",
    )(*ins)


def _sc_workers():
    info = plsc.get_sparse_core_info()
    return info.num_cores, info.num_subcores


def _sc_mesh():
    return plsc.VectorSubcoreMesh(core_axis_name="c", subcore_axis_name="s")


def _worker_id(n_cores):
    return lax.axis_index("s") * n_cores + lax.axis_index("c")


def _dispatch(slots3, hp, xs_ref):
    n_chunks = slots3.shape[0]
    n_cores, n_sub = _sc_workers()
    per_worker = n_chunks // (n_cores * n_sub)
    assert per_worker * n_cores * n_sub == n_chunks

    @functools.partial(
        pl.kernel, mesh=_sc_mesh(), out_type=(),
        scratch_types=[pltpu.VMEM((TOP_K, SC_CHUNK), I32), pltpu.VMEM((SC_CHUNK, D_MODEL // 2), U32),
                       pltpu.SemaphoreType.DMA],
        name="dispatch")
    def run(slots_hbm, hp_hbm, xs_hbm, idx_v, rows_v, sem):
        first = _worker_id(n_cores) * per_worker

        @pl.loop(0, per_worker)
        def _(j):
            c = first + j
            pltpu.sync_copy(slots_hbm.at[c], idx_v)
            pltpu.sync_copy(hp_hbm.at[pl.ds(c * SC_CHUNK, SC_CHUNK)], rows_v)
            copies = [pltpu.async_copy(rows_v, xs_hbm.at[idx_v.at[k]], sem) for k in range(TOP_K)]
            for cp in copies:
                cp.wait()

    run(slots3, hp, xs_ref)


def _experts_kernel(bexp_ref, bidx_ref, bval_ref, nused_ref, next_ref, wslot_ref,
                    xs_ref, wg_ref, wu_ref, wd_ref, y_ref, xbuf, wg_buf, wu_buf, wd_buf, wgu_s, wd_s, xsem, wsem):
    b = pl.program_id(0)
    n_used = nused_ref[0]

    def block_copy(blk):
        slot = blk % XS_SLOTS
        rows = pl.ds(pl.multiple_of(bidx_ref[blk] * SLOT_BLK, SLOT_BLK), SLOT_BLK)
        return pltpu.make_async_copy(xs_ref.at[rows], xbuf.at[slot], xsem.at[slot])

    @pl.when(b == 0)
    def _():
        for blk in range(XS_AHEAD):
            @pl.when(blk < n_used)
            def _():
                block_copy(blk).start()

    @pl.when(b + XS_AHEAD < n_used)
    def _():
        block_copy(b + XS_AHEAD).start()

    def weight_copies(e, slot):
        return [pltpu.make_async_copy(wg_ref.at[e], wg_buf.at[slot], wsem.at[slot]),
                pltpu.make_async_copy(wu_ref.at[e], wu_buf.at[slot], wsem.at[slot]),
                pltpu.make_async_copy(wd_ref.at[e], wd_buf.at[slot], wsem.at[slot])]

    @pl.when(b < n_used)
    def _():
        e = bexp_ref[b]
        slot = wslot_ref[e]

        @pl.when(b == 0)
        def _():
            for cp in weight_copies(e, slot):
                cp.start()

        @pl.when((b == 0) | (bexp_ref[jnp.maximum(b - 1, 0)] != e))
        def _():
            for cp in weight_copies(e, slot):
                cp.wait()
            wgu_s[:, :D_EXPERT] = wg_buf[slot].astype(BF16)
            wgu_s[:, D_EXPERT:] = wu_buf[slot].astype(BF16)
            wd_s[...] = wd_buf[slot].astype(BF16)
            nxt = next_ref[e]

            @pl.when(nxt >= 0)
            def _():
                for cp in weight_copies(nxt, 1 - slot):
                    cp.start()

        srow = lax.broadcasted_iota(I32, (SLOT_BLK, 1), 0)
        block_copy(b).wait()
        xw = jnp.where(srow < bval_ref[b], xbuf[b % XS_SLOTS], jnp.uint32(0))
        xa, xb = _unpack_bf16_pair(xw)
        x = jnp.concatenate([xa.astype(BF16), xb.astype(BF16)], axis=1)
        gu = jnp.dot(x, wgu_s[...], preferred_element_type=F32)
        mid = (jax.nn.silu(gu[:, :D_EXPERT]) * gu[:, D_EXPERT:]).astype(BF16)
        y = jnp.dot(mid, wd_s[...], preferred_element_type=F32)
        y_ref[...] = _pack_bf16_pair(y[:, :D_MODEL // 2], y[:, D_MODEL // 2:])


def _experts(bexp, bidx, bval, nused, next_exp, wslot, xs, wg, wu, wd):
    n_blocks = bexp.shape[0]
    xmap = lambda b, be, bi, bv, nu, nx, ws: (bi[b], 0)
    grid_spec = pltpu.PrefetchScalarGridSpec(
        num_scalar_prefetch=6,
        grid=(n_blocks,),
        in_specs=[pl.BlockSpec(memory_space=pl.ANY),
                  pl.BlockSpec(memory_space=pl.ANY), pl.BlockSpec(memory_space=pl.ANY),
                  pl.BlockSpec(memory_space=pl.ANY)],
        out_specs=pl.BlockSpec((SLOT_BLK, D_MODEL // 2), xmap),
        scratch_shapes=[pltpu.VMEM((XS_SLOTS, SLOT_BLK, D_MODEL // 2), U32),
                        pltpu.VMEM((2, D_MODEL, D_EXPERT), F32), pltpu.VMEM((2, D_MODEL, D_EXPERT), F32),
                        pltpu.VMEM((2, D_EXPERT, D_MODEL), F32),
                        pltpu.VMEM((D_MODEL, 2 * D_EXPERT), BF16), pltpu.VMEM((D_EXPERT, D_MODEL), BF16),
                        pltpu.SemaphoreType.DMA((XS_SLOTS,)), pltpu.SemaphoreType.DMA((2,))],
    )
    return pl.pallas_call(
        _experts_kernel,
        grid_spec=grid_spec,
        out_shape=jax.ShapeDtypeStruct(xs.shape, U32),
        compiler_params=_cparams("arbitrary"),
        name="experts",
    )(bexp, bidx, bval, nused, next_exp, wslot, xs, wg, wu, wd)


def _gather(slots3, y, n):
    n_chunks = slots3.shape[0]
    n_cores, n_sub = _sc_workers()
    per_worker = n_chunks // (n_cores * n_sub)
    assert per_worker * n_cores * n_sub == n_chunks

    @functools.partial(
        pl.kernel, mesh=_sc_mesh(),
        out_type=jax.ShapeDtypeStruct((TOP_K, n, D_MODEL // 2), U32),
        scratch_types=[pltpu.VMEM((TOP_K, SC_CHUNK), I32),
                       pltpu.VMEM((SC_CHUNK, D_MODEL // 2), U32), pltpu.VMEM((SC_CHUNK, D_MODEL // 2), U32),
                       pltpu.SemaphoreType.DMA, pltpu.SemaphoreType.DMA, pltpu.SemaphoreType.DMA],
        name="gather")
    def run(slots_hbm, y_hbm, g_hbm, idx_v, buf0, buf1, gsem, wsem0, wsem1):
        first = _worker_id(n_cores) * per_worker
        bufs = (buf0, buf1)
        wsems = (wsem0, wsem1)

        @pl.loop(0, per_worker)
        def _(j):
            c = first + j
            pltpu.sync_copy(slots_hbm.at[c], idx_v)
            gathers = [None] * TOP_K
            writes = [None] * TOP_K
            gathers[0] = pltpu.async_copy(y_hbm.at[idx_v.at[0]], bufs[0], gsem)
            for k in range(TOP_K):
                gathers[k].wait()
                if k >= 1:
                    writes[k - 1].wait()
                if k + 1 < TOP_K:
                    gathers[k + 1] = pltpu.async_copy(y_hbm.at[idx_v.at[k + 1]], bufs[(k + 1) % 2], gsem)
                writes[k] = pltpu.async_copy(bufs[k % 2], g_hbm.at[k, pl.ds(c * SC_CHUNK, SC_CHUNK)],
                                             wsems[k % 2])
            writes[TOP_K - 1].wait()

    return run(slots3, y)


def _combine_kernel(wt_ref, x1_ref, hp_ref, g_ref, wsgu_ref, wsd_ref, *rest):
    o_ref = rest[-1]
    half = D_MODEL // 2
    ha, hb = _unpack_bf16_pair(hp_ref[...])
    h2b = jnp.concatenate([ha.astype(BF16), hb.astype(BF16)], axis=1)
    gu = jnp.dot(h2b, wsgu_ref[...], preferred_element_type=F32)
    mid = (jax.nn.silu(gu[:, :D_SHARED]) * gu[:, D_SHARED:]).astype(BF16)
    shared = jnp.dot(mid, wsd_ref[...], preferred_element_type=F32)
    acc_a = x1_ref[:, :half] + shared[:, :half]
    acc_b = x1_ref[:, half:] + shared[:, half:]
    w_all = wt_ref[...].T
    for k in range(TOP_K):
        ya, yb = _unpack_bf16_pair(g_ref[k])
        w = w_all[:, k:k + 1]
        acc_a = acc_a + w * ya
        acc_b = acc_b + w * yb
    o_ref[:, :half] = acc_a
    o_ref[:, half:] = acc_b


def _combine(wt, x1, hp, g, wsgu, wsd, part, n, out_so_far, tm=256):
    steps = g.shape[1] // tm
    full = lambda i: (0, 0)
    in_specs = [pl.BlockSpec((TOP_K, tm), lambda i: (0, i)), pl.BlockSpec((tm, D_MODEL), lambda i: (i, 0)),
                pl.BlockSpec((tm, D_MODEL // 2), lambda i: (i, 0)),
                pl.BlockSpec((TOP_K, tm, D_MODEL // 2), lambda i: (0, i, 0)),
                pl.BlockSpec(wsgu.shape, full), pl.BlockSpec(wsd.shape, full)]
    args = [wt, x1, hp, g, wsgu, wsd]
    aliases = {}
    if out_so_far is not None:
        in_specs.append(pl.BlockSpec(memory_space=pl.ANY))
        args.append(out_so_far)
        aliases = {len(args) - 1: 0}
    return pl.pallas_call(
        _combine_kernel,
        grid=(steps,),
        in_specs=in_specs,
        out_specs=pl.BlockSpec((tm, D_MODEL), lambda i: (part * steps + i, 0)),
        out_shape=jax.ShapeDtypeStruct((n, D_MODEL), F32),
        input_output_aliases=aliases,
        compiler_params=_cparams("parallel"),
        name="combine",
    )(*args)


def _layer(x, norm_mix_g, w_in, q_norm_g, k_norm_g, rpb, conv_dw_w, conv_dw_b, conv_ln_g, conv_ln_b,
           w_conv_out, w_attn_out, w_o, norm_ffn_g, w_router, router_bias, w_exp_gate, w_exp_up,
           w_exp_down, w_sh_gate, w_sh_up, w_sh_down):
    n = x.shape[0]
    c_qkv = 2 * D_CONV + 3 * D_ATTN
    w_qkv = w_in[:, :c_qkv].astype(BF16)
    w_gates = w_in[:, c_qkv:].astype(BF16)
    head_of = jnp.arange(D_ATTN // 2) // HEAD_DIM
    bsum = (head_of[:, None] == head_of[None, :]).astype(BF16)
    row = lambda v: v.reshape(1, -1).astype(F32)

    u, q, k, v = _inproj(x, row(norm_mix_g), w_qkv, bsum,
                         row(jnp.tile(q_norm_g, N_HEADS)), row(jnp.tile(k_norm_g, N_HEADS)))
    uc = _conv(u, conv_dw_w.reshape(CONV_WIDTH, D_CONV), row(conv_dw_b), row(conv_ln_g), row(conv_ln_b))
    at = _attn(q, k, v, _bias_table(rpb))

    wr_t = w_router.T
    wr_hi = wr_t.astype(BF16)
    wr_lo = (wr_t - wr_hi.astype(F32)).astype(BF16)
    wsgu = jnp.concatenate([w_sh_gate, w_sh_up], axis=1).astype(BF16)
    mix_weights = (row(norm_mix_g), w_gates, w_conv_out.astype(BF16), w_attn_out.astype(BF16), w_o.astype(BF16),
                   row(norm_ffn_g), wr_hi, wr_lo, router_bias.reshape(N_EXPERTS, 1).astype(F32))
    wsd = w_sh_down.astype(BF16)

    cap = n
    n_part = n // TOKEN_PARTS
    xs_ref = jax.empty_ref(jax.ShapeDtypeStruct((N_EXPERTS * cap, D_MODEL // 2), U32))
    cnt = jnp.zeros((N_EXPERTS, 128), F32)
    parts = []
    for part in range(TOKEN_PARTS):
        x1_p, hp_p, idx_p, rank_p, wt_p, cnt = _mix_route(x, uc, at, *mix_weights, cnt, part)
        slots_p = (idx_p * cap + rank_p).reshape(TOP_K, n_part // SC_CHUNK, SC_CHUNK).transpose(1, 0, 2)
        _dispatch(slots_p, hp_p, xs_ref)
        parts.append((x1_p, hp_p, wt_p, slots_p))
    xs = jax.freeze(xs_ref)

    counts = cnt[:, 0].astype(I32)
    blocks_of = (counts + SLOT_BLK - 1) // SLOT_BLK
    blocks_end = jnp.cumsum(blocks_of)
    n_blocks = (n * TOP_K + N_EXPERTS * (SLOT_BLK - 1)) // SLOT_BLK
    n_used = blocks_end[-1]
    bseq = jnp.minimum(jnp.arange(n_blocks, dtype=I32), n_used - 1)
    bexp = jnp.minimum(jnp.sum((blocks_end[None, :] <= bseq[:, None]).astype(I32), axis=1), N_EXPERTS - 1)
    owner = (bexp[:, None] == jnp.arange(N_EXPERTS)[None, :]).astype(I32)
    local = bseq - jnp.sum(owner * (blocks_end - blocks_of)[None, :], axis=1)
    bidx = (bexp * (cap // SLOT_BLK) + local).astype(I32)
    bval = jnp.clip(jnp.sum(owner * counts[None, :], axis=1) - local * SLOT_BLK, 0, SLOT_BLK).astype(I32)
    eids = jnp.arange(N_EXPERTS, dtype=I32)
    later = (eids[None, :] > eids[:, None]) & (counts[None, :] > 0)
    next_exp = jnp.min(jnp.where(later, eids[None, :], N_EXPERTS), axis=1)
    next_exp = jnp.where(next_exp == N_EXPERTS, -1, next_exp).astype(I32)
    wslot = ((jnp.cumsum((counts > 0).astype(I32)) - 1) & 1).astype(I32)

    y = _experts(bexp, bidx, bval, n_used.reshape(1).astype(I32), next_exp, wslot, xs,
                 w_exp_gate, w_exp_up, w_exp_down)
    out = None
    for part, (x1_p, hp_p, wt_p, slots_p) in enumerate(parts):
        out = _combine(wt_p, x1_p, hp_p, _gather(slots_p, y, n_part), wsgu, wsd, part, n, out)
    return out


def kernel(x, norm_mix_g, w_in, q_norm_g, k_norm_g, rpb, conv_dw_w, conv_dw_b, conv_ln_g, conv_ln_b, w_conv_out, w_attn_out, w_o, norm_ffn_g, w_router, router_bias, w_exp_gate, w_exp_up, w_exp_down, w_sh_gate, w_sh_up, w_sh_down):
    b, s, d = x.shape
    assert b == 1 and d == D_MODEL and s % (TOKEN_PARTS * 1024) == 0, x.shape
    xf = x.reshape(b * s, d)
    depth = norm_mix_g.shape[0]
    for l in range(depth):
        xf = _layer(xf, norm_mix_g[l], w_in[l], q_norm_g[l], k_norm_g[l], rpb[l], conv_dw_w[l], conv_dw_b[l],
                    conv_ln_g[l], conv_ln_b[l], w_conv_out[l], w_attn_out[l], w_o[l], norm_ffn_g[l],
                    w_router[l], router_bias[l], w_exp_gate[l], w_exp_up[l], w_exp_down[l], w_sh_gate[l],
                    w_sh_up[l], w_sh_down[l])
    return xf.reshape(b, s, d)
```

```python
import functools

import jax
import jax.numpy as jnp
from jax import lax
from jax.experimental import pallas as pl
from jax.experimental.pallas import tpu as pltpu
from jax.experimental.pallas import tpu_sc as plsc

D_MODEL = 1024
GRID_W = 64
D_CONV = 512
CONV_WIDTH = 31
CONV_PAD = CONV_WIDTH // 2
N_HEADS = 8
HEAD_DIM = 64
D_ATTN = N_HEADS * HEAD_DIM
WIN_H = 8
WIN_W = 16
N_EXPERTS = 64
TOP_K = 8
N_GROUPS = 8
GROUP_SIZE = N_EXPERTS // N_GROUPS
TOPK_GROUPS = 4
D_EXPERT = 256
D_SHARED = 256
ROUTE_SCALE = 2.5
EPS = 1e-6
LOG2E = 1.4426950408889634

SUBLANES = 8
NEG = -1e30
HALO = 16
ATT_ROWS = 4
ATT_BLK = ATT_ROWS * GRID_W
SLOT_BLK = 512
MIX_SPLIT = 2
XS_AHEAD = 3
XS_SLOTS = XS_AHEAD + 1
TOKEN_PARTS = 4
SC_CHUNK = 64
VMEM_LIMIT = 56 * 1024 * 1024

F32 = jnp.float32
BF16 = jnp.bfloat16
I32 = jnp.int32
U32 = jnp.uint32


def _cparams(*sem):
    return pltpu.CompilerParams(dimension_semantics=sem, vmem_limit_bytes=VMEM_LIMIT)


def _rms(x, g):
    return x * lax.rsqrt(jnp.mean(x * x, axis=-1, keepdims=True) + EPS) * g


def _pack_bf16_pair(a, b):
    def bf16_bits(v):
        return lax.bitcast_convert_type(v.astype(BF16).astype(F32), U32)
    return bf16_bits(a) | (bf16_bits(b) >> 16)


def _unpack_bf16_pair(w):
    a = lax.bitcast_convert_type(w & jnp.uint32(0xFFFF0000), F32)
    b = lax.bitcast_convert_type(w << 16, F32)
    return a, b


def _inproj_kernel(x_ref, g_ref, w_ref, bsum_ref, qg_ref, kg_ref, u_ref, q_ref, k_ref, v_ref):
    h = _rms(x_ref[...], g_ref[...]).astype(BF16)
    ag = jnp.dot(h, w_ref[:, 0:2 * D_CONV], preferred_element_type=F32)
    u_ref[...] = ag[:, :D_CONV] * jax.nn.sigmoid(ag[:, D_CONV:])
    bsum = bsum_ref[...]

    def head_norm(z, g):
        zz = (z * z).astype(BF16)
        hw = D_ATTN // 2
        ss = jnp.concatenate([jnp.dot(zz[:, :hw], bsum, preferred_element_type=F32),
                              jnp.dot(zz[:, hw:], bsum, preferred_element_type=F32)], axis=1)
        return z * lax.rsqrt(ss * (1.0 / HEAD_DIM) + EPS) * g

    c0 = 2 * D_CONV
    q = jnp.dot(h, w_ref[:, c0:c0 + D_ATTN], preferred_element_type=F32)
    q_ref[...] = (head_norm(q, qg_ref[...]) * (HEAD_DIM ** -0.5 * LOG2E)).astype(BF16)
    k = jnp.dot(h, w_ref[:, c0 + D_ATTN:c0 + 2 * D_ATTN], preferred_element_type=F32)
    k_ref[...] = head_norm(k, kg_ref[...]).astype(BF16)
    v = jnp.dot(h, w_ref[:, c0 + 2 * D_ATTN:c0 + 3 * D_ATTN], preferred_element_type=F32)
    v_ref[...] = v.astype(BF16)


def _inproj(x, g, w, bsum, qg, kg, tm=512):
    n = x.shape[0]
    wc = w.shape[1]
    full = lambda i: (0, 0)
    row = lambda i: (i, 0)
    return pl.pallas_call(
        _inproj_kernel,
        grid=(n // tm,),
        in_specs=[pl.BlockSpec((tm, D_MODEL), row), pl.BlockSpec((1, D_MODEL), full),
                  pl.BlockSpec((D_MODEL, wc), full), pl.BlockSpec(bsum.shape, full),
                  pl.BlockSpec((1, D_ATTN), full), pl.BlockSpec((1, D_ATTN), full)],
        out_specs=[pl.BlockSpec((tm, D_CONV), row), pl.BlockSpec((tm, D_ATTN), row),
                   pl.BlockSpec((tm, D_ATTN), row), pl.BlockSpec((tm, D_ATTN), row)],
        out_shape=[jax.ShapeDtypeStruct((n, D_CONV), F32), jax.ShapeDtypeStruct((n, D_ATTN), BF16),
                   jax.ShapeDtypeStruct((n, D_ATTN), BF16), jax.ShapeDtypeStruct((n, D_ATTN), BF16)],
        compiler_params=_cparams("parallel"),
        name="inproj",
    )(x, g, w, bsum, qg, kg)


def _conv_kernel(up_ref, uc_ref, un_ref, w_ref, b_ref, lg_ref, lb_ref, o_ref, ext_ref, sh_ref, *, tc, ch):
    i = pl.program_id(0)
    last = pl.num_programs(0) - 1
    ext_ref[0:HALO, :] = jnp.where(i > 0, up_ref[...], 0.0)
    ext_ref[HALO:HALO + tc, :] = uc_ref[...]
    ext_ref[HALO + tc:2 * HALO + tc, :] = jnp.where(i < last, un_ref[...], 0.0)
    span = sh_ref.shape[1]
    for b in range(SUBLANES):
        sh_ref[b] = ext_ref[b:b + span, :]
    bias = b_ref[...]
    lg = lg_ref[...]
    lb = lb_ref[...]
    for c in range(tc // ch):
        acc = jnp.broadcast_to(bias, (ch, D_CONV))
        for j in range(CONV_WIDTH):
            shift = HALO - CONV_PAD + j
            row0 = c * ch + shift // SUBLANES * SUBLANES
            acc = acc + sh_ref[shift % SUBLANES, row0:row0 + ch, :] * w_ref[j:j + 1, :]
        mu = jnp.mean(acc, axis=-1, keepdims=True)
        d = acc - mu
        var = jnp.mean(d * d, axis=-1, keepdims=True)
        y = d * lax.rsqrt(var + EPS) * lg + lb
        o_ref[c * ch:(c + 1) * ch, :] = (y * jax.nn.sigmoid(y)).astype(BF16)


def _conv(u, w, b, lg, lb, tc=512, ch=64):
    n = u.shape[0]
    hb = tc // HALO
    nh = n // HALO
    full = lambda i: (0, 0)
    return pl.pallas_call(
        functools.partial(_conv_kernel, tc=tc, ch=ch),
        grid=(n // tc,),
        in_specs=[pl.BlockSpec((HALO, D_CONV), lambda i: (jnp.maximum(i * hb - 1, 0), 0)),
                  pl.BlockSpec((tc, D_CONV), lambda i: (i, 0)),
                  pl.BlockSpec((HALO, D_CONV), lambda i: (jnp.minimum((i + 1) * hb, nh - 1), 0)),
                  pl.BlockSpec((CONV_WIDTH, D_CONV), full), pl.BlockSpec((1, D_CONV), full),
                  pl.BlockSpec((1, D_CONV), full), pl.BlockSpec((1, D_CONV), full)],
        out_specs=pl.BlockSpec((tc, D_CONV), lambda i: (i, 0)),
        out_shape=jax.ShapeDtypeStruct((n, D_CONV), BF16),
        scratch_shapes=[pltpu.VMEM((tc + 2 * HALO, D_CONV), F32),
                        pltpu.VMEM((SUBLANES, tc + 2 * HALO - SUBLANES, D_CONV), F32)],
        compiler_params=_cparams("parallel"),
        name="conv",
    )(u, u, u, w, b, lg, lb)


def _attn_kernel(q_ref, kp_ref, kc_ref, kn_ref, vp_ref, vc_ref, vn_ref, tb_ref, o_ref, *, rows):
    i = pl.program_id(0)
    nkey = 3 * ATT_BLK
    lrow = lax.broadcasted_iota(I32, (1, nkey), 1) >> 6
    lane = lax.broadcasted_iota(I32, (1, 2 * HEAD_DIM), 1)
    masks = []
    for j in range(ATT_ROWS):
        r = i * ATT_ROWS + j
        rs = jnp.clip(r - WIN_H // 2, 0, rows - WIN_H)
        lo = rs - (i - 1) * ATT_ROWS
        masks.append(jnp.where((lrow >= lo) & (lrow < lo + WIN_H), 0.0, NEG))
    dn = (((1,), (1,)), ((), ()))

    def pair_slice(h):
        return slice(2 * HEAD_DIM * (h // 2), 2 * HEAD_DIM * (h // 2 + 1))

    def in_half(h):
        return (lane >= HEAD_DIM * (h % 2)) & (lane < HEAD_DIM * (h % 2 + 1))

    def scores(h):
        sl = pair_slice(h)
        q2 = q_ref[:, sl]
        qm = jnp.where(in_half(h), q2, jnp.zeros_like(q2))
        s = jnp.concatenate([lax.dot_general(qm, kk[:, sl], dn, preferred_element_type=F32)
                             for kk in (kp_ref, kc_ref, kn_ref)], axis=1)
        parts = []
        for j in range(ATT_ROWS):
            bias = jnp.concatenate([tb_ref[h, 2 * m - j + 3] for m in range(nkey // (2 * GRID_W))], axis=1)
            parts.append(s[j * GRID_W:(j + 1) * GRID_W, :] + bias + masks[j])
        return jnp.concatenate(parts, axis=0)

    def softmax(s):
        mx = jnp.max(s, axis=-1, keepdims=True)
        e = jnp.exp2(s - mx)
        return e.astype(BF16), jnp.sum(e, axis=-1, keepdims=True)

    def values(h, pb, den):
        sl = pair_slice(h)
        o = sum(jnp.dot(pb[:, ATT_BLK * t:ATT_BLK * (t + 1)], vv[:, sl], preferred_element_type=F32)
                for t, vv in enumerate((vp_ref, vc_ref, vn_ref)))
        return o / den

    stage_s = {h: scores(h) for h in range(2)}
    stage_p = {0: softmax(stage_s.pop(0))}
    out_pair = None
    for h in range(N_HEADS):
        if h + 2 < N_HEADS:
            stage_s[h + 2] = scores(h + 2)
        if h + 1 < N_HEADS:
            stage_p[h + 1] = softmax(stage_s.pop(h + 1))
        o = values(h, *stage_p.pop(h))
        if h % 2 == 0:
            out_pair = o
        else:
            o_ref[:, pair_slice(h)] = jnp.where(in_half(h), o, out_pair).astype(BF16)


def _attn(q, k, v, tb):
    n = q.shape[0]
    rows = n // GRID_W
    nb = n // ATT_BLK
    cur = lambda i: (i, 0)
    prev = lambda i: (jnp.maximum(i - 1, 0), 0)
    nxt = lambda i: (jnp.minimum(i + 1, nb - 1), 0)
    blk = (ATT_BLK, D_ATTN)
    return pl.pallas_call(
        functools.partial(_attn_kernel, rows=rows),
        grid=(nb,),
        in_specs=[pl.BlockSpec(blk, cur), pl.BlockSpec(blk, prev), pl.BlockSpec(blk, cur), pl.BlockSpec(blk, nxt),
                  pl.BlockSpec(blk, prev), pl.BlockSpec(blk, cur), pl.BlockSpec(blk, nxt),
                  pl.BlockSpec(tb.shape, lambda i: (0, 0, 0, 0))],
        out_specs=pl.BlockSpec(blk, cur),
        out_shape=jax.ShapeDtypeStruct((n, D_ATTN), BF16),
        compiler_params=_cparams("parallel"),
        name="attn",
    )(q, k, k, k, v, v, v, tb)


def _bias_table(rpb):
    cols = jnp.arange(GRID_W)
    start = jnp.clip(cols - WIN_W // 2, 0, GRID_W - WIN_W)
    kc = cols[None, :]
    inwin = (kc >= start[:, None]) & (kc < start[:, None] + WIN_W)
    rel_idx = kc - cols[:, None] + (WIN_W - 1)
    pick = ((rel_idx[None] == jnp.arange(2 * WIN_W - 1)[:, None, None]) & inwin[None]).astype(F32)
    rel = jnp.einsum('hdj,jck->hdck', rpb.astype(F32), pick, precision=lax.Precision.HIGHEST)
    t = jnp.where(inwin[None, None], rel * LOG2E, NEG)
    t = jnp.concatenate([t, jnp.full_like(t[:, :1], NEG)], axis=1)
    t_next = jnp.concatenate([t[:, 1:], t[:, -1:]], axis=1)
    return jnp.concatenate([t, t_next], axis=-1).astype(F32)


def _mix_route_kernel(x_ref, uc_ref, at_ref, g1_ref, wg_ref, wco_ref, wao_ref, wo_ref, g2_ref,
                      wrh_ref, wrl_ref, rb_ref, tri_ref, cin_ref,
                      xo_ref, hp_ref, idx_ref, rank_ref, wt_ref, cnt_ref, carry_ref, *, tm):
    i = pl.program_id(0)

    @pl.when(i == 0)
    def _():
        carry_ref[...] = cin_ref[...]

    th = tm // MIX_SPLIT
    dn = (((1,), (1,)), ((), ()))

    def mix(p):
        rs = slice(p * th, (p + 1) * th)
        x = x_ref[rs, :]
        h = _rms(x, g1_ref[...]).astype(BF16)
        gates = jax.nn.sigmoid(jnp.dot(h, wg_ref[...], preferred_element_type=F32))
        y_conv = jnp.dot(uc_ref[rs, :], wco_ref[...], preferred_element_type=F32)
        y_attn = jnp.dot(at_ref[rs, :], wao_ref[...], preferred_element_type=F32)
        merged = gates[:, :D_MODEL] * y_conv + gates[:, D_MODEL:] * y_attn
        x1 = x + jnp.dot(merged.astype(BF16), wo_ref[...], preferred_element_type=F32)
        h2 = _rms(x1, g2_ref[...])
        xo_ref[rs, :] = x1
        hp_ref[rs, :] = _pack_bf16_pair(h2[:, :D_MODEL // 2], h2[:, D_MODEL // 2:])
        return (h2,)

    def router_logits(p, h2):
        h2b = h2.astype(BF16)
        h2l = (h2 - h2b.astype(F32)).astype(BF16)
        return (lax.dot_general(wrh_ref[...], h2b, dn, preferred_element_type=F32)
                + lax.dot_general(wrl_ref[...], h2b, dn, preferred_element_type=F32)
                + lax.dot_general(wrh_ref[...], h2l, dn, preferred_element_type=F32))

    def route(p, logits, carry):
        cs = slice(p * th, (p + 1) * th)
        scores = jax.nn.sigmoid(logits)
        biased = scores + rb_ref[...]

        sub = lax.broadcasted_iota(I32, (GROUP_SIZE, th), 0).astype(F32)
        groups, gscore = [], []
        for g in range(N_GROUPS):
            bg = biased[g * GROUP_SIZE:(g + 1) * GROUP_SIZE, :]
            m1 = jnp.max(bg, axis=0, keepdims=True)
            first = jnp.min(jnp.where(bg == m1, sub, float(GROUP_SIZE)), axis=0, keepdims=True)
            m2 = jnp.max(jnp.where(sub == first, -jnp.inf, bg), axis=0, keepdims=True)
            groups.append(bg)
            gscore.append(m1 + m2)
        masked = []
        for g in range(N_GROUPS):
            beaten = jnp.zeros((1, th), F32)
            for o in range(N_GROUPS):
                if o == g:
                    continue
                wins = (gscore[o] >= gscore[g]) if o < g else (gscore[o] > gscore[g])
                beaten = beaten + jnp.where(wins, 1.0, 0.0)
            masked.append(jnp.where(beaten < TOPK_GROUPS, groups[g], -jnp.inf))
        cur = jnp.concatenate(masked, axis=0)

        eid = lax.broadcasted_iota(I32, (N_EXPERTS, th), 0).astype(F32)
        idx_rows, sc_rows = [], []
        sel = jnp.zeros((N_EXPERTS, th), F32)
        for _ in range(TOP_K):
            m = jnp.max(cur, axis=0, keepdims=True)
            idx = jnp.min(jnp.where(cur == m, eid, float(N_EXPERTS)), axis=0, keepdims=True)
            hit = eid == idx
            sel = jnp.where(hit, 1.0, sel)
            cur = jnp.where(hit, -jnp.inf, cur)
            idx_rows.append(idx)
            sc_rows.append(jnp.sum(jnp.where(hit, scores, 0.0), axis=0, keepdims=True))
        wsum = sc_rows[0]
        for k in range(1, TOP_K):
            wsum = wsum + sc_rows[k]

        prefix = jnp.dot(sel.astype(BF16), tri_ref[...], preferred_element_type=F32)
        rank = carry + prefix
        rank_rows = [jnp.sum(jnp.where(eid == idx_rows[k], rank, 0.0), axis=0, keepdims=True)
                     for k in range(TOP_K)]
        idx_ref[:, cs] = jnp.concatenate(idx_rows, axis=0).astype(I32)
        rank_ref[:, cs] = jnp.concatenate(rank_rows, axis=0).astype(I32)
        wt_ref[:, cs] = jnp.concatenate([s / wsum * ROUTE_SCALE for s in sc_rows], axis=0)
        return carry + jnp.sum(sel, axis=1, keepdims=True)

    carry = carry_ref[:, 0:1]
    logits = router_logits(0, *mix(0))
    for p in range(MIX_SPLIT):
        if p + 1 < MIX_SPLIT:
            mixed_next = mix(p + 1)
        carry = route(p, logits, carry)
        if p + 1 < MIX_SPLIT:
            logits = router_logits(p + 1, *mixed_next)
    carry_ref[...] = jnp.broadcast_to(carry, carry_ref.shape)
    cnt_ref[...] = carry_ref[...]


def _mix_route(x, uc, at, g1, wg, wco, wao, wo, g2, wrh, wrl, rb, counts_in, part, tm=1024):
    n = x.shape[0] // TOKEN_PARTS
    steps = n // tm
    th = tm // MIX_SPLIT
    tri = (jnp.arange(th)[:, None] < jnp.arange(th)[None, :]).astype(BF16)
    src = lambda i: (part * steps + i, 0)
    row = lambda i: (i, 0)
    col = lambda i: (0, i)
    full = lambda i: (0, 0)
    ins = [x, uc, at, g1, wg, wco, wao, wo, g2, wrh, wrl, rb, tri, counts_in]
    in_specs = [pl.BlockSpec((tm, D_MODEL), src), pl.BlockSpec((tm, D_CONV), src), pl.BlockSpec((tm, D_ATTN), src)]
    in_specs += [pl.BlockSpec(a.shape, full) for a in ins[3:]]
    return pl.pallas_call(
        functools.partial(_mix_route_kernel, tm=tm),
        grid=(steps,),
        in_specs=in_specs,
        out_specs=[pl.BlockSpec((tm, D_MODEL), row), pl.BlockSpec((tm, D_MODEL // 2), row),
                   pl.BlockSpec((TOP_K, tm), col), pl.BlockSpec((TOP_K, tm), col), pl.BlockSpec((TOP_K, tm), col),
                   pl.BlockSpec((N_EXPERTS, 128), full)],
        out_shape=[jax.ShapeDtypeStruct((n, D_MODEL), F32), jax.ShapeDtypeStruct((n, D_MODEL // 2), U32),
                   jax.ShapeDtypeStruct((TOP_K, n), I32), jax.ShapeDtypeStruct((TOP_K, n), I32),
                   jax.ShapeDtypeStruct((TOP_K, n), F32), jax.ShapeDtypeStruct((N_EXPERTS, 128), F32)],
        scratch_shapes=[pltpu.VMEM((N_EXPERTS, 128), F32)],
        compiler_params=_cparams("arbitrary"),
        name="mix_route",
    )(*ins)


def _sc_workers():
    info = plsc.get_sparse_core_info()
    return info.num_cores, info.num_subcores


def _sc_mesh():
    return plsc.VectorSubcoreMesh(core_axis_name="c", subcore_axis_name="s")


def _worker_id(n_cores):
    return lax.axis_index("s") * n_cores + lax.axis_index("c")


def _dispatch(slots3, hp, xs_ref):
    n_chunks = slots3.shape[0]
    n_cores, n_sub = _sc_workers()
    per_worker = n_chunks // (n_cores * n_sub)
    assert per_worker * n_cores * n_sub == n_chunks

    @functools.partial(
        pl.kernel, mesh=_sc_mesh(), out_type=(),
        scratch_types=[pltpu.VMEM((TOP_K, SC_CHUNK), I32), pltpu.VMEM((SC_CHUNK, D_MODEL // 2), U32),
                       pltpu.SemaphoreType.DMA],
        name="dispatch")
    def run(slots_hbm, hp_hbm, xs_hbm, idx_v, rows_v, sem):
        first = _worker_id(n_cores) * per_worker

        @pl.loop(0, per_worker)
        def _(j):
            c = first + j
            pltpu.sync_copy(slots_hbm.at[c], idx_v)
            pltpu.sync_copy(hp_hbm.at[pl.ds(c * SC_CHUNK, SC_CHUNK)], rows_v)
            copies = [pltpu.async_copy(rows_v, xs_hbm.at[idx_v.at[k]], sem) for k in range(TOP_K)]
            for cp in copies:
                cp.wait()

    run(slots3, hp, xs_ref)


def _experts_kernel(bexp_ref, bidx_ref, bval_ref, nused_ref, next_ref, wslot_ref,
                    xs_ref, wg_ref, wu_ref, wd_ref, y_ref, xbuf, wg_buf, wu_buf, wd_buf, wgu_s, wd_s, xsem, wsem):
    b = pl.program_id(0)
    n_used = nused_ref[0]

    def block_copy(blk):
        slot = blk % XS_SLOTS
        rows = pl.ds(pl.multiple_of(bidx_ref[blk] * SLOT_BLK, SLOT_BLK), SLOT_BLK)
        return pltpu.make_async_copy(xs_ref.at[rows], xbuf.at[slot], xsem.at[slot])

    @pl.when(b == 0)
    def _():
        for blk in range(XS_AHEAD):
            @pl.when(blk < n_used)
            def _():
                block_copy(blk).start()

    @pl.when(b + XS_AHEAD < n_used)
    def _():
        block_copy(b + XS_AHEAD).start()

    def weight_copies(e, slot):
        return [pltpu.make_async_copy(wg_ref.at[e], wg_buf.at[slot], wsem.at[slot]),
                pltpu.make_async_copy(wu_ref.at[e], wu_buf.at[slot], wsem.at[slot]),
                pltpu.make_async_copy(wd_ref.at[e], wd_buf.at[slot], wsem.at[slot])]

    @pl.when(b < n_used)
    def _():
        e = bexp_ref[b]
        slot = wslot_ref[e]

        @pl.when(b == 0)
        def _():
            for cp in weight_copies(e, slot):
                cp.start()

        @pl.when((b == 0) | (bexp_ref[jnp.maximum(b - 1, 0)] != e))
        def _():
            for cp in weight_copies(e, slot):
                cp.wait()
            wgu_s[:, :D_EXPERT] = wg_buf[slot].astype(BF16)
            wgu_s[:, D_EXPERT:] = wu_buf[slot].astype(BF16)
            wd_s[...] = wd_buf[slot].astype(BF16)
            nxt = next_ref[e]

            @pl.when(nxt >= 0)
            def _():
                for cp in weight_copies(nxt, 1 - slot):
                    cp.start()

        srow = lax.broadcasted_iota(I32, (SLOT_BLK, 1), 0)
        block_copy(b).wait()
        xw = jnp.where(srow < bval_ref[b], xbuf[b % XS_SLOTS], jnp.uint32(0))
        xa, xb = _unpack_bf16_pair(xw)
        x = jnp.concatenate([xa.astype(BF16), xb.astype(BF16)], axis=1)
        gu = jnp.dot(x, wgu_s[...], preferred_element_type=F32)
        mid = (jax.nn.silu(gu[:, :D_EXPERT]) * gu[:, D_EXPERT:]).astype(BF16)
        y = jnp.dot(mid, wd_s[...], preferred_element_type=F32)
        y_ref[...] = _pack_bf16_pair(y[:, :D_MODEL // 2], y[:, D_MODEL // 2:])


def _experts(bexp, bidx, bval, nused, next_exp, wslot, xs, wg, wu, wd):
    n_blocks = bexp.shape[0]
    xmap = lambda b, be, bi, bv, nu, nx, ws: (bi[b], 0)
    grid_spec = pltpu.PrefetchScalarGridSpec(
        num_scalar_prefetch=6,
        grid=(n_blocks,),
        in_specs=[pl.BlockSpec(memory_space=pl.ANY),
                  pl.BlockSpec(memory_space=pl.ANY), pl.BlockSpec(memory_space=pl.ANY),
                  pl.BlockSpec(memory_space=pl.ANY)],
        out_specs=pl.BlockSpec((SLOT_BLK, D_MODEL // 2), xmap),
        scratch_shapes=[pltpu.VMEM((XS_SLOTS, SLOT_BLK, D_MODEL // 2), U32),
                        pltpu.VMEM((2, D_MODEL, D_EXPERT), F32), pltpu.VMEM((2, D_MODEL, D_EXPERT), F32),
                        pltpu.VMEM((2, D_EXPERT, D_MODEL), F32),
                        pltpu.VMEM((D_MODEL, 2 * D_EXPERT), BF16), pltpu.VMEM((D_EXPERT, D_MODEL), BF16),
                        pltpu.SemaphoreType.DMA((XS_SLOTS,)), pltpu.SemaphoreType.DMA((2,))],
    )
    return pl.pallas_call(
        _experts_kernel,
        grid_spec=grid_spec,
        out_shape=jax.ShapeDtypeStruct(xs.shape, U32),
        compiler_params=_cparams("arbitrary"),
        name="experts",
    )(bexp, bidx, bval, nused, next_exp, wslot, xs, wg, wu, wd)


def _gather(slots3, y, n):
    n_chunks = slots3.shape[0]
    n_cores, n_sub = _sc_workers()
    per_worker = n_chunks // (n_cores * n_sub)
    assert per_worker * n_cores * n_sub == n_chunks

    @functools.partial(
        pl.kernel, mesh=_sc_mesh(),
        out_type=jax.ShapeDtypeStruct((TOP_K, n, D_MODEL // 2), U32),
        scratch_types=[pltpu.VMEM((TOP_K, SC_CHUNK), I32),
                       pltpu.VMEM((SC_CHUNK, D_MODEL // 2), U32), pltpu.VMEM((SC_CHUNK, D_MODEL // 2), U32),
                       pltpu.SemaphoreType.DMA, pltpu.SemaphoreType.DMA, pltpu.SemaphoreType.DMA],
        name="gather")
    def run(slots_hbm, y_hbm, g_hbm, idx_v, buf0, buf1, gsem, wsem0, wsem1):
        first = _worker_id(n_cores) * per_worker
        bufs = (buf0, buf1)
        wsems = (wsem0, wsem1)

        @pl.loop(0, per_worker)
        def _(j):
            c = first + j
            pltpu.sync_copy(slots_hbm.at[c], idx_v)
            gathers = [None] * TOP_K
            writes = [None] * TOP_K
            gathers[0] = pltpu.async_copy(y_hbm.at[idx_v.at[0]], bufs[0], gsem)
            for k in range(TOP_K):
                gathers[k].wait()
                if k >= 1:
                    writes[k - 1].wait()
                if k + 1 < TOP_K:
                    gathers[k + 1] = pltpu.async_copy(y_hbm.at[idx_v.at[k + 1]], bufs[(k + 1) % 2], gsem)
                writes[k] = pltpu.async_copy(bufs[k % 2], g_hbm.at[k, pl.ds(c * SC_CHUNK, SC_CHUNK)],
                                             wsems[k % 2])
            writes[TOP_K - 1].wait()

    return run(slots3, y)


def _combine_kernel(wt_ref, x1_ref, hp_ref, g_ref, wsgu_ref, wsd_ref, *rest):
    o_ref = rest[-1]
    half = D_MODEL // 2
    ha, hb = _unpack_bf16_pair(hp_ref[...])
    h2b = jnp.concatenate([ha.astype(BF16), hb.astype(BF16)], axis=1)
    gu = jnp.dot(h2b, wsgu_ref[...], preferred_element_type=F32)
    mid = (jax.nn.silu(gu[:, :D_SHARED]) * gu[:, D_SHARED:]).astype(BF16)
    shared = jnp.dot(mid, wsd_ref[...], preferred_element_type=F32)
    acc_a = x1_ref[:, :half] + shared[:, :half]
    acc_b = x1_ref[:, half:] + shared[:, half:]
    w_all = wt_ref[...].T
    for k in range(TOP_K):
        ya, yb = _unpack_bf16_pair(g_ref[k])
        w = w_all[:, k:k + 1]
        acc_a = acc_a + w * ya
        acc_b = acc_b + w * yb
    o_ref[:, :half] = acc_a
    o_ref[:, half:] = acc_b


def _combine(wt, x1, hp, g, wsgu, wsd, part, n, out_so_far, tm=256):
    steps = g.shape[1] // tm
    full = lambda i: (0, 0)
    in_specs = [pl.BlockSpec((TOP_K, tm), lambda i: (0, i)), pl.BlockSpec((tm, D_MODEL), lambda i: (i, 0)),
                pl.BlockSpec((tm, D_MODEL // 2), lambda i: (i, 0)),
                pl.BlockSpec((TOP_K, tm, D_MODEL // 2), lambda i: (0, i, 0)),
                pl.BlockSpec(wsgu.shape, full), pl.BlockSpec(wsd.shape, full)]
    args = [wt, x1, hp, g, wsgu, wsd]
    aliases = {}
    if out_so_far is not None:
        in_specs.append(pl.BlockSpec(memory_space=pl.ANY))
        args.append(out_so_far)
        aliases = {len(args) - 1: 0}
    return pl.pallas_call(
        _combine_kernel,
        grid=(steps,),
        in_specs=in_specs,
        out_specs=pl.BlockSpec((tm, D_MODEL), lambda i: (part * steps + i, 0)),
        out_shape=jax.ShapeDtypeStruct((n, D_MODEL), F32),
        input_output_aliases=aliases,
        compiler_params=_cparams("parallel"),
        name="combine",
    )(*args)


def _layer(x, norm_mix_g, w_in, q_norm_g, k_norm_g, rpb, conv_dw_w, conv_dw_b, conv_ln_g, conv_ln_b,
           w_conv_out, w_attn_out, w_o, norm_ffn_g, w_router, router_bias, w_exp_gate, w_exp_up,
           w_exp_down, w_sh_gate, w_sh_up, w_sh_down):
    n = x.shape[0]
    c_qkv = 2 * D_CONV + 3 * D_ATTN
    w_qkv = w_in[:, :c_qkv].astype(BF16)
    w_gates = w_in[:, c_qkv:].astype(BF16)
    head_of = jnp.arange(D_ATTN // 2) // HEAD_DIM
    bsum = (head_of[:, None] == head_of[None, :]).astype(BF16)
    row = lambda v: v.reshape(1, -1).astype(F32)

    u, q, k, v = _inproj(x, row(norm_mix_g), w_qkv, bsum,
                         row(jnp.tile(q_norm_g, N_HEADS)), row(jnp.tile(k_norm_g, N_HEADS)))
    uc = _conv(u, conv_dw_w.reshape(CONV_WIDTH, D_CONV), row(conv_dw_b), row(conv_ln_g), row(conv_ln_b))
    at = _attn(q, k, v, _bias_table(rpb))

    wr_t = w_router.T
    wr_hi = wr_t.astype(BF16)
    wr_lo = (wr_t - wr_hi.astype(F32)).astype(BF16)
    wsgu = jnp.concatenate([w_sh_gate, w_sh_up], axis=1).astype(BF16)
    mix_weights = (row(norm_mix_g), w_gates, w_conv_out.astype(BF16), w_attn_out.astype(BF16), w_o.astype(BF16),
                   row(norm_ffn_g), wr_hi, wr_lo, router_bias.reshape(N_EXPERTS, 1).astype(F32))
    wsd = w_sh_down.astype(BF16)

    cap = n
    n_part = n // TOKEN_PARTS
    xs_ref = jax.empty_ref(jax.ShapeDtypeStruct((N_EXPERTS * cap, D_MODEL // 2), U32))
    cnt = jnp.zeros((N_EXPERTS, 128), F32)
    parts = []
    for part in range(TOKEN_PARTS):
        x1_p, hp_p, idx_p, rank_p, wt_p, cnt = _mix_route(x, uc, at, *mix_weights, cnt, part)
        slots_p = (idx_p * cap + rank_p).reshape(TOP_K, n_part // SC_CHUNK, SC_CHUNK).transpose(1, 0, 2)
        _dispatch(slots_p, hp_p, xs_ref)
        parts.append((x1_p, hp_p, wt_p, slots_p))
    xs = jax.freeze(xs_ref)

    counts = cnt[:, 0].astype(I32)
    blocks_of = (counts + SLOT_BLK - 1) // SLOT_BLK
    blocks_end = jnp.cumsum(blocks_of)
    n_blocks = (n * TOP_K + N_EXPERTS * (SLOT_BLK - 1)) // SLOT_BLK
    n_used = blocks_end[-1]
    bseq = jnp.minimum(jnp.arange(n_blocks, dtype=I32), n_used - 1)
    bexp = jnp.minimum(jnp.sum((blocks_end[None, :] <= bseq[:, None]).astype(I32), axis=1), N_EXPERTS - 1)
    owner = (bexp[:, None] == jnp.arange(N_EXPERTS)[None, :]).astype(I32)
    local = bseq - jnp.sum(owner * (blocks_end - blocks_of)[None, :], axis=1)
    bidx = (bexp * (cap // SLOT_BLK) + local).astype(I32)
    bval = jnp.clip(jnp.sum(owner * counts[None, :], axis=1) - local * SLOT_BLK, 0, SLOT_BLK).astype(I32)
    eids = jnp.arange(N_EXPERTS, dtype=I32)
    later = (eids[None, :] > eids[:, None]) & (counts[None, :] > 0)
    next_exp = jnp.min(jnp.where(later, eids[None, :], N_EXPERTS), axis=1)
    next_exp = jnp.where(next_exp == N_EXPERTS, -1, next_exp).astype(I32)
    wslot = ((jnp.cumsum((counts > 0).astype(I32)) - 1) & 1).astype(I32)

    y = _experts(bexp, bidx, bval, n_used.reshape(1).astype(I32), next_exp, wslot, xs,
                 w_exp_gate, w_exp_up, w_exp_down)
    out = None
    for part, (x1_p, hp_p, wt_p, slots_p) in enumerate(parts):
        out = _combine(wt_p, x1_p, hp_p, _gather(slots_p, y, n_part), wsgu, wsd, part, n, out)
    return out


def kernel(x, norm_mix_g, w_in, q_norm_g, k_norm_g, rpb, conv_dw_w, conv_dw_b, conv_ln_g, conv_ln_b, w_conv_out, w_attn_out, w_o, norm_ffn_g, w_router, router_bias, w_exp_gate, w_exp_up, w_exp_down, w_sh_gate, w_sh_up, w_sh_down):
    b, s, d = x.shape
    assert b == 1 and d == D_MODEL and s % (TOKEN_PARTS * 1024) == 0, x.shape
    xf = x.reshape(b * s, d)
    depth = norm_mix_g.shape[0]
    for l in range(depth):
        xf = _layer(xf, norm_mix_g[l], w_in[l], q_norm_g[l], k_norm_g[l], rpb[l], conv_dw_w[l], conv_dw_b[l],
                    conv_ln_g[l], conv_ln_b[l], w_conv_out[l], w_attn_out[l], w_o[l], norm_ffn_g[l],
                    w_router[l], router_bias[l], w_exp_gate[l], w_exp_up[l], w_exp_down[l], w_sh_gate[l],
                    w_sh_up[l], w_sh_down[l])
    return xf.reshape(b, s, d)
```

```python
import functools

import jax
import jax.numpy as jnp
from jax import lax
from jax.experimental import pallas as pl
from jax.experimental.pallas import tpu as pltpu
from jax.experimental.pallas import tpu_sc as plsc

D_MODEL = 1024
GRID_W = 64
D_CONV = 512
CONV_WIDTH = 31
CONV_PAD = CONV_WIDTH // 2
N_HEADS = 8
HEAD_DIM = 64
D_ATTN = N_HEADS * HEAD_DIM
WIN_H = 8
WIN_W = 16
N_EXPERTS = 64
TOP_K = 8
N_GROUPS = 8
GROUP_SIZE = N_EXPERTS // N_GROUPS
TOPK_GROUPS = 4
D_EXPERT = 256
D_SHARED = 256
ROUTE_SCALE = 2.5
EPS = 1e-6
LOG2E = 1.4426950408889634

SUBLANES = 8
NEG = -1e30
HALO = 16
ATT_ROWS = 4
ATT_BLK = ATT_ROWS * GRID_W
SLOT_BLK = 512
MIX_SPLIT = 2
XS_AHEAD = 3
XS_SLOTS = XS_AHEAD + 1
PART_SHARES = (2, 3, 2, 1)
PART_DENOM = 8
SC_CHUNK = 64
VMEM_LIMIT = 56 * 1024 * 1024

F32 = jnp.float32
BF16 = jnp.bfloat16
I32 = jnp.int32
U32 = jnp.uint32


def _cparams(*sem):
    return pltpu.CompilerParams(dimension_semantics=sem, vmem_limit_bytes=VMEM_LIMIT)


def _rms(x, g):
    return x * lax.rsqrt(jnp.mean(x * x, axis=-1, keepdims=True) + EPS) * g


def _pack_bf16_pair(a, b):
    def bf16_bits(v):
        return lax.bitcast_convert_type(v.astype(BF16).astype(F32), U32)
    return bf16_bits(a) | (bf16_bits(b) >> 16)


def _unpack_bf16_pair(w):
    a = lax.bitcast_convert_type(w & jnp.uint32(0xFFFF0000), F32)
    b = lax.bitcast_convert_type(w << 16, F32)
    return a, b


def _inproj_kernel(x_ref, g_ref, w_ref, bsum_ref, qg_ref, kg_ref, u_ref, q_ref, k_ref, v_ref):
    h = _rms(x_ref[...], g_ref[...]).astype(BF16)
    ag = jnp.dot(h, w_ref[:, 0:2 * D_CONV], preferred_element_type=F32)
    u_ref[...] = ag[:, :D_CONV] * jax.nn.sigmoid(ag[:, D_CONV:])
    bsum = bsum_ref[...]

    def head_norm(z, g):
        zz = (z * z).astype(BF16)
        hw = D_ATTN // 2
        ss = jnp.concatenate([jnp.dot(zz[:, :hw], bsum, preferred_element_type=F32),
                              jnp.dot(zz[:, hw:], bsum, preferred_element_type=F32)], axis=1)
        return z * lax.rsqrt(ss * (1.0 / HEAD_DIM) + EPS) * g

    c0 = 2 * D_CONV
    q = jnp.dot(h, w_ref[:, c0:c0 + D_ATTN], preferred_element_type=F32)
    q_ref[...] = (head_norm(q, qg_ref[...]) * (HEAD_DIM ** -0.5 * LOG2E)).astype(BF16)
    k = jnp.dot(h, w_ref[:, c0 + D_ATTN:c0 + 2 * D_ATTN], preferred_element_type=F32)
    k_ref[...] = head_norm(k, kg_ref[...]).astype(BF16)
    v = jnp.dot(h, w_ref[:, c0 + 2 * D_ATTN:c0 + 3 * D_ATTN], preferred_element_type=F32)
    v_ref[...] = v.astype(BF16)


def _inproj(x, g, w, bsum, qg, kg, tm=512):
    n = x.shape[0]
    wc = w.shape[1]
    full = lambda i: (0, 0)
    row = lambda i: (i, 0)
    return pl.pallas_call(
        _inproj_kernel,
        grid=(n // tm,),
        in_specs=[pl.BlockSpec((tm, D_MODEL), row), pl.BlockSpec((1, D_MODEL), full),
                  pl.BlockSpec((D_MODEL, wc), full), pl.BlockSpec(bsum.shape, full),
                  pl.BlockSpec((1, D_ATTN), full), pl.BlockSpec((1, D_ATTN), full)],
        out_specs=[pl.BlockSpec((tm, D_CONV), row), pl.BlockSpec((tm, D_ATTN), row),
                   pl.BlockSpec((tm, D_ATTN), row), pl.BlockSpec((tm, D_ATTN), row)],
        out_shape=[jax.ShapeDtypeStruct((n, D_CONV), F32), jax.ShapeDtypeStruct((n, D_ATTN), BF16),
                   jax.ShapeDtypeStruct((n, D_ATTN), BF16), jax.ShapeDtypeStruct((n, D_ATTN), BF16)],
        compiler_params=_cparams("parallel"),
        name="inproj",
    )(x, g, w, bsum, qg, kg)


def _conv_kernel(up_ref, uc_ref, un_ref, w_ref, b_ref, lg_ref, lb_ref, o_ref, ext_ref, sh_ref, *, tc, ch):
    i = pl.program_id(0)
    last = pl.num_programs(0) - 1
    ext_ref[0:HALO, :] = jnp.where(i > 0, up_ref[...], 0.0)
    ext_ref[HALO:HALO + tc, :] = uc_ref[...]
    ext_ref[HALO + tc:2 * HALO + tc, :] = jnp.where(i < last, un_ref[...], 0.0)
    span = sh_ref.shape[1]
    for b in range(SUBLANES):
        sh_ref[b] = ext_ref[b:b + span, :]
    bias = b_ref[...]
    lg = lg_ref[...]
    lb = lb_ref[...]
    for c in range(tc // ch):
        acc = jnp.broadcast_to(bias, (ch, D_CONV))
        for j in range(CONV_WIDTH):
            shift = HALO - CONV_PAD + j
            row0 = c * ch + shift // SUBLANES * SUBLANES
            acc = acc + sh_ref[shift % SUBLANES, row0:row0 + ch, :] * w_ref[j:j + 1, :]
        mu = jnp.mean(acc, axis=-1, keepdims=True)
        d = acc - mu
        var = jnp.mean(d * d, axis=-1, keepdims=True)
        y = d * lax.rsqrt(var + EPS) * lg + lb
        o_ref[c * ch:(c + 1) * ch, :] = (y * jax.nn.sigmoid(y)).astype(BF16)


def _conv(u, w, b, lg, lb, tc=512, ch=512):
    n = u.shape[0]
    hb = tc // HALO
    nh = n // HALO
    full = lambda i: (0, 0)
    return pl.pallas_call(
        functools.partial(_conv_kernel, tc=tc, ch=ch),
        grid=(n // tc,),
        in_specs=[pl.BlockSpec((HALO, D_CONV), lambda i: (jnp.maximum(i * hb - 1, 0), 0)),
                  pl.BlockSpec((tc, D_CONV), lambda i: (i, 0)),
                  pl.BlockSpec((HALO, D_CONV), lambda i: (jnp.minimum((i + 1) * hb, nh - 1), 0)),
                  pl.BlockSpec((CONV_WIDTH, D_CONV), full), pl.BlockSpec((1, D_CONV), full),
                  pl.BlockSpec((1, D_CONV), full), pl.BlockSpec((1, D_CONV), full)],
        out_specs=pl.BlockSpec((tc, D_CONV), lambda i: (i, 0)),
        out_shape=jax.ShapeDtypeStruct((n, D_CONV), BF16),
        scratch_shapes=[pltpu.VMEM((tc + 2 * HALO, D_CONV), F32),
                        pltpu.VMEM((SUBLANES, tc + 2 * HALO - SUBLANES, D_CONV), F32)],
        compiler_params=_cparams("parallel"),
        name="conv",
    )(u, u, u, w, b, lg, lb)


def _attn_kernel(q_ref, kp_ref, kc_ref, kn_ref, vp_ref, vc_ref, vn_ref, tb_ref, o_ref, *, rows):
    i = pl.program_id(0)
    nkey = 3 * ATT_BLK
    lrow = lax.broadcasted_iota(I32, (1, nkey), 1) >> 6
    lane = lax.broadcasted_iota(I32, (1, 2 * HEAD_DIM), 1)
    masks = []
    for j in range(ATT_ROWS):
        r = i * ATT_ROWS + j
        rs = jnp.clip(r - WIN_H // 2, 0, rows - WIN_H)
        lo = rs - (i - 1) * ATT_ROWS
        masks.append(jnp.where((lrow >= lo) & (lrow < lo + WIN_H), 0.0, NEG))
    dn = (((1,), (1,)), ((), ()))

    def pair_slice(h):
        return slice(2 * HEAD_DIM * (h // 2), 2 * HEAD_DIM * (h // 2 + 1))

    def in_half(h):
        return (lane >= HEAD_DIM * (h % 2)) & (lane < HEAD_DIM * (h % 2 + 1))

    def scores(h):
        sl = pair_slice(h)
        q2 = q_ref[:, sl]
        qm = jnp.where(in_half(h), q2, jnp.zeros_like(q2))
        s = jnp.concatenate([lax.dot_general(qm, kk[:, sl], dn, preferred_element_type=F32)
                             for kk in (kp_ref, kc_ref, kn_ref)], axis=1)
        parts = []
        for j in range(ATT_ROWS):
            bias = jnp.concatenate([tb_ref[h, 2 * m - j + 3] for m in range(nkey // (2 * GRID_W))], axis=1)
            parts.append(s[j * GRID_W:(j + 1) * GRID_W, :] + bias + masks[j])
        return jnp.concatenate(parts, axis=0)

    def softmax(s):
        mx = jnp.max(s, axis=-1, keepdims=True)
        e = jnp.exp2(s - mx)
        return e.astype(BF16), jnp.sum(e, axis=-1, keepdims=True)

    def values(h, pb, den):
        sl = pair_slice(h)
        o = sum(jnp.dot(pb[:, ATT_BLK * t:ATT_BLK * (t + 1)], vv[:, sl], preferred_element_type=F32)
                for t, vv in enumerate((vp_ref, vc_ref, vn_ref)))
        return o / den

    stage_s = {h: scores(h) for h in range(2)}
    stage_p = {0: softmax(stage_s.pop(0))}
    out_pair = None
    for h in range(N_HEADS):
        if h + 2 < N_HEADS:
            stage_s[h + 2] = scores(h + 2)
        if h + 1 < N_HEADS:
            stage_p[h + 1] = softmax(stage_s.pop(h + 1))
        o = values(h, *stage_p.pop(h))
        if h % 2 == 0:
            out_pair = o
        else:
            o_ref[:, pair_slice(h)] = jnp.where(in_half(h), o, out_pair).astype(BF16)


def _attn(q, k, v, tb):
    n = q.shape[0]
    rows = n // GRID_W
    nb = n // ATT_BLK
    cur = lambda i: (i, 0)
    prev = lambda i: (jnp.maximum(i - 1, 0), 0)
    nxt = lambda i: (jnp.minimum(i + 1, nb - 1), 0)
    blk = (ATT_BLK, D_ATTN)
    return pl.pallas_call(
        functools.partial(_attn_kernel, rows=rows),
        grid=(nb,),
        in_specs=[pl.BlockSpec(blk, cur), pl.BlockSpec(blk, prev), pl.BlockSpec(blk, cur), pl.BlockSpec(blk, nxt),
                  pl.BlockSpec(blk, prev), pl.BlockSpec(blk, cur), pl.BlockSpec(blk, nxt),
                  pl.BlockSpec(tb.shape, lambda i: (0, 0, 0, 0))],
        out_specs=pl.BlockSpec(blk, cur),
        out_shape=jax.ShapeDtypeStruct((n, D_ATTN), BF16),
        compiler_params=_cparams("parallel"),
        name="attn",
    )(q, k, k, k, v, v, v, tb)


def _bias_table(rpb):
    cols = jnp.arange(GRID_W)
    start = jnp.clip(cols - WIN_W // 2, 0, GRID_W - WIN_W)
    kc = cols[None, :]
    inwin = (kc >= start[:, None]) & (kc < start[:, None] + WIN_W)
    rel_idx = kc - cols[:, None] + (WIN_W - 1)
    pick = ((rel_idx[None] == jnp.arange(2 * WIN_W - 1)[:, None, None]) & inwin[None]).astype(F32)
    rel = jnp.einsum('hdj,jck->hdck', rpb.astype(F32), pick, precision=lax.Precision.HIGHEST)
    t = jnp.where(inwin[None, None], rel * LOG2E, NEG)
    t = jnp.concatenate([t, jnp.full_like(t[:, :1], NEG)], axis=1)
    t_next = jnp.concatenate([t[:, 1:], t[:, -1:]], axis=1)
    return jnp.concatenate([t, t_next], axis=-1).astype(F32)


def _mix_route_kernel(x_ref, uc_ref, at_ref, g1_ref, wg_ref, wco_ref, wao_ref, wo_ref, g2_ref,
                      wrh_ref, wrl_ref, rb_ref, tri_ref, cin_ref,
                      xo_ref, hp_ref, idx_ref, rank_ref, wt_ref, cnt_ref, carry_ref, *, tm):
    i = pl.program_id(0)

    @pl.when(i == 0)
    def _():
        carry_ref[...] = cin_ref[...]

    th = tm // MIX_SPLIT
    dn = (((1,), (1,)), ((), ()))

    def mix(p):
        rs = slice(p * th, (p + 1) * th)
        x = x_ref[rs, :]
        h = _rms(x, g1_ref[...]).astype(BF16)
        gates = jax.nn.sigmoid(jnp.dot(h, wg_ref[...], preferred_element_type=F32))
        y_conv = jnp.dot(uc_ref[rs, :], wco_ref[...], preferred_element_type=F32)
        y_attn = jnp.dot(at_ref[rs, :], wao_ref[...], preferred_element_type=F32)
        merged = gates[:, :D_MODEL] * y_conv + gates[:, D_MODEL:] * y_attn
        x1 = x + jnp.dot(merged.astype(BF16), wo_ref[...], preferred_element_type=F32)
        h2 = _rms(x1, g2_ref[...])
        xo_ref[rs, :] = x1
        hp_ref[rs, :] = _pack_bf16_pair(h2[:, :D_MODEL // 2], h2[:, D_MODEL // 2:])
        return (h2,)

    def router_logits(p, h2):
        h2b = h2.astype(BF16)
        h2l = (h2 - h2b.astype(F32)).astype(BF16)
        return (lax.dot_general(wrh_ref[...], h2b, dn, preferred_element_type=F32)
                + lax.dot_general(wrl_ref[...], h2b, dn, preferred_element_type=F32)
                + lax.dot_general(wrh_ref[...], h2l, dn, preferred_element_type=F32))

    def route(p, logits, carry):
        cs = slice(p * th, (p + 1) * th)
        scores = jax.nn.sigmoid(logits)
        biased = scores + rb_ref[...]

        sub = lax.broadcasted_iota(I32, (GROUP_SIZE, th), 0).astype(F32)
        groups, gscore = [], []
        for g in range(N_GROUPS):
            bg = biased[g * GROUP_SIZE:(g + 1) * GROUP_SIZE, :]
            m1 = jnp.max(bg, axis=0, keepdims=True)
            first = jnp.min(jnp.where(bg == m1, sub, float(GROUP_SIZE)), axis=0, keepdims=True)
            m2 = jnp.max(jnp.where(sub == first, -jnp.inf, bg), axis=0, keepdims=True)
            groups.append(bg)
            gscore.append(m1 + m2)
        masked = []
        for g in range(N_GROUPS):
            beaten = jnp.zeros((1, th), F32)
            for o in range(N_GROUPS):
                if o == g:
                    continue
                wins = (gscore[o] >= gscore[g]) if o < g else (gscore[o] > gscore[g])
                beaten = beaten + jnp.where(wins, 1.0, 0.0)
            masked.append(jnp.where(beaten < TOPK_GROUPS, groups[g], -jnp.inf))
        cur = jnp.concatenate(masked, axis=0)

        eid = lax.broadcasted_iota(I32, (N_EXPERTS, th), 0).astype(F32)
        idx_rows, sc_rows = [], []
        sel = jnp.zeros((N_EXPERTS, th), F32)
        for _ in range(TOP_K):
            m = jnp.max(cur, axis=0, keepdims=True)
            idx = jnp.min(jnp.where(cur == m, eid, float(N_EXPERTS)), axis=0, keepdims=True)
            hit = eid == idx
            sel = jnp.where(hit, 1.0, sel)
            cur = jnp.where(hit, -jnp.inf, cur)
            idx_rows.append(idx)
            sc_rows.append(jnp.sum(jnp.where(hit, scores, 0.0), axis=0, keepdims=True))
        wsum = sc_rows[0]
        for k in range(1, TOP_K):
            wsum = wsum + sc_rows[k]

        prefix = jnp.dot(sel.astype(BF16), tri_ref[...], preferred_element_type=F32)
        rank = carry + prefix
        rank_rows = [jnp.sum(jnp.where(eid == idx_rows[k], rank, 0.0), axis=0, keepdims=True)
                     for k in range(TOP_K)]
        idx_ref[:, cs] = jnp.concatenate(idx_rows, axis=0).astype(I32)
        rank_ref[:, cs] = jnp.concatenate(rank_rows, axis=0).astype(I32)
        wt_ref[:, cs] = jnp.concatenate([s / wsum * ROUTE_SCALE for s in sc_rows], axis=0)
        return carry + jnp.sum(sel, axis=1, keepdims=True)

    carry = carry_ref[:, 0:1]
    logits = router_logits(0, *mix(0))
    for p in range(MIX_SPLIT):
        if p + 1 < MIX_SPLIT:
            mixed_next = mix(p + 1)
        carry = route(p, logits, carry)
        if p + 1 < MIX_SPLIT:
            logits = router_logits(p + 1, *mixed_next)
    carry_ref[...] = jnp.broadcast_to(carry, carry_ref.shape)
    cnt_ref[...] = carry_ref[...]


def _mix_route(x, uc, at, g1, wg, wco, wao, wo, g2, wrh, wrl, rb, counts_in, start, n, tm=1024):
    steps = n // tm
    th = tm // MIX_SPLIT
    tri = (jnp.arange(th)[:, None] < jnp.arange(th)[None, :]).astype(BF16)
    src = lambda i: (start // tm + i, 0)
    row = lambda i: (i, 0)
    col = lambda i: (0, i)
    full = lambda i: (0, 0)
    ins = [x, uc, at, g1, wg, wco, wao, wo, g2, wrh, wrl, rb, tri, counts_in]
    in_specs = [pl.BlockSpec((tm, D_MODEL), src), pl.BlockSpec((tm, D_CONV), src), pl.BlockSpec((tm, D_ATTN), src)]
    in_specs += [pl.BlockSpec(a.shape, full) for a in ins[3:]]
    return pl.pallas_call(
        functools.partial(_mix_route_kernel, tm=tm),
        grid=(steps,),
        in_specs=in_specs,
        out_specs=[pl.BlockSpec((tm, D_MODEL), row), pl.BlockSpec((tm, D_MODEL // 2), row),
                   pl.BlockSpec((TOP_K, tm), col), pl.BlockSpec((TOP_K, tm), col), pl.BlockSpec((TOP_K, tm), col),
                   pl.BlockSpec((N_EXPERTS, 128), full)],
        out_shape=[jax.ShapeDtypeStruct((n, D_MODEL), F32), jax.ShapeDtypeStruct((n, D_MODEL // 2), U32),
                   jax.ShapeDtypeStruct((TOP_K, n), I32), jax.ShapeDtypeStruct((TOP_K, n), I32),
                   jax.ShapeDtypeStruct((TOP_K, n), F32), jax.ShapeDtypeStruct((N_EXPERTS, 128), F32)],
        scratch_shapes=[pltpu.VMEM((N_EXPERTS, 128), F32)],
        compiler_params=_cparams("arbitrary"),
        name="mix_route",
    )(*ins)


def _sc_workers():
    info = plsc.get_sparse_core_info()
    return info.num_cores, info.num_subcores


def _sc_mesh():
    return plsc.VectorSubcoreMesh(core_axis_name="c", subcore_axis_name="s")


def _worker_id(n_cores):
    return lax.axis_index("s") * n_cores + lax.axis_index("c")


def _dispatch(slots3, hp, xs_ref):
    n_chunks = slots3.shape[0]
    n_cores, n_sub = _sc_workers()
    per_worker = n_chunks // (n_cores * n_sub)
    assert per_worker * n_cores * n_sub == n_chunks

    @functools.partial(
        pl.kernel, mesh=_sc_mesh(), out_type=(),
        scratch_types=[pltpu.VMEM((TOP_K, SC_CHUNK), I32), pltpu.VMEM((SC_CHUNK, D_MODEL // 2), U32),
                       pltpu.SemaphoreType.DMA],
        name="dispatch")
    def run(slots_hbm, hp_hbm, xs_hbm, idx_v, rows_v, sem):
        first = _worker_id(n_cores) * per_worker

        @pl.loop(0, per_worker)
        def _(j):
            c = first + j
            pltpu.sync_copy(slots_hbm.at[c], idx_v)
            pltpu.sync_copy(hp_hbm.at[pl.ds(c * SC_CHUNK, SC_CHUNK)], rows_v)
            copies = [pltpu.async_copy(rows_v, xs_hbm.at[idx_v.at[k]], sem) for k in range(TOP_K)]
            for cp in copies:
                cp.wait()

    run(slots3, hp, xs_ref)


def _experts_kernel(bexp_ref, bidx_ref, bval_ref, nused_ref, next_ref, wslot_ref,
                    xs_ref, wg_ref, wu_ref, wd_ref, y_ref, xbuf, wg_buf, wu_buf, wd_buf, wgu_s, wd_s, xsem, wsem):
    b = pl.program_id(0)
    n_used = nused_ref[0]

    def block_copy(blk):
        slot = blk % XS_SLOTS
        rows = pl.ds(pl.multiple_of(bidx_ref[blk] * SLOT_BLK, SLOT_BLK), SLOT_BLK)
        return pltpu.make_async_copy(xs_ref.at[rows], xbuf.at[slot], xsem.at[slot])

    @pl.when(b == 0)
    def _():
        for blk in range(XS_AHEAD):
            @pl.when(blk < n_used)
            def _():
                block_copy(blk).start()

    @pl.when(b + XS_AHEAD < n_used)
    def _():
        block_copy(b + XS_AHEAD).start()

    def weight_copies(e, slot):
        return [pltpu.make_async_copy(wg_ref.at[e], wg_buf.at[slot], wsem.at[slot]),
                pltpu.make_async_copy(wu_ref.at[e], wu_buf.at[slot], wsem.at[slot]),
                pltpu.make_async_copy(wd_ref.at[e], wd_buf.at[slot], wsem.at[slot])]

    @pl.when(b < n_used)
    def _():
        e = bexp_ref[b]
        slot = wslot_ref[e]

        @pl.when(b == 0)
        def _():
            for cp in weight_copies(e, slot):
                cp.start()

        @pl.when((b == 0) | (bexp_ref[jnp.maximum(b - 1, 0)] != e))
        def _():
            for cp in weight_copies(e, slot):
                cp.wait()
            wgu_s[:, :D_EXPERT] = wg_buf[slot].astype(BF16)
            wgu_s[:, D_EXPERT:] = wu_buf[slot].astype(BF16)
            wd_s[...] = wd_buf[slot].astype(BF16)
            nxt = next_ref[e]

            @pl.when(nxt >= 0)
            def _():
                for cp in weight_copies(nxt, 1 - slot):
                    cp.start()

        srow = lax.broadcasted_iota(I32, (SLOT_BLK, 1), 0)
        block_copy(b).wait()
        xw = jnp.where(srow < bval_ref[b], xbuf[b % XS_SLOTS], jnp.uint32(0))
        xa, xb = _unpack_bf16_pair(xw)
        x = jnp.concatenate([xa.astype(BF16), xb.astype(BF16)], axis=1)
        gu = jnp.dot(x, wgu_s[...], preferred_element_type=F32)
        mid = (jax.nn.silu(gu[:, :D_EXPERT]) * gu[:, D_EXPERT:]).astype(BF16)
        y = jnp.dot(mid, wd_s[...], preferred_element_type=F32)
        y_ref[...] = _pack_bf16_pair(y[:, :D_MODEL // 2], y[:, D_MODEL // 2:])


def _experts(bexp, bidx, bval, nused, next_exp, wslot, xs, wg, wu, wd):
    n_blocks = bexp.shape[0]
    xmap = lambda b, be, bi, bv, nu, nx, ws: (bi[b], 0)
    grid_spec = pltpu.PrefetchScalarGridSpec(
        num_scalar_prefetch=6,
        grid=(n_blocks,),
        in_specs=[pl.BlockSpec(memory_space=pl.ANY),
                  pl.BlockSpec(memory_space=pl.ANY), pl.BlockSpec(memory_space=pl.ANY),
                  pl.BlockSpec(memory_space=pl.ANY)],
        out_specs=pl.BlockSpec((SLOT_BLK, D_MODEL // 2), xmap),
        scratch_shapes=[pltpu.VMEM((XS_SLOTS, SLOT_BLK, D_MODEL // 2), U32),
                        pltpu.VMEM((2, D_MODEL, D_EXPERT), F32), pltpu.VMEM((2, D_MODEL, D_EXPERT), F32),
                        pltpu.VMEM((2, D_EXPERT, D_MODEL), F32),
                        pltpu.VMEM((D_MODEL, 2 * D_EXPERT), BF16), pltpu.VMEM((D_EXPERT, D_MODEL), BF16),
                        pltpu.SemaphoreType.DMA((XS_SLOTS,)), pltpu.SemaphoreType.DMA((2,))],
    )
    return pl.pallas_call(
        _experts_kernel,
        grid_spec=grid_spec,
        out_shape=jax.ShapeDtypeStruct(xs.shape, U32),
        compiler_params=_cparams("arbitrary"),
        name="experts",
    )(bexp, bidx, bval, nused, next_exp, wslot, xs, wg, wu, wd)


def _gather(slots3, y, n):
    n_chunks = slots3.shape[0]
    n_cores, n_sub = _sc_workers()
    per_worker = n_chunks // (n_cores * n_sub)
    assert per_worker * n_cores * n_sub == n_chunks

    @functools.partial(
        pl.kernel, mesh=_sc_mesh(),
        out_type=jax.ShapeDtypeStruct((TOP_K, n, D_MODEL // 2), U32),
        scratch_types=[pltpu.VMEM((TOP_K, SC_CHUNK), I32),
                       pltpu.VMEM((SC_CHUNK, D_MODEL // 2), U32), pltpu.VMEM((SC_CHUNK, D_MODEL // 2), U32),
                       pltpu.SemaphoreType.DMA, pltpu.SemaphoreType.DMA, pltpu.SemaphoreType.DMA],
        name="gather")
    def run(slots_hbm, y_hbm, g_hbm, idx_v, buf0, buf1, gsem, wsem0, wsem1):
        first = _worker_id(n_cores) * per_worker
        bufs = (buf0, buf1)
        wsems = (wsem0, wsem1)

        @pl.loop(0, per_worker)
        def _(j):
            c = first + j
            pltpu.sync_copy(slots_hbm.at[c], idx_v)
            gathers = [None] * TOP_K
            writes = [None] * TOP_K
            gathers[0] = pltpu.async_copy(y_hbm.at[idx_v.at[0]], bufs[0], gsem)
            for k in range(TOP_K):
                gathers[k].wait()
                if k >= 1:
                    writes[k - 1].wait()
                if k + 1 < TOP_K:
                    gathers[k + 1] = pltpu.async_copy(y_hbm.at[idx_v.at[k + 1]], bufs[(k + 1) % 2], gsem)
                writes[k] = pltpu.async_copy(bufs[k % 2], g_hbm.at[k, pl.ds(c * SC_CHUNK, SC_CHUNK)],
                                             wsems[k % 2])
            writes[TOP_K - 1].wait()

    return run(slots3, y)


def _combine_kernel(wt_ref, x1_ref, hp_ref, g_ref, wsgu_ref, wsd_ref, *rest):
    o_ref = rest[-1]
    half = D_MODEL // 2
    ha, hb = _unpack_bf16_pair(hp_ref[...])
    h2b = jnp.concatenate([ha.astype(BF16), hb.astype(BF16)], axis=1)
    gu = jnp.dot(h2b, wsgu_ref[...], preferred_element_type=F32)
    mid = (jax.nn.silu(gu[:, :D_SHARED]) * gu[:, D_SHARED:]).astype(BF16)
    shared = jnp.dot(mid, wsd_ref[...], preferred_element_type=F32)
    acc_a = x1_ref[:, :half] + shared[:, :half]
    acc_b = x1_ref[:, half:] + shared[:, half:]
    w_all = wt_ref[...].T
    for k in range(TOP_K):
        ya, yb = _unpack_bf16_pair(g_ref[k])
        w = w_all[:, k:k + 1]
        acc_a = acc_a + w * ya
        acc_b = acc_b + w * yb
    o_ref[:, :half] = acc_a
    o_ref[:, half:] = acc_b


def _combine(wt, x1, hp, g, wsgu, wsd, start, n, out_so_far, tm=256):
    steps = g.shape[1] // tm
    full = lambda i: (0, 0)
    in_specs = [pl.BlockSpec((TOP_K, tm), lambda i: (0, i)), pl.BlockSpec((tm, D_MODEL), lambda i: (i, 0)),
                pl.BlockSpec((tm, D_MODEL // 2), lambda i: (i, 0)),
                pl.BlockSpec((TOP_K, tm, D_MODEL // 2), lambda i: (0, i, 0)),
                pl.BlockSpec(wsgu.shape, full), pl.BlockSpec(wsd.shape, full)]
    args = [wt, x1, hp, g, wsgu, wsd]
    aliases = {}
    if out_so_far is not None:
        in_specs.append(pl.BlockSpec(memory_space=pl.ANY))
        args.append(out_so_far)
        aliases = {len(args) - 1: 0}
    return pl.pallas_call(
        _combine_kernel,
        grid=(steps,),
        in_specs=in_specs,
        out_specs=pl.BlockSpec((tm, D_MODEL), lambda i: (start // tm + i, 0)),
        out_shape=jax.ShapeDtypeStruct((n, D_MODEL), F32),
        input_output_aliases=aliases,
        compiler_params=_cparams("parallel"),
        name="combine",
    )(*args)


def _layer(x, norm_mix_g, w_in, q_norm_g, k_norm_g, rpb, conv_dw_w, conv_dw_b, conv_ln_g, conv_ln_b,
           w_conv_out, w_attn_out, w_o, norm_ffn_g, w_router, router_bias, w_exp_gate, w_exp_up,
           w_exp_down, w_sh_gate, w_sh_up, w_sh_down):
    n = x.shape[0]
    c_qkv = 2 * D_CONV + 3 * D_ATTN
    w_qkv = w_in[:, :c_qkv].astype(BF16)
    w_gates = w_in[:, c_qkv:].astype(BF16)
    head_of = jnp.arange(D_ATTN // 2) // HEAD_DIM
    bsum = (head_of[:, None] == head_of[None, :]).astype(BF16)
    row = lambda v: v.reshape(1, -1).astype(F32)

    u, q, k, v = _inproj(x, row(norm_mix_g), w_qkv, bsum,
                         row(jnp.tile(q_norm_g, N_HEADS)), row(jnp.tile(k_norm_g, N_HEADS)))
    uc = _conv(u, conv_dw_w.reshape(CONV_WIDTH, D_CONV), row(conv_dw_b), row(conv_ln_g), row(conv_ln_b))
    at = _attn(q, k, v, _bias_table(rpb))

    wr_t = w_router.T
    wr_hi = wr_t.astype(BF16)
    wr_lo = (wr_t - wr_hi.astype(F32)).astype(BF16)
    wsgu = jnp.concatenate([w_sh_gate, w_sh_up], axis=1).astype(BF16)
    mix_weights = (row(norm_mix_g), w_gates, w_conv_out.astype(BF16), w_attn_out.astype(BF16), w_o.astype(BF16),
                   row(norm_ffn_g), wr_hi, wr_lo, router_bias.reshape(N_EXPERTS, 1).astype(F32))
    wsd = w_sh_down.astype(BF16)

    cap = n
    sizes = [n * f // PART_DENOM for f in PART_SHARES]
    xs_ref = jax.empty_ref(jax.ShapeDtypeStruct((N_EXPERTS * cap, D_MODEL // 2), U32))
    cnt = jnp.zeros((N_EXPERTS, 128), F32)
    parts = []
    start = 0
    for size in sizes:
        x1_p, hp_p, idx_p, rank_p, wt_p, cnt = _mix_route(x, uc, at, *mix_weights, cnt, start, size)
        slots_p = (idx_p * cap + rank_p).reshape(TOP_K, size // SC_CHUNK, SC_CHUNK).transpose(1, 0, 2)
        _dispatch(slots_p, hp_p, xs_ref)
        parts.append((start, size, x1_p, hp_p, wt_p, slots_p))
        start += size
    xs = jax.freeze(xs_ref)

    counts = cnt[:, 0].astype(I32)
    blocks_of = (counts + SLOT_BLK - 1) // SLOT_BLK
    blocks_end = jnp.cumsum(blocks_of)
    n_blocks = (n * TOP_K + N_EXPERTS * (SLOT_BLK - 1)) // SLOT_BLK
    n_used = blocks_end[-1]
    bseq = jnp.minimum(jnp.arange(n_blocks, dtype=I32), n_used - 1)
    bexp = jnp.minimum(jnp.sum((blocks_end[None, :] <= bseq[:, None]).astype(I32), axis=1), N_EXPERTS - 1)
    owner = (bexp[:, None] == jnp.arange(N_EXPERTS)[None, :]).astype(I32)
    local = bseq - jnp.sum(owner * (blocks_end - blocks_of)[None, :], axis=1)
    bidx = (bexp * (cap // SLOT_BLK) + local).astype(I32)
    bval = jnp.clip(jnp.sum(owner * counts[None, :], axis=1) - local * SLOT_BLK, 0, SLOT_BLK).astype(I32)
    eids = jnp.arange(N_EXPERTS, dtype=I32)
    later = (eids[None, :] > eids[:, None]) & (counts[None, :] > 0)
    next_exp = jnp.min(jnp.where(later, eids[None, :], N_EXPERTS), axis=1)
    next_exp = jnp.where(next_exp == N_EXPERTS, -1, next_exp).astype(I32)
    wslot = ((jnp.cumsum((counts > 0).astype(I32)) - 1) & 1).astype(I32)

    y = _experts(bexp, bidx, bval, n_used.reshape(1).astype(I32), next_exp, wslot, xs,
                 w_exp_gate, w_exp_up, w_exp_down)
    out = None
    for start, size, x1_p, hp_p, wt_p, slots_p in parts[-1:] + parts[:-1]:
        out = _combine(wt_p, x1_p, hp_p, _gather(slots_p, y, size), wsgu, wsd, start, n, out)
    return out


def kernel(x, norm_mix_g, w_in, q_norm_g, k_norm_g, rpb, conv_dw_w, conv_dw_b, conv_ln_g, conv_ln_b, w_conv_out, w_attn_out, w_o, norm_ffn_g, w_router, router_bias, w_exp_gate, w_exp_up, w_exp_down, w_sh_gate, w_sh_up, w_sh_down):
    b, s, d = x.shape
    assert b == 1 and d == D_MODEL and s % (PART_DENOM * 2048) == 0, x.shape
    xf = x.reshape(b * s, d)
    depth = norm_mix_g.shape[0]
    for l in range(depth):
        xf = _layer(xf, norm_mix_g[l], w_in[l], q_norm_g[l], k_norm_g[l], rpb[l], conv_dw_w[l], conv_dw_b[l],
                    conv_ln_g[l], conv_ln_b[l], w_conv_out[l], w_attn_out[l], w_o[l], norm_ffn_g[l],
                    w_router[l], router_bias[l], w_exp_gate[l], w_exp_up[l], w_exp_down[l], w_sh_gate[l],
                    w_sh_up[l], w_sh_down[l])
    return xf.reshape(b, s, d)
```

```python
import functools

import jax
import jax.numpy as jnp
from jax import lax
from jax.experimental import pallas as pl
from jax.experimental.pallas import tpu as pltpu
from jax.experimental.pallas import tpu_sc as plsc

D_MODEL = 1024
GRID_W = 64
D_CONV = 512
CONV_WIDTH = 31
CONV_PAD = CONV_WIDTH // 2
N_HEADS = 8
HEAD_DIM = 64
D_ATTN = N_HEADS * HEAD_DIM
WIN_H = 8
WIN_W = 16
N_EXPERTS = 64
TOP_K = 8
N_GROUPS = 8
GROUP_SIZE = N_EXPERTS // N_GROUPS
TOPK_GROUPS = 4
D_EXPERT = 256
D_SHARED = 256
ROUTE_SCALE = 2.5
EPS = 1e-6
LOG2E = 1.4426950408889634

SUBLANES = 8
NEG = -1e30
HALO = 16
ATT_ROWS = 4
ATT_BLK = ATT_ROWS * GRID_W
SLOT_BLK = 512
MIX_SPLIT = 2
XS_AHEAD = 3
XS_SLOTS = XS_AHEAD + 1
PART_SHARES = (2, 3, 2, 1)
PART_DENOM = 8
SC_CHUNK = 64
VMEM_LIMIT = 56 * 1024 * 1024

F32 = jnp.float32
BF16 = jnp.bfloat16
I32 = jnp.int32
U32 = jnp.uint32


def _cparams(*sem):
    return pltpu.CompilerParams(dimension_semantics=sem, vmem_limit_bytes=VMEM_LIMIT)


def _rms(x, g):
    return x * lax.rsqrt(jnp.mean(x * x, axis=-1, keepdims=True) + EPS) * g


def _pack_bf16_pair(a, b):
    def bf16_bits(v):
        return lax.bitcast_convert_type(v.astype(BF16).astype(F32), U32)
    return bf16_bits(a) | (bf16_bits(b) >> 16)


def _unpack_bf16_pair(w):
    a = lax.bitcast_convert_type(w & jnp.uint32(0xFFFF0000), F32)
    b = lax.bitcast_convert_type(w << 16, F32)
    return a, b


def _inproj_kernel(x_ref, g_ref, w_ref, bsum_ref, qg_ref, kg_ref, u_ref, q_ref, k_ref, v_ref):
    h = _rms(x_ref[...], g_ref[...]).astype(BF16)
    ag = jnp.dot(h, w_ref[:, 0:2 * D_CONV], preferred_element_type=F32)
    u_ref[...] = ag[:, :D_CONV] * jax.nn.sigmoid(ag[:, D_CONV:])
    bsum = bsum_ref[...]

    def head_norm(z, g):
        zz = (z * z).astype(BF16)
        hw = D_ATTN // 2
        ss = jnp.concatenate([jnp.dot(zz[:, :hw], bsum, preferred_element_type=F32),
                              jnp.dot(zz[:, hw:], bsum, preferred_element_type=F32)], axis=1)
        return z * lax.rsqrt(ss * (1.0 / HEAD_DIM) + EPS) * g

    c0 = 2 * D_CONV
    q = jnp.dot(h, w_ref[:, c0:c0 + D_ATTN], preferred_element_type=F32)
    q_ref[...] = (head_norm(q, qg_ref[...]) * (HEAD_DIM ** -0.5 * LOG2E)).astype(BF16)
    k = jnp.dot(h, w_ref[:, c0 + D_ATTN:c0 + 2 * D_ATTN], preferred_element_type=F32)
    k_ref[...] = head_norm(k, kg_ref[...]).astype(BF16)
    v = jnp.dot(h, w_ref[:, c0 + 2 * D_ATTN:c0 + 3 * D_ATTN], preferred_element_type=F32)
    v_ref[...] = v.astype(BF16)


def _inproj(x, g, w, bsum, qg, kg, tm=512):
    n = x.shape[0]
    wc = w.shape[1]
    full = lambda i: (0, 0)
    row = lambda i: (i, 0)
    return pl.pallas_call(
        _inproj_kernel,
        grid=(n // tm,),
        in_specs=[pl.BlockSpec((tm, D_MODEL), row), pl.BlockSpec((1, D_MODEL), full),
                  pl.BlockSpec((D_MODEL, wc), full), pl.BlockSpec(bsum.shape, full),
                  pl.BlockSpec((1, D_ATTN), full), pl.BlockSpec((1, D_ATTN), full)],
        out_specs=[pl.BlockSpec((tm, D_CONV), row), pl.BlockSpec((tm, D_ATTN), row),
                   pl.BlockSpec((tm, D_ATTN), row), pl.BlockSpec((tm, D_ATTN), row)],
        out_shape=[jax.ShapeDtypeStruct((n, D_CONV), F32), jax.ShapeDtypeStruct((n, D_ATTN), BF16),
                   jax.ShapeDtypeStruct((n, D_ATTN), BF16), jax.ShapeDtypeStruct((n, D_ATTN), BF16)],
        compiler_params=_cparams("parallel"),
        name="inproj",
    )(x, g, w, bsum, qg, kg)


def _conv_kernel(up_ref, uc_ref, un_ref, w_ref, b_ref, lg_ref, lb_ref, o_ref, ext_ref, sh_ref, *, tc, ch):
    i = pl.program_id(0)
    last = pl.num_programs(0) - 1
    ext_ref[0:HALO, :] = jnp.where(i > 0, up_ref[...], 0.0)
    ext_ref[HALO:HALO + tc, :] = uc_ref[...]
    ext_ref[HALO + tc:2 * HALO + tc, :] = jnp.where(i < last, un_ref[...], 0.0)
    span = sh_ref.shape[1]
    for b in range(SUBLANES):
        sh_ref[b] = ext_ref[b:b + span, :]
    bias = b_ref[...]
    lg = lg_ref[...]
    lb = lb_ref[...]
    for c in range(tc // ch):
        acc = jnp.broadcast_to(bias, (ch, D_CONV))
        for j in range(CONV_WIDTH):
            shift = HALO - CONV_PAD + j
            row0 = c * ch + shift // SUBLANES * SUBLANES
            acc = acc + sh_ref[shift % SUBLANES, row0:row0 + ch, :] * w_ref[j:j + 1, :]
        mu = jnp.mean(acc, axis=-1, keepdims=True)
        d = acc - mu
        var = jnp.mean(d * d, axis=-1, keepdims=True)
        y = d * lax.rsqrt(var + EPS) * lg + lb
        o_ref[c * ch:(c + 1) * ch, :] = (y * jax.nn.sigmoid(y)).astype(BF16)


def _conv(u, w, b, lg, lb, tc=512, ch=512):
    n = u.shape[0]
    hb = tc // HALO
    nh = n // HALO
    full = lambda i: (0, 0)
    return pl.pallas_call(
        functools.partial(_conv_kernel, tc=tc, ch=ch),
        grid=(n // tc,),
        in_specs=[pl.BlockSpec((HALO, D_CONV), lambda i: (jnp.maximum(i * hb - 1, 0), 0)),
                  pl.BlockSpec((tc, D_CONV), lambda i: (i, 0)),
                  pl.BlockSpec((HALO, D_CONV), lambda i: (jnp.minimum((i + 1) * hb, nh - 1), 0)),
                  pl.BlockSpec((CONV_WIDTH, D_CONV), full), pl.BlockSpec((1, D_CONV), full),
                  pl.BlockSpec((1, D_CONV), full), pl.BlockSpec((1, D_CONV), full)],
        out_specs=pl.BlockSpec((tc, D_CONV), lambda i: (i, 0)),
        out_shape=jax.ShapeDtypeStruct((n, D_CONV), BF16),
        scratch_shapes=[pltpu.VMEM((tc + 2 * HALO, D_CONV), F32),
                        pltpu.VMEM((SUBLANES, tc + 2 * HALO - SUBLANES, D_CONV), F32)],
        compiler_params=_cparams("parallel"),
        name="conv",
    )(u, u, u, w, b, lg, lb)


def _attn_kernel(q_ref, kp_ref, kc_ref, kn_ref, vp_ref, vc_ref, vn_ref, tb_ref, o_ref, *, rows):
    i = pl.program_id(0)
    nkey = 3 * ATT_BLK
    lrow = lax.broadcasted_iota(I32, (1, nkey), 1) >> 6
    lane = lax.broadcasted_iota(I32, (1, 2 * HEAD_DIM), 1)
    masks = []
    for j in range(ATT_ROWS):
        r = i * ATT_ROWS + j
        rs = jnp.clip(r - WIN_H // 2, 0, rows - WIN_H)
        lo = rs - (i - 1) * ATT_ROWS
        masks.append(jnp.where((lrow >= lo) & (lrow < lo + WIN_H), 0.0, NEG))
    dn = (((1,), (1,)), ((), ()))

    def pair_slice(h):
        return slice(2 * HEAD_DIM * (h // 2), 2 * HEAD_DIM * (h // 2 + 1))

    def in_half(h):
        return (lane >= HEAD_DIM * (h % 2)) & (lane < HEAD_DIM * (h % 2 + 1))

    def scores(h):
        sl = pair_slice(h)
        q2 = q_ref[:, sl]
        qm = jnp.where(in_half(h), q2, jnp.zeros_like(q2))
        s = jnp.concatenate([lax.dot_general(qm, kk[:, sl], dn, preferred_element_type=F32)
                             for kk in (kp_ref, kc_ref, kn_ref)], axis=1)
        parts = []
        for j in range(ATT_ROWS):
            bias = jnp.concatenate([tb_ref[h, 2 * m - j + 3] for m in range(nkey // (2 * GRID_W))], axis=1)
            parts.append(s[j * GRID_W:(j + 1) * GRID_W, :] + bias + masks[j])
        return jnp.concatenate(parts, axis=0)

    def softmax(s):
        mx = jnp.max(s, axis=-1, keepdims=True)
        e = jnp.exp2(s - mx)
        return e.astype(BF16), jnp.sum(e, axis=-1, keepdims=True)

    def values(h, pb, den):
        sl = pair_slice(h)
        o = sum(jnp.dot(pb[:, ATT_BLK * t:ATT_BLK * (t + 1)], vv[:, sl], preferred_element_type=F32)
                for t, vv in enumerate((vp_ref, vc_ref, vn_ref)))
        return o / den

    stage_s = {h: scores(h) for h in range(2)}
    stage_p = {0: softmax(stage_s.pop(0))}
    out_pair = None
    for h in range(N_HEADS):
        if h + 2 < N_HEADS:
            stage_s[h + 2] = scores(h + 2)
        if h + 1 < N_HEADS:
            stage_p[h + 1] = softmax(stage_s.pop(h + 1))
        o = values(h, *stage_p.pop(h))
        if h % 2 == 0:
            out_pair = o
        else:
            o_ref[:, pair_slice(h)] = jnp.where(in_half(h), o, out_pair).astype(BF16)


def _attn(q, k, v, tb):
    n = q.shape[0]
    rows = n // GRID_W
    nb = n // ATT_BLK
    cur = lambda i: (i, 0)
    prev = lambda i: (jnp.maximum(i - 1, 0), 0)
    nxt = lambda i: (jnp.minimum(i + 1, nb - 1), 0)
    blk = (ATT_BLK, D_ATTN)
    return pl.pallas_call(
        functools.partial(_attn_kernel, rows=rows),
        grid=(nb,),
        in_specs=[pl.BlockSpec(blk, cur), pl.BlockSpec(blk, prev), pl.BlockSpec(blk, cur), pl.BlockSpec(blk, nxt),
                  pl.BlockSpec(blk, prev), pl.BlockSpec(blk, cur), pl.BlockSpec(blk, nxt),
                  pl.BlockSpec(tb.shape, lambda i: (0, 0, 0, 0))],
        out_specs=pl.BlockSpec(blk, cur),
        out_shape=jax.ShapeDtypeStruct((n, D_ATTN), BF16),
        compiler_params=_cparams("parallel"),
        name="attn",
    )(q, k, k, k, v, v, v, tb)


def _bias_table(rpb):
    cols = jnp.arange(GRID_W)
    start = jnp.clip(cols - WIN_W // 2, 0, GRID_W - WIN_W)
    kc = cols[None, :]
    inwin = (kc >= start[:, None]) & (kc < start[:, None] + WIN_W)
    rel_idx = kc - cols[:, None] + (WIN_W - 1)
    pick = ((rel_idx[None] == jnp.arange(2 * WIN_W - 1)[:, None, None]) & inwin[None]).astype(F32)
    rel = jnp.einsum('hdj,jck->hdck', rpb.astype(F32), pick, precision=lax.Precision.HIGHEST)
    t = jnp.where(inwin[None, None], rel * LOG2E, NEG)
    t = jnp.concatenate([t, jnp.full_like(t[:, :1], NEG)], axis=1)
    t_next = jnp.concatenate([t[:, 1:], t[:, -1:]], axis=1)
    return jnp.concatenate([t, t_next], axis=-1).astype(F32)


def _mix_route_kernel(x_ref, uc_ref, at_ref, g1_ref, wg_ref, wco_ref, wao_ref, wo_ref, g2_ref,
                      wrh_ref, wrl_ref, rb_ref, tri_ref, cin_ref,
                      xo_ref, hp_ref, idx_ref, rank_ref, wt_ref, cnt_ref, carry_ref, *, tm):
    i = pl.program_id(0)

    @pl.when(i == 0)
    def _():
        carry_ref[...] = cin_ref[...]

    th = tm // MIX_SPLIT
    dn = (((1,), (1,)), ((), ()))

    def mix(p):
        rs = slice(p * th, (p + 1) * th)
        x = x_ref[rs, :]
        h = _rms(x, g1_ref[...]).astype(BF16)
        gates = jax.nn.sigmoid(jnp.dot(h, wg_ref[...], preferred_element_type=F32))
        y_conv = jnp.dot(uc_ref[rs, :], wco_ref[...], preferred_element_type=F32)
        y_attn = jnp.dot(at_ref[rs, :], wao_ref[...], preferred_element_type=F32)
        merged = gates[:, :D_MODEL] * y_conv + gates[:, D_MODEL:] * y_attn
        x1 = x + jnp.dot(merged.astype(BF16), wo_ref[...], preferred_element_type=F32)
        h2 = _rms(x1, g2_ref[...])
        xo_ref[rs, :] = x1
        hp_ref[rs, :] = _pack_bf16_pair(h2[:, :D_MODEL // 2], h2[:, D_MODEL // 2:])
        return (h2,)

    def router_logits(p, h2):
        h2b = h2.astype(BF16)
        h2l = (h2 - h2b.astype(F32)).astype(BF16)
        return (lax.dot_general(wrh_ref[...], h2b, dn, preferred_element_type=F32)
                + lax.dot_general(wrl_ref[...], h2b, dn, preferred_element_type=F32)
                + lax.dot_general(wrh_ref[...], h2l, dn, preferred_element_type=F32))

    def route(p, logits, carry):
        cs = slice(p * th, (p + 1) * th)
        scores = jax.nn.sigmoid(logits)
        biased = scores + rb_ref[...]

        sub = lax.broadcasted_iota(I32, (GROUP_SIZE, th), 0).astype(F32)
        groups, gscore = [], []
        for g in range(N_GROUPS):
            bg = biased[g * GROUP_SIZE:(g + 1) * GROUP_SIZE, :]
            m1 = jnp.max(bg, axis=0, keepdims=True)
            first = jnp.min(jnp.where(bg == m1, sub, float(GROUP_SIZE)), axis=0, keepdims=True)
            m2 = jnp.max(jnp.where(sub == first, -jnp.inf, bg), axis=0, keepdims=True)
            groups.append(bg)
            gscore.append(m1 + m2)
        masked = []
        for g in range(N_GROUPS):
            beaten = jnp.zeros((1, th), F32)
            for o in range(N_GROUPS):
                if o == g:
                    continue
                wins = (gscore[o] >= gscore[g]) if o < g else (gscore[o] > gscore[g])
                beaten = beaten + jnp.where(wins, 1.0, 0.0)
            masked.append(jnp.where(beaten < TOPK_GROUPS, groups[g], -jnp.inf))
        cur = jnp.concatenate(masked, axis=0)

        eid = lax.broadcasted_iota(I32, (N_EXPERTS, th), 0).astype(F32)
        idx_rows, sc_rows = [], []
        sel = jnp.zeros((N_EXPERTS, th), F32)
        for _ in range(TOP_K):
            m = jnp.max(cur, axis=0, keepdims=True)
            idx = jnp.min(jnp.where(cur == m, eid, float(N_EXPERTS)), axis=0, keepdims=True)
            hit = eid == idx
            sel = jnp.where(hit, 1.0, sel)
            cur = jnp.where(hit, -jnp.inf, cur)
            idx_rows.append(idx)
            sc_rows.append(jnp.sum(jnp.where(hit, scores, 0.0), axis=0, keepdims=True))
        wsum = sc_rows[0]
        for k in range(1, TOP_K):
            wsum = wsum + sc_rows[k]

        prefix = jnp.dot(sel.astype(BF16), tri_ref[...], preferred_element_type=F32)
        rank = carry + prefix
        rank_rows = [jnp.sum(jnp.where(eid == idx_rows[k], rank, 0.0), axis=0, keepdims=True)
                     for k in range(TOP_K)]
        idx_ref[:, cs] = jnp.concatenate(idx_rows, axis=0).astype(I32)
        rank_ref[:, cs] = jnp.concatenate(rank_rows, axis=0).astype(I32)
        wt_ref[:, cs] = jnp.concatenate([s / wsum * ROUTE_SCALE for s in sc_rows], axis=0)
        return carry + jnp.sum(sel, axis=1, keepdims=True)

    carry = carry_ref[:, 0:1]
    logits = router_logits(0, *mix(0))
    for p in range(MIX_SPLIT):
        if p + 1 < MIX_SPLIT:
            mixed_next = mix(p + 1)
        carry = route(p, logits, carry)
        if p + 1 < MIX_SPLIT:
            logits = router_logits(p + 1, *mixed_next)
    carry_ref[...] = jnp.broadcast_to(carry, carry_ref.shape)
    cnt_ref[...] = carry_ref[...]


def _mix_route(x, uc, at, g1, wg, wco, wao, wo, g2, wrh, wrl, rb, counts_in, start, n, tm=1024):
    steps = n // tm
    th = tm // MIX_SPLIT
    tri = (jnp.arange(th)[:, None] < jnp.arange(th)[None, :]).astype(BF16)
    src = lambda i: (start // tm + i, 0)
    row = lambda i: (i, 0)
    col = lambda i: (0, i)
    full = lambda i: (0, 0)
    ins = [x, uc, at, g1, wg, wco, wao, wo, g2, wrh, wrl, rb, tri, counts_in]
    in_specs = [pl.BlockSpec((tm, D_MODEL), src), pl.BlockSpec((tm, D_CONV), src), pl.BlockSpec((tm, D_ATTN), src)]
    in_specs += [pl.BlockSpec(a.shape, full) for a in ins[3:]]
    return pl.pallas_call(
        functools.partial(_mix_route_kernel, tm=tm),
        grid=(steps,),
        in_specs=in_specs,
        out_specs=[pl.BlockSpec((tm, D_MODEL), row), pl.BlockSpec((tm, D_MODEL // 2), row),
                   pl.BlockSpec((TOP_K, tm), col), pl.BlockSpec((TOP_K, tm), col), pl.BlockSpec((TOP_K, tm), col),
                   pl.BlockSpec((N_EXPERTS, 128), full)],
        out_shape=[jax.ShapeDtypeStruct((n, D_MODEL), F32), jax.ShapeDtypeStruct((n, D_MODEL // 2), U32),
                   jax.ShapeDtypeStruct((TOP_K, n), I32), jax.ShapeDtypeStruct((TOP_K, n), I32),
                   jax.ShapeDtypeStruct((TOP_K, n), F32), jax.ShapeDtypeStruct((N_EXPERTS, 128), F32)],
        scratch_shapes=[pltpu.VMEM((N_EXPERTS, 128), F32)],
        compiler_params=_cparams("arbitrary"),
        name="mix_route",
    )(*ins)


def _sc_workers():
    info = plsc.get_sparse_core_info()
    return info.num_cores, info.num_subcores


def _sc_mesh():
    return plsc.VectorSubcoreMesh(core_axis_name="c", subcore_axis_name="s")


def _worker_id(n_cores):
    return lax.axis_index("s") * n_cores + lax.axis_index("c")


def _dispatch(slots3, hp, xs_ref):
    n_chunks = slots3.shape[0]
    n_cores, n_sub = _sc_workers()
    per_worker = n_chunks // (n_cores * n_sub)
    assert per_worker * n_cores * n_sub == n_chunks

    @functools.partial(
        pl.kernel, mesh=_sc_mesh(), out_type=(),
        scratch_types=[pltpu.VMEM((TOP_K, SC_CHUNK), I32), pltpu.VMEM((SC_CHUNK, D_MODEL // 2), U32),
                       pltpu.SemaphoreType.DMA],
        name="dispatch")
    def run(slots_hbm, hp_hbm, xs_hbm, idx_v, rows_v, sem):
        first = _worker_id(n_cores) * per_worker

        @pl.loop(0, per_worker)
        def _(j):
            c = first + j
            pltpu.sync_copy(slots_hbm.at[c], idx_v)
            pltpu.sync_copy(hp_hbm.at[pl.ds(c * SC_CHUNK, SC_CHUNK)], rows_v)
            copies = [pltpu.async_copy(rows_v, xs_hbm.at[idx_v.at[k]], sem) for k in range(TOP_K)]
            for cp in copies:
                cp.wait()

    run(slots3, hp, xs_ref)


def _experts_kernel(bexp_ref, bidx_ref, bval_ref, nused_ref, next_ref, wslot_ref,
                    xs_ref, wg_ref, wu_ref, wd_ref, y_ref, xbuf, wg_buf, wu_buf, wd_buf, wgu_s, wd_s, xsem, wsem):
    b = pl.program_id(0)
    n_used = nused_ref[0]

    def block_copy(blk):
        slot = blk % XS_SLOTS
        rows = pl.ds(pl.multiple_of(bidx_ref[blk] * SLOT_BLK, SLOT_BLK), SLOT_BLK)
        return pltpu.make_async_copy(xs_ref.at[rows], xbuf.at[slot], xsem.at[slot])

    @pl.when(b == 0)
    def _():
        for blk in range(XS_AHEAD):
            @pl.when(blk < n_used)
            def _():
                block_copy(blk).start()

    @pl.when(b + XS_AHEAD < n_used)
    def _():
        block_copy(b + XS_AHEAD).start()

    def weight_copies(e, slot):
        return [pltpu.make_async_copy(wg_ref.at[e], wg_buf.at[slot], wsem.at[slot]),
                pltpu.make_async_copy(wu_ref.at[e], wu_buf.at[slot], wsem.at[slot]),
                pltpu.make_async_copy(wd_ref.at[e], wd_buf.at[slot], wsem.at[slot])]

    @pl.when(b < n_used)
    def _():
        e = bexp_ref[b]
        slot = wslot_ref[e]

        @pl.when(b == 0)
        def _():
            for cp in weight_copies(e, slot):
                cp.start()

        @pl.when((b == 0) | (bexp_ref[jnp.maximum(b - 1, 0)] != e))
        def _():
            for cp in weight_copies(e, slot):
                cp.wait()
            wgu_s[:, :D_EXPERT] = wg_buf[slot].astype(BF16)
            wgu_s[:, D_EXPERT:] = wu_buf[slot].astype(BF16)
            wd_s[...] = wd_buf[slot].astype(BF16)
            nxt = next_ref[e]

            @pl.when(nxt >= 0)
            def _():
                for cp in weight_copies(nxt, 1 - slot):
                    cp.start()

        srow = lax.broadcasted_iota(I32, (SLOT_BLK, 1), 0)
        block_copy(b).wait()
        xw = jnp.where(srow < bval_ref[b], xbuf[b % XS_SLOTS], jnp.uint32(0))
        xa, xb = _unpack_bf16_pair(xw)
        x = jnp.concatenate([xa.astype(BF16), xb.astype(BF16)], axis=1)
        gu = jnp.dot(x, wgu_s[...], preferred_element_type=F32)
        mid = (jax.nn.silu(gu[:, :D_EXPERT]) * gu[:, D_EXPERT:]).astype(BF16)
        y = jnp.dot(mid, wd_s[...], preferred_element_type=F32)
        y_ref[...] = _pack_bf16_pair(y[:, :D_MODEL // 2], y[:, D_MODEL // 2:])


def _experts(bexp, bidx, bval, nused, next_exp, wslot, xs, wg, wu, wd):
    n_blocks = bexp.shape[0]
    xmap = lambda b, be, bi, bv, nu, nx, ws: (bi[b], 0)
    grid_spec = pltpu.PrefetchScalarGridSpec(
        num_scalar_prefetch=6,
        grid=(n_blocks,),
        in_specs=[pl.BlockSpec(memory_space=pl.ANY),
                  pl.BlockSpec(memory_space=pl.ANY), pl.BlockSpec(memory_space=pl.ANY),
                  pl.BlockSpec(memory_space=pl.ANY)],
        out_specs=pl.BlockSpec((SLOT_BLK, D_MODEL // 2), xmap),
        scratch_shapes=[pltpu.VMEM((XS_SLOTS, SLOT_BLK, D_MODEL // 2), U32),
                        pltpu.VMEM((2, D_MODEL, D_EXPERT), F32), pltpu.VMEM((2, D_MODEL, D_EXPERT), F32),
                        pltpu.VMEM((2, D_EXPERT, D_MODEL), F32),
                        pltpu.VMEM((D_MODEL, 2 * D_EXPERT), BF16), pltpu.VMEM((D_EXPERT, D_MODEL), BF16),
                        pltpu.SemaphoreType.DMA((XS_SLOTS,)), pltpu.SemaphoreType.DMA((2,))],
    )
    return pl.pallas_call(
        _experts_kernel,
        grid_spec=grid_spec,
        out_shape=jax.ShapeDtypeStruct(xs.shape, U32),
        compiler_params=_cparams("arbitrary"),
        name="experts",
    )(bexp, bidx, bval, nused, next_exp, wslot, xs, wg, wu, wd)


def _gather(slots3, y, n):
    n_chunks = slots3.shape[0]
    n_cores, n_sub = _sc_workers()
    per_worker = n_chunks // (n_cores * n_sub)
    assert per_worker * n_cores * n_sub == n_chunks

    @functools.partial(
        pl.kernel, mesh=_sc_mesh(),
        out_type=jax.ShapeDtypeStruct((TOP_K, n, D_MODEL // 2), U32),
        scratch_types=[pltpu.VMEM((TOP_K, SC_CHUNK), I32),
                       pltpu.VMEM((SC_CHUNK, D_MODEL // 2), U32), pltpu.VMEM((SC_CHUNK, D_MODEL // 2), U32),
                       pltpu.SemaphoreType.DMA, pltpu.SemaphoreType.DMA, pltpu.SemaphoreType.DMA],
        name="gather")
    def run(slots_hbm, y_hbm, g_hbm, idx_v, buf0, buf1, gsem, wsem0, wsem1):
        first = _worker_id(n_cores) * per_worker
        bufs = (buf0, buf1)
        wsems = (wsem0, wsem1)

        @pl.loop(0, per_worker)
        def _(j):
            c = first + j
            pltpu.sync_copy(slots_hbm.at[c], idx_v)
            gathers = [None] * TOP_K
            writes = [None] * TOP_K
            gathers[0] = pltpu.async_copy(y_hbm.at[idx_v.at[0]], bufs[0], gsem)
            for k in range(TOP_K):
                gathers[k].wait()
                if k >= 1:
                    writes[k - 1].wait()
                if k + 1 < TOP_K:
                    gathers[k + 1] = pltpu.async_copy(y_hbm.at[idx_v.at[k + 1]], bufs[(k + 1) % 2], gsem)
                writes[k] = pltpu.async_copy(bufs[k % 2], g_hbm.at[k, pl.ds(c * SC_CHUNK, SC_CHUNK)],
                                             wsems[k % 2])
            writes[TOP_K - 1].wait()

    return run(slots3, y)


def _combine_kernel(wt_ref, x1_ref, hp_ref, g_ref, wsgu_ref, wsd_ref, *rest):
    o_ref = rest[-1]
    half = D_MODEL // 2
    ha, hb = _unpack_bf16_pair(hp_ref[...])
    h2b = jnp.concatenate([ha.astype(BF16), hb.astype(BF16)], axis=1)
    gu = jnp.dot(h2b, wsgu_ref[...], preferred_element_type=F32)
    mid = (jax.nn.silu(gu[:, :D_SHARED]) * gu[:, D_SHARED:]).astype(BF16)
    shared = jnp.dot(mid, wsd_ref[...], preferred_element_type=F32)
    acc_a = x1_ref[:, :half] + shared[:, :half]
    acc_b = x1_ref[:, half:] + shared[:, half:]
    w_all = wt_ref[...].T
    for k in range(TOP_K):
        ya, yb = _unpack_bf16_pair(g_ref[k])
        w = w_all[:, k:k + 1]
        acc_a = acc_a + w * ya
        acc_b = acc_b + w * yb
    o_ref[:, :half] = acc_a
    o_ref[:, half:] = acc_b


def _combine(wt, x1, hp, g, wsgu, wsd, start, n, out_so_far, tm=256):
    steps = g.shape[1] // tm
    full = lambda i: (0, 0)
    in_specs = [pl.BlockSpec((TOP_K, tm), lambda i: (0, i)), pl.BlockSpec((tm, D_MODEL), lambda i: (i, 0)),
                pl.BlockSpec((tm, D_MODEL // 2), lambda i: (i, 0)),
                pl.BlockSpec((TOP_K, tm, D_MODEL // 2), lambda i: (0, i, 0)),
                pl.BlockSpec(wsgu.shape, full), pl.BlockSpec(wsd.shape, full)]
    args = [wt, x1, hp, g, wsgu, wsd]
    aliases = {}
    if out_so_far is not None:
        in_specs.append(pl.BlockSpec(memory_space=pl.ANY))
        args.append(out_so_far)
        aliases = {len(args) - 1: 0}
    return pl.pallas_call(
        _combine_kernel,
        grid=(steps,),
        in_specs=in_specs,
        out_specs=pl.BlockSpec((tm, D_MODEL), lambda i: (start // tm + i, 0)),
        out_shape=jax.ShapeDtypeStruct((n, D_MODEL), F32),
        input_output_aliases=aliases,
        compiler_params=_cparams("parallel"),
        name="combine",
    )(*args)


def _layer(x, norm_mix_g, w_in, q_norm_g, k_norm_g, rpb, conv_dw_w, conv_dw_b, conv_ln_g, conv_ln_b,
           w_conv_out, w_attn_out, w_o, norm_ffn_g, w_router, router_bias, w_exp_gate, w_exp_up,
           w_exp_down, w_sh_gate, w_sh_up, w_sh_down):
    n = x.shape[0]
    c_qkv = 2 * D_CONV + 3 * D_ATTN
    w_qkv = w_in[:, :c_qkv].astype(BF16)
    w_gates = w_in[:, c_qkv:].astype(BF16)
    head_of = jnp.arange(D_ATTN // 2) // HEAD_DIM
    bsum = (head_of[:, None] == head_of[None, :]).astype(BF16)
    row = lambda v: v.reshape(1, -1).astype(F32)

    u, q, k, v = _inproj(x, row(norm_mix_g), w_qkv, bsum,
                         row(jnp.tile(q_norm_g, N_HEADS)), row(jnp.tile(k_norm_g, N_HEADS)))
    uc = _conv(u, conv_dw_w.reshape(CONV_WIDTH, D_CONV), row(conv_dw_b), row(conv_ln_g), row(conv_ln_b))
    at = _attn(q, k, v, _bias_table(rpb))

    wr_t = w_router.T
    wr_hi = wr_t.astype(BF16)
    wr_lo = (wr_t - wr_hi.astype(F32)).astype(BF16)
    wsgu = jnp.concatenate([w_sh_gate, w_sh_up], axis=1).astype(BF16)
    mix_weights = (row(norm_mix_g), w_gates, w_conv_out.astype(BF16), w_attn_out.astype(BF16), w_o.astype(BF16),
                   row(norm_ffn_g), wr_hi, wr_lo, router_bias.reshape(N_EXPERTS, 1).astype(F32))
    wsd = w_sh_down.astype(BF16)

    cap = n
    sizes = [n * f // PART_DENOM for f in PART_SHARES]
    xs_ref = jax.empty_ref(jax.ShapeDtypeStruct((N_EXPERTS * cap, D_MODEL // 2), U32))
    cnt = jnp.zeros((N_EXPERTS, 128), F32)
    parts = []
    start = 0
    for size in sizes:
        x1_p, hp_p, idx_p, rank_p, wt_p, cnt = _mix_route(x, uc, at, *mix_weights, cnt, start, size)
        slots_p = (idx_p * cap + rank_p).reshape(TOP_K, size // SC_CHUNK, SC_CHUNK).transpose(1, 0, 2)
        _dispatch(slots_p, hp_p, xs_ref)
        parts.append((start, size, x1_p, hp_p, wt_p, slots_p))
        start += size
    xs = jax.freeze(xs_ref)

    counts = cnt[:, 0].astype(I32)
    blocks_of = (counts + SLOT_BLK - 1) // SLOT_BLK
    blocks_end = jnp.cumsum(blocks_of)
    n_blocks = (n * TOP_K + N_EXPERTS * (SLOT_BLK - 1)) // SLOT_BLK
    n_used = blocks_end[-1]
    bseq = jnp.minimum(jnp.arange(n_blocks, dtype=I32), n_used - 1)
    bexp = jnp.minimum(jnp.sum((blocks_end[None, :] <= bseq[:, None]).astype(I32), axis=1), N_EXPERTS - 1)
    owner = (bexp[:, None] == jnp.arange(N_EXPERTS)[None, :]).astype(I32)
    local = bseq - jnp.sum(owner * (blocks_end - blocks_of)[None, :], axis=1)
    bidx = (bexp * (cap // SLOT_BLK) + local).astype(I32)
    bval = jnp.clip(jnp.sum(owner * counts[None, :], axis=1) - local * SLOT_BLK, 0, SLOT_BLK).astype(I32)
    eids = jnp.arange(N_EXPERTS, dtype=I32)
    later = (eids[None, :] > eids[:, None]) & (counts[None, :] > 0)
    next_exp = jnp.min(jnp.where(later, eids[None, :], N_EXPERTS), axis=1)
    next_exp = jnp.where(next_exp == N_EXPERTS, -1, next_exp).astype(I32)
    wslot = ((jnp.cumsum((counts > 0).astype(I32)) - 1) & 1).astype(I32)

    y = _experts(bexp, bidx, bval, n_used.reshape(1).astype(I32), next_exp, wslot, xs,
                 w_exp_gate, w_exp_up, w_exp_down)
    out = None
    for start, size, x1_p, hp_p, wt_p, slots_p in parts:
        out = _combine(wt_p, x1_p, hp_p, _gather(slots_p, y, size), wsgu, wsd, start, n, out)
    return out


def kernel(x, norm_mix_g, w_in, q_norm_g, k_norm_g, rpb, conv_dw_w, conv_dw_b, conv_ln_g, conv_ln_b, w_conv_out, w_attn_out, w_o, norm_ffn_g, w_router, router_bias, w_exp_gate, w_exp_up, w_exp_down, w_sh_gate, w_sh_up, w_sh_down):
    b, s, d = x.shape
    assert b == 1 and d == D_MODEL and s % (PART_DENOM * 2048) == 0, x.shape
    xf = x.reshape(b * s, d)
    depth = norm_mix_g.shape[0]
    for l in range(depth):
        xf = _layer(xf, norm_mix_g[l], w_in[l], q_norm_g[l], k_norm_g[l], rpb[l], conv_dw_w[l], conv_dw_b[l],
                    conv_ln_g[l], conv_ln_b[l], w_conv_out[l], w_attn_out[l], w_o[l], norm_ffn_g[l],
                    w_router[l], router_bias[l], w_exp_gate[l], w_exp_up[l], w_exp_down[l], w_sh_gate[l],
                    w_sh_up[l], w_sh_down[l])
    return xf.reshape(b, s, d)
```

```python
import functools

import jax
import jax.numpy as jnp
from jax import lax
from jax.experimental import pallas as pl
from jax.experimental.pallas import tpu as pltpu
from jax.experimental.pallas import tpu_sc as plsc

D_MODEL = 1024
GRID_W = 64
D_CONV = 512
CONV_WIDTH = 31
CONV_PAD = CONV_WIDTH // 2
N_HEADS = 8
HEAD_DIM = 64
D_ATTN = N_HEADS * HEAD_DIM
WIN_H = 8
WIN_W = 16
N_EXPERTS = 64
TOP_K = 8
N_GROUPS = 8
GROUP_SIZE = N_EXPERTS // N_GROUPS
TOPK_GROUPS = 4
D_EXPERT = 256
D_SHARED = 256
ROUTE_SCALE = 2.5
EPS = 1e-6
LOG2E = 1.4426950408889634

SUBLANES = 8
NEG = -1e30
HALO = 16
ATT_ROWS = 4
ATT_BLK = ATT_ROWS * GRID_W
SLOT_BLK = 512
MIX_SPLIT = 2
XS_AHEAD = 3
XS_SLOTS = XS_AHEAD + 1
PART_SHARES = (2, 2, 2, 1, 1)
PART_DENOM = 8
SC_CHUNK = 64
VMEM_LIMIT = 56 * 1024 * 1024

F32 = jnp.float32
BF16 = jnp.bfloat16
I32 = jnp.int32
U32 = jnp.uint32


def _cparams(*sem):
    return pltpu.CompilerParams(dimension_semantics=sem, vmem_limit_bytes=VMEM_LIMIT)


def _rms(x, g):
    return x * lax.rsqrt(jnp.mean(x * x, axis=-1, keepdims=True) + EPS) * g


def _pack_bf16_pair(a, b):
    def bf16_bits(v):
        return lax.bitcast_convert_type(v.astype(BF16).astype(F32), U32)
    return bf16_bits(a) | (bf16_bits(b) >> 16)


def _unpack_bf16_pair(w):
    a = lax.bitcast_convert_type(w & jnp.uint32(0xFFFF0000), F32)
    b = lax.bitcast_convert_type(w << 16, F32)
    return a, b


def _inproj_kernel(x_ref, g_ref, w_ref, bsum_ref, qg_ref, kg_ref, u_ref, q_ref, k_ref, v_ref):
    h = _rms(x_ref[...], g_ref[...]).astype(BF16)
    ag = jnp.dot(h, w_ref[:, 0:2 * D_CONV], preferred_element_type=F32)
    u_ref[...] = ag[:, :D_CONV] * jax.nn.sigmoid(ag[:, D_CONV:])
    bsum = bsum_ref[...]

    def head_norm(z, g):
        zz = (z * z).astype(BF16)
        hw = D_ATTN // 2
        ss = jnp.concatenate([jnp.dot(zz[:, :hw], bsum, preferred_element_type=F32),
                              jnp.dot(zz[:, hw:], bsum, preferred_element_type=F32)], axis=1)
        return z * lax.rsqrt(ss * (1.0 / HEAD_DIM) + EPS) * g

    c0 = 2 * D_CONV
    q = jnp.dot(h, w_ref[:, c0:c0 + D_ATTN], preferred_element_type=F32)
    q_ref[...] = (head_norm(q, qg_ref[...]) * (HEAD_DIM ** -0.5 * LOG2E)).astype(BF16)
    k = jnp.dot(h, w_ref[:, c0 + D_ATTN:c0 + 2 * D_ATTN], preferred_element_type=F32)
    k_ref[...] = head_norm(k, kg_ref[...]).astype(BF16)
    v = jnp.dot(h, w_ref[:, c0 + 2 * D_ATTN:c0 + 3 * D_ATTN], preferred_element_type=F32)
    v_ref[...] = v.astype(BF16)


def _inproj(x, g, w, bsum, qg, kg, tm=512):
    n = x.shape[0]
    wc = w.shape[1]
    full = lambda i: (0, 0)
    row = lambda i: (i, 0)
    return pl.pallas_call(
        _inproj_kernel,
        grid=(n // tm,),
        in_specs=[pl.BlockSpec((tm, D_MODEL), row), pl.BlockSpec((1, D_MODEL), full),
                  pl.BlockSpec((D_MODEL, wc), full), pl.BlockSpec(bsum.shape, full),
                  pl.BlockSpec((1, D_ATTN), full), pl.BlockSpec((1, D_ATTN), full)],
        out_specs=[pl.BlockSpec((tm, D_CONV), row), pl.BlockSpec((tm, D_ATTN), row),
                   pl.BlockSpec((tm, D_ATTN), row), pl.BlockSpec((tm, D_ATTN), row)],
        out_shape=[jax.ShapeDtypeStruct((n, D_CONV), F32), jax.ShapeDtypeStruct((n, D_ATTN), BF16),
                   jax.ShapeDtypeStruct((n, D_ATTN), BF16), jax.ShapeDtypeStruct((n, D_ATTN), BF16)],
        compiler_params=_cparams("parallel"),
        name="inproj",
    )(x, g, w, bsum, qg, kg)


def _conv_kernel(up_ref, uc_ref, un_ref, w_ref, b_ref, lg_ref, lb_ref, o_ref, ext_ref, sh_ref, *, tc, ch):
    i = pl.program_id(0)
    last = pl.num_programs(0) - 1
    ext_ref[0:HALO, :] = jnp.where(i > 0, up_ref[...], 0.0)
    ext_ref[HALO:HALO + tc, :] = uc_ref[...]
    ext_ref[HALO + tc:2 * HALO + tc, :] = jnp.where(i < last, un_ref[...], 0.0)
    span = sh_ref.shape[1]
    for b in range(SUBLANES):
        sh_ref[b] = ext_ref[b:b + span, :]
    bias = b_ref[...]
    lg = lg_ref[...]
    lb = lb_ref[...]
    for c in range(tc // ch):
        acc = jnp.broadcast_to(bias, (ch, D_CONV))
        for j in range(CONV_WIDTH):
            shift = HALO - CONV_PAD + j
            row0 = c * ch + shift // SUBLANES * SUBLANES
            acc = acc + sh_ref[shift % SUBLANES, row0:row0 + ch, :] * w_ref[j:j + 1, :]
        mu = jnp.mean(acc, axis=-1, keepdims=True)
        d = acc - mu
        var = jnp.mean(d * d, axis=-1, keepdims=True)
        y = d * lax.rsqrt(var + EPS) * lg + lb
        o_ref[c * ch:(c + 1) * ch, :] = (y * jax.nn.sigmoid(y)).astype(BF16)


def _conv(u, w, b, lg, lb, tc=512, ch=512):
    n = u.shape[0]
    hb = tc // HALO
    nh = n // HALO
    full = lambda i: (0, 0)
    return pl.pallas_call(
        functools.partial(_conv_kernel, tc=tc, ch=ch),
        grid=(n // tc,),
        in_specs=[pl.BlockSpec((HALO, D_CONV), lambda i: (jnp.maximum(i * hb - 1, 0), 0)),
                  pl.BlockSpec((tc, D_CONV), lambda i: (i, 0)),
                  pl.BlockSpec((HALO, D_CONV), lambda i: (jnp.minimum((i + 1) * hb, nh - 1), 0)),
                  pl.BlockSpec((CONV_WIDTH, D_CONV), full), pl.BlockSpec((1, D_CONV), full),
                  pl.BlockSpec((1, D_CONV), full), pl.BlockSpec((1, D_CONV), full)],
        out_specs=pl.BlockSpec((tc, D_CONV), lambda i: (i, 0)),
        out_shape=jax.ShapeDtypeStruct((n, D_CONV), BF16),
        scratch_shapes=[pltpu.VMEM((tc + 2 * HALO, D_CONV), F32),
                        pltpu.VMEM((SUBLANES, tc + 2 * HALO - SUBLANES, D_CONV), F32)],
        compiler_params=_cparams("parallel"),
        name="conv",
    )(u, u, u, w, b, lg, lb)


def _attn_kernel(q_ref, kp_ref, kc_ref, kn_ref, vp_ref, vc_ref, vn_ref, tb_ref, o_ref, *, rows):
    i = pl.program_id(0)
    nkey = 3 * ATT_BLK
    lrow = lax.broadcasted_iota(I32, (1, nkey), 1) >> 6
    lane = lax.broadcasted_iota(I32, (1, 2 * HEAD_DIM), 1)
    masks = []
    for j in range(ATT_ROWS):
        r = i * ATT_ROWS + j
        rs = jnp.clip(r - WIN_H // 2, 0, rows - WIN_H)
        lo = rs - (i - 1) * ATT_ROWS
        masks.append(jnp.where((lrow >= lo) & (lrow < lo + WIN_H), 0.0, NEG))
    dn = (((1,), (1,)), ((), ()))

    def pair_slice(h):
        return slice(2 * HEAD_DIM * (h // 2), 2 * HEAD_DIM * (h // 2 + 1))

    def in_half(h):
        return (lane >= HEAD_DIM * (h % 2)) & (lane < HEAD_DIM * (h % 2 + 1))

    def scores(h):
        sl = pair_slice(h)
        q2 = q_ref[:, sl]
        qm = jnp.where(in_half(h), q2, jnp.zeros_like(q2))
        s = jnp.concatenate([lax.dot_general(qm, kk[:, sl], dn, preferred_element_type=F32)
                             for kk in (kp_ref, kc_ref, kn_ref)], axis=1)
        parts = []
        for j in range(ATT_ROWS):
            bias = jnp.concatenate([tb_ref[h, 2 * m - j + 3] for m in range(nkey // (2 * GRID_W))], axis=1)
            parts.append(s[j * GRID_W:(j + 1) * GRID_W, :] + bias + masks[j])
        return jnp.concatenate(parts, axis=0)

    def softmax(s):
        mx = jnp.max(s, axis=-1, keepdims=True)
        e = jnp.exp2(s - mx)
        return e.astype(BF16), jnp.sum(e, axis=-1, keepdims=True)

    def values(h, pb, den):
        sl = pair_slice(h)
        o = sum(jnp.dot(pb[:, ATT_BLK * t:ATT_BLK * (t + 1)], vv[:, sl], preferred_element_type=F32)
                for t, vv in enumerate((vp_ref, vc_ref, vn_ref)))
        return o / den

    stage_s = {h: scores(h) for h in range(2)}
    stage_p = {0: softmax(stage_s.pop(0))}
    out_pair = None
    for h in range(N_HEADS):
        if h + 2 < N_HEADS:
            stage_s[h + 2] = scores(h + 2)
        if h + 1 < N_HEADS:
            stage_p[h + 1] = softmax(stage_s.pop(h + 1))
        o = values(h, *stage_p.pop(h))
        if h % 2 == 0:
            out_pair = o
        else:
            o_ref[:, pair_slice(h)] = jnp.where(in_half(h), o, out_pair).astype(BF16)


def _attn(q, k, v, tb):
    n = q.shape[0]
    rows = n // GRID_W
    nb = n // ATT_BLK
    cur = lambda i: (i, 0)
    prev = lambda i: (jnp.maximum(i - 1, 0), 0)
    nxt = lambda i: (jnp.minimum(i + 1, nb - 1), 0)
    blk = (ATT_BLK, D_ATTN)
    return pl.pallas_call(
        functools.partial(_attn_kernel, rows=rows),
        grid=(nb,),
        in_specs=[pl.BlockSpec(blk, cur), pl.BlockSpec(blk, prev), pl.BlockSpec(blk, cur), pl.BlockSpec(blk, nxt),
                  pl.BlockSpec(blk, prev), pl.BlockSpec(blk, cur), pl.BlockSpec(blk, nxt),
                  pl.BlockSpec(tb.shape, lambda i: (0, 0, 0, 0))],
        out_specs=pl.BlockSpec(blk, cur),
        out_shape=jax.ShapeDtypeStruct((n, D_ATTN), BF16),
        compiler_params=_cparams("parallel"),
        name="attn",
    )(q, k, k, k, v, v, v, tb)


def _bias_table(rpb):
    cols = jnp.arange(GRID_W)
    start = jnp.clip(cols - WIN_W // 2, 0, GRID_W - WIN_W)
    kc = cols[None, :]
    inwin = (kc >= start[:, None]) & (kc < start[:, None] + WIN_W)
    rel_idx = kc - cols[:, None] + (WIN_W - 1)
    pick = ((rel_idx[None] == jnp.arange(2 * WIN_W - 1)[:, None, None]) & inwin[None]).astype(F32)
    rel = jnp.einsum('hdj,jck->hdck', rpb.astype(F32), pick, precision=lax.Precision.HIGHEST)
    t = jnp.where(inwin[None, None], rel * LOG2E, NEG)
    t = jnp.concatenate([t, jnp.full_like(t[:, :1], NEG)], axis=1)
    t_next = jnp.concatenate([t[:, 1:], t[:, -1:]], axis=1)
    return jnp.concatenate([t, t_next], axis=-1).astype(F32)


def _mix_route_kernel(x_ref, uc_ref, at_ref, g1_ref, wg_ref, wco_ref, wao_ref, wo_ref, g2_ref,
                      wrh_ref, wrl_ref, rb_ref, tri_ref, cin_ref,
                      xo_ref, hp_ref, idx_ref, rank_ref, wt_ref, cnt_ref, carry_ref, *, tm):
    i = pl.program_id(0)

    @pl.when(i == 0)
    def _():
        carry_ref[...] = cin_ref[...]

    th = tm // MIX_SPLIT
    dn = (((1,), (1,)), ((), ()))

    def mix(p):
        rs = slice(p * th, (p + 1) * th)
        x = x_ref[rs, :]
        h = _rms(x, g1_ref[...]).astype(BF16)
        gates = jax.nn.sigmoid(jnp.dot(h, wg_ref[...], preferred_element_type=F32))
        y_conv = jnp.dot(uc_ref[rs, :], wco_ref[...], preferred_element_type=F32)
        y_attn = jnp.dot(at_ref[rs, :], wao_ref[...], preferred_element_type=F32)
        merged = gates[:, :D_MODEL] * y_conv + gates[:, D_MODEL:] * y_attn
        x1 = x + jnp.dot(merged.astype(BF16), wo_ref[...], preferred_element_type=F32)
        h2 = _rms(x1, g2_ref[...])
        xo_ref[rs, :] = x1
        hp_ref[rs, :] = _pack_bf16_pair(h2[:, :D_MODEL // 2], h2[:, D_MODEL // 2:])
        return (h2,)

    def router_logits(p, h2):
        h2b = h2.astype(BF16)
        h2l = (h2 - h2b.astype(F32)).astype(BF16)
        return (lax.dot_general(wrh_ref[...], h2b, dn, preferred_element_type=F32)
                + lax.dot_general(wrl_ref[...], h2b, dn, preferred_element_type=F32)
                + lax.dot_general(wrh_ref[...], h2l, dn, preferred_element_type=F32))

    def route(p, logits, carry):
        cs = slice(p * th, (p + 1) * th)
        scores = jax.nn.sigmoid(logits)
        biased = scores + rb_ref[...]

        sub = lax.broadcasted_iota(I32, (GROUP_SIZE, th), 0).astype(F32)
        groups, gscore = [], []
        for g in range(N_GROUPS):
            bg = biased[g * GROUP_SIZE:(g + 1) * GROUP_SIZE, :]
            m1 = jnp.max(bg, axis=0, keepdims=True)
            first = jnp.min(jnp.where(bg == m1, sub, float(GROUP_SIZE)), axis=0, keepdims=True)
            m2 = jnp.max(jnp.where(sub == first, -jnp.inf, bg), axis=0, keepdims=True)
            groups.append(bg)
            gscore.append(m1 + m2)
        masked = []
        for g in range(N_GROUPS):
            beaten = jnp.zeros((1, th), F32)
            for o in range(N_GROUPS):
                if o == g:
                    continue
                wins = (gscore[o] >= gscore[g]) if o < g else (gscore[o] > gscore[g])
                beaten = beaten + jnp.where(wins, 1.0, 0.0)
            masked.append(jnp.where(beaten < TOPK_GROUPS, groups[g], -jnp.inf))
        cur = jnp.concatenate(masked, axis=0)

        eid = lax.broadcasted_iota(I32, (N_EXPERTS, th), 0).astype(F32)
        idx_rows, sc_rows = [], []
        sel = jnp.zeros((N_EXPERTS, th), F32)
        for _ in range(TOP_K):
            m = jnp.max(cur, axis=0, keepdims=True)
            idx = jnp.min(jnp.where(cur == m, eid, float(N_EXPERTS)), axis=0, keepdims=True)
            hit = eid == idx
            sel = jnp.where(hit, 1.0, sel)
            cur = jnp.where(hit, -jnp.inf, cur)
            idx_rows.append(idx)
            sc_rows.append(jnp.sum(jnp.where(hit, scores, 0.0), axis=0, keepdims=True))
        wsum = sc_rows[0]
        for k in range(1, TOP_K):
            wsum = wsum + sc_rows[k]

        prefix = jnp.dot(sel.astype(BF16), tri_ref[...], preferred_element_type=F32)
        rank = carry + prefix
        rank_rows = [jnp.sum(jnp.where(eid == idx_rows[k], rank, 0.0), axis=0, keepdims=True)
                     for k in range(TOP_K)]
        idx_ref[:, cs] = jnp.concatenate(idx_rows, axis=0).astype(I32)
        rank_ref[:, cs] = jnp.concatenate(rank_rows, axis=0).astype(I32)
        wt_ref[:, cs] = jnp.concatenate([s / wsum * ROUTE_SCALE for s in sc_rows], axis=0)
        return carry + jnp.sum(sel, axis=1, keepdims=True)

    carry = carry_ref[:, 0:1]
    logits = router_logits(0, *mix(0))
    for p in range(MIX_SPLIT):
        if p + 1 < MIX_SPLIT:
            mixed_next = mix(p + 1)
        carry = route(p, logits, carry)
        if p + 1 < MIX_SPLIT:
            logits = router_logits(p + 1, *mixed_next)
    carry_ref[...] = jnp.broadcast_to(carry, carry_ref.shape)
    cnt_ref[...] = carry_ref[...]


def _mix_route(x, uc, at, g1, wg, wco, wao, wo, g2, wrh, wrl, rb, counts_in, start, n, tm=1024):
    steps = n // tm
    th = tm // MIX_SPLIT
    tri = (jnp.arange(th)[:, None] < jnp.arange(th)[None, :]).astype(BF16)
    src = lambda i: (start // tm + i, 0)
    row = lambda i: (i, 0)
    col = lambda i: (0, i)
    full = lambda i: (0, 0)
    ins = [x, uc, at, g1, wg, wco, wao, wo, g2, wrh, wrl, rb, tri, counts_in]
    in_specs = [pl.BlockSpec((tm, D_MODEL), src), pl.BlockSpec((tm, D_CONV), src), pl.BlockSpec((tm, D_ATTN), src)]
    in_specs += [pl.BlockSpec(a.shape, full) for a in ins[3:]]
    return pl.pallas_call(
        functools.partial(_mix_route_kernel, tm=tm),
        grid=(steps,),
        in_specs=in_specs,
        out_specs=[pl.BlockSpec((tm, D_MODEL), row), pl.BlockSpec((tm, D_MODEL // 2), row),
                   pl.BlockSpec((TOP_K, tm), col), pl.BlockSpec((TOP_K, tm), col), pl.BlockSpec((TOP_K, tm), col),
                   pl.BlockSpec((N_EXPERTS, 128), full)],
        out_shape=[jax.ShapeDtypeStruct((n, D_MODEL), F32), jax.ShapeDtypeStruct((n, D_MODEL // 2), U32),
                   jax.ShapeDtypeStruct((TOP_K, n), I32), jax.ShapeDtypeStruct((TOP_K, n), I32),
                   jax.ShapeDtypeStruct((TOP_K, n), F32), jax.ShapeDtypeStruct((N_EXPERTS, 128), F32)],
        scratch_shapes=[pltpu.VMEM((N_EXPERTS, 128), F32)],
        compiler_params=_cparams("arbitrary"),
        name="mix_route",
    )(*ins)


def _sc_workers():
    info = plsc.get_sparse_core_info()
    return info.num_cores, info.num_subcores


def _sc_mesh():
    return plsc.VectorSubcoreMesh(core_axis_name="c", subcore_axis_name="s")


def _worker_id(n_cores):
    return lax.axis_index("s") * n_cores + lax.axis_index("c")


def _dispatch(slots3, hp, xs_ref):
    n_chunks = slots3.shape[0]
    n_cores, n_sub = _sc_workers()
    per_worker = n_chunks // (n_cores * n_sub)
    assert per_worker * n_cores * n_sub == n_chunks

    @functools.partial(
        pl.kernel, mesh=_sc_mesh(), out_type=(),
        scratch_types=[pltpu.VMEM((TOP_K, SC_CHUNK), I32), pltpu.VMEM((SC_CHUNK, D_MODEL // 2), U32),
                       pltpu.SemaphoreType.DMA],
        name="dispatch")
    def run(slots_hbm, hp_hbm, xs_hbm, idx_v, rows_v, sem):
        first = _worker_id(n_cores) * per_worker

        @pl.loop(0, per_worker)
        def _(j):
            c = first + j
            pltpu.sync_copy(slots_hbm.at[c], idx_v)
            pltpu.sync_copy(hp_hbm.at[pl.ds(c * SC_CHUNK, SC_CHUNK)], rows_v)
            copies = [pltpu.async_copy(rows_v, xs_hbm.at[idx_v.at[k]], sem) for k in range(TOP_K)]
            for cp in copies:
                cp.wait()

    run(slots3, hp, xs_ref)


def _experts_kernel(bexp_ref, bidx_ref, bval_ref, nused_ref, next_ref, wslot_ref,
                    xs_ref, wg_ref, wu_ref, wd_ref, y_ref, xbuf, wg_buf, wu_buf, wd_buf, wgu_s, wd_s, xsem, wsem):
    b = pl.program_id(0)
    n_used = nused_ref[0]

    def block_copy(blk):
        slot = blk % XS_SLOTS
        rows = pl.ds(pl.multiple_of(bidx_ref[blk] * SLOT_BLK, SLOT_BLK), SLOT_BLK)
        return pltpu.make_async_copy(xs_ref.at[rows], xbuf.at[slot], xsem.at[slot])

    @pl.when(b == 0)
    def _():
        for blk in range(XS_AHEAD):
            @pl.when(blk < n_used)
            def _():
                block_copy(blk).start()

    @pl.when(b + XS_AHEAD < n_used)
    def _():
        block_copy(b + XS_AHEAD).start()

    def weight_copies(e, slot):
        return [pltpu.make_async_copy(wg_ref.at[e], wg_buf.at[slot], wsem.at[slot]),
                pltpu.make_async_copy(wu_ref.at[e], wu_buf.at[slot], wsem.at[slot]),
                pltpu.make_async_copy(wd_ref.at[e], wd_buf.at[slot], wsem.at[slot])]

    @pl.when(b < n_used)
    def _():
        e = bexp_ref[b]
        slot = wslot_ref[e]

        @pl.when(b == 0)
        def _():
            for cp in weight_copies(e, slot):
                cp.start()

        @pl.when((b == 0) | (bexp_ref[jnp.maximum(b - 1, 0)] != e))
        def _():
            for cp in weight_copies(e, slot):
                cp.wait()
            wgu_s[:, :D_EXPERT] = wg_buf[slot].astype(BF16)
            wgu_s[:, D_EXPERT:] = wu_buf[slot].astype(BF16)
            wd_s[...] = wd_buf[slot].astype(BF16)
            nxt = next_ref[e]

            @pl.when(nxt >= 0)
            def _():
                for cp in weight_copies(nxt, 1 - slot):
                    cp.start()

        srow = lax.broadcasted_iota(I32, (SLOT_BLK, 1), 0)
        block_copy(b).wait()
        xw = jnp.where(srow < bval_ref[b], xbuf[b % XS_SLOTS], jnp.uint32(0))
        xa, xb = _unpack_bf16_pair(xw)
        x = jnp.concatenate([xa.astype(BF16), xb.astype(BF16)], axis=1)
        gu = jnp.dot(x, wgu_s[...], preferred_element_type=F32)
        mid = (jax.nn.silu(gu[:, :D_EXPERT]) * gu[:, D_EXPERT:]).astype(BF16)
        y = jnp.dot(mid, wd_s[...], preferred_element_type=F32)
        y_ref[...] = _pack_bf16_pair(y[:, :D_MODEL // 2], y[:, D_MODEL // 2:])


def _experts(bexp, bidx, bval, nused, next_exp, wslot, xs, wg, wu, wd):
    n_blocks = bexp.shape[0]
    xmap = lambda b, be, bi, bv, nu, nx, ws: (bi[b], 0)
    grid_spec = pltpu.PrefetchScalarGridSpec(
        num_scalar_prefetch=6,
        grid=(n_blocks,),
        in_specs=[pl.BlockSpec(memory_space=pl.ANY),
                  pl.BlockSpec(memory_space=pl.ANY), pl.BlockSpec(memory_space=pl.ANY),
                  pl.BlockSpec(memory_space=pl.ANY)],
        out_specs=pl.BlockSpec((SLOT_BLK, D_MODEL // 2), xmap),
        scratch_shapes=[pltpu.VMEM((XS_SLOTS, SLOT_BLK, D_MODEL // 2), U32),
                        pltpu.VMEM((2, D_MODEL, D_EXPERT), F32), pltpu.VMEM((2, D_MODEL, D_EXPERT), F32),
                        pltpu.VMEM((2, D_EXPERT, D_MODEL), F32),
                        pltpu.VMEM((D_MODEL, 2 * D_EXPERT), BF16), pltpu.VMEM((D_EXPERT, D_MODEL), BF16),
                        pltpu.SemaphoreType.DMA((XS_SLOTS,)), pltpu.SemaphoreType.DMA((2,))],
    )
    return pl.pallas_call(
        _experts_kernel,
        grid_spec=grid_spec,
        out_shape=jax.ShapeDtypeStruct(xs.shape, U32),
        compiler_params=_cparams("arbitrary"),
        name="experts",
    )(bexp, bidx, bval, nused, next_exp, wslot, xs, wg, wu, wd)


def _gather(slots3, y, n):
    n_chunks = slots3.shape[0]
    n_cores, n_sub = _sc_workers()
    per_worker = n_chunks // (n_cores * n_sub)
    assert per_worker * n_cores * n_sub == n_chunks

    @functools.partial(
        pl.kernel, mesh=_sc_mesh(),
        out_type=jax.ShapeDtypeStruct((TOP_K, n, D_MODEL // 2), U32),
        scratch_types=[pltpu.VMEM((TOP_K, SC_CHUNK), I32),
                       pltpu.VMEM((SC_CHUNK, D_MODEL // 2), U32), pltpu.VMEM((SC_CHUNK, D_MODEL // 2), U32),
                       pltpu.SemaphoreType.DMA, pltpu.SemaphoreType.DMA, pltpu.SemaphoreType.DMA],
        name="gather")
    def run(slots_hbm, y_hbm, g_hbm, idx_v, buf0, buf1, gsem, wsem0, wsem1):
        first = _worker_id(n_cores) * per_worker
        bufs = (buf0, buf1)
        wsems = (wsem0, wsem1)

        @pl.loop(0, per_worker)
        def _(j):
            c = first + j
            pltpu.sync_copy(slots_hbm.at[c], idx_v)
            gathers = [None] * TOP_K
            writes = [None] * TOP_K
            gathers[0] = pltpu.async_copy(y_hbm.at[idx_v.at[0]], bufs[0], gsem)
            for k in range(TOP_K):
                gathers[k].wait()
                if k >= 1:
                    writes[k - 1].wait()
                if k + 1 < TOP_K:
                    gathers[k + 1] = pltpu.async_copy(y_hbm.at[idx_v.at[k + 1]], bufs[(k + 1) % 2], gsem)
                writes[k] = pltpu.async_copy(bufs[k % 2], g_hbm.at[k, pl.ds(c * SC_CHUNK, SC_CHUNK)],
                                             wsems[k % 2])
            writes[TOP_K - 1].wait()

    return run(slots3, y)


def _combine_kernel(wt_ref, x1_ref, hp_ref, g_ref, wsgu_ref, wsd_ref, *rest):
    o_ref = rest[-1]
    half = D_MODEL // 2
    ha, hb = _unpack_bf16_pair(hp_ref[...])
    h2b = jnp.concatenate([ha.astype(BF16), hb.astype(BF16)], axis=1)
    gu = jnp.dot(h2b, wsgu_ref[...], preferred_element_type=F32)
    mid = (jax.nn.silu(gu[:, :D_SHARED]) * gu[:, D_SHARED:]).astype(BF16)
    shared = jnp.dot(mid, wsd_ref[...], preferred_element_type=F32)
    acc_a = x1_ref[:, :half] + shared[:, :half]
    acc_b = x1_ref[:, half:] + shared[:, half:]
    w_all = wt_ref[...].T
    for k in range(TOP_K):
        ya, yb = _unpack_bf16_pair(g_ref[k])
        w = w_all[:, k:k + 1]
        acc_a = acc_a + w * ya
        acc_b = acc_b + w * yb
    o_ref[:, :half] = acc_a
    o_ref[:, half:] = acc_b


def _combine(wt, x1, hp, g, wsgu, wsd, start, n, out_so_far, tm=256):
    steps = g.shape[1] // tm
    full = lambda i: (0, 0)
    in_specs = [pl.BlockSpec((TOP_K, tm), lambda i: (0, i)), pl.BlockSpec((tm, D_MODEL), lambda i: (i, 0)),
                pl.BlockSpec((tm, D_MODEL // 2), lambda i: (i, 0)),
                pl.BlockSpec((TOP_K, tm, D_MODEL // 2), lambda i: (0, i, 0)),
                pl.BlockSpec(wsgu.shape, full), pl.BlockSpec(wsd.shape, full)]
    args = [wt, x1, hp, g, wsgu, wsd]
    aliases = {}
    if out_so_far is not None:
        in_specs.append(pl.BlockSpec(memory_space=pl.ANY))
        args.append(out_so_far)
        aliases = {len(args) - 1: 0}
    return pl.pallas_call(
        _combine_kernel,
        grid=(steps,),
        in_specs=in_specs,
        out_specs=pl.BlockSpec((tm, D_MODEL), lambda i: (start // tm + i, 0)),
        out_shape=jax.ShapeDtypeStruct((n, D_MODEL), F32),
        input_output_aliases=aliases,
        compiler_params=_cparams("parallel"),
        name="combine",
    )(*args)


def _layer(x, norm_mix_g, w_in, q_norm_g, k_norm_g, rpb, conv_dw_w, conv_dw_b, conv_ln_g, conv_ln_b,
           w_conv_out, w_attn_out, w_o, norm_ffn_g, w_router, router_bias, w_exp_gate, w_exp_up,
           w_exp_down, w_sh_gate, w_sh_up, w_sh_down):
    n = x.shape[0]
    c_qkv = 2 * D_CONV + 3 * D_ATTN
    w_qkv = w_in[:, :c_qkv].astype(BF16)
    w_gates = w_in[:, c_qkv:].astype(BF16)
    head_of = jnp.arange(D_ATTN // 2) // HEAD_DIM
    bsum = (head_of[:, None] == head_of[None, :]).astype(BF16)
    row = lambda v: v.reshape(1, -1).astype(F32)

    u, q, k, v = _inproj(x, row(norm_mix_g), w_qkv, bsum,
                         row(jnp.tile(q_norm_g, N_HEADS)), row(jnp.tile(k_norm_g, N_HEADS)))
    uc = _conv(u, conv_dw_w.reshape(CONV_WIDTH, D_CONV), row(conv_dw_b), row(conv_ln_g), row(conv_ln_b))
    at = _attn(q, k, v, _bias_table(rpb))

    wr_t = w_router.T
    wr_hi = wr_t.astype(BF16)
    wr_lo = (wr_t - wr_hi.astype(F32)).astype(BF16)
    wsgu = jnp.concatenate([w_sh_gate, w_sh_up], axis=1).astype(BF16)
    mix_weights = (row(norm_mix_g), w_gates, w_conv_out.astype(BF16), w_attn_out.astype(BF16), w_o.astype(BF16),
                   row(norm_ffn_g), wr_hi, wr_lo, router_bias.reshape(N_EXPERTS, 1).astype(F32))
    wsd = w_sh_down.astype(BF16)

    cap = n
    sizes = [n * f // PART_DENOM for f in PART_SHARES]
    xs_ref = jax.empty_ref(jax.ShapeDtypeStruct((N_EXPERTS * cap, D_MODEL // 2), U32))
    cnt = jnp.zeros((N_EXPERTS, 128), F32)
    parts = []
    start = 0
    for size in sizes:
        x1_p, hp_p, idx_p, rank_p, wt_p, cnt = _mix_route(x, uc, at, *mix_weights, cnt, start, size)
        slots_p = (idx_p * cap + rank_p).reshape(TOP_K, size // SC_CHUNK, SC_CHUNK).transpose(1, 0, 2)
        _dispatch(slots_p, hp_p, xs_ref)
        parts.append((start, size, x1_p, hp_p, wt_p, slots_p))
        start += size
    xs = jax.freeze(xs_ref)

    counts = cnt[:, 0].astype(I32)
    blocks_of = (counts + SLOT_BLK - 1) // SLOT_BLK
    blocks_end = jnp.cumsum(blocks_of)
    n_blocks = (n * TOP_K + N_EXPERTS * (SLOT_BLK - 1)) // SLOT_BLK
    n_used = blocks_end[-1]
    bseq = jnp.minimum(jnp.arange(n_blocks, dtype=I32), n_used - 1)
    bexp = jnp.minimum(jnp.sum((blocks_end[None, :] <= bseq[:, None]).astype(I32), axis=1), N_EXPERTS - 1)
    owner = (bexp[:, None] == jnp.arange(N_EXPERTS)[None, :]).astype(I32)
    local = bseq - jnp.sum(owner * (blocks_end - blocks_of)[None, :], axis=1)
    bidx = (bexp * (cap // SLOT_BLK) + local).astype(I32)
    bval = jnp.clip(jnp.sum(owner * counts[None, :], axis=1) - local * SLOT_BLK, 0, SLOT_BLK).astype(I32)
    eids = jnp.arange(N_EXPERTS, dtype=I32)
    later = (eids[None, :] > eids[:, None]) & (counts[None, :] > 0)
    next_exp = jnp.min(jnp.where(later, eids[None, :], N_EXPERTS), axis=1)
    next_exp = jnp.where(next_exp == N_EXPERTS, -1, next_exp).astype(I32)
    wslot = ((jnp.cumsum((counts > 0).astype(I32)) - 1) & 1).astype(I32)

    y = _experts(bexp, bidx, bval, n_used.reshape(1).astype(I32), next_exp, wslot, xs,
                 w_exp_gate, w_exp_up, w_exp_down)
    out = None
    for start, size, x1_p, hp_p, wt_p, slots_p in parts:
        out = _combine(wt_p, x1_p, hp_p, _gather(slots_p, y, size), wsgu, wsd, start, n, out)
    return out


def kernel(x, norm_mix_g, w_in, q_norm_g, k_norm_g, rpb, conv_dw_w, conv_dw_b, conv_ln_g, conv_ln_b, w_conv_out, w_attn_out, w_o, norm_ffn_g, w_router, router_bias, w_exp_gate, w_exp_up, w_exp_down, w_sh_gate, w_sh_up, w_sh_down):
    b, s, d = x.shape
    assert b == 1 and d == D_MODEL and s % (PART_DENOM * 2048) == 0, x.shape
    xf = x.reshape(b * s, d)
    depth = norm_mix_g.shape[0]
    for l in range(depth):
        xf = _layer(xf, norm_mix_g[l], w_in[l], q_norm_g[l], k_norm_g[l], rpb[l], conv_dw_w[l], conv_dw_b[l],
                    conv_ln_g[l], conv_ln_b[l], w_conv_out[l], w_attn_out[l], w_o[l], norm_ffn_g[l],
                    w_router[l], router_bias[l], w_exp_gate[l], w_exp_up[l], w_exp_down[l], w_sh_gate[l],
                    w_sh_up[l], w_sh_down[l])
    return xf.reshape(b, s, d)
```

```python
import functools

import jax
import jax.numpy as jnp
from jax import lax
from jax.experimental import pallas as pl
from jax.experimental.pallas import tpu as pltpu
from jax.experimental.pallas import tpu_sc as plsc

D_MODEL = 1024
GRID_W = 64
D_CONV = 512
CONV_WIDTH = 31
CONV_PAD = CONV_WIDTH // 2
N_HEADS = 8
HEAD_DIM = 64
D_ATTN = N_HEADS * HEAD_DIM
WIN_H = 8
WIN_W = 16
N_EXPERTS = 64
TOP_K = 8
N_GROUPS = 8
GROUP_SIZE = N_EXPERTS // N_GROUPS
TOPK_GROUPS = 4
D_EXPERT = 256
D_SHARED = 256
ROUTE_SCALE = 2.5
EPS = 1e-6
LOG2E = 1.4426950408889634

SUBLANES = 8
NEG = -1e30
HALO = 16
ATT_ROWS = 4
ATT_BLK = ATT_ROWS * GRID_W
SLOT_BLK = 512
MIX_SPLIT = 2
XS_AHEAD = 3
XS_SLOTS = XS_AHEAD + 1
PART_SHARES = (2, 3, 2, 1)
PART_DENOM = 8
SC_CHUNK = 64
VMEM_LIMIT = 56 * 1024 * 1024

F32 = jnp.float32
BF16 = jnp.bfloat16
I32 = jnp.int32
U32 = jnp.uint32


def _cparams(*sem):
    return pltpu.CompilerParams(dimension_semantics=sem, vmem_limit_bytes=VMEM_LIMIT)


def _rms(x, g):
    return x * lax.rsqrt(jnp.mean(x * x, axis=-1, keepdims=True) + EPS) * g


def _pack_bf16_pair(a, b):
    def bf16_bits(v):
        return lax.bitcast_convert_type(v.astype(BF16).astype(F32), U32)
    return bf16_bits(a) | (bf16_bits(b) >> 16)


def _unpack_bf16_pair(w):
    a = lax.bitcast_convert_type(w & jnp.uint32(0xFFFF0000), F32)
    b = lax.bitcast_convert_type(w << 16, F32)
    return a, b


def _inproj_kernel(x_ref, g_ref, w_ref, bsum_ref, qg_ref, kg_ref, u_ref, q_ref, k_ref, v_ref):
    h = _rms(x_ref[...], g_ref[...]).astype(BF16)
    ag = jnp.dot(h, w_ref[:, 0:2 * D_CONV], preferred_element_type=F32)
    u_ref[...] = ag[:, :D_CONV] * jax.nn.sigmoid(ag[:, D_CONV:])
    bsum = bsum_ref[...]

    def head_norm(z, g):
        zz = (z * z).astype(BF16)
        hw = D_ATTN // 2
        ss = jnp.concatenate([jnp.dot(zz[:, :hw], bsum, preferred_element_type=F32),
                              jnp.dot(zz[:, hw:], bsum, preferred_element_type=F32)], axis=1)
        return z * lax.rsqrt(ss * (1.0 / HEAD_DIM) + EPS) * g

    c0 = 2 * D_CONV
    q = jnp.dot(h, w_ref[:, c0:c0 + D_ATTN], preferred_element_type=F32)
    q_ref[...] = (head_norm(q, qg_ref[...]) * (HEAD_DIM ** -0.5 * LOG2E)).astype(BF16)
    k = jnp.dot(h, w_ref[:, c0 + D_ATTN:c0 + 2 * D_ATTN], preferred_element_type=F32)
    k_ref[...] = head_norm(k, kg_ref[...]).astype(BF16)
    v = jnp.dot(h, w_ref[:, c0 + 2 * D_ATTN:c0 + 3 * D_ATTN], preferred_element_type=F32)
    v_ref[...] = v.astype(BF16)


def _inproj(x, g, w, bsum, qg, kg, tm=512):
    n = x.shape[0]
    wc = w.shape[1]
    full = lambda i: (0, 0)
    row = lambda i: (i, 0)
    return pl.pallas_call(
        _inproj_kernel,
        grid=(n // tm,),
        in_specs=[pl.BlockSpec((tm, D_MODEL), row), pl.BlockSpec((1, D_MODEL), full),
                  pl.BlockSpec((D_MODEL, wc), full), pl.BlockSpec(bsum.shape, full),
                  pl.BlockSpec((1, D_ATTN), full), pl.BlockSpec((1, D_ATTN), full)],
        out_specs=[pl.BlockSpec((tm, D_CONV), row), pl.BlockSpec((tm, D_ATTN), row),
                   pl.BlockSpec((tm, D_ATTN), row), pl.BlockSpec((tm, D_ATTN), row)],
        out_shape=[jax.ShapeDtypeStruct((n, D_CONV), F32), jax.ShapeDtypeStruct((n, D_ATTN), BF16),
                   jax.ShapeDtypeStruct((n, D_ATTN), BF16), jax.ShapeDtypeStruct((n, D_ATTN), BF16)],
        compiler_params=_cparams("parallel"),
        name="inproj",
    )(x, g, w, bsum, qg, kg)


def _conv_kernel(up_ref, uc_ref, un_ref, w_ref, b_ref, lg_ref, lb_ref, o_ref, ext_ref, sh_ref, *, tc, ch):
    i = pl.program_id(0)
    last = pl.num_programs(0) - 1
    ext_ref[0:HALO, :] = jnp.where(i > 0, up_ref[...], 0.0)
    ext_ref[HALO:HALO + tc, :] = uc_ref[...]
    ext_ref[HALO + tc:2 * HALO + tc, :] = jnp.where(i < last, un_ref[...], 0.0)
    span = sh_ref.shape[1]
    for b in range(SUBLANES):
        sh_ref[b] = ext_ref[b:b + span, :]
    bias = b_ref[...]
    lg = lg_ref[...]
    lb = lb_ref[...]
    for c in range(tc // ch):
        acc = jnp.broadcast_to(bias, (ch, D_CONV))
        for j in range(CONV_WIDTH):
            shift = HALO - CONV_PAD + j
            row0 = c * ch + shift // SUBLANES * SUBLANES
            acc = acc + sh_ref[shift % SUBLANES, row0:row0 + ch, :] * w_ref[j:j + 1, :]
        mu = jnp.mean(acc, axis=-1, keepdims=True)
        d = acc - mu
        var = jnp.mean(d * d, axis=-1, keepdims=True)
        y = d * lax.rsqrt(var + EPS) * lg + lb
        o_ref[c * ch:(c + 1) * ch, :] = (y * jax.nn.sigmoid(y)).astype(BF16)


def _conv(u, w, b, lg, lb, tc=512, ch=512):
    n = u.shape[0]
    hb = tc // HALO
    nh = n // HALO
    full = lambda i: (0, 0)
    return pl.pallas_call(
        functools.partial(_conv_kernel, tc=tc, ch=ch),
        grid=(n // tc,),
        in_specs=[pl.BlockSpec((HALO, D_CONV), lambda i: (jnp.maximum(i * hb - 1, 0), 0)),
                  pl.BlockSpec((tc, D_CONV), lambda i: (i, 0)),
                  pl.BlockSpec((HALO, D_CONV), lambda i: (jnp.minimum((i + 1) * hb, nh - 1), 0)),
                  pl.BlockSpec((CONV_WIDTH, D_CONV), full), pl.BlockSpec((1, D_CONV), full),
                  pl.BlockSpec((1, D_CONV), full), pl.BlockSpec((1, D_CONV), full)],
        out_specs=pl.BlockSpec((tc, D_CONV), lambda i: (i, 0)),
        out_shape=jax.ShapeDtypeStruct((n, D_CONV), BF16),
        scratch_shapes=[pltpu.VMEM((tc + 2 * HALO, D_CONV), F32),
                        pltpu.VMEM((SUBLANES, tc + 2 * HALO - SUBLANES, D_CONV), F32)],
        compiler_params=_cparams("parallel"),
        name="conv",
    )(u, u, u, w, b, lg, lb)


def _attn_kernel(q_ref, kp_ref, kc_ref, kn_ref, vp_ref, vc_ref, vn_ref, tb_ref, o_ref, *, rows):
    i = pl.program_id(0)
    nkey = 3 * ATT_BLK
    lrow = lax.broadcasted_iota(I32, (1, nkey), 1) >> 6
    lane = lax.broadcasted_iota(I32, (1, 2 * HEAD_DIM), 1)
    masks = []
    for j in range(ATT_ROWS):
        r = i * ATT_ROWS + j
        rs = jnp.clip(r - WIN_H // 2, 0, rows - WIN_H)
        lo = rs - (i - 1) * ATT_ROWS
        masks.append(jnp.where((lrow >= lo) & (lrow < lo + WIN_H), 0.0, NEG))
    dn = (((1,), (1,)), ((), ()))

    def pair_slice(h):
        return slice(2 * HEAD_DIM * (h // 2), 2 * HEAD_DIM * (h // 2 + 1))

    def in_half(h):
        return (lane >= HEAD_DIM * (h % 2)) & (lane < HEAD_DIM * (h % 2 + 1))

    def scores(h):
        sl = pair_slice(h)
        q2 = q_ref[:, sl]
        qm = jnp.where(in_half(h), q2, jnp.zeros_like(q2))
        s = jnp.concatenate([lax.dot_general(qm, kk[:, sl], dn, preferred_element_type=F32)
                             for kk in (kp_ref, kc_ref, kn_ref)], axis=1)
        parts = []
        for j in range(ATT_ROWS):
            bias = jnp.concatenate([tb_ref[h, 2 * m - j + 3] for m in range(nkey // (2 * GRID_W))], axis=1)
            parts.append(s[j * GRID_W:(j + 1) * GRID_W, :] + bias + masks[j])
        return jnp.concatenate(parts, axis=0)

    def softmax(s):
        mx = jnp.max(s, axis=-1, keepdims=True)
        e = jnp.exp2(s - mx)
        return e.astype(BF16), jnp.sum(e, axis=-1, keepdims=True)

    def values(h, pb, den):
        sl = pair_slice(h)
        o = sum(jnp.dot(pb[:, ATT_BLK * t:ATT_BLK * (t + 1)], vv[:, sl], preferred_element_type=F32)
                for t, vv in enumerate((vp_ref, vc_ref, vn_ref)))
        return o / den

    stage_s = {h: scores(h) for h in range(2)}
    stage_p = {0: softmax(stage_s.pop(0))}
    out_pair = None
    for h in range(N_HEADS):
        if h + 2 < N_HEADS:
            stage_s[h + 2] = scores(h + 2)
        if h + 1 < N_HEADS:
            stage_p[h + 1] = softmax(stage_s.pop(h + 1))
        o = values(h, *stage_p.pop(h))
        if h % 2 == 0:
            out_pair = o
        else:
            o_ref[:, pair_slice(h)] = jnp.where(in_half(h), o, out_pair).astype(BF16)


def _attn(q, k, v, tb):
    n = q.shape[0]
    rows = n // GRID_W
    nb = n // ATT_BLK
    cur = lambda i: (i, 0)
    prev = lambda i: (jnp.maximum(i - 1, 0), 0)
    nxt = lambda i: (jnp.minimum(i + 1, nb - 1), 0)
    blk = (ATT_BLK, D_ATTN)
    return pl.pallas_call(
        functools.partial(_attn_kernel, rows=rows),
        grid=(nb,),
        in_specs=[pl.BlockSpec(blk, cur), pl.BlockSpec(blk, prev), pl.BlockSpec(blk, cur), pl.BlockSpec(blk, nxt),
                  pl.BlockSpec(blk, prev), pl.BlockSpec(blk, cur), pl.BlockSpec(blk, nxt),
                  pl.BlockSpec(tb.shape, lambda i: (0, 0, 0, 0))],
        out_specs=pl.BlockSpec(blk, cur),
        out_shape=jax.ShapeDtypeStruct((n, D_ATTN), BF16),
        compiler_params=_cparams("parallel"),
        name="attn",
    )(q, k, k, k, v, v, v, tb)


def _bias_table(rpb):
    cols = jnp.arange(GRID_W)
    start = jnp.clip(cols - WIN_W // 2, 0, GRID_W - WIN_W)
    kc = cols[None, :]
    inwin = (kc >= start[:, None]) & (kc < start[:, None] + WIN_W)
    rel_idx = kc - cols[:, None] + (WIN_W - 1)
    pick = ((rel_idx[None] == jnp.arange(2 * WIN_W - 1)[:, None, None]) & inwin[None]).astype(F32)
    rel = jnp.einsum('hdj,jck->hdck', rpb.astype(F32), pick, precision=lax.Precision.HIGHEST)
    t = jnp.where(inwin[None, None], rel * LOG2E, NEG)
    t = jnp.concatenate([t, jnp.full_like(t[:, :1], NEG)], axis=1)
    t_next = jnp.concatenate([t[:, 1:], t[:, -1:]], axis=1)
    return jnp.concatenate([t, t_next], axis=-1).astype(F32)


def _mix_route_kernel(x_ref, uc_ref, at_ref, g1_ref, wg_ref, wco_ref, wao_ref, wo_ref, g2_ref,
                      wrh_ref, wrl_ref, rb_ref, tri_ref, cin_ref,
                      xo_ref, hp_ref, idx_ref, rank_ref, wt_ref, cnt_ref, carry_ref, *, tm):
    i = pl.program_id(0)

    @pl.when(i == 0)
    def _():
        carry_ref[...] = cin_ref[...]

    th = tm // MIX_SPLIT
    dn = (((1,), (1,)), ((), ()))

    def mix(p):
        rs = slice(p * th, (p + 1) * th)
        x = x_ref[rs, :]
        h = _rms(x, g1_ref[...]).astype(BF16)
        gates = jax.nn.sigmoid(jnp.dot(h, wg_ref[...], preferred_element_type=F32))
        y_conv = jnp.dot(uc_ref[rs, :], wco_ref[...], preferred_element_type=F32)
        y_attn = jnp.dot(at_ref[rs, :], wao_ref[...], preferred_element_type=F32)
        merged = gates[:, :D_MODEL] * y_conv + gates[:, D_MODEL:] * y_attn
        x1 = x + jnp.dot(merged.astype(BF16), wo_ref[...], preferred_element_type=F32)
        h2 = _rms(x1, g2_ref[...])
        xo_ref[rs, :] = x1
        hp_ref[rs, :] = _pack_bf16_pair(h2[:, :D_MODEL // 2], h2[:, D_MODEL // 2:])
        return (h2,)

    def router_logits(p, h2):
        h2b = h2.astype(BF16)
        h2l = (h2 - h2b.astype(F32)).astype(BF16)
        return (lax.dot_general(wrh_ref[...], h2b, dn, preferred_element_type=F32)
                + lax.dot_general(wrl_ref[...], h2b, dn, preferred_element_type=F32)
                + lax.dot_general(wrh_ref[...], h2l, dn, preferred_element_type=F32))

    def route(p, logits, carry):
        cs = slice(p * th, (p + 1) * th)
        scores = jax.nn.sigmoid(logits)
        biased = scores + rb_ref[...]

        sub = lax.broadcasted_iota(I32, (GROUP_SIZE, th), 0).astype(F32)
        groups, gscore = [], []
        for g in range(N_GROUPS):
            bg = biased[g * GROUP_SIZE:(g + 1) * GROUP_SIZE, :]
            m1 = jnp.max(bg, axis=0, keepdims=True)
            first = jnp.min(jnp.where(bg == m1, sub, float(GROUP_SIZE)), axis=0, keepdims=True)
            m2 = jnp.max(jnp.where(sub == first, -jnp.inf, bg), axis=0, keepdims=True)
            groups.append(bg)
            gscore.append(m1 + m2)
        masked = []
        for g in range(N_GROUPS):
            beaten = jnp.zeros((1, th), F32)
            for o in range(N_GROUPS):
                if o == g:
                    continue
                wins = (gscore[o] >= gscore[g]) if o < g else (gscore[o] > gscore[g])
                beaten = beaten + jnp.where(wins, 1.0, 0.0)
            masked.append(jnp.where(beaten < TOPK_GROUPS, groups[g], -jnp.inf))
        cur = jnp.concatenate(masked, axis=0)

        eid = lax.broadcasted_iota(I32, (N_EXPERTS, th), 0).astype(F32)
        idx_rows, sc_rows = [], []
        sel = jnp.zeros((N_EXPERTS, th), F32)
        for _ in range(TOP_K):
            m = jnp.max(cur, axis=0, keepdims=True)
            idx = jnp.min(jnp.where(cur == m, eid, float(N_EXPERTS)), axis=0, keepdims=True)
            hit = eid == idx
            sel = jnp.where(hit, 1.0, sel)
            cur = jnp.where(hit, -jnp.inf, cur)
            idx_rows.append(idx)
            sc_rows.append(jnp.sum(jnp.where(hit, scores, 0.0), axis=0, keepdims=True))
        wsum = sc_rows[0]
        for k in range(1, TOP_K):
            wsum = wsum + sc_rows[k]

        prefix = jnp.dot(sel.astype(BF16), tri_ref[...], preferred_element_type=F32)
        rank = carry + prefix
        rank_rows = [jnp.sum(jnp.where(eid == idx_rows[k], rank, 0.0), axis=0, keepdims=True)
                     for k in range(TOP_K)]
        idx_ref[:, cs] = jnp.concatenate(idx_rows, axis=0).astype(I32)
        rank_ref[:, cs] = jnp.concatenate(rank_rows, axis=0).astype(I32)
        wt_ref[:, cs] = jnp.concatenate([s / wsum * ROUTE_SCALE for s in sc_rows], axis=0)
        return carry + jnp.sum(sel, axis=1, keepdims=True)

    carry = carry_ref[:, 0:1]
    logits = router_logits(0, *mix(0))
    for p in range(MIX_SPLIT):
        if p + 1 < MIX_SPLIT:
            mixed_next = mix(p + 1)
        carry = route(p, logits, carry)
        if p + 1 < MIX_SPLIT:
            logits = router_logits(p + 1, *mixed_next)
    carry_ref[...] = jnp.broadcast_to(carry, carry_ref.shape)
    cnt_ref[...] = carry_ref[...]


def _mix_route(x, uc, at, g1, wg, wco, wao, wo, g2, wrh, wrl, rb, counts_in, start, n, tm=1024):
    steps = n // tm
    th = tm // MIX_SPLIT
    tri = (jnp.arange(th)[:, None] < jnp.arange(th)[None, :]).astype(BF16)
    src = lambda i: (start // tm + i, 0)
    row = lambda i: (i, 0)
    col = lambda i: (0, i)
    full = lambda i: (0, 0)
    ins = [x, uc, at, g1, wg, wco, wao, wo, g2, wrh, wrl, rb, tri, counts_in]
    in_specs = [pl.BlockSpec((tm, D_MODEL), src), pl.BlockSpec((tm, D_CONV), src), pl.BlockSpec((tm, D_ATTN), src)]
    in_specs += [pl.BlockSpec(a.shape, full) for a in ins[3:]]
    return pl.pallas_call(
        functools.partial(_mix_route_kernel, tm=tm),
        grid=(steps,),
        in_specs=in_specs,
        out_specs=[pl.BlockSpec((tm, D_MODEL), row), pl.BlockSpec((tm, D_MODEL // 2), row),
                   pl.BlockSpec((TOP_K, tm), col), pl.BlockSpec((TOP_K, tm), col), pl.BlockSpec((TOP_K, tm), col),
                   pl.BlockSpec((N_EXPERTS, 128), full)],
        out_shape=[jax.ShapeDtypeStruct((n, D_MODEL), F32), jax.ShapeDtypeStruct((n, D_MODEL // 2), U32),
                   jax.ShapeDtypeStruct((TOP_K, n), I32), jax.ShapeDtypeStruct((TOP_K, n), I32),
                   jax.ShapeDtypeStruct((TOP_K, n), F32), jax.ShapeDtypeStruct((N_EXPERTS, 128), F32)],
        scratch_shapes=[pltpu.VMEM((N_EXPERTS, 128), F32)],
        compiler_params=_cparams("arbitrary"),
        name="mix_route",
    )(*ins)


def _sc_workers():
    info = plsc.get_sparse_core_info()
    return info.num_cores, info.num_subcores


def _sc_mesh():
    return plsc.VectorSubcoreMesh(core_axis_name="c", subcore_axis_name="s")


def _worker_id(n_cores):
    return lax.axis_index("s") * n_cores + lax.axis_index("c")


def _dispatch(slots3, hp, xs_ref):
    n_chunks = slots3.shape[0]
    n_cores, n_sub = _sc_workers()
    per_worker = n_chunks // (n_cores * n_sub)
    assert per_worker * n_cores * n_sub == n_chunks

    @functools.partial(
        pl.kernel, mesh=_sc_mesh(), out_type=(),
        scratch_types=[pltpu.VMEM((TOP_K, SC_CHUNK), I32), pltpu.VMEM((SC_CHUNK, D_MODEL // 2), U32),
                       pltpu.SemaphoreType.DMA],
        name="dispatch")
    def run(slots_hbm, hp_hbm, xs_hbm, idx_v, rows_v, sem):
        first = _worker_id(n_cores) * per_worker

        @pl.loop(0, per_worker)
        def _(j):
            c = first + j
            pltpu.sync_copy(slots_hbm.at[c], idx_v)
            pltpu.sync_copy(hp_hbm.at[pl.ds(c * SC_CHUNK, SC_CHUNK)], rows_v)
            copies = [pltpu.async_copy(rows_v, xs_hbm.at[idx_v.at[k]], sem) for k in range(TOP_K)]
            for cp in copies:
                cp.wait()

    run(slots3, hp, xs_ref)


def _experts_kernel(bexp_ref, bidx_ref, bval_ref, nused_ref, next_ref, wslot_ref,
                    xs_ref, wg_ref, wu_ref, wd_ref, y_ref, xbuf, wg_buf, wu_buf, wd_buf, wgu_s, wd_s, xsem, wsem):
    b = pl.program_id(0)
    n_used = nused_ref[0]

    def block_copy(blk):
        slot = blk % XS_SLOTS
        rows = pl.ds(pl.multiple_of(bidx_ref[blk] * SLOT_BLK, SLOT_BLK), SLOT_BLK)
        return pltpu.make_async_copy(xs_ref.at[rows], xbuf.at[slot], xsem.at[slot])

    @pl.when(b == 0)
    def _():
        for blk in range(XS_AHEAD):
            @pl.when(blk < n_used)
            def _():
                block_copy(blk).start()

    @pl.when(b + XS_AHEAD < n_used)
    def _():
        block_copy(b + XS_AHEAD).start()

    def weight_copies(e, slot):
        return [pltpu.make_async_copy(wg_ref.at[e], wg_buf.at[slot], wsem.at[slot]),
                pltpu.make_async_copy(wu_ref.at[e], wu_buf.at[slot], wsem.at[slot]),
                pltpu.make_async_copy(wd_ref.at[e], wd_buf.at[slot], wsem.at[slot])]

    @pl.when(b < n_used)
    def _():
        e = bexp_ref[b]
        slot = wslot_ref[e]

        @pl.when(b == 0)
        def _():
            for cp in weight_copies(e, slot):
                cp.start()

        @pl.when((b == 0) | (bexp_ref[jnp.maximum(b - 1, 0)] != e))
        def _():
            for cp in weight_copies(e, slot):
                cp.wait()
            wgu_s[:, :D_EXPERT] = wg_buf[slot].astype(BF16)
            wgu_s[:, D_EXPERT:] = wu_buf[slot].astype(BF16)
            wd_s[...] = wd_buf[slot].astype(BF16)
            nxt = next_ref[e]

            @pl.when(nxt >= 0)
            def _():
                for cp in weight_copies(nxt, 1 - slot):
                    cp.start()

        srow = lax.broadcasted_iota(I32, (SLOT_BLK, 1), 0)
        block_copy(b).wait()
        xw = jnp.where(srow < bval_ref[b], xbuf[b % XS_SLOTS], jnp.uint32(0))
        xa, xb = _unpack_bf16_pair(xw)
        x = jnp.concatenate([xa.astype(BF16), xb.astype(BF16)], axis=1)
        gu = jnp.dot(x, wgu_s[...], preferred_element_type=F32)
        mid = (jax.nn.silu(gu[:, :D_EXPERT]) * gu[:, D_EXPERT:]).astype(BF16)
        y = jnp.dot(mid, wd_s[...], preferred_element_type=F32)
        y_ref[...] = _pack_bf16_pair(y[:, :D_MODEL // 2], y[:, D_MODEL // 2:])


def _experts(bexp, bidx, bval, nused, next_exp, wslot, xs, wg, wu, wd):
    n_blocks = bexp.shape[0]
    xmap = lambda b, be, bi, bv, nu, nx, ws: (bi[b], 0)
    grid_spec = pltpu.PrefetchScalarGridSpec(
        num_scalar_prefetch=6,
        grid=(n_blocks,),
        in_specs=[pl.BlockSpec(memory_space=pl.ANY),
                  pl.BlockSpec(memory_space=pl.ANY), pl.BlockSpec(memory_space=pl.ANY),
                  pl.BlockSpec(memory_space=pl.ANY)],
        out_specs=pl.BlockSpec((SLOT_BLK, D_MODEL // 2), xmap),
        scratch_shapes=[pltpu.VMEM((XS_SLOTS, SLOT_BLK, D_MODEL // 2), U32),
                        pltpu.VMEM((2, D_MODEL, D_EXPERT), F32), pltpu.VMEM((2, D_MODEL, D_EXPERT), F32),
                        pltpu.VMEM((2, D_EXPERT, D_MODEL), F32),
                        pltpu.VMEM((D_MODEL, 2 * D_EXPERT), BF16), pltpu.VMEM((D_EXPERT, D_MODEL), BF16),
                        pltpu.SemaphoreType.DMA((XS_SLOTS,)), pltpu.SemaphoreType.DMA((2,))],
    )
    return pl.pallas_call(
        _experts_kernel,
        grid_spec=grid_spec,
        out_shape=jax.ShapeDtypeStruct(xs.shape, U32),
        compiler_params=_cparams("arbitrary"),
        name="experts",
    )(bexp, bidx, bval, nused, next_exp, wslot, xs, wg, wu, wd)


def _gather(slots3, y, n):
    n_chunks = slots3.shape[0]
    n_cores, n_sub = _sc_workers()
    per_worker = n_chunks // (n_cores * n_sub)
    assert per_worker * n_cores * n_sub == n_chunks

    @functools.partial(
        pl.kernel, mesh=_sc_mesh(),
        out_type=jax.ShapeDtypeStruct((TOP_K, n, D_MODEL // 2), U32),
        scratch_types=[pltpu.VMEM((TOP_K, SC_CHUNK), I32),
                       pltpu.VMEM((SC_CHUNK, D_MODEL // 2), U32), pltpu.VMEM((SC_CHUNK, D_MODEL // 2), U32),
                       pltpu.SemaphoreType.DMA, pltpu.SemaphoreType.DMA, pltpu.SemaphoreType.DMA],
        name="gather")
    def run(slots_hbm, y_hbm, g_hbm, idx_v, buf0, buf1, gsem, wsem0, wsem1):
        first = _worker_id(n_cores) * per_worker
        bufs = (buf0, buf1)
        wsems = (wsem0, wsem1)

        @pl.loop(0, per_worker)
        def _(j):
            c = first + j
            pltpu.sync_copy(slots_hbm.at[c], idx_v)
            gathers = [None] * TOP_K
            writes = [None] * TOP_K
            gathers[0] = pltpu.async_copy(y_hbm.at[idx_v.at[0]], bufs[0], gsem)
            for k in range(TOP_K):
                gathers[k].wait()
                if k >= 1:
                    writes[k - 1].wait()
                if k + 1 < TOP_K:
                    gathers[k + 1] = pltpu.async_copy(y_hbm.at[idx_v.at[k + 1]], bufs[(k + 1) % 2], gsem)
                writes[k] = pltpu.async_copy(bufs[k % 2], g_hbm.at[k, pl.ds(c * SC_CHUNK, SC_CHUNK)],
                                             wsems[k % 2])
            writes[TOP_K - 1].wait()

    return run(slots3, y)


def _combine_kernel(wt_ref, x1_ref, hp_ref, g_ref, wsgu_ref, wsd_ref, *rest):
    o_ref = rest[-1]
    half = D_MODEL // 2
    ha, hb = _unpack_bf16_pair(hp_ref[...])
    h2b = jnp.concatenate([ha.astype(BF16), hb.astype(BF16)], axis=1)
    gu = jnp.dot(h2b, wsgu_ref[...], preferred_element_type=F32)
    mid = (jax.nn.silu(gu[:, :D_SHARED]) * gu[:, D_SHARED:]).astype(BF16)
    shared = jnp.dot(mid, wsd_ref[...], preferred_element_type=F32)
    acc_a = x1_ref[:, :half] + shared[:, :half]
    acc_b = x1_ref[:, half:] + shared[:, half:]
    w_all = wt_ref[...].T
    for k in range(TOP_K):
        ya, yb = _unpack_bf16_pair(g_ref[k])
        w = w_all[:, k:k + 1]
        acc_a = acc_a + w * ya
        acc_b = acc_b + w * yb
    o_ref[:, :half] = acc_a
    o_ref[:, half:] = acc_b


def _combine(wt, x1, hp, g, wsgu, wsd, start, n, out_so_far, tm=512):
    steps = g.shape[1] // tm
    full = lambda i: (0, 0)
    in_specs = [pl.BlockSpec((TOP_K, tm), lambda i: (0, i)), pl.BlockSpec((tm, D_MODEL), lambda i: (i, 0)),
                pl.BlockSpec((tm, D_MODEL // 2), lambda i: (i, 0)),
                pl.BlockSpec((TOP_K, tm, D_MODEL // 2), lambda i: (0, i, 0)),
                pl.BlockSpec(wsgu.shape, full), pl.BlockSpec(wsd.shape, full)]
    args = [wt, x1, hp, g, wsgu, wsd]
    aliases = {}
    if out_so_far is not None:
        in_specs.append(pl.BlockSpec(memory_space=pl.ANY))
        args.append(out_so_far)
        aliases = {len(args) - 1: 0}
    return pl.pallas_call(
        _combine_kernel,
        grid=(steps,),
        in_specs=in_specs,
        out_specs=pl.BlockSpec((tm, D_MODEL), lambda i: (start // tm + i, 0)),
        out_shape=jax.ShapeDtypeStruct((n, D_MODEL), F32),
        input_output_aliases=aliases,
        compiler_params=_cparams("parallel"),
        name="combine",
    )(*args)


def _layer(x, norm_mix_g, w_in, q_norm_g, k_norm_g, rpb, conv_dw_w, conv_dw_b, conv_ln_g, conv_ln_b,
           w_conv_out, w_attn_out, w_o, norm_ffn_g, w_router, router_bias, w_exp_gate, w_exp_up,
           w_exp_down, w_sh_gate, w_sh_up, w_sh_down):
    n = x.shape[0]
    c_qkv = 2 * D_CONV + 3 * D_ATTN
    w_qkv = w_in[:, :c_qkv].astype(BF16)
    w_gates = w_in[:, c_qkv:].astype(BF16)
    head_of = jnp.arange(D_ATTN // 2) // HEAD_DIM
    bsum = (head_of[:, None] == head_of[None, :]).astype(BF16)
    row = lambda v: v.reshape(1, -1).astype(F32)

    u, q, k, v = _inproj(x, row(norm_mix_g), w_qkv, bsum,
                         row(jnp.tile(q_norm_g, N_HEADS)), row(jnp.tile(k_norm_g, N_HEADS)))
    uc = _conv(u, conv_dw_w.reshape(CONV_WIDTH, D_CONV), row(conv_dw_b), row(conv_ln_g), row(conv_ln_b))
    at = _attn(q, k, v, _bias_table(rpb))

    wr_t = w_router.T
    wr_hi = wr_t.astype(BF16)
    wr_lo = (wr_t - wr_hi.astype(F32)).astype(BF16)
    wsgu = jnp.concatenate([w_sh_gate, w_sh_up], axis=1).astype(BF16)
    mix_weights = (row(norm_mix_g), w_gates, w_conv_out.astype(BF16), w_attn_out.astype(BF16), w_o.astype(BF16),
                   row(norm_ffn_g), wr_hi, wr_lo, router_bias.reshape(N_EXPERTS, 1).astype(F32))
    wsd = w_sh_down.astype(BF16)

    cap = n
    sizes = [n * f // PART_DENOM for f in PART_SHARES]
    xs_ref = jax.empty_ref(jax.ShapeDtypeStruct((N_EXPERTS * cap, D_MODEL // 2), U32))
    cnt = jnp.zeros((N_EXPERTS, 128), F32)
    parts = []
    start = 0
    for size in sizes:
        x1_p, hp_p, idx_p, rank_p, wt_p, cnt = _mix_route(x, uc, at, *mix_weights, cnt, start, size)
        slots_p = (idx_p * cap + rank_p).reshape(TOP_K, size // SC_CHUNK, SC_CHUNK).transpose(1, 0, 2)
        _dispatch(slots_p, hp_p, xs_ref)
        parts.append((start, size, x1_p, hp_p, wt_p, slots_p))
        start += size
    xs = jax.freeze(xs_ref)

    counts = cnt[:, 0].astype(I32)
    blocks_of = (counts + SLOT_BLK - 1) // SLOT_BLK
    blocks_end = jnp.cumsum(blocks_of)
    n_blocks = (n * TOP_K + N_EXPERTS * (SLOT_BLK - 1)) // SLOT_BLK
    n_used = blocks_end[-1]
    bseq = jnp.minimum(jnp.arange(n_blocks, dtype=I32), n_used - 1)
    bexp = jnp.minimum(jnp.sum((blocks_end[None, :] <= bseq[:, None]).astype(I32), axis=1), N_EXPERTS - 1)
    owner = (bexp[:, None] == jnp.arange(N_EXPERTS)[None, :]).astype(I32)
    local = bseq - jnp.sum(owner * (blocks_end - blocks_of)[None, :], axis=1)
    bidx = (bexp * (cap // SLOT_BLK) + local).astype(I32)
    bval = jnp.clip(jnp.sum(owner * counts[None, :], axis=1) - local * SLOT_BLK, 0, SLOT_BLK).astype(I32)
    eids = jnp.arange(N_EXPERTS, dtype=I32)
    later = (eids[None, :] > eids[:, None]) & (counts[None, :] > 0)
    next_exp = jnp.min(jnp.where(later, eids[None, :], N_EXPERTS), axis=1)
    next_exp = jnp.where(next_exp == N_EXPERTS, -1, next_exp).astype(I32)
    wslot = ((jnp.cumsum((counts > 0).astype(I32)) - 1) & 1).astype(I32)

    y = _experts(bexp, bidx, bval, n_used.reshape(1).astype(I32), next_exp, wslot, xs,
                 w_exp_gate, w_exp_up, w_exp_down)
    out = None
    for start, size, x1_p, hp_p, wt_p, slots_p in parts:
        out = _combine(wt_p, x1_p, hp_p, _gather(slots_p, y, size), wsgu, wsd, start, n, out)
    return out


def kernel(x, norm_mix_g, w_in, q_norm_g, k_norm_g, rpb, conv_dw_w, conv_dw_b, conv_ln_g, conv_ln_b, w_conv_out, w_attn_out, w_o, norm_ffn_g, w_router, router_bias, w_exp_gate, w_exp_up, w_exp_down, w_sh_gate, w_sh_up, w_sh_down):
    b, s, d = x.shape
    assert b == 1 and d == D_MODEL and s % (PART_DENOM * 2048) == 0, x.shape
    xf = x.reshape(b * s, d)
    depth = norm_mix_g.shape[0]
    for l in range(depth):
        xf = _layer(xf, norm_mix_g[l], w_in[l], q_norm_g[l], k_norm_g[l], rpb[l], conv_dw_w[l], conv_dw_b[l],
                    conv_ln_g[l], conv_ln_b[l], w_conv_out[l], w_attn_out[l], w_o[l], norm_ffn_g[l],
                    w_router[l], router_bias[l], w_exp_gate[l], w_exp_up[l], w_exp_down[l], w_sh_gate[l],
                    w_sh_up[l], w_sh_down[l])
    return xf.reshape(b, s, d)
```
